```python
import math
import jax, jax.numpy as jnp
from jax import lax
import numpy as np

D_MODEL = 1024
BATCH = 8
SEQ = 4096
DEPTH = 1

N_ATTN_HEADS = 4
ATTN_HEAD_DIM = 64
ATTN_V_DIM = 2 * ATTN_HEAD_DIM
ATTN_WIDTH = N_ATTN_HEADS * ATTN_V_DIM
ROPE_THETA = 500000.0
ROPE_DIM = ATTN_HEAD_DIM // 4
Q_BLOCK = 128
SSM_WIDTH = D_MODEL // 2
SSM_GROUP = 16
SSM_GROUPS = SSM_WIDTH // SSM_GROUP
SSM_STATE = 64
DT_MIN = 1e-3
DT_MAX = 1e-1
N_BRANCH = 2
Q_COLS = N_ATTN_HEADS * 2 * ATTN_HEAD_DIM
K_COLS = N_ATTN_HEADS * 2 * ATTN_HEAD_DIM
V_COLS = ATTN_WIDTH
U_COLS = SSM_WIDTH
G_COLS = N_BRANCH * D_MODEL
IN_COLS = Q_COLS + K_COLS + V_COLS + U_COLS + G_COLS
N_EXPERT_GROUPS = 4
EXPERTS_PER_GROUP = 8
TOP_K = 2
D_EXPERT = D_MODEL // 4
EPS = 1e-6

kernel_name = "hybrid_diffattn_s5_hiermoe"


def rms_norm(x, g):
    xf = x.astype(jnp.float32)
    y = xf * lax.rsqrt(jnp.mean(xf * xf, axis=-1, keepdims=True) + EPS)
    return (y * g.astype(jnp.float32)).astype(x.dtype)


def rope_partial(t, positions):
    half = ROPE_DIM // 2
    inv = ROPE_THETA ** (-jnp.arange(0, ROPE_DIM, 2, dtype=jnp.float32) / ROPE_DIM)
    ang = positions.astype(jnp.float32)[..., None] * inv
    cos = jnp.cos(ang)[:, :, None, None, :]
    sin = jnp.sin(ang)[:, :, None, None, :]
    tf = t.astype(jnp.float32)
    r1, r2, rest = tf[..., :half], tf[..., half:ROPE_DIM], tf[..., ROPE_DIM:]
    out = jnp.concatenate([r1 * cos - r2 * sin, r2 * cos + r1 * sin, rest], axis=-1)
    return out.astype(t.dtype)


def diff_attention(q, k, v, lam):
    b, h, _, s, dh = q.shape
    dv = v.shape[-1]
    nb = s // Q_BLOCK
    qb = q.reshape(b, h, 2, nb, Q_BLOCK, dh).transpose(3, 0, 1, 2, 4, 5)
    kpos = jnp.arange(s)
    scale = dh ** -0.5
    neg = jnp.finfo(jnp.float32).min

    def block(args):
        qi, i = args
        sc = jnp.einsum('bhcqd,bhckd->bhcqk', qi, k).astype(jnp.float32) * scale
        qpos = i * Q_BLOCK + jnp.arange(Q_BLOCK)
        mask = kpos[None, :] <= qpos[:, None]
        p = jax.nn.softmax(jnp.where(mask, sc, neg), axis=-1)
        w = p[:, :, 0] - lam * p[:, :, 1]
        return jnp.einsum('bhqk,bhkd->bhqd', w.astype(v.dtype), v)

    out = lax.map(block, (qb, jnp.arange(nb)))
    return out.transpose(1, 2, 0, 3, 4).reshape(b, h, s, dv)


def s5_grouped(u, lam_re, lam_im, log_dt, b_re, b_im, c_re, c_im, d_skip):
    bsz, s, _ = u.shape
    uf = u.astype(jnp.float32).reshape(bsz, s, SSM_GROUPS, SSM_GROUP)
    lam = lax.complex(lam_re.astype(jnp.float32), lam_im.astype(jnp.float32))
    dt = jnp.exp(log_dt.astype(jnp.float32))[:, None]
    lam_bar = jnp.exp(lam * dt)
    bmat = lax.complex(b_re.astype(jnp.float32), b_im.astype(jnp.float32))
    b_bar = ((lam_bar - 1.0) / lam)[..., None] * bmat
    bu = jnp.einsum('gph,bsgh->bsgp', b_bar, uf.astype(jnp.complex64))
    a = jnp.broadcast_to(lam_bar, (1, s) + lam_bar.shape)

    def combine(e1, e2):
        a1, b1 = e1
        a2, b2 = e2
        return a1 * a2, a2 * b1 + b2

    _, states = lax.associative_scan(combine, (a, bu), axis=1)
    cmat = lax.complex(c_re.astype(jnp.float32), c_im.astype(jnp.float32))
    y = jnp.einsum('ghp,bsgp->bsgh', cmat, states).real
    y = y + d_skip.astype(jnp.float32).reshape(SSM_GROUPS, SSM_GROUP) * uf
    return y.reshape(bsz, s, SSM_WIDTH).astype(u.dtype)


def hier_moe(h, w_rg, b_rg, w_re, b_re, w_gate, w_up, w_down):
    bsz, s, d = h.shape
    t = h.reshape(-1, d)
    g_prob = jax.nn.softmax((t @ w_rg).astype(jnp.float32) + b_rg.astype(jnp.float32), axis=-1)
    grp = jnp.argmax(g_prob, axis=-1)
    p_grp = jnp.max(g_prob, axis=-1)
    grp_oh = jax.nn.one_hot(grp, N_EXPERT_GROUPS, dtype=jnp.float32)
    e_logits = jnp.einsum('td,dge->tge', t, w_re).astype(jnp.float32) + b_re.astype(jnp.float32)
    e_sel = jnp.einsum('tge,tg->te', e_logits, grp_oh)
    top_v, top_i = lax.top_k(e_sel, TOP_K)
    top_w = jax.nn.softmax(top_v, axis=-1) * p_grp[:, None]
    expert_w = jnp.sum(jax.nn.one_hot(top_i, EXPERTS_PER_GROUP, dtype=jnp.float32) * top_w[..., None], axis=1)
    comb = (grp_oh[:, :, None] * expert_w[:, None, :]).astype(h.dtype)
    out = jnp.zeros((t.shape[0], d), jnp.float32)
    for gi in range(N_EXPERT_GROUPS):
        hg = jax.nn.silu(jnp.einsum('td,edf->tef', t, w_gate[gi])) * jnp.einsum('td,edf->tef', t, w_up[gi])
        hg = hg * comb[:, gi, :, None]
        out = out + jnp.einsum('tef,efd->td', hg, w_down[gi]).astype(jnp.float32)
    return out.reshape(bsz, s, d).astype(h.dtype)


def setup_inputs(seed: int = 0) -> dict:
    key = jax.random.key(seed)
    ks = jax.random.split(key, 32)
    f32 = jnp.float32
    L = DEPTH
    nrm = lambda k, shp, sc: jax.random.normal(k, shp, f32) * sc
    x = jax.random.normal(ks[0], (BATCH, SEQ, D_MODEL), f32)
    offset = jax.random.randint(ks[1], (BATCH, 1), 0, 1024, dtype=jnp.int32)
    positions = (offset + jnp.arange(SEQ, dtype=jnp.int32)[None, :]).astype(jnp.int32)
    return {
        "x": x,
        "positions": positions,
        "norm_mix_g": 1.0 + nrm(ks[2], (L, D_MODEL), 0.02),
        "w_in": nrm(ks[3], (L, D_MODEL, IN_COLS), D_MODEL ** -0.5),
        "q_norm_g": 1.0 + nrm(ks[4], (L, ATTN_HEAD_DIM), 0.02),
        "k_norm_g": 1.0 + nrm(ks[5], (L, ATTN_HEAD_DIM), 0.02),
        "lambda_q1": nrm(ks[6], (L, ATTN_HEAD_DIM), 0.1),
        "lambda_k1": nrm(ks[7], (L, ATTN_HEAD_DIM), 0.1),
        "lambda_q2": nrm(ks[8], (L, ATTN_HEAD_DIM), 0.1),
        "lambda_k2": nrm(ks[9], (L, ATTN_HEAD_DIM), 0.1),
        "subln_g": 1.0 + nrm(ks[10], (L, ATTN_V_DIM), 0.02),
        "w_o_attn": nrm(ks[11], (L, ATTN_WIDTH, D_MODEL), ATTN_WIDTH ** -0.5),
        "ssm_lambda_re": -0.5 + nrm(ks[12], (L, SSM_GROUPS, SSM_STATE), 0.01),
        "ssm_lambda_im": math.pi * jnp.arange(SSM_STATE, dtype=f32)[None, None, :] + nrm(ks[13], (L, SSM_GROUPS, SSM_STATE), 0.01),
        "ssm_log_dt": jax.random.uniform(ks[14], (L, SSM_GROUPS), f32, math.log(DT_MIN), math.log(DT_MAX)),
        "ssm_b_re": nrm(ks[15], (L, SSM_GROUPS, SSM_STATE, SSM_GROUP), (2 * SSM_GROUP) ** -0.5),
        "ssm_b_im": nrm(ks[16], (L, SSM_GROUPS, SSM_STATE, SSM_GROUP), (2 * SSM_GROUP) ** -0.5),
        "ssm_c_re": nrm(ks[17], (L, SSM_GROUPS, SSM_GROUP, SSM_STATE), SSM_STATE ** -0.5),
        "ssm_c_im": nrm(ks[18], (L, SSM_GROUPS, SSM_GROUP, SSM_STATE), SSM_STATE ** -0.5),
        "ssm_d": nrm(ks[19], (L, SSM_WIDTH), 1.0),
        "w_glu": nrm(ks[20], (L, SSM_WIDTH, 2 * D_MODEL), SSM_WIDTH ** -0.5),
        "w_out": nrm(ks[21], (L, D_MODEL, D_MODEL), D_MODEL ** -0.5),
        "norm_ffn_g": 1.0 + nrm(ks[22], (L, D_MODEL), 0.02),
        "w_router_group": nrm(ks[23], (L, D_MODEL, N_EXPERT_GROUPS), D_MODEL ** -0.5),
        "b_router_group": nrm(ks[24], (L, N_EXPERT_GROUPS), 0.01),
        "w_router_expert": nrm(ks[25], (L, D_MODEL, N_EXPERT_GROUPS, EXPERTS_PER_GROUP), D_MODEL ** -0.5),
        "b_router_expert": nrm(ks[26], (L, N_EXPERT_GROUPS, EXPERTS_PER_GROUP), 0.01),
        "w_expert_gate": nrm(ks[27], (L, N_EXPERT_GROUPS, EXPERTS_PER_GROUP, D_MODEL, D_EXPERT), D_MODEL ** -0.5),
        "w_expert_up": nrm(ks[28], (L, N_EXPERT_GROUPS, EXPERTS_PER_GROUP, D_MODEL, D_EXPERT), D_MODEL ** -0.5),
        "w_expert_down": nrm(ks[29], (L, N_EXPERT_GROUPS, EXPERTS_PER_GROUP, D_EXPERT, D_MODEL), D_EXPERT ** -0.5),
    }


def reference(x, positions, norm_mix_g, w_in, q_norm_g, k_norm_g, lambda_q1, lambda_k1,
              lambda_q2, lambda_k2, subln_g, w_o_attn, ssm_lambda_re, ssm_lambda_im,
              ssm_log_dt, ssm_b_re, ssm_b_im, ssm_c_re, ssm_c_im, ssm_d, w_glu, w_out,
              norm_ffn_g, w_router_group, b_router_group, w_router_expert, b_router_expert,
              w_expert_gate, w_expert_up, w_expert_down):
    bsz, s, _ = x.shape
    splits = [Q_COLS, Q_COLS + K_COLS, Q_COLS + K_COLS + V_COLS, Q_COLS + K_COLS + V_COLS + U_COLS]
    for l in range(DEPTH):
        lam_init = 0.8 - 0.6 * math.exp(-0.3 * l)
        h = rms_norm(x, norm_mix_g[l])
        proj = h @ w_in[l]
        q, k, v, u, gates = jnp.split(proj, splits, axis=-1)

        q = q.reshape(bsz, s, N_ATTN_HEADS, 2, ATTN_HEAD_DIM)
        k = k.reshape(bsz, s, N_ATTN_HEADS, 2, ATTN_HEAD_DIM)
        q = rope_partial(rms_norm(q, q_norm_g[l]), positions).transpose(0, 2, 3, 1, 4)
        k = rope_partial(rms_norm(k, k_norm_g[l]), positions).transpose(0, 2, 3, 1, 4)
        v = v.reshape(bsz, s, N_ATTN_HEADS, ATTN_V_DIM).transpose(0, 2, 1, 3)
        lam = (jnp.exp(jnp.sum(lambda_q1[l].astype(jnp.float32) * lambda_k1[l].astype(jnp.float32)))
               - jnp.exp(jnp.sum(lambda_q2[l].astype(jnp.float32) * lambda_k2[l].astype(jnp.float32)))
               + lam_init)
        o = diff_attention(q, k, v, lam)
        o = rms_norm(o, subln_g[l]) * (1.0 - lam_init)
        o_a = o.transpose(0, 2, 1, 3).reshape(bsz, s, ATTN_WIDTH) @ w_o_attn[l]

        y = s5_grouped(u, ssm_lambda_re[l], ssm_lambda_im[l], ssm_log_dt[l], ssm_b_re[l],
                       ssm_b_im[l], ssm_c_re[l], ssm_c_im[l], ssm_d[l])
        z = jax.nn.gelu(y) @ w_glu[l]
        o_s = z[..., :D_MODEL] * jax.nn.sigmoid(z[..., D_MODEL:])

        g = jax.nn.sigmoid(gates.astype(jnp.float32)).reshape(bsz, s, N_BRANCH, D_MODEL)
        merged = g[..., 0, :] * o_a.astype(jnp.float32) + g[..., 1, :] * o_s.astype(jnp.float32)
        x = x + merged.astype(x.dtype) @ w_out[l]

        x = x + hier_moe(rms_norm(x, norm_ffn_g[l]), w_router_group[l], b_router_group[l],
                         w_router_expert[l], b_router_expert[l], w_expert_gate[l],
                         w_expert_up[l], w_expert_down[l])
    return x
```

```python
import functools
import math

import jax
import jax.numpy as jnp
from jax import lax
from jax.experimental import pallas as pl
from jax.experimental.pallas import tpu as pltpu

F32 = jnp.float32
BF16 = jnp.bfloat16

D_MODEL = 1024
N_HEADS = 4
HEAD_DIM = 64
V_DIM = 2 * HEAD_DIM
ATTN_WIDTH = N_HEADS * V_DIM
ROPE_THETA = 500000.0
ROPE_DIM = HEAD_DIM // 4
ROPE_HALF = ROPE_DIM // 2
SSM_WIDTH = D_MODEL // 2
SSM_GROUP = 16
SSM_GROUPS = SSM_WIDTH // SSM_GROUP
SSM_STATE = 64
N_STATE = SSM_GROUPS * SSM_STATE
N_EXPERT_GROUPS = 4
EXPERTS_PER_GROUP = 8
N_EXPERTS = N_EXPERT_GROUPS * EXPERTS_PER_GROUP
D_EXPERT = D_MODEL // 4
EPS = 1e-6
LAM_INIT = 0.8 - 0.6 * math.exp(-0.3 * 0)
QKVU_COLS = 4 * ATTN_WIDTH
LANES = 128
SUBLANES = 8
NEG_BIG = -1e30

VMEM_LIMIT = 48 * 1024 * 1024

TM_IN = 512
TQ = 512
SSM_CHUNK = 64
SCAN_LANES = 512
TM_MERGE = 256
TM_MOE = 512


def _const_spec(shape):
    return pl.BlockSpec(shape, lambda *_: (0,) * len(shape))


def _in_proj_kernel(x_ref, g_ref, w_ref, cos_ref, sin_ref, qg_ref, kg_ref,
                    q_ref, k_ref, v_ref, u_ref):
    x = x_ref[...]
    ms = jnp.mean(x * x, axis=-1, keepdims=True)
    h = (x * lax.rsqrt(ms + EPS) * g_ref[...]).astype(BF16)
    cos_t = cos_ref[...]
    sin_t = sin_ref[...]
    lane = lax.broadcasted_iota(jnp.int32, (x.shape[0], LANES), 1)
    first_comp = lane < HEAD_DIM
    low_half = (lane % HEAD_DIM) < ROPE_HALF

    def norm_rope(blk, gain, scale):
        sq = blk * blk
        s_all = jnp.sum(sq, axis=-1, keepdims=True)
        s_lo = jnp.sum(jnp.where(first_comp, sq, 0.0), axis=-1, keepdims=True)
        ssum = jnp.where(first_comp, s_lo, s_all - s_lo)
        nb = blk * lax.rsqrt(ssum * (1.0 / HEAD_DIM) + EPS) * gain
        up = pltpu.roll(nb, LANES - ROPE_HALF, axis=1)
        dn = pltpu.roll(nb, ROPE_HALF, axis=1)
        partner = jnp.where(low_half, up, dn)
        return ((nb * cos_t + partner * sin_t) * scale).astype(BF16)

    for j in range(N_HEADS):
        sl = slice(j * LANES, (j + 1) * LANES)
        qb = jnp.dot(h, w_ref[:, sl], preferred_element_type=F32)
        q_ref[:, sl] = norm_rope(qb, qg_ref[...], HEAD_DIM ** -0.5)
        kb = jnp.dot(h, w_ref[:, ATTN_WIDTH + j * LANES:ATTN_WIDTH + (j + 1) * LANES],
                     preferred_element_type=F32)
        k_ref[:, sl] = norm_rope(kb, kg_ref[...], 1.0)
    v_ref[...] = jnp.dot(h, w_ref[:, 2 * ATTN_WIDTH:3 * ATTN_WIDTH],
                         preferred_element_type=F32).astype(BF16)
    u_ref[...] = jnp.dot(h, w_ref[:, 3 * ATTN_WIDTH:4 * ATTN_WIDTH],
                         preferred_element_type=F32).astype(BF16)


def _in_proj(x2d, g, w, cos_t, sin_t, qg, kg):
    t = x2d.shape[0]
    tok = lambda i: (i, 0)
    out = jax.ShapeDtypeStruct((t, ATTN_WIDTH), BF16)
    return pl.pallas_call(
        _in_proj_kernel,
        grid=(t // TM_IN,),
        in_specs=[
            pl.BlockSpec((TM_IN, D_MODEL), tok),
            _const_spec((1, D_MODEL)),
            _const_spec((D_MODEL, QKVU_COLS)),
            pl.BlockSpec((TM_IN, LANES), tok),
            pl.BlockSpec((TM_IN, LANES), tok),
            _const_spec((1, LANES)),
            _const_spec((1, LANES)),
        ],
        out_specs=[pl.BlockSpec((TM_IN, ATTN_WIDTH), tok)] * 4,
        out_shape=[out] * 4,
        compiler_params=pltpu.CompilerParams(
            dimension_semantics=("arbitrary",), vmem_limit_bytes=VMEM_LIMIT),
        name="in_proj",
    )(x2d, g, w, cos_t, sin_t, qg, kg)


def _attn_kernel(qi_ref, kj_ref, lam_ref, q_ref, k_ref, v_ref, sg_ref, o_ref,
                 qs_ref, m_ref, l_ref, acc_ref):
    step = pl.program_id(2)
    qi = qi_ref[step]
    kj = kj_ref[step]

    @pl.when(kj == 0)
    def _init():
        q = q_ref[...]
        lane = lax.broadcasted_iota(jnp.int32, q.shape, 1)
        zero = jnp.zeros_like(q)
        qs_ref[0:TQ, :] = jnp.where(lane < HEAD_DIM, q, zero)
        qs_ref[TQ:2 * TQ, :] = jnp.where(lane < HEAD_DIM, zero, q)
        m_ref[...] = jnp.full(m_ref.shape, NEG_BIG, F32)
        l_ref[...] = jnp.zeros(l_ref.shape, F32)
        acc_ref[...] = jnp.zeros(acc_ref.shape, F32)

    def update(masked):
        s = lax.dot_general(qs_ref[...], k_ref[...], (((1,), (1,)), ((), ())),
                            preferred_element_type=F32)
        if masked:
            row = lax.broadcasted_iota(jnp.int32, s.shape, 0) % TQ
            col = lax.broadcasted_iota(jnp.int32, s.shape, 1)
            s = jnp.where(col <= row, s, NEG_BIG)
        m_old = m_ref[...]
        m_new = jnp.maximum(m_old, jnp.max(s, axis=-1, keepdims=True))
        alpha = jnp.exp(m_old - m_new)
        p = jnp.exp(s - m_new)
        l_ref[...] = alpha * l_ref[...] + jnp.sum(p, axis=-1, keepdims=True)
        acc_ref[...] = alpha * acc_ref[...] + jnp.dot(
            p.astype(BF16), v_ref[...], preferred_element_type=F32)
        m_ref[...] = m_new

    @pl.when(kj < qi)
    def _off_diag():
        update(False)

    @pl.when(kj == qi)
    def _diag():
        update(True)
        o = acc_ref[...] / l_ref[...]
        d = o[0:TQ, :] - lam_ref[0] * o[TQ:2 * TQ, :]
        ms = jnp.mean(d * d, axis=-1, keepdims=True)
        d = d * lax.rsqrt(ms + EPS) * sg_ref[...] * (1.0 - LAM_INIT)
        o_ref[...] = d.astype(BF16)


def _diff_attn(q, k, v, lam, subln_g, bsz, seq):
    nq = seq // TQ
    pairs = [(i, j) for i in range(nq) for j in range(i + 1)]
    qi = jnp.asarray([p[0] for p in pairs], jnp.int32)
    kj = jnp.asarray([p[1] for p in pairs], jnp.int32)
    grid_spec = pltpu.PrefetchScalarGridSpec(
        num_scalar_prefetch=2,
        grid=(bsz, N_HEADS, len(pairs)),
        in_specs=[
            pl.BlockSpec(memory_space=pltpu.SMEM),
            pl.BlockSpec((TQ, LANES), lambda b, h, s, qi, kj: (b * nq + qi[s], h)),
            pl.BlockSpec((TQ, LANES), lambda b, h, s, qi, kj: (b * nq + kj[s], h)),
            pl.BlockSpec((TQ, LANES), lambda b, h, s, qi, kj: (b * nq + kj[s], h)),
            pl.BlockSpec((1, LANES), lambda b, h, s, qi, kj: (0, 0)),
        ],
        out_specs=pl.BlockSpec((TQ, LANES), lambda b, h, s, qi, kj: (b * nq + qi[s], h)),
        scratch_shapes=[
            pltpu.VMEM((2 * TQ, LANES), BF16),
            pltpu.VMEM((2 * TQ, 1), F32),
            pltpu.VMEM((2 * TQ, 1), F32),
            pltpu.VMEM((2 * TQ, LANES), F32),
        ],
    )
    return pl.pallas_call(
        _attn_kernel,
        grid_spec=grid_spec,
        out_shape=jax.ShapeDtypeStruct((bsz * seq, ATTN_WIDTH), BF16),
        compiler_params=pltpu.CompilerParams(
            dimension_semantics=("arbitrary", "arbitrary", "arbitrary"),
            vmem_limit_bytes=VMEM_LIMIT),
        name="diff_attn",
    )(qi, kj, lam, q, k, v, subln_g)


def _gelu_tanh(x):
    c = math.sqrt(2.0 / math.pi)
    return 0.5 * x * (1.0 + jnp.tanh(c * (x + 0.044715 * (x * x * x))))


def _s5_kernel(u_ref, p_ref, pt_ref, b_ref, are_ref, aim_ref, c_ref, d_ref, o_ref,
               st_ref, state_ref):
    bsz, chunk, width = u_ref.shape
    rows = bsz * chunk

    @pl.when(pl.program_id(0) == 0)
    def _init():
        state_ref[...] = jnp.zeros(state_ref.shape, F32)

    u_bt = u_ref[...].reshape(rows, width)
    u_tm = jnp.dot(p_ref[...], u_bt, preferred_element_type=F32)
    st_ref[...] = jnp.dot(u_tm.astype(BF16), b_ref[...], preferred_element_type=F32)

    for ch in range(N_STATE // SCAN_LANES):
        re = slice(ch * SCAN_LANES, (ch + 1) * SCAN_LANES)
        im = slice(N_STATE + ch * SCAN_LANES, N_STATE + (ch + 1) * SCAN_LANES)
        a_re = are_ref[:, re]
        a_im = aim_ref[:, re]

        def body(t, carry, re=re, im=im, a_re=a_re, a_im=a_im):
            s_re, s_im = carry
            r0 = pl.multiple_of(t * SUBLANES, SUBLANES)
            n_re = a_re * s_re - a_im * s_im + st_ref[pl.ds(r0, SUBLANES), re]
            n_im = a_re * s_im + a_im * s_re + st_ref[pl.ds(r0, SUBLANES), im]
            st_ref[pl.ds(r0, SUBLANES), re] = n_re
            st_ref[pl.ds(r0, SUBLANES), im] = n_im
            return n_re, n_im

        s_re, s_im = lax.fori_loop(0, chunk, body, (state_ref[:, re], state_ref[:, im]),
                                   unroll=8)
        state_ref[:, re] = s_re
        state_ref[:, im] = s_im

    y = jnp.dot(st_ref[...].astype(BF16), c_ref[...], preferred_element_type=F32)
    y = y + d_ref[...] * u_tm
    g_tm = _gelu_tanh(y).astype(BF16)
    g_bt = jnp.dot(pt_ref[...], g_tm, preferred_element_type=F32)
    o_ref[...] = g_bt.astype(BF16).reshape(bsz, chunk, width)


def _s5_scan(u3, perm, perm_t, bmat, a_re, a_im, cmat, dvec):
    bsz, seq, width = u3.shape
    rows = bsz * SSM_CHUNK
    blk = pl.BlockSpec((bsz, SSM_CHUNK, width), lambda c: (0, c, 0))
    return pl.pallas_call(
        _s5_kernel,
        grid=(seq // SSM_CHUNK,),
        in_specs=[
            blk,
            _const_spec((rows, rows)),
            _const_spec((rows, rows)),
            _const_spec((width, 2 * N_STATE)),
            _const_spec((bsz, N_STATE)),
            _const_spec((bsz, N_STATE)),
            _const_spec((2 * N_STATE, width)),
            _const_spec((1, width)),
        ],
        out_specs=blk,
        out_shape=jax.ShapeDtypeStruct(u3.shape, BF16),
        scratch_shapes=[
            pltpu.VMEM((rows, 2 * N_STATE), F32),
            pltpu.VMEM((bsz, 2 * N_STATE), F32),
        ],
        compiler_params=pltpu.CompilerParams(
            dimension_semantics=("arbitrary",), vmem_limit_bytes=VMEM_LIMIT),
        name="s5_scan",
    )(u3, perm, perm_t, bmat, a_re, a_im, cmat, dvec)


def _merge_kernel(x_ref, oa_ref, gy_ref, g1_ref, wg_ref, woa_ref, wglu_ref, wout_ref,
                  g2_ref, wrh_ref, wrl_ref, br_ref, x2_ref, h2_ref, comb_ref):
    x = x_ref[...]
    ms = jnp.mean(x * x, axis=-1, keepdims=True)
    h = (x * lax.rsqrt(ms + EPS) * g1_ref[...]).astype(BF16)
    o_a = jnp.dot(oa_ref[...], woa_ref[...], preferred_element_type=F32)
    gy = gy_ref[...]
    z_lin = jnp.dot(gy, wglu_ref[:, 0:D_MODEL], preferred_element_type=F32)
    z_gate = jnp.dot(gy, wglu_ref[:, D_MODEL:2 * D_MODEL], preferred_element_type=F32)
    o_s = z_lin * jax.nn.sigmoid(z_gate)
    gate_a = jax.nn.sigmoid(jnp.dot(h, wg_ref[:, 0:D_MODEL], preferred_element_type=F32))
    merged = gate_a * o_a
    gate_s = jax.nn.sigmoid(
        jnp.dot(h, wg_ref[:, D_MODEL:2 * D_MODEL], preferred_element_type=F32))
    merged = merged + gate_s * o_s
    x2 = x + jnp.dot(merged.astype(BF16), wout_ref[...], preferred_element_type=F32)
    x2_ref[...] = x2

    ms2 = jnp.mean(x2 * x2, axis=-1, keepdims=True)
    h2 = x2 * lax.rsqrt(ms2 + EPS) * g2_ref[...]
    h2_hi = h2.astype(BF16)
    h2_lo = (h2 - h2_hi.astype(F32)).astype(BF16)
    h2_ref[...] = h2_hi

    logits = (jnp.dot(h2_hi, wrh_ref[...], preferred_element_type=F32)
              + jnp.dot(h2_lo, wrh_ref[...], preferred_element_type=F32)
              + jnp.dot(h2_hi, wrl_ref[...], preferred_element_type=F32)
              + br_ref[...])
    lane = lax.broadcasted_iota(jnp.int32, logits.shape, 1).astype(F32)
    is_grp = (lane >= N_EXPERTS) & (lane < N_EXPERTS + N_EXPERT_GROUPS)
    gl = jnp.where(is_grp, logits, NEG_BIG)
    gmax = jnp.max(gl, axis=-1, keepdims=True)
    gsum = jnp.sum(jnp.where(is_grp, jnp.exp(gl - gmax), 0.0), axis=-1, keepdims=True)
    p_grp = 1.0 / gsum
    big = float(4 * LANES)
    grp = jnp.min(jnp.where(is_grp & (gl == gmax), lane, big), axis=-1,
                  keepdims=True) - N_EXPERTS
    in_grp = (lane >= grp * EXPERTS_PER_GROUP) & (lane < (grp + 1) * EXPERTS_PER_GROUP)
    es = jnp.where(in_grp, logits, NEG_BIG)
    top1 = jnp.max(es, axis=-1, keepdims=True)
    i1 = jnp.min(jnp.where(in_grp & (es == top1), lane, big), axis=-1, keepdims=True)
    es2 = jnp.where(lane == i1, NEG_BIG, es)
    top2 = jnp.max(es2, axis=-1, keepdims=True)
    i2 = jnp.min(jnp.where(in_grp & (lane != i1) & (es2 == top2), lane, big), axis=-1,
                 keepdims=True)
    e2 = jnp.exp(top2 - top1)
    w1 = p_grp / (1.0 + e2)
    w2 = p_grp * e2 / (1.0 + e2)
    comb_ref[...] = jnp.where(lane == i1, w1, 0.0) + jnp.where(lane == i2, w2, 0.0)


def _merge(x2d, o_attn, gy, g1, w_gates, w_oa, w_glu, w_out, g2, wr_hi, wr_lo, b_r):
    t = x2d.shape[0]
    tok = lambda i: (i, 0)
    return pl.pallas_call(
        _merge_kernel,
        grid=(t // TM_MERGE,),
        in_specs=[
            pl.BlockSpec((TM_MERGE, D_MODEL), tok),
            pl.BlockSpec((TM_MERGE, ATTN_WIDTH), tok),
            pl.BlockSpec((TM_MERGE, SSM_WIDTH), tok),
            _const_spec((1, D_MODEL)),
            _const_spec((D_MODEL, 2 * D_MODEL)),
            _const_spec((ATTN_WIDTH, D_MODEL)),
            _const_spec((SSM_WIDTH, 2 * D_MODEL)),
            _const_spec((D_MODEL, D_MODEL)),
            _const_spec((1, D_MODEL)),
            _const_spec((D_MODEL, LANES)),
            _const_spec((D_MODEL, LANES)),
            _const_spec((1, LANES)),
        ],
        out_specs=[
            pl.BlockSpec((TM_MERGE, D_MODEL), tok),
            pl.BlockSpec((TM_MERGE, D_MODEL), tok),
            pl.BlockSpec((TM_MERGE, LANES), tok),
        ],
        out_shape=[
            jax.ShapeDtypeStruct((t, D_MODEL), F32),
            jax.ShapeDtypeStruct((t, D_MODEL), BF16),
            jax.ShapeDtypeStruct((t, LANES), F32),
        ],
        compiler_params=pltpu.CompilerParams(
            dimension_semantics=("arbitrary",), vmem_limit_bytes=VMEM_LIMIT),
        name="merge",
    )(x2d, o_attn, gy, g1, w_gates, w_oa, w_glu, w_out, g2, wr_hi, wr_lo, b_r)


def _moe_kernel(h_ref, comb_ref, x2_ref, wg_ref, wu_ref, wd_ref, o_ref, a_ref):
    grp = pl.program_id(1)

    @pl.when(grp == 0)
    def _init():
        o_ref[...] = x2_ref[...]

    t = h_ref[...]
    comb = comb_ref[...]
    lane = lax.broadcasted_iota(jnp.int32, comb.shape, 1)
    for e in range(EXPERTS_PER_GROUP):
        sl = slice(e * D_EXPERT, (e + 1) * D_EXPERT)
        hg = jnp.dot(t, wg_ref[:, sl], preferred_element_type=F32)
        hu = jnp.dot(t, wu_ref[:, sl], preferred_element_type=F32)
        cw = jnp.sum(jnp.where(lane == grp * EXPERTS_PER_GROUP + e, comb, 0.0),
                     axis=-1, keepdims=True)
        a_ref[:, sl] = (hg * jax.nn.sigmoid(hg) * hu * cw).astype(BF16)
    o_ref[...] += jnp.dot(a_ref[...], wd_ref[...], preferred_element_type=F32)


def _moe(h2, comb, x2, wg, wu, wd):
    t = h2.shape[0]
    ge = EXPERTS_PER_GROUP * D_EXPERT
    tok = lambda i, g: (i, 0)
    return pl.pallas_call(
        _moe_kernel,
        grid=(t // TM_MOE, N_EXPERT_GROUPS),
        in_specs=[
            pl.BlockSpec((TM_MOE, D_MODEL), tok),
            pl.BlockSpec((TM_MOE, LANES), tok),
            pl.BlockSpec((TM_MOE, D_MODEL), tok),
            pl.BlockSpec((None, D_MODEL, ge), lambda i, g: (g, 0, 0)),
            pl.BlockSpec((None, D_MODEL, ge), lambda i, g: (g, 0, 0)),
            pl.BlockSpec((None, ge, D_MODEL), lambda i, g: (g, 0, 0)),
        ],
        out_specs=pl.BlockSpec((TM_MOE, D_MODEL), tok),
        out_shape=jax.ShapeDtypeStruct((t, D_MODEL), F32),
        scratch_shapes=[pltpu.VMEM((TM_MOE, ge), BF16)],
        compiler_params=pltpu.CompilerParams(
            dimension_semantics=("arbitrary", "arbitrary"), vmem_limit_bytes=VMEM_LIMIT),
        name="moe",
    )(h2, comb, x2, wg, wu, wd)


def _rope_tables(positions):
    inv = ROPE_THETA ** (-jnp.arange(0, ROPE_DIM, 2, dtype=F32) / ROPE_DIM)
    ang = positions.astype(F32).reshape(-1, 1) * inv
    cos, sin = jnp.cos(ang), jnp.sin(ang)
    t = ang.shape[0]
    ones = jnp.ones((t, HEAD_DIM - ROPE_DIM), F32)
    zeros = jnp.zeros((t, HEAD_DIM - ROPE_DIM), F32)
    cos64 = jnp.concatenate([cos, cos, ones], axis=-1)
    sin64 = jnp.concatenate([-sin, sin, zeros], axis=-1)
    return jnp.tile(cos64, (1, 2)), jnp.tile(sin64, (1, 2))


def _s5_params(lam_re, lam_im, log_dt, b_re, b_im, c_re, c_im, bsz):
    dt = jnp.exp(log_dt)[:, None]
    mag = jnp.exp(lam_re * dt)
    lb_re = mag * jnp.cos(lam_im * dt)
    lb_im = mag * jnp.sin(lam_im * dt)
    den = lam_re * lam_re + lam_im * lam_im
    k_re = ((lb_re - 1.0) * lam_re + lb_im * lam_im) / den
    k_im = (lb_im * lam_re - (lb_re - 1.0) * lam_im) / den
    bb_re = k_re[..., None] * b_re - k_im[..., None] * b_im
    bb_im = k_re[..., None] * b_im + k_im[..., None] * b_re
    eye = jnp.eye(SSM_GROUPS, dtype=F32)
    blk_b = lambda m: jnp.einsum('gph,gk->ghkp', m, eye).reshape(SSM_WIDTH, N_STATE)
    blk_c = lambda m: jnp.einsum('ghp,gk->gpkh', m, eye).reshape(N_STATE, SSM_WIDTH)
    bmat = jnp.concatenate([blk_b(bb_re), blk_b(bb_im)], axis=1).astype(BF16)
    cmat = jnp.concatenate([blk_c(c_re), blk_c(-c_im)], axis=0).astype(BF16)
    a_re = jnp.broadcast_to(lb_re.reshape(1, N_STATE), (bsz, N_STATE))
    a_im = jnp.broadcast_to(lb_im.reshape(1, N_STATE), (bsz, N_STATE))
    return bmat, a_re, a_im, cmat


def _time_major_perm(bsz, chunk):
    r = jnp.arange(bsz * chunk)
    src = (r % bsz) * chunk + r // bsz
    perm = (src[:, None] == jnp.arange(bsz * chunk)[None, :]).astype(BF16)
    return perm, perm.T


def kernel(x, positions, norm_mix_g, w_in, q_norm_g, k_norm_g, lambda_q1, lambda_k1, lambda_q2, lambda_k2, subln_g, w_o_attn, ssm_lambda_re, ssm_lambda_im, ssm_log_dt, ssm_b_re, ssm_b_im, ssm_c_re, ssm_c_im, ssm_d, w_glu, w_out, norm_ffn_g, w_router_group, b_router_group, w_router_expert, b_router_expert, w_expert_gate, w_expert_up, w_expert_down):
    bsz, seq, _ = x.shape
    assert bsz == SUBLANES and seq % TQ == 0 and seq % SSM_CHUNK == 0
    assert norm_mix_g.shape[0] == 1
    t = bsz * seq
    x2d = x.reshape(t, D_MODEL)
    l = 0

    cos_t, sin_t = _rope_tables(positions)
    w_qkvu = w_in[l][:, :QKVU_COLS].astype(BF16)
    w_gates = w_in[l][:, QKVU_COLS:].astype(BF16)
    qg = jnp.tile(q_norm_g[l].reshape(1, HEAD_DIM), (1, 2))
    kg = jnp.tile(k_norm_g[l].reshape(1, HEAD_DIM), (1, 2))
    q, k, v, u = _in_proj(x2d, norm_mix_g[l].reshape(1, D_MODEL), w_qkvu, cos_t, sin_t, qg, kg)

    lam = (jnp.exp(jnp.sum(lambda_q1[l] * lambda_k1[l]))
           - jnp.exp(jnp.sum(lambda_q2[l] * lambda_k2[l])) + LAM_INIT).reshape(1)
    o_attn = _diff_attn(q, k, v, lam, subln_g[l].reshape(1, V_DIM), bsz, seq)

    bmat, a_re, a_im, cmat = _s5_params(
        ssm_lambda_re[l], ssm_lambda_im[l], ssm_log_dt[l], ssm_b_re[l], ssm_b_im[l],
        ssm_c_re[l], ssm_c_im[l], bsz)
    perm, perm_t = _time_major_perm(bsz, SSM_CHUNK)
    gy = _s5_scan(u.reshape(bsz, seq, SSM_WIDTH), perm, perm_t, bmat, a_re, a_im, cmat,
                  ssm_d[l].reshape(1, SSM_WIDTH)).reshape(t, SSM_WIDTH)

    w_r = jnp.concatenate(
        [w_router_expert[l].reshape(D_MODEL, N_EXPERTS), w_router_group[l],
         jnp.zeros((D_MODEL, LANES - N_EXPERTS - N_EXPERT_GROUPS), F32)], axis=1)
    b_r = jnp.concatenate(
        [b_router_expert[l].reshape(N_EXPERTS), b_router_group[l],
         jnp.zeros((LANES - N_EXPERTS - N_EXPERT_GROUPS,), F32)]).reshape(1, LANES)
    wr_hi = w_r.astype(BF16)
    wr_lo = (w_r - wr_hi.astype(F32)).astype(BF16)
    x2, h2, comb = _merge(
        x2d, o_attn, gy, norm_mix_g[l].reshape(1, D_MODEL), w_gates,
        w_o_attn[l].astype(BF16), w_glu[l].astype(BF16), w_out[l].astype(BF16),
        norm_ffn_g[l].reshape(1, D_MODEL), wr_hi, wr_lo, b_r)

    ge = EXPERTS_PER_GROUP * D_EXPERT
    wg = w_expert_gate[l].astype(BF16).transpose(0, 2, 1, 3).reshape(N_EXPERT_GROUPS, D_MODEL, ge)
    wu = w_expert_up[l].astype(BF16).transpose(0, 2, 1, 3).reshape(N_EXPERT_GROUPS, D_MODEL, ge)
    wd = w_expert_down[l].astype(BF16).reshape(N_EXPERT_GROUPS, ge, D_MODEL)
    out = _moe(h2, comb, x2, wg, wu, wd)
    return out.reshape(bsz, seq, D_MODEL)
```

```python
import functools
import math

import jax
import jax.numpy as jnp
from jax import lax
from jax.experimental import pallas as pl
from jax.experimental.pallas import tpu as pltpu

F32 = jnp.float32
BF16 = jnp.bfloat16

D_MODEL = 1024
N_HEADS = 4
HEAD_DIM = 64
V_DIM = 2 * HEAD_DIM
ATTN_WIDTH = N_HEADS * V_DIM
ROPE_THETA = 500000.0
ROPE_DIM = HEAD_DIM // 4
ROPE_HALF = ROPE_DIM // 2
SSM_WIDTH = D_MODEL // 2
SSM_GROUP = 16
SSM_GROUPS = SSM_WIDTH // SSM_GROUP
SSM_STATE = 64
N_STATE = SSM_GROUPS * SSM_STATE
N_EXPERT_GROUPS = 4
EXPERTS_PER_GROUP = 8
N_EXPERTS = N_EXPERT_GROUPS * EXPERTS_PER_GROUP
D_EXPERT = D_MODEL // 4
EPS = 1e-6
LAM_INIT = 0.8 - 0.6 * math.exp(-0.3 * 0)
QKVU_COLS = 4 * ATTN_WIDTH
LANES = 128
SUBLANES = 8
NEG_BIG = -1e30
LOG2_E = math.log2(math.e)

VMEM_LIMIT = 48 * 1024 * 1024

TM_IN = 512
TQ = 512
SSM_CHUNK = 64
SCAN_LANES = 512
TM_MERGE = 256
TM_MOE = 512


def _const_spec(shape):
    return pl.BlockSpec(shape, lambda *_: (0,) * len(shape))


def _in_proj_kernel(x_ref, g_ref, w_ref, cos_ref, sin_ref, qg_ref, kg_ref,
                    q_ref, k_ref, v_ref, u_ref):
    x = x_ref[...]
    ms = jnp.mean(x * x, axis=-1, keepdims=True)
    h = (x * lax.rsqrt(ms + EPS) * g_ref[...]).astype(BF16)
    cos_t = cos_ref[...]
    sin_t = sin_ref[...]
    lane = lax.broadcasted_iota(jnp.int32, (x.shape[0], LANES), 1)
    first_comp = lane < HEAD_DIM
    low_half = (lane % HEAD_DIM) < ROPE_HALF

    def norm_rope(blk, gain, scale):
        sq = blk * blk
        s_all = jnp.sum(sq, axis=-1, keepdims=True)
        s_lo = jnp.sum(jnp.where(first_comp, sq, 0.0), axis=-1, keepdims=True)
        ssum = jnp.where(first_comp, s_lo, s_all - s_lo)
        nb = blk * lax.rsqrt(ssum * (1.0 / HEAD_DIM) + EPS) * gain
        up = pltpu.roll(nb, LANES - ROPE_HALF, axis=1)
        dn = pltpu.roll(nb, ROPE_HALF, axis=1)
        partner = jnp.where(low_half, up, dn)
        return ((nb * cos_t + partner * sin_t) * scale).astype(BF16)

    for j in range(N_HEADS):
        sl = slice(j * LANES, (j + 1) * LANES)
        qb = jnp.dot(h, w_ref[:, sl], preferred_element_type=F32)
        q_ref[:, sl] = norm_rope(qb, qg_ref[...], LOG2_E * HEAD_DIM ** -0.5)
        kb = jnp.dot(h, w_ref[:, ATTN_WIDTH + j * LANES:ATTN_WIDTH + (j + 1) * LANES],
                     preferred_element_type=F32)
        k_ref[:, sl] = norm_rope(kb, kg_ref[...], 1.0)
    v_ref[...] = jnp.dot(h, w_ref[:, 2 * ATTN_WIDTH:3 * ATTN_WIDTH],
                         preferred_element_type=F32).astype(BF16)
    u_ref[...] = jnp.dot(h, w_ref[:, 3 * ATTN_WIDTH:4 * ATTN_WIDTH],
                         preferred_element_type=F32).astype(BF16)


def _in_proj(x2d, g, w, cos_t, sin_t, qg, kg):
    t = x2d.shape[0]
    tok = lambda i: (i, 0)
    out = jax.ShapeDtypeStruct((t, ATTN_WIDTH), BF16)
    return pl.pallas_call(
        _in_proj_kernel,
        grid=(t // TM_IN,),
        in_specs=[
            pl.BlockSpec((TM_IN, D_MODEL), tok),
            _const_spec((1, D_MODEL)),
            _const_spec((D_MODEL, QKVU_COLS)),
            pl.BlockSpec((TM_IN, LANES), tok),
            pl.BlockSpec((TM_IN, LANES), tok),
            _const_spec((1, LANES)),
            _const_spec((1, LANES)),
        ],
        out_specs=[pl.BlockSpec((TM_IN, ATTN_WIDTH), tok)] * 4,
        out_shape=[out] * 4,
        compiler_params=pltpu.CompilerParams(
            dimension_semantics=("arbitrary",), vmem_limit_bytes=VMEM_LIMIT),
        name="in_proj",
    )(x2d, g, w, cos_t, sin_t, qg, kg)


def _attn_kernel(qi_ref, kj_ref, lam_ref, q_ref, k_ref, v_ref, sg_ref, o_ref,
                 qs_ref, m_ref, l_ref, acc_ref):
    step = pl.program_id(2)
    qi = qi_ref[step]
    kj = kj_ref[step]

    @pl.when(kj == 0)
    def _init():
        q = q_ref[...]
        lane = lax.broadcasted_iota(jnp.int32, q.shape, 1)
        zero = jnp.zeros_like(q)
        qs_ref[0:TQ, :] = jnp.where(lane < HEAD_DIM, q, zero)
        qs_ref[TQ:2 * TQ, :] = jnp.where(lane < HEAD_DIM, zero, q)
        m_ref[...] = jnp.full(m_ref.shape, NEG_BIG, F32)
        l_ref[...] = jnp.zeros(l_ref.shape, F32)
        acc_ref[...] = jnp.zeros(acc_ref.shape, F32)

    def update(masked):
        s = lax.dot_general(qs_ref[...], k_ref[...], (((1,), (1,)), ((), ())),
                            preferred_element_type=F32)
        if masked:
            row = lax.broadcasted_iota(jnp.int32, s.shape, 0) % TQ
            col = lax.broadcasted_iota(jnp.int32, s.shape, 1)
            s = jnp.where(col <= row, s, NEG_BIG)
        m_old = m_ref[...]
        m_new = jnp.maximum(m_old, jnp.max(s, axis=-1, keepdims=True))
        alpha = jnp.exp2(m_old - m_new)
        p = jnp.exp2(s - pltpu.repeat(m_new, TQ // LANES, axis=1))
        psum = p[:, 0:LANES]
        for c in range(1, TQ // LANES):
            psum = psum + p[:, c * LANES:(c + 1) * LANES]
        l_ref[...] = alpha * l_ref[...] + psum
        acc_ref[...] = alpha * acc_ref[...] + jnp.dot(
            p.astype(BF16), v_ref[...], preferred_element_type=F32)
        m_ref[...] = m_new

    @pl.when(kj < qi)
    def _off_diag():
        update(False)

    @pl.when(kj == qi)
    def _diag():
        update(True)
        o = acc_ref[...] / jnp.sum(l_ref[...], axis=-1, keepdims=True)
        d = o[0:TQ, :] - lam_ref[0] * o[TQ:2 * TQ, :]
        ms = jnp.mean(d * d, axis=-1, keepdims=True)
        d = d * lax.rsqrt(ms + EPS) * sg_ref[...] * (1.0 - LAM_INIT)
        o_ref[...] = d.astype(BF16)


def _diff_attn(q, k, v, lam, subln_g, bsz, seq):
    nq = seq // TQ
    pairs = [(i, j) for i in range(nq) for j in range(i + 1)]
    qi = jnp.asarray([p[0] for p in pairs], jnp.int32)
    kj = jnp.asarray([p[1] for p in pairs], jnp.int32)
    grid_spec = pltpu.PrefetchScalarGridSpec(
        num_scalar_prefetch=2,
        grid=(bsz, N_HEADS, len(pairs)),
        in_specs=[
            pl.BlockSpec(memory_space=pltpu.SMEM),
            pl.BlockSpec((TQ, LANES), lambda b, h, s, qi, kj: (b * nq + qi[s], h)),
            pl.BlockSpec((TQ, LANES), lambda b, h, s, qi, kj: (b * nq + kj[s], h)),
            pl.BlockSpec((TQ, LANES), lambda b, h, s, qi, kj: (b * nq + kj[s], h)),
            pl.BlockSpec((1, LANES), lambda b, h, s, qi, kj: (0, 0)),
        ],
        out_specs=pl.BlockSpec((TQ, LANES), lambda b, h, s, qi, kj: (b * nq + qi[s], h)),
        scratch_shapes=[
            pltpu.VMEM((2 * TQ, LANES), BF16),
            pltpu.VMEM((2 * TQ, LANES), F32),
            pltpu.VMEM((2 * TQ, LANES), F32),
            pltpu.VMEM((2 * TQ, LANES), F32),
        ],
    )
    return pl.pallas_call(
        _attn_kernel,
        grid_spec=grid_spec,
        out_shape=jax.ShapeDtypeStruct((bsz * seq, ATTN_WIDTH), BF16),
        compiler_params=pltpu.CompilerParams(
            dimension_semantics=("arbitrary", "arbitrary", "arbitrary"),
            vmem_limit_bytes=VMEM_LIMIT),
        name="diff_attn",
    )(qi, kj, lam, q, k, v, subln_g)


def _gelu_tanh(x):
    c = math.sqrt(2.0 / math.pi)
    return 0.5 * x * (1.0 + jnp.tanh(c * (x + 0.044715 * (x * x * x))))


def _s5_kernel(u_ref, p_ref, pt_ref, b_ref, are_ref, aim_ref, c_ref, d_ref, o_ref,
               st_ref, state_ref):
    bsz, chunk, width = u_ref.shape
    rows = bsz * chunk

    @pl.when(pl.program_id(0) == 0)
    def _init():
        state_ref[...] = jnp.zeros(state_ref.shape, F32)

    u_bt = u_ref[...].reshape(rows, width)
    u_tm = jnp.dot(p_ref[...], u_bt, preferred_element_type=F32)
    st_ref[...] = jnp.dot(u_tm.astype(BF16), b_ref[...], preferred_element_type=F32)

    for ch in range(N_STATE // SCAN_LANES):
        re = slice(ch * SCAN_LANES, (ch + 1) * SCAN_LANES)
        im = slice(N_STATE + ch * SCAN_LANES, N_STATE + (ch + 1) * SCAN_LANES)
        a_re = are_ref[:, re]
        a_im = aim_ref[:, re]

        def body(t, carry, re=re, im=im, a_re=a_re, a_im=a_im):
            s_re, s_im = carry
            r0 = pl.multiple_of(t * SUBLANES, SUBLANES)
            n_re = a_re * s_re - a_im * s_im + st_ref[pl.ds(r0, SUBLANES), re]
            n_im = a_re * s_im + a_im * s_re + st_ref[pl.ds(r0, SUBLANES), im]
            st_ref[pl.ds(r0, SUBLANES), re] = n_re
            st_ref[pl.ds(r0, SUBLANES), im] = n_im
            return n_re, n_im

        s_re, s_im = lax.fori_loop(0, chunk, body, (state_ref[:, re], state_ref[:, im]),
                                   unroll=8)
        state_ref[:, re] = s_re
        state_ref[:, im] = s_im

    y = jnp.dot(st_ref[...].astype(BF16), c_ref[...], preferred_element_type=F32)
    y = y + d_ref[...] * u_tm
    g_tm = _gelu_tanh(y).astype(BF16)
    g_bt = jnp.dot(pt_ref[...], g_tm, preferred_element_type=F32)
    o_ref[...] = g_bt.astype(BF16).reshape(bsz, chunk, width)


def _s5_scan(u3, perm, perm_t, bmat, a_re, a_im, cmat, dvec):
    bsz, seq, width = u3.shape
    rows = bsz * SSM_CHUNK
    blk = pl.BlockSpec((bsz, SSM_CHUNK, width), lambda c: (0, c, 0))
    return pl.pallas_call(
        _s5_kernel,
        grid=(seq // SSM_CHUNK,),
        in_specs=[
            blk,
            _const_spec((rows, rows)),
            _const_spec((rows, rows)),
            _const_spec((width, 2 * N_STATE)),
            _const_spec((bsz, N_STATE)),
            _const_spec((bsz, N_STATE)),
            _const_spec((2 * N_STATE, width)),
            _const_spec((1, width)),
        ],
        out_specs=blk,
        out_shape=jax.ShapeDtypeStruct(u3.shape, BF16),
        scratch_shapes=[
            pltpu.VMEM((rows, 2 * N_STATE), F32),
            pltpu.VMEM((bsz, 2 * N_STATE), F32),
        ],
        compiler_params=pltpu.CompilerParams(
            dimension_semantics=("arbitrary",), vmem_limit_bytes=VMEM_LIMIT),
        name="s5_scan",
    )(u3, perm, perm_t, bmat, a_re, a_im, cmat, dvec)


def _merge_kernel(x_ref, oa_ref, gy_ref, g1_ref, wg_ref, woa_ref, wglu_ref, wout_ref,
                  g2_ref, wrh_ref, wrl_ref, br_ref, x2_ref, h2_ref, comb_ref):
    x = x_ref[...]
    ms = jnp.mean(x * x, axis=-1, keepdims=True)
    h = (x * lax.rsqrt(ms + EPS) * g1_ref[...]).astype(BF16)
    o_a = jnp.dot(oa_ref[...], woa_ref[...], preferred_element_type=F32)
    gy = gy_ref[...]
    z_lin = jnp.dot(gy, wglu_ref[:, 0:D_MODEL], preferred_element_type=F32)
    z_gate = jnp.dot(gy, wglu_ref[:, D_MODEL:2 * D_MODEL], preferred_element_type=F32)
    o_s = z_lin * jax.nn.sigmoid(z_gate)
    gate_a = jax.nn.sigmoid(jnp.dot(h, wg_ref[:, 0:D_MODEL], preferred_element_type=F32))
    merged = gate_a * o_a
    gate_s = jax.nn.sigmoid(
        jnp.dot(h, wg_ref[:, D_MODEL:2 * D_MODEL], preferred_element_type=F32))
    merged = merged + gate_s * o_s
    x2 = x + jnp.dot(merged.astype(BF16), wout_ref[...], preferred_element_type=F32)
    x2_ref[...] = x2

    ms2 = jnp.mean(x2 * x2, axis=-1, keepdims=True)
    h2 = x2 * lax.rsqrt(ms2 + EPS) * g2_ref[...]
    h2_hi = h2.astype(BF16)
    h2_lo = (h2 - h2_hi.astype(F32)).astype(BF16)
    h2_ref[...] = h2_hi

    logits = (jnp.dot(h2_hi, wrh_ref[...], preferred_element_type=F32)
              + jnp.dot(h2_lo, wrh_ref[...], preferred_element_type=F32)
              + jnp.dot(h2_hi, wrl_ref[...], preferred_element_type=F32)
              + br_ref[...])
    lane = lax.broadcasted_iota(jnp.int32, logits.shape, 1).astype(F32)
    is_grp = (lane >= N_EXPERTS) & (lane < N_EXPERTS + N_EXPERT_GROUPS)
    gl = jnp.where(is_grp, logits, NEG_BIG)
    gmax = jnp.max(gl, axis=-1, keepdims=True)
    gsum = jnp.sum(jnp.where(is_grp, jnp.exp(gl - gmax), 0.0), axis=-1, keepdims=True)
    p_grp = 1.0 / gsum
    big = float(4 * LANES)
    grp = jnp.min(jnp.where(is_grp & (gl == gmax), lane, big), axis=-1,
                  keepdims=True) - N_EXPERTS
    in_grp = (lane >= grp * EXPERTS_PER_GROUP) & (lane < (grp + 1) * EXPERTS_PER_GROUP)
    es = jnp.where(in_grp, logits, NEG_BIG)
    top1 = jnp.max(es, axis=-1, keepdims=True)
    i1 = jnp.min(jnp.where(in_grp & (es == top1), lane, big), axis=-1, keepdims=True)
    es2 = jnp.where(lane == i1, NEG_BIG, es)
    top2 = jnp.max(es2, axis=-1, keepdims=True)
    i2 = jnp.min(jnp.where(in_grp & (lane != i1) & (es2 == top2), lane, big), axis=-1,
                 keepdims=True)
    e2 = jnp.exp(top2 - top1)
    w1 = p_grp / (1.0 + e2)
    w2 = p_grp * e2 / (1.0 + e2)
    comb_ref[...] = jnp.where(lane == i1, w1, 0.0) + jnp.where(lane == i2, w2, 0.0)


def _merge(x2d, o_attn, gy, g1, w_gates, w_oa, w_glu, w_out, g2, wr_hi, wr_lo, b_r):
    t = x2d.shape[0]
    tok = lambda i: (i, 0)
    return pl.pallas_call(
        _merge_kernel,
        grid=(t // TM_MERGE,),
        in_specs=[
            pl.BlockSpec((TM_MERGE, D_MODEL), tok),
            pl.BlockSpec((TM_MERGE, ATTN_WIDTH), tok),
            pl.BlockSpec((TM_MERGE, SSM_WIDTH), tok),
            _const_spec((1, D_MODEL)),
            _const_spec((D_MODEL, 2 * D_MODEL)),
            _const_spec((ATTN_WIDTH, D_MODEL)),
            _const_spec((SSM_WIDTH, 2 * D_MODEL)),
            _const_spec((D_MODEL, D_MODEL)),
            _const_spec((1, D_MODEL)),
            _const_spec((D_MODEL, LANES)),
            _const_spec((D_MODEL, LANES)),
            _const_spec((1, LANES)),
        ],
        out_specs=[
            pl.BlockSpec((TM_MERGE, D_MODEL), tok),
            pl.BlockSpec((TM_MERGE, D_MODEL), tok),
            pl.BlockSpec((TM_MERGE, LANES), tok),
        ],
        out_shape=[
            jax.ShapeDtypeStruct((t, D_MODEL), F32),
            jax.ShapeDtypeStruct((t, D_MODEL), BF16),
            jax.ShapeDtypeStruct((t, LANES), F32),
        ],
        compiler_params=pltpu.CompilerParams(
            dimension_semantics=("arbitrary",), vmem_limit_bytes=VMEM_LIMIT),
        name="merge",
    )(x2d, o_attn, gy, g1, w_gates, w_oa, w_glu, w_out, g2, wr_hi, wr_lo, b_r)


def _moe_kernel(h_ref, comb_ref, x2_ref, wg_ref, wu_ref, wd_ref, o_ref, a_ref):
    grp = pl.program_id(1)

    @pl.when(grp == 0)
    def _init():
        o_ref[...] = x2_ref[...]

    t = h_ref[...]
    comb = comb_ref[...]
    lane = lax.broadcasted_iota(jnp.int32, comb.shape, 1)
    for e in range(EXPERTS_PER_GROUP):
        sl = slice(e * D_EXPERT, (e + 1) * D_EXPERT)
        hg = jnp.dot(t, wg_ref[:, sl], preferred_element_type=F32)
        hu = jnp.dot(t, wu_ref[:, sl], preferred_element_type=F32)
        cw = jnp.sum(jnp.where(lane == grp * EXPERTS_PER_GROUP + e, comb, 0.0),
                     axis=-1, keepdims=True)
        a_ref[:, sl] = (hg * jax.nn.sigmoid(hg) * hu * cw).astype(BF16)
    o_ref[...] += jnp.dot(a_ref[...], wd_ref[...], preferred_element_type=F32)


def _moe(h2, comb, x2, wg, wu, wd):
    t = h2.shape[0]
    ge = EXPERTS_PER_GROUP * D_EXPERT
    tok = lambda i, g: (i, 0)
    return pl.pallas_call(
        _moe_kernel,
        grid=(t // TM_MOE, N_EXPERT_GROUPS),
        in_specs=[
            pl.BlockSpec((TM_MOE, D_MODEL), tok),
            pl.BlockSpec((TM_MOE, LANES), tok),
            pl.BlockSpec((TM_MOE, D_MODEL), tok),
            pl.BlockSpec((None, D_MODEL, ge), lambda i, g: (g, 0, 0)),
            pl.BlockSpec((None, D_MODEL, ge), lambda i, g: (g, 0, 0)),
            pl.BlockSpec((None, ge, D_MODEL), lambda i, g: (g, 0, 0)),
        ],
        out_specs=pl.BlockSpec((TM_MOE, D_MODEL), tok),
        out_shape=jax.ShapeDtypeStruct((t, D_MODEL), F32),
        scratch_shapes=[pltpu.VMEM((TM_MOE, ge), BF16)],
        compiler_params=pltpu.CompilerParams(
            dimension_semantics=("arbitrary", "arbitrary"), vmem_limit_bytes=VMEM_LIMIT),
        name="moe",
    )(h2, comb, x2, wg, wu, wd)


def _rope_tables(positions):
    inv = ROPE_THETA ** (-jnp.arange(0, ROPE_DIM, 2, dtype=F32) / ROPE_DIM)
    ang = positions.astype(F32).reshape(-1, 1) * inv
    cos, sin = jnp.cos(ang), jnp.sin(ang)
    t = ang.shape[0]
    ones = jnp.ones((t, HEAD_DIM - ROPE_DIM), F32)
    zeros = jnp.zeros((t, HEAD_DIM - ROPE_DIM), F32)
    cos64 = jnp.concatenate([cos, cos, ones], axis=-1)
    sin64 = jnp.concatenate([-sin, sin, zeros], axis=-1)
    return jnp.tile(cos64, (1, 2)), jnp.tile(sin64, (1, 2))


def _s5_params(lam_re, lam_im, log_dt, b_re, b_im, c_re, c_im, bsz):
    dt = jnp.exp(log_dt)[:, None]
    mag = jnp.exp(lam_re * dt)
    lb_re = mag * jnp.cos(lam_im * dt)
    lb_im = mag * jnp.sin(lam_im * dt)
    den = lam_re * lam_re + lam_im * lam_im
    k_re = ((lb_re - 1.0) * lam_re + lb_im * lam_im) / den
    k_im = (lb_im * lam_re - (lb_re - 1.0) * lam_im) / den
    bb_re = k_re[..., None] * b_re - k_im[..., None] * b_im
    bb_im = k_re[..., None] * b_im + k_im[..., None] * b_re
    eye = jnp.eye(SSM_GROUPS, dtype=F32)
    blk_b = lambda m: jnp.einsum('gph,gk->ghkp', m, eye).reshape(SSM_WIDTH, N_STATE)
    blk_c = lambda m: jnp.einsum('ghp,gk->gpkh', m, eye).reshape(N_STATE, SSM_WIDTH)
    bmat = jnp.concatenate([blk_b(bb_re), blk_b(bb_im)], axis=1).astype(BF16)
    cmat = jnp.concatenate([blk_c(c_re), blk_c(-c_im)], axis=0).astype(BF16)
    a_re = jnp.broadcast_to(lb_re.reshape(1, N_STATE), (bsz, N_STATE))
    a_im = jnp.broadcast_to(lb_im.reshape(1, N_STATE), (bsz, N_STATE))
    return bmat, a_re, a_im, cmat


def _time_major_perm(bsz, chunk):
    r = jnp.arange(bsz * chunk)
    src = (r % bsz) * chunk + r // bsz
    perm = (src[:, None] == jnp.arange(bsz * chunk)[None, :]).astype(BF16)
    return perm, perm.T


def kernel(x, positions, norm_mix_g, w_in, q_norm_g, k_norm_g, lambda_q1, lambda_k1, lambda_q2, lambda_k2, subln_g, w_o_attn, ssm_lambda_re, ssm_lambda_im, ssm_log_dt, ssm_b_re, ssm_b_im, ssm_c_re, ssm_c_im, ssm_d, w_glu, w_out, norm_ffn_g, w_router_group, b_router_group, w_router_expert, b_router_expert, w_expert_gate, w_expert_up, w_expert_down):
    bsz, seq, _ = x.shape
    assert bsz == SUBLANES and seq % TQ == 0 and seq % SSM_CHUNK == 0
    assert norm_mix_g.shape[0] == 1
    t = bsz * seq
    x2d = x.reshape(t, D_MODEL)
    l = 0

    cos_t, sin_t = _rope_tables(positions)
    w_qkvu = w_in[l][:, :QKVU_COLS].astype(BF16)
    w_gates = w_in[l][:, QKVU_COLS:].astype(BF16)
    qg = jnp.tile(q_norm_g[l].reshape(1, HEAD_DIM), (1, 2))
    kg = jnp.tile(k_norm_g[l].reshape(1, HEAD_DIM), (1, 2))
    q, k, v, u = _in_proj(x2d, norm_mix_g[l].reshape(1, D_MODEL), w_qkvu, cos_t, sin_t, qg, kg)

    lam = (jnp.exp(jnp.sum(lambda_q1[l] * lambda_k1[l]))
           - jnp.exp(jnp.sum(lambda_q2[l] * lambda_k2[l])) + LAM_INIT).reshape(1)
    o_attn = _diff_attn(q, k, v, lam, subln_g[l].reshape(1, V_DIM), bsz, seq)

    bmat, a_re, a_im, cmat = _s5_params(
        ssm_lambda_re[l], ssm_lambda_im[l], ssm_log_dt[l], ssm_b_re[l], ssm_b_im[l],
        ssm_c_re[l], ssm_c_im[l], bsz)
    perm, perm_t = _time_major_perm(bsz, SSM_CHUNK)
    gy = _s5_scan(u.reshape(bsz, seq, SSM_WIDTH), perm, perm_t, bmat, a_re, a_im, cmat,
                  ssm_d[l].reshape(1, SSM_WIDTH)).reshape(t, SSM_WIDTH)

    w_r = jnp.concatenate(
        [w_router_expert[l].reshape(D_MODEL, N_EXPERTS), w_router_group[l],
         jnp.zeros((D_MODEL, LANES - N_EXPERTS - N_EXPERT_GROUPS), F32)], axis=1)
    b_r = jnp.concatenate(
        [b_router_expert[l].reshape(N_EXPERTS), b_router_group[l],
         jnp.zeros((LANES - N_EXPERTS - N_EXPERT_GROUPS,), F32)]).reshape(1, LANES)
    wr_hi = w_r.astype(BF16)
    wr_lo = (w_r - wr_hi.astype(F32)).astype(BF16)
    x2, h2, comb = _merge(
        x2d, o_attn, gy, norm_mix_g[l].reshape(1, D_MODEL), w_gates,
        w_o_attn[l].astype(BF16), w_glu[l].astype(BF16), w_out[l].astype(BF16),
        norm_ffn_g[l].reshape(1, D_MODEL), wr_hi, wr_lo, b_r)

    ge = EXPERTS_PER_GROUP * D_EXPERT
    wg = w_expert_gate[l].astype(BF16).transpose(0, 2, 1, 3).reshape(N_EXPERT_GROUPS, D_MODEL, ge)
    wu = w_expert_up[l].astype(BF16).transpose(0, 2, 1, 3).reshape(N_EXPERT_GROUPS, D_MODEL, ge)
    wd = w_expert_down[l].astype(BF16).reshape(N_EXPERT_GROUPS, ge, D_MODEL)
    out = _moe(h2, comb, x2, wg, wu, wd)
    return out.reshape(bsz, seq, D_MODEL)
```

```python
import functools
import math

import jax
import jax.numpy as jnp
from jax import lax
from jax.experimental import pallas as pl
from jax.experimental.pallas import tpu as pltpu

F32 = jnp.float32
BF16 = jnp.bfloat16

D_MODEL = 1024
N_HEADS = 4
HEAD_DIM = 64
V_DIM = 2 * HEAD_DIM
ATTN_WIDTH = N_HEADS * V_DIM
ROPE_THETA = 500000.0
ROPE_DIM = HEAD_DIM // 4
ROPE_HALF = ROPE_DIM // 2
SSM_WIDTH = D_MODEL // 2
SSM_GROUP = 16
SSM_GROUPS = SSM_WIDTH // SSM_GROUP
SSM_STATE = 64
N_STATE = SSM_GROUPS * SSM_STATE
N_EXPERT_GROUPS = 4
EXPERTS_PER_GROUP = 8
N_EXPERTS = N_EXPERT_GROUPS * EXPERTS_PER_GROUP
D_EXPERT = D_MODEL // 4
EPS = 1e-6
LAM_INIT = 0.8 - 0.6 * math.exp(-0.3 * 0)
QKVU_COLS = 4 * ATTN_WIDTH
LANES = 128
SUBLANES = 8
NEG_BIG = -1e30
LOG2_E = math.log2(math.e)

VMEM_LIMIT = 48 * 1024 * 1024

TM_IN = 512
TQ = 512
SSM_CHUNK = 64
SCAN_LANES = 512
TM_MERGE = 256
TS_MOE = 512
ROW_ALIGN = 16
ROUTE_GROUP_LANE = 8


def _const_spec(shape):
    return pl.BlockSpec(shape, lambda *_: (0,) * len(shape))


def _in_proj_kernel(x_ref, g_ref, w_ref, cos_ref, sin_ref, qg_ref, kg_ref,
                    q_ref, k_ref, v_ref, u_ref):
    x = x_ref[...]
    ms = jnp.mean(x * x, axis=-1, keepdims=True)
    h = (x * lax.rsqrt(ms + EPS) * g_ref[...]).astype(BF16)
    cos_t = cos_ref[...]
    sin_t = sin_ref[...]
    lane = lax.broadcasted_iota(jnp.int32, (x.shape[0], LANES), 1)
    first_comp = lane < HEAD_DIM
    low_half = (lane % HEAD_DIM) < ROPE_HALF

    def norm_rope(blk, gain, scale):
        sq = blk * blk
        s_all = jnp.sum(sq, axis=-1, keepdims=True)
        s_lo = jnp.sum(jnp.where(first_comp, sq, 0.0), axis=-1, keepdims=True)
        ssum = jnp.where(first_comp, s_lo, s_all - s_lo)
        nb = blk * lax.rsqrt(ssum * (1.0 / HEAD_DIM) + EPS) * gain
        up = pltpu.roll(nb, LANES - ROPE_HALF, axis=1)
        dn = pltpu.roll(nb, ROPE_HALF, axis=1)
        partner = jnp.where(low_half, up, dn)
        return ((nb * cos_t + partner * sin_t) * scale).astype(BF16)

    for j in range(N_HEADS):
        sl = slice(j * LANES, (j + 1) * LANES)
        qb = jnp.dot(h, w_ref[:, sl], preferred_element_type=F32)
        q_ref[:, sl] = norm_rope(qb, qg_ref[...], LOG2_E * HEAD_DIM ** -0.5)
        kb = jnp.dot(h, w_ref[:, ATTN_WIDTH + j * LANES:ATTN_WIDTH + (j + 1) * LANES],
                     preferred_element_type=F32)
        k_ref[:, sl] = norm_rope(kb, kg_ref[...], 1.0)
    v_ref[...] = jnp.dot(h, w_ref[:, 2 * ATTN_WIDTH:3 * ATTN_WIDTH],
                         preferred_element_type=F32).astype(BF16)
    u_ref[...] = jnp.dot(h, w_ref[:, 3 * ATTN_WIDTH:4 * ATTN_WIDTH],
                         preferred_element_type=F32).astype(BF16)


def _in_proj(x2d, g, w, cos_t, sin_t, qg, kg):
    t = x2d.shape[0]
    tok = lambda i: (i, 0)
    out = jax.ShapeDtypeStruct((t, ATTN_WIDTH), BF16)
    return pl.pallas_call(
        _in_proj_kernel,
        grid=(t // TM_IN,),
        in_specs=[
            pl.BlockSpec((TM_IN, D_MODEL), tok),
            _const_spec((1, D_MODEL)),
            _const_spec((D_MODEL, QKVU_COLS)),
            pl.BlockSpec((TM_IN, LANES), tok),
            pl.BlockSpec((TM_IN, LANES), tok),
            _const_spec((1, LANES)),
            _const_spec((1, LANES)),
        ],
        out_specs=[pl.BlockSpec((TM_IN, ATTN_WIDTH), tok)] * 4,
        out_shape=[out] * 4,
        compiler_params=pltpu.CompilerParams(
            dimension_semantics=("arbitrary",), vmem_limit_bytes=VMEM_LIMIT),
        name="in_proj",
    )(x2d, g, w, cos_t, sin_t, qg, kg)


def _attn_kernel(qi_ref, kj_ref, lam_ref, q_ref, k_ref, v_ref, sg_ref, o_ref,
                 qs_ref, m_ref, l_ref, acc_ref):
    step = pl.program_id(2)
    qi = qi_ref[step]
    kj = kj_ref[step]

    @pl.when(kj == 0)
    def _init():
        q = q_ref[...]
        lane = lax.broadcasted_iota(jnp.int32, q.shape, 1)
        zero = jnp.zeros_like(q)
        qs_ref[0:TQ, :] = jnp.where(lane < HEAD_DIM, q, zero)
        qs_ref[TQ:2 * TQ, :] = jnp.where(lane < HEAD_DIM, zero, q)
        m_ref[...] = jnp.full(m_ref.shape, NEG_BIG, F32)
        l_ref[...] = jnp.zeros(l_ref.shape, F32)
        acc_ref[...] = jnp.zeros(acc_ref.shape, F32)

    def update(masked):
        s = lax.dot_general(qs_ref[...], k_ref[...], (((1,), (1,)), ((), ())),
                            preferred_element_type=F32)
        if masked:
            row = lax.broadcasted_iota(jnp.int32, s.shape, 0) % TQ
            col = lax.broadcasted_iota(jnp.int32, s.shape, 1)
            s = jnp.where(col <= row, s, NEG_BIG)
        m_old = m_ref[...]
        m_new = jnp.maximum(m_old, jnp.max(s, axis=-1, keepdims=True))
        alpha = jnp.exp2(m_old - m_new)
        p = jnp.exp2(s - jnp.concatenate([m_new] * (TQ // LANES), axis=1))
        psum = p[:, 0:LANES]
        for c in range(1, TQ // LANES):
            psum = psum + p[:, c * LANES:(c + 1) * LANES]
        l_ref[...] = alpha * l_ref[...] + psum
        acc_ref[...] = alpha * acc_ref[...] + jnp.dot(
            p.astype(BF16), v_ref[...], preferred_element_type=F32)
        m_ref[...] = m_new

    @pl.when(kj < qi)
    def _off_diag():
        update(False)

    @pl.when(kj == qi)
    def _diag():
        update(True)
        o = acc_ref[...] / jnp.sum(l_ref[...], axis=-1, keepdims=True)
        d = o[0:TQ, :] - lam_ref[0] * o[TQ:2 * TQ, :]
        ms = jnp.mean(d * d, axis=-1, keepdims=True)
        d = d * lax.rsqrt(ms + EPS) * sg_ref[...] * (1.0 - LAM_INIT)
        o_ref[...] = d.astype(BF16)


def _diff_attn(q, k, v, lam, subln_g, bsz, seq):
    nq = seq // TQ
    pairs = [(i, j) for i in range(nq) for j in range(i + 1)]
    qi = jnp.asarray([p[0] for p in pairs], jnp.int32)
    kj = jnp.asarray([p[1] for p in pairs], jnp.int32)
    grid_spec = pltpu.PrefetchScalarGridSpec(
        num_scalar_prefetch=2,
        grid=(bsz, N_HEADS, len(pairs)),
        in_specs=[
            pl.BlockSpec(memory_space=pltpu.SMEM),
            pl.BlockSpec((TQ, LANES), lambda b, h, s, qi, kj: (b * nq + qi[s], h)),
            pl.BlockSpec((TQ, LANES), lambda b, h, s, qi, kj: (b * nq + kj[s], h)),
            pl.BlockSpec((TQ, LANES), lambda b, h, s, qi, kj: (b * nq + kj[s], h)),
            pl.BlockSpec((1, LANES), lambda b, h, s, qi, kj: (0, 0)),
        ],
        out_specs=pl.BlockSpec((TQ, LANES), lambda b, h, s, qi, kj: (b * nq + qi[s], h)),
        scratch_shapes=[
            pltpu.VMEM((2 * TQ, LANES), BF16),
            pltpu.VMEM((2 * TQ, LANES), F32),
            pltpu.VMEM((2 * TQ, LANES), F32),
            pltpu.VMEM((2 * TQ, LANES), F32),
        ],
    )
    return pl.pallas_call(
        _attn_kernel,
        grid_spec=grid_spec,
        out_shape=jax.ShapeDtypeStruct((bsz * seq, ATTN_WIDTH), BF16),
        compiler_params=pltpu.CompilerParams(
            dimension_semantics=("arbitrary", "arbitrary", "arbitrary"),
            vmem_limit_bytes=VMEM_LIMIT),
        name="diff_attn",
    )(qi, kj, lam, q, k, v, subln_g)


def _gelu_tanh(x):
    c = math.sqrt(2.0 / math.pi)
    return 0.5 * x * (1.0 + jnp.tanh(c * (x + 0.044715 * (x * x * x))))


def _s5_kernel(u_ref, p_ref, pt_ref, b_ref, are_ref, aim_ref, c_ref, d_ref, o_ref,
               st_ref, state_ref):
    bsz, chunk, width = u_ref.shape
    rows = bsz * chunk

    @pl.when(pl.program_id(0) == 0)
    def _init():
        state_ref[...] = jnp.zeros(state_ref.shape, F32)

    u_bt = u_ref[...].reshape(rows, width)
    u_tm = jnp.dot(p_ref[...], u_bt, preferred_element_type=F32)
    st_ref[...] = jnp.dot(u_tm.astype(BF16), b_ref[...], preferred_element_type=F32)

    for ch in range(N_STATE // SCAN_LANES):
        re = slice(ch * SCAN_LANES, (ch + 1) * SCAN_LANES)
        im = slice(N_STATE + ch * SCAN_LANES, N_STATE + (ch + 1) * SCAN_LANES)
        a_re = are_ref[:, re]
        a_im = aim_ref[:, re]

        def body(t, carry, re=re, im=im, a_re=a_re, a_im=a_im):
            s_re, s_im = carry
            r0 = pl.multiple_of(t * SUBLANES, SUBLANES)
            n_re = a_re * s_re - a_im * s_im + st_ref[pl.ds(r0, SUBLANES), re]
            n_im = a_re * s_im + a_im * s_re + st_ref[pl.ds(r0, SUBLANES), im]
            st_ref[pl.ds(r0, SUBLANES), re] = n_re
            st_ref[pl.ds(r0, SUBLANES), im] = n_im
            return n_re, n_im

        s_re, s_im = lax.fori_loop(0, chunk, body, (state_ref[:, re], state_ref[:, im]),
                                   unroll=8)
        state_ref[:, re] = s_re
        state_ref[:, im] = s_im

    y = jnp.dot(st_ref[...].astype(BF16), c_ref[...], preferred_element_type=F32)
    y = y + d_ref[...] * u_tm
    g_tm = _gelu_tanh(y).astype(BF16)
    g_bt = jnp.dot(pt_ref[...], g_tm, preferred_element_type=F32)
    o_ref[...] = g_bt.astype(BF16).reshape(bsz, chunk, width)


def _s5_scan(u3, perm, perm_t, bmat, a_re, a_im, cmat, dvec):
    bsz, seq, width = u3.shape
    rows = bsz * SSM_CHUNK
    blk = pl.BlockSpec((bsz, SSM_CHUNK, width), lambda c: (0, c, 0))
    return pl.pallas_call(
        _s5_kernel,
        grid=(seq // SSM_CHUNK,),
        in_specs=[
            blk,
            _const_spec((rows, rows)),
            _const_spec((rows, rows)),
            _const_spec((width, 2 * N_STATE)),
            _const_spec((bsz, N_STATE)),
            _const_spec((bsz, N_STATE)),
            _const_spec((2 * N_STATE, width)),
            _const_spec((1, width)),
        ],
        out_specs=blk,
        out_shape=jax.ShapeDtypeStruct(u3.shape, BF16),
        scratch_shapes=[
            pltpu.VMEM((rows, 2 * N_STATE), F32),
            pltpu.VMEM((bsz, 2 * N_STATE), F32),
        ],
        compiler_params=pltpu.CompilerParams(
            dimension_semantics=("arbitrary",), vmem_limit_bytes=VMEM_LIMIT),
        name="s5_scan",
    )(u3, perm, perm_t, bmat, a_re, a_im, cmat, dvec)


def _merge_kernel(x_ref, oa_ref, gy_ref, g1_ref, wg_ref, woa_ref, wglu_ref, wout_ref,
                  g2_ref, wrh_ref, wrl_ref, br_ref, x2_ref, h2_ref, route_ref, cnt_ref):
    x = x_ref[...]
    ms = jnp.mean(x * x, axis=-1, keepdims=True)
    h = (x * lax.rsqrt(ms + EPS) * g1_ref[...]).astype(BF16)
    o_a = jnp.dot(oa_ref[...], woa_ref[...], preferred_element_type=F32)
    gy = gy_ref[...]
    z_lin = jnp.dot(gy, wglu_ref[:, 0:D_MODEL], preferred_element_type=F32)
    z_gate = jnp.dot(gy, wglu_ref[:, D_MODEL:2 * D_MODEL], preferred_element_type=F32)
    o_s = z_lin * jax.nn.sigmoid(z_gate)
    gate_a = jax.nn.sigmoid(jnp.dot(h, wg_ref[:, 0:D_MODEL], preferred_element_type=F32))
    merged = gate_a * o_a
    gate_s = jax.nn.sigmoid(
        jnp.dot(h, wg_ref[:, D_MODEL:2 * D_MODEL], preferred_element_type=F32))
    merged = merged + gate_s * o_s
    x2 = x + jnp.dot(merged.astype(BF16), wout_ref[...], preferred_element_type=F32)
    x2_ref[...] = x2

    ms2 = jnp.mean(x2 * x2, axis=-1, keepdims=True)
    h2 = x2 * lax.rsqrt(ms2 + EPS) * g2_ref[...]
    h2_hi = h2.astype(BF16)
    h2_lo = (h2 - h2_hi.astype(F32)).astype(BF16)
    h2_ref[...] = h2_hi

    logits = (jnp.dot(h2_hi, wrh_ref[...], preferred_element_type=F32)
              + jnp.dot(h2_lo, wrh_ref[...], preferred_element_type=F32)
              + jnp.dot(h2_hi, wrl_ref[...], preferred_element_type=F32)
              + br_ref[...])
    lane = lax.broadcasted_iota(jnp.int32, logits.shape, 1).astype(F32)
    is_grp = (lane >= N_EXPERTS) & (lane < N_EXPERTS + N_EXPERT_GROUPS)
    gl = jnp.where(is_grp, logits, NEG_BIG)
    gmax = jnp.max(gl, axis=-1, keepdims=True)
    gsum = jnp.sum(jnp.where(is_grp, jnp.exp(gl - gmax), 0.0), axis=-1, keepdims=True)
    p_grp = 1.0 / gsum
    big = float(4 * LANES)
    grp = jnp.min(jnp.where(is_grp & (gl == gmax), lane, big), axis=-1,
                  keepdims=True) - N_EXPERTS
    sel = logits
    for g in range(1, N_EXPERT_GROUPS):
        rolled = pltpu.roll(logits, LANES - g * EXPERTS_PER_GROUP, axis=1)
        sel = jnp.where(grp == g, rolled, sel)
    in_grp = lane < EXPERTS_PER_GROUP
    es = jnp.where(in_grp, sel, NEG_BIG)
    top1 = jnp.max(es, axis=-1, keepdims=True)
    i1 = jnp.min(jnp.where(in_grp & (es == top1), lane, big), axis=-1, keepdims=True)
    es2 = jnp.where(lane == i1, NEG_BIG, es)
    top2 = jnp.max(es2, axis=-1, keepdims=True)
    i2 = jnp.min(jnp.where(in_grp & (lane != i1) & (es2 == top2), lane, big), axis=-1,
                 keepdims=True)
    e2 = jnp.exp(top2 - top1)
    w1 = p_grp / (1.0 + e2)
    w2 = p_grp * e2 / (1.0 + e2)
    route_ref[...] = (jnp.where(lane == i1, w1, 0.0) + jnp.where(lane == i2, w2, 0.0)
                      + jnp.where(lane == ROUTE_GROUP_LANE, grp, 0.0))
    grp_onehot = jnp.where((lane == grp) & (lane < N_EXPERT_GROUPS), 1.0, 0.0)
    cnt_ref[...] = jnp.sum(grp_onehot, axis=0, keepdims=True)


def _merge(x2d, o_attn, gy, g1, w_gates, w_oa, w_glu, w_out, g2, wr_hi, wr_lo, b_r):
    t = x2d.shape[0]
    tok = lambda i: (i, 0)
    return pl.pallas_call(
        _merge_kernel,
        grid=(t // TM_MERGE,),
        in_specs=[
            pl.BlockSpec((TM_MERGE, D_MODEL), tok),
            pl.BlockSpec((TM_MERGE, ATTN_WIDTH), tok),
            pl.BlockSpec((TM_MERGE, SSM_WIDTH), tok),
            _const_spec((1, D_MODEL)),
            _const_spec((D_MODEL, 2 * D_MODEL)),
            _const_spec((ATTN_WIDTH, D_MODEL)),
            _const_spec((SSM_WIDTH, 2 * D_MODEL)),
            _const_spec((D_MODEL, D_MODEL)),
            _const_spec((1, D_MODEL)),
            _const_spec((D_MODEL, LANES)),
            _const_spec((D_MODEL, LANES)),
            _const_spec((1, LANES)),
        ],
        out_specs=[
            pl.BlockSpec((TM_MERGE, D_MODEL), tok),
            pl.BlockSpec((TM_MERGE, D_MODEL), tok),
            pl.BlockSpec((TM_MERGE, LANES), tok),
            pl.BlockSpec((None, 1, LANES), lambda i: (i, 0, 0)),
        ],
        out_shape=[
            jax.ShapeDtypeStruct((t, D_MODEL), F32),
            jax.ShapeDtypeStruct((t, D_MODEL), BF16),
            jax.ShapeDtypeStruct((t, LANES), F32),
            jax.ShapeDtypeStruct((t // TM_MERGE, 1, LANES), F32),
        ],
        compiler_params=pltpu.CompilerParams(
            dimension_semantics=("arbitrary",), vmem_limit_bytes=VMEM_LIMIT),
        name="merge",
    )(x2d, o_attn, gy, g1, w_gates, w_oa, w_glu, w_out, g2, wr_hi, wr_lo, b_r)


def _moe_layout(t):
    ntiles = t // TM_MERGE
    pad_rows = ntiles * (ROW_ALIGN - 1)
    cap = -(-(t + pad_rows + TM_MERGE + TS_MOE) // TS_MOE) * TS_MOE
    nsteps = -(-(t + N_EXPERT_GROUPS * (pad_rows + TM_MERGE)) // TS_MOE) + N_EXPERT_GROUPS
    return ntiles, cap, nsteps


def _dispatch_kernel(base_ref, tail_ref, h_ref, route_ref, tri_ref, xs_ref, xr_ref,
                     slot_ref, buf_ref, rbuf_ref, zbuf_ref, zrbuf_ref, sem_ref, tsem_ref):
    i = pl.program_id(0)
    last = pl.num_programs(0) - 1
    cur = i % 2
    tm = TM_MERGE
    route = route_ref[...]
    lane = lax.broadcasted_iota(jnp.int32, route.shape, 1)
    lane_f = lane.astype(F32)
    g_col = jnp.sum(jnp.where(lane == ROUTE_GROUP_LANE, route, 0.0), axis=-1, keepdims=True)
    onehot = jnp.where((lane_f == g_col) & (lane < N_EXPERT_GROUPS), 1.0, 0.0).astype(BF16)
    rank_all = jnp.dot(tri_ref[...], onehot, preferred_element_type=F32)
    r_col = jnp.sum(jnp.where(lane_f == g_col, rank_all, 0.0), axis=-1, keepdims=True)
    slot_ref[...] = jnp.where(lane == 0, g_col * tm + r_col, 0.0)
    rg = (jnp.where(lane == 0, r_col, 0.0) + jnp.where(lane == 1, g_col, 0.0)).astype(BF16)
    eye8 = (lax.broadcasted_iota(jnp.int32, (SUBLANES, LANES), 0)
            == lax.broadcasted_iota(jnp.int32, (SUBLANES, LANES), 1)).astype(BF16)
    rows = lax.dot_general(eye8, rg, (((1,), (1,)), ((), ())), preferred_element_type=F32)
    r_row = rows[0:1, :]
    g_row = rows[1:2, :]
    sub = lax.broadcasted_iota(jnp.int32, (tm, tm), 0).astype(F32)
    r_hi = route.astype(BF16)
    rem = route - r_hi.astype(F32)
    r_mid = rem.astype(BF16)
    r_lo = (rem - r_mid.astype(F32)).astype(BF16)
    h = h_ref[...]
    for g in range(N_EXPERT_GROUPS):
        perm = jnp.where((r_row == sub) & (g_row == float(g)), 1.0, 0.0).astype(BF16)
        rs = slice(g * tm, (g + 1) * tm)
        buf_ref[cur, rs, :] = jnp.dot(perm, h, preferred_element_type=F32).astype(BF16)
        rbuf_ref[cur, rs, :] = (jnp.dot(perm, r_hi, preferred_element_type=F32)
                                + jnp.dot(perm, r_mid, preferred_element_type=F32)
                                + jnp.dot(perm, r_lo, preferred_element_type=F32))

    def copies(step, slot):
        out = []
        for g in range(N_EXPERT_GROUPS):
            dst = pl.multiple_of(base_ref[step * N_EXPERT_GROUPS + g], ROW_ALIGN)
            out.append(pltpu.make_async_copy(
                buf_ref.at[slot, pl.ds(g * tm, tm)], xs_ref.at[pl.ds(dst, tm)],
                sem_ref.at[slot, g]))
            out.append(pltpu.make_async_copy(
                rbuf_ref.at[slot, pl.ds(g * tm, tm)], xr_ref.at[pl.ds(dst, tm)],
                sem_ref.at[slot, N_EXPERT_GROUPS + g]))
        return out

    @pl.when(i > 0)
    def _wait_prev():
        for c in copies(i - 1, 1 - cur):
            c.wait()

    for c in copies(i, cur):
        c.start()

    @pl.when(i == last)
    def _finish():
        for c in copies(i, cur):
            c.wait()
        zbuf_ref[...] = jnp.zeros(zbuf_ref.shape, BF16)
        zrbuf_ref[...] = jnp.zeros(zrbuf_ref.shape, F32)
        tails = []
        for g in range(N_EXPERT_GROUPS):
            dst = pl.multiple_of(tail_ref[g], ROW_ALIGN)
            tails.append(pltpu.make_async_copy(zbuf_ref, xs_ref.at[pl.ds(dst, TS_MOE)],
                                               tsem_ref.at[g]))
            tails.append(pltpu.make_async_copy(zrbuf_ref, xr_ref.at[pl.ds(dst, TS_MOE)],
                                               tsem_ref.at[N_EXPERT_GROUPS + g]))
        for c in tails:
            c.start()
        for c in tails:
            c.wait()


def _dispatch(h2, route, base, tail, tri, rows):
    t = h2.shape[0]
    tm = TM_MERGE
    tok = lambda i, *_: (i, 0)
    grid_spec = pltpu.PrefetchScalarGridSpec(
        num_scalar_prefetch=2,
        grid=(t // tm,),
        in_specs=[
            pl.BlockSpec((tm, D_MODEL), tok),
            pl.BlockSpec((tm, LANES), tok),
            pl.BlockSpec((tm, tm), lambda i, *_: (0, 0)),
        ],
        out_specs=[
            pl.BlockSpec(memory_space=pl.ANY),
            pl.BlockSpec(memory_space=pl.ANY),
            pl.BlockSpec((tm, LANES), tok),
        ],
        scratch_shapes=[
            pltpu.VMEM((2, N_EXPERT_GROUPS * tm, D_MODEL), BF16),
            pltpu.VMEM((2, N_EXPERT_GROUPS * tm, LANES), F32),
            pltpu.VMEM((TS_MOE, D_MODEL), BF16),
            pltpu.VMEM((TS_MOE, LANES), F32),
            pltpu.SemaphoreType.DMA((2, 2 * N_EXPERT_GROUPS)),
            pltpu.SemaphoreType.DMA((2 * N_EXPERT_GROUPS,)),
        ],
    )
    return pl.pallas_call(
        _dispatch_kernel,
        grid_spec=grid_spec,
        out_shape=[
            jax.ShapeDtypeStruct((rows, D_MODEL), BF16),
            jax.ShapeDtypeStruct((rows, LANES), F32),
            jax.ShapeDtypeStruct((t, LANES), F32),
        ],
        compiler_params=pltpu.CompilerParams(
            dimension_semantics=("arbitrary",), vmem_limit_bytes=VMEM_LIMIT),
        name="dispatch",
    )(base, tail, h2, route, tri)


def _moe_kernel(blk_ref, grp_ref, nvalid_ref, xs_ref, xr_ref, wg_ref, wu_ref, wd_ref,
                ys_ref, a_ref):
    s = pl.program_id(0)

    @pl.when(s < nvalid_ref[0])
    def _compute():
        t = xs_ref[...]
        comb = xr_ref[...]
        lane = lax.broadcasted_iota(jnp.int32, comb.shape, 1)
        for e in range(EXPERTS_PER_GROUP):
            sl = slice(e * D_EXPERT, (e + 1) * D_EXPERT)
            hg = jnp.dot(t, wg_ref[:, sl], preferred_element_type=F32)
            hu = jnp.dot(t, wu_ref[:, sl], preferred_element_type=F32)
            cw = jnp.sum(jnp.where(lane == e, comb, 0.0), axis=-1, keepdims=True)
            a_ref[:, sl] = (hg * jax.nn.sigmoid(hg) * hu * cw).astype(BF16)
        ys_ref[...] = jnp.dot(a_ref[...], wd_ref[...],
                              preferred_element_type=F32).astype(BF16)

    @pl.when(s >= nvalid_ref[0])
    def _spare():
        ys_ref[...] = jnp.zeros(ys_ref.shape, BF16)


def _moe(xs, xr, blk, grp, nvalid, wg, wu, wd):
    ge = EXPERTS_PER_GROUP * D_EXPERT
    row = lambda s, blk, grp, nv: (blk[s], 0)
    wsel = lambda s, blk, grp, nv: (grp[s], 0, 0)
    grid_spec = pltpu.PrefetchScalarGridSpec(
        num_scalar_prefetch=3,
        grid=(blk.shape[0],),
        in_specs=[
            pl.BlockSpec((TS_MOE, D_MODEL), row),
            pl.BlockSpec((TS_MOE, LANES), row),
            pl.BlockSpec((None, D_MODEL, ge), wsel),
            pl.BlockSpec((None, D_MODEL, ge), wsel),
            pl.BlockSpec((None, ge, D_MODEL), wsel),
        ],
        out_specs=pl.BlockSpec((TS_MOE, D_MODEL), row),
        scratch_shapes=[pltpu.VMEM((TS_MOE, ge), BF16)],
    )
    return pl.pallas_call(
        _moe_kernel,
        grid_spec=grid_spec,
        out_shape=jax.ShapeDtypeStruct(xs.shape, BF16),
        compiler_params=pltpu.CompilerParams(
            dimension_semantics=("arbitrary",), vmem_limit_bytes=VMEM_LIMIT),
        name="moe",
    )(blk, grp, nvalid, xs, xr, wg, wu, wd)


def _combine_kernel(base_ref, x2_ref, slot_ref, ys_ref, o_ref, stage_ref, sem_ref):
    i = pl.program_id(0)
    n = pl.num_programs(0)
    cur = i % 2
    tm = TM_MERGE

    def copies(step, slot):
        out = []
        for g in range(N_EXPERT_GROUPS):
            src = pl.multiple_of(base_ref[step * N_EXPERT_GROUPS + g], ROW_ALIGN)
            out.append(pltpu.make_async_copy(
                ys_ref.at[pl.ds(src, tm)], stage_ref.at[slot, pl.ds(g * tm, tm)],
                sem_ref.at[slot, g]))
        return out

    @pl.when(i == 0)
    def _first():
        for c in copies(0, 0):
            c.start()

    @pl.when(i + 1 < n)
    def _prefetch():
        for c in copies(i + 1, 1 - cur):
            c.start()

    for c in copies(i, cur):
        c.wait()
    slot = slot_ref[...]
    lane128 = lax.broadcasted_iota(jnp.int32, slot.shape, 1)
    slot_col = jnp.sum(jnp.where(lane128 == 0, slot, 0.0), axis=-1, keepdims=True)
    lane = lax.broadcasted_iota(jnp.int32, (tm, N_EXPERT_GROUPS * tm), 1).astype(F32)
    pick = jnp.where(lane == slot_col, 1.0, 0.0).astype(BF16)
    o_ref[...] = x2_ref[...] + jnp.dot(pick, stage_ref[cur], preferred_element_type=F32)


def _combine(x2, slot, ys, base):
    t = x2.shape[0]
    tm = TM_MERGE
    tok = lambda i, *_: (i, 0)
    grid_spec = pltpu.PrefetchScalarGridSpec(
        num_scalar_prefetch=1,
        grid=(t // tm,),
        in_specs=[
            pl.BlockSpec((tm, D_MODEL), tok),
            pl.BlockSpec((tm, LANES), tok),
            pl.BlockSpec(memory_space=pl.ANY),
        ],
        out_specs=pl.BlockSpec((tm, D_MODEL), tok),
        scratch_shapes=[
            pltpu.VMEM((2, N_EXPERT_GROUPS * tm, D_MODEL), BF16),
            pltpu.SemaphoreType.DMA((2, N_EXPERT_GROUPS)),
        ],
    )
    return pl.pallas_call(
        _combine_kernel,
        grid_spec=grid_spec,
        out_shape=jax.ShapeDtypeStruct((t, D_MODEL), F32),
        compiler_params=pltpu.CompilerParams(
            dimension_semantics=("arbitrary",), vmem_limit_bytes=VMEM_LIMIT),
        name="combine",
    )(base, x2, slot, ys)


def _sparse_moe(h2, route, cnt, x2, wg, wu, wd):
    t = h2.shape[0]
    ntiles, cap, nsteps = _moe_layout(t)
    rows = N_EXPERT_GROUPS * cap + TS_MOE
    counts = cnt[:, 0, :N_EXPERT_GROUPS].astype(jnp.int32)
    padded = (counts + (ROW_ALIGN - 1)) // ROW_ALIGN * ROW_ALIGN
    region = jnp.arange(N_EXPERT_GROUPS, dtype=jnp.int32) * cap
    start = jnp.cumsum(padded, axis=0) - padded
    base = (start + region[None, :]).reshape(-1)
    written = start[-1] + TM_MERGE
    tail = written + region
    ntile_g = (written + (TS_MOE - 1)) // TS_MOE
    first = jnp.cumsum(ntile_g) - ntile_g
    nvalid = jnp.sum(ntile_g).reshape(1)
    s = jnp.arange(nsteps, dtype=jnp.int32)
    grp = jnp.minimum(jnp.sum(s[:, None] >= (first + ntile_g)[None, :], axis=1),
                      N_EXPERT_GROUPS - 1).astype(jnp.int32)
    blk = jnp.where(s < nvalid[0], region[grp] // TS_MOE + s - first[grp],
                    N_EXPERT_GROUPS * cap // TS_MOE).astype(jnp.int32)
    tri = (jnp.arange(TM_MERGE)[:, None] > jnp.arange(TM_MERGE)[None, :]).astype(BF16)
    xs, xr, slot = _dispatch(h2, route, base, tail, tri, rows)
    ys = _moe(xs, xr, blk, grp, nvalid, wg, wu, wd)
    return _combine(x2, slot, ys, base)


def _rope_tables(positions):
    inv = ROPE_THETA ** (-jnp.arange(0, ROPE_DIM, 2, dtype=F32) / ROPE_DIM)
    ang = positions.astype(F32).reshape(-1, 1) * inv
    cos, sin = jnp.cos(ang), jnp.sin(ang)
    t = ang.shape[0]
    ones = jnp.ones((t, HEAD_DIM - ROPE_DIM), F32)
    zeros = jnp.zeros((t, HEAD_DIM - ROPE_DIM), F32)
    cos64 = jnp.concatenate([cos, cos, ones], axis=-1)
    sin64 = jnp.concatenate([-sin, sin, zeros], axis=-1)
    return jnp.tile(cos64, (1, 2)), jnp.tile(sin64, (1, 2))


def _s5_params(lam_re, lam_im, log_dt, b_re, b_im, c_re, c_im, bsz):
    dt = jnp.exp(log_dt)[:, None]
    mag = jnp.exp(lam_re * dt)
    lb_re = mag * jnp.cos(lam_im * dt)
    lb_im = mag * jnp.sin(lam_im * dt)
    den = lam_re * lam_re + lam_im * lam_im
    k_re = ((lb_re - 1.0) * lam_re + lb_im * lam_im) / den
    k_im = (lb_im * lam_re - (lb_re - 1.0) * lam_im) / den
    bb_re = k_re[..., None] * b_re - k_im[..., None] * b_im
    bb_im = k_re[..., None] * b_im + k_im[..., None] * b_re
    eye = jnp.eye(SSM_GROUPS, dtype=F32)
    blk_b = lambda m: jnp.einsum('gph,gk->ghkp', m, eye).reshape(SSM_WIDTH, N_STATE)
    blk_c = lambda m: jnp.einsum('ghp,gk->gpkh', m, eye).reshape(N_STATE, SSM_WIDTH)
    bmat = jnp.concatenate([blk_b(bb_re), blk_b(bb_im)], axis=1).astype(BF16)
    cmat = jnp.concatenate([blk_c(c_re), blk_c(-c_im)], axis=0).astype(BF16)
    a_re = jnp.broadcast_to(lb_re.reshape(1, N_STATE), (bsz, N_STATE))
    a_im = jnp.broadcast_to(lb_im.reshape(1, N_STATE), (bsz, N_STATE))
    return bmat, a_re, a_im, cmat


def _time_major_perm(bsz, chunk):
    r = jnp.arange(bsz * chunk)
    src = (r % bsz) * chunk + r // bsz
    perm = (src[:, None] == jnp.arange(bsz * chunk)[None, :]).astype(BF16)
    return perm, perm.T


def kernel(x, positions, norm_mix_g, w_in, q_norm_g, k_norm_g, lambda_q1, lambda_k1, lambda_q2, lambda_k2, subln_g, w_o_attn, ssm_lambda_re, ssm_lambda_im, ssm_log_dt, ssm_b_re, ssm_b_im, ssm_c_re, ssm_c_im, ssm_d, w_glu, w_out, norm_ffn_g, w_router_group, b_router_group, w_router_expert, b_router_expert, w_expert_gate, w_expert_up, w_expert_down):
    bsz, seq, _ = x.shape
    assert bsz == SUBLANES and seq % TQ == 0 and seq % SSM_CHUNK == 0
    assert norm_mix_g.shape[0] == 1
    t = bsz * seq
    x2d = x.reshape(t, D_MODEL)
    l = 0

    cos_t, sin_t = _rope_tables(positions)
    w_qkvu = w_in[l][:, :QKVU_COLS].astype(BF16)
    w_gates = w_in[l][:, QKVU_COLS:].astype(BF16)
    qg = jnp.tile(q_norm_g[l].reshape(1, HEAD_DIM), (1, 2))
    kg = jnp.tile(k_norm_g[l].reshape(1, HEAD_DIM), (1, 2))
    q, k, v, u = _in_proj(x2d, norm_mix_g[l].reshape(1, D_MODEL), w_qkvu, cos_t, sin_t, qg, kg)

    lam = (jnp.exp(jnp.sum(lambda_q1[l] * lambda_k1[l]))
           - jnp.exp(jnp.sum(lambda_q2[l] * lambda_k2[l])) + LAM_INIT).reshape(1)
    o_attn = _diff_attn(q, k, v, lam, subln_g[l].reshape(1, V_DIM), bsz, seq)

    bmat, a_re, a_im, cmat = _s5_params(
        ssm_lambda_re[l], ssm_lambda_im[l], ssm_log_dt[l], ssm_b_re[l], ssm_b_im[l],
        ssm_c_re[l], ssm_c_im[l], bsz)
    perm, perm_t = _time_major_perm(bsz, SSM_CHUNK)
    gy = _s5_scan(u.reshape(bsz, seq, SSM_WIDTH), perm, perm_t, bmat, a_re, a_im, cmat,
                  ssm_d[l].reshape(1, SSM_WIDTH)).reshape(t, SSM_WIDTH)

    w_r = jnp.concatenate(
        [w_router_expert[l].reshape(D_MODEL, N_EXPERTS), w_router_group[l],
         jnp.zeros((D_MODEL, LANES - N_EXPERTS - N_EXPERT_GROUPS), F32)], axis=1)
    b_r = jnp.concatenate(
        [b_router_expert[l].reshape(N_EXPERTS), b_router_group[l],
         jnp.zeros((LANES - N_EXPERTS - N_EXPERT_GROUPS,), F32)]).reshape(1, LANES)
    wr_hi = w_r.astype(BF16)
    wr_lo = (w_r - wr_hi.astype(F32)).astype(BF16)
    x2, h2, route, cnt = _merge(
        x2d, o_attn, gy, norm_mix_g[l].reshape(1, D_MODEL), w_gates,
        w_o_attn[l].astype(BF16), w_glu[l].astype(BF16), w_out[l].astype(BF16),
        norm_ffn_g[l].reshape(1, D_MODEL), wr_hi, wr_lo, b_r)

    ge = EXPERTS_PER_GROUP * D_EXPERT
    wg = w_expert_gate[l].astype(BF16).transpose(0, 2, 1, 3).reshape(N_EXPERT_GROUPS, D_MODEL, ge)
    wu = w_expert_up[l].astype(BF16).transpose(0, 2, 1, 3).reshape(N_EXPERT_GROUPS, D_MODEL, ge)
    wd = w_expert_down[l].astype(BF16).reshape(N_EXPERT_GROUPS, ge, D_MODEL)
    out = _sparse_moe(h2, route, cnt, x2, wg, wu, wd)
    return out.reshape(bsz, seq, D_MODEL)
```

```python
import functools
import math

import jax
import jax.numpy as jnp
from jax import lax
from jax.experimental import pallas as pl
from jax.experimental.pallas import tpu as pltpu

F32 = jnp.float32
BF16 = jnp.bfloat16

D_MODEL = 1024
N_HEADS = 4
HEAD_DIM = 64
V_DIM = 2 * HEAD_DIM
ATTN_WIDTH = N_HEADS * V_DIM
ROPE_THETA = 500000.0
ROPE_DIM = HEAD_DIM // 4
ROPE_HALF = ROPE_DIM // 2
SSM_WIDTH = D_MODEL // 2
SSM_GROUP = 16
SSM_GROUPS = SSM_WIDTH // SSM_GROUP
SSM_STATE = 64
N_STATE = SSM_GROUPS * SSM_STATE
N_EXPERT_GROUPS = 4
EXPERTS_PER_GROUP = 8
N_EXPERTS = N_EXPERT_GROUPS * EXPERTS_PER_GROUP
D_EXPERT = D_MODEL // 4
EPS = 1e-6
LAM_INIT = 0.8 - 0.6 * math.exp(-0.3 * 0)
QKVU_COLS = 4 * ATTN_WIDTH
LANES = 128
SUBLANES = 8
NEG_BIG = -1e30
LOG2_E = math.log2(math.e)

VMEM_LIMIT = 48 * 1024 * 1024

TM_IN = 512
TQ = 512
SSM_CHUNK = 64
SCAN_LANES = 512
TM_MERGE = 512
TM_DISP = 256
TS_MOE = 512
ROW_ALIGN = 16
ROUTE_GROUP_LANE = 8


def _const_spec(shape):
    return pl.BlockSpec(shape, lambda *_: (0,) * len(shape))


def _in_proj_kernel(x_ref, g_ref, w_ref, cos_ref, sin_ref, qg_ref, kg_ref,
                    q_ref, k_ref, v_ref, u_ref):
    x = x_ref[...]
    ms = jnp.mean(x * x, axis=-1, keepdims=True)
    h = (x * lax.rsqrt(ms + EPS) * g_ref[...]).astype(BF16)
    cos_t = cos_ref[...]
    sin_t = sin_ref[...]
    lane = lax.broadcasted_iota(jnp.int32, (x.shape[0], LANES), 1)
    first_comp = lane < HEAD_DIM
    low_half = (lane % HEAD_DIM) < ROPE_HALF

    def norm_rope(blk, gain, scale):
        sq = blk * blk
        s_all = jnp.sum(sq, axis=-1, keepdims=True)
        s_lo = jnp.sum(jnp.where(first_comp, sq, 0.0), axis=-1, keepdims=True)
        ssum = jnp.where(first_comp, s_lo, s_all - s_lo)
        nb = blk * lax.rsqrt(ssum * (1.0 / HEAD_DIM) + EPS) * gain
        up = pltpu.roll(nb, LANES - ROPE_HALF, axis=1)
        dn = pltpu.roll(nb, ROPE_HALF, axis=1)
        partner = jnp.where(low_half, up, dn)
        return ((nb * cos_t + partner * sin_t) * scale).astype(BF16)

    for j in range(0, N_HEADS, 2):
        sl2 = slice(j * LANES, (j + 2) * LANES)
        qb = jnp.dot(h, w_ref[:, sl2], preferred_element_type=F32)
        kb = jnp.dot(h, w_ref[:, ATTN_WIDTH + j * LANES:ATTN_WIDTH + (j + 2) * LANES],
                     preferred_element_type=F32)
        for jj in range(2):
            sl = slice((j + jj) * LANES, (j + jj + 1) * LANES)
            half = slice(jj * LANES, (jj + 1) * LANES)
            q_ref[:, sl] = norm_rope(qb[:, half], qg_ref[...], LOG2_E * HEAD_DIM ** -0.5)
            k_ref[:, sl] = norm_rope(kb[:, half], kg_ref[...], 1.0)
    v_ref[...] = jnp.dot(h, w_ref[:, 2 * ATTN_WIDTH:3 * ATTN_WIDTH],
                         preferred_element_type=F32).astype(BF16)
    u_ref[...] = jnp.dot(h, w_ref[:, 3 * ATTN_WIDTH:4 * ATTN_WIDTH],
                         preferred_element_type=F32).astype(BF16)


def _in_proj(x2d, g, w, cos_t, sin_t, qg, kg):
    t = x2d.shape[0]
    tok = lambda i: (i, 0)
    out = jax.ShapeDtypeStruct((t, ATTN_WIDTH), BF16)
    return pl.pallas_call(
        _in_proj_kernel,
        grid=(t // TM_IN,),
        in_specs=[
            pl.BlockSpec((TM_IN, D_MODEL), tok),
            _const_spec((1, D_MODEL)),
            _const_spec((D_MODEL, QKVU_COLS)),
            pl.BlockSpec((TM_IN, LANES), tok),
            pl.BlockSpec((TM_IN, LANES), tok),
            _const_spec((1, LANES)),
            _const_spec((1, LANES)),
        ],
        out_specs=[pl.BlockSpec((TM_IN, ATTN_WIDTH), tok)] * 4,
        out_shape=[out] * 4,
        compiler_params=pltpu.CompilerParams(
            dimension_semantics=("arbitrary",), vmem_limit_bytes=VMEM_LIMIT),
        name="in_proj",
    )(x2d, g, w, cos_t, sin_t, qg, kg)


def _attn_kernel(qi_ref, kj_ref, lam_ref, q_ref, k_ref, v_ref, sg_ref, o_ref,
                 qs_ref, m_ref, l_ref, acc_ref):
    step = pl.program_id(1)
    qi = qi_ref[step]
    kj = kj_ref[step]

    @pl.when(kj == 0)
    def _init():
        for h in range(N_HEADS):
            q = q_ref[:, h * LANES:(h + 1) * LANES]
            lane = lax.broadcasted_iota(jnp.int32, q.shape, 1)
            zero = jnp.zeros_like(q)
            qs_ref[h, 0:TQ, :] = jnp.where(lane < HEAD_DIM, q, zero)
            qs_ref[h, TQ:2 * TQ, :] = jnp.where(lane < HEAD_DIM, zero, q)
        m_ref[...] = jnp.full(m_ref.shape, NEG_BIG, F32)
        l_ref[...] = jnp.zeros(l_ref.shape, F32)
        acc_ref[...] = jnp.zeros(acc_ref.shape, F32)

    def update(h, masked):
        hs = slice(h * LANES, (h + 1) * LANES)
        s = lax.dot_general(qs_ref[h], k_ref[:, hs], (((1,), (1,)), ((), ())),
                            preferred_element_type=F32)
        if masked:
            row = lax.broadcasted_iota(jnp.int32, s.shape, 0) % TQ
            col = lax.broadcasted_iota(jnp.int32, s.shape, 1)
            s = jnp.where(col <= row, s, NEG_BIG)
        m_old = m_ref[h]
        m_new = jnp.maximum(m_old, jnp.max(s, axis=-1, keepdims=True))
        alpha = jnp.exp2(m_old - m_new)
        p = jnp.exp2(s - jnp.concatenate([m_new] * (TQ // LANES), axis=1))
        psum = p[:, 0:LANES]
        for c in range(1, TQ // LANES):
            psum = psum + p[:, c * LANES:(c + 1) * LANES]
        l_ref[h] = alpha * l_ref[h] + psum
        acc_ref[h] = alpha * acc_ref[h] + jnp.dot(
            p.astype(BF16), v_ref[:, hs], preferred_element_type=F32)
        m_ref[h] = m_new

    @pl.when(kj < qi)
    def _off_diag():
        for h in range(N_HEADS):
            update(h, False)

    @pl.when(kj == qi)
    def _diag():
        for h in range(N_HEADS):
            update(h, True)
            o = acc_ref[h] / jnp.sum(l_ref[h], axis=-1, keepdims=True)
            d = o[0:TQ, :] - lam_ref[0] * o[TQ:2 * TQ, :]
            ms = jnp.mean(d * d, axis=-1, keepdims=True)
            d = d * lax.rsqrt(ms + EPS) * sg_ref[...] * (1.0 - LAM_INIT)
            o_ref[:, h * LANES:(h + 1) * LANES] = d.astype(BF16)


def _diff_attn(q, k, v, lam, subln_g, bsz, seq):
    nq = seq // TQ
    pairs = [(i, j) for i in range(nq) for j in range(i + 1)]
    qi = jnp.asarray([p[0] for p in pairs], jnp.int32)
    kj = jnp.asarray([p[1] for p in pairs], jnp.int32)
    q_map = lambda b, s, qi, kj: (b * nq + qi[s], 0)
    k_map = lambda b, s, qi, kj: (b * nq + kj[s], 0)
    grid_spec = pltpu.PrefetchScalarGridSpec(
        num_scalar_prefetch=2,
        grid=(bsz, len(pairs)),
        in_specs=[
            pl.BlockSpec(memory_space=pltpu.SMEM),
            pl.BlockSpec((TQ, ATTN_WIDTH), q_map),
            pl.BlockSpec((TQ, ATTN_WIDTH), k_map),
            pl.BlockSpec((TQ, ATTN_WIDTH), k_map),
            pl.BlockSpec((1, LANES), lambda b, s, qi, kj: (0, 0)),
        ],
        out_specs=pl.BlockSpec((TQ, ATTN_WIDTH), q_map),
        scratch_shapes=[
            pltpu.VMEM((N_HEADS, 2 * TQ, LANES), BF16),
            pltpu.VMEM((N_HEADS, 2 * TQ, LANES), F32),
            pltpu.VMEM((N_HEADS, 2 * TQ, LANES), F32),
            pltpu.VMEM((N_HEADS, 2 * TQ, LANES), F32),
        ],
    )
    return pl.pallas_call(
        _attn_kernel,
        grid_spec=grid_spec,
        out_shape=jax.ShapeDtypeStruct((bsz * seq, ATTN_WIDTH), BF16),
        compiler_params=pltpu.CompilerParams(
            dimension_semantics=("arbitrary", "arbitrary"),
            vmem_limit_bytes=VMEM_LIMIT),
        name="diff_attn",
    )(qi, kj, lam, q, k, v, subln_g)


def _gelu_tanh(x):
    c = math.sqrt(2.0 / math.pi)
    return 0.5 * x * (1.0 + jnp.tanh(c * (x + 0.044715 * (x * x * x))))


def _s5_kernel(u_ref, p_ref, pt_ref, b_ref, are_ref, aim_ref, c_ref, d_ref, o_ref,
               st_ref, state_ref):
    bsz, chunk, width = u_ref.shape
    rows = bsz * chunk

    @pl.when(pl.program_id(0) == 0)
    def _init():
        state_ref[...] = jnp.zeros(state_ref.shape, F32)

    u_bt = u_ref[...].reshape(rows, width)
    u_tm = jnp.dot(p_ref[...], u_bt, preferred_element_type=F32)
    u_tm_b = u_tm.astype(BF16)

    slab = SCAN_LANES // SSM_STATE * SSM_GROUP
    g_parts = []
    for ch in range(N_STATE // SCAN_LANES):
        re = slice(ch * SCAN_LANES, (ch + 1) * SCAN_LANES)
        im = slice(N_STATE + ch * SCAN_LANES, N_STATE + (ch + 1) * SCAN_LANES)
        cs = slice(ch * slab, (ch + 1) * slab)
        st_ref[:, re] = jnp.dot(u_tm_b[:, cs], b_ref[cs, re], preferred_element_type=F32)
        st_ref[:, im] = jnp.dot(u_tm_b[:, cs], b_ref[cs, im], preferred_element_type=F32)
        a_re = are_ref[:, re]
        a_im = aim_ref[:, re]

        def body(t, carry, re=re, im=im, a_re=a_re, a_im=a_im):
            s_re, s_im = carry
            r0 = pl.multiple_of(t * SUBLANES, SUBLANES)
            n_re = a_re * s_re - a_im * s_im + st_ref[pl.ds(r0, SUBLANES), re]
            n_im = a_re * s_im + a_im * s_re + st_ref[pl.ds(r0, SUBLANES), im]
            st_ref[pl.ds(r0, SUBLANES), re] = n_re
            st_ref[pl.ds(r0, SUBLANES), im] = n_im
            return n_re, n_im

        s_re, s_im = lax.fori_loop(0, chunk, body, (state_ref[:, re], state_ref[:, im]),
                                   unroll=True)
        state_ref[:, re] = s_re
        state_ref[:, im] = s_im
        y = (jnp.dot(st_ref[:, re].astype(BF16), c_ref[re, cs], preferred_element_type=F32)
             + jnp.dot(st_ref[:, im].astype(BF16), c_ref[im, cs], preferred_element_type=F32))
        y = y + d_ref[:, cs] * u_tm[:, cs]
        g_parts.append(_gelu_tanh(y).astype(BF16))

    g_tm = jnp.concatenate(g_parts, axis=1)
    g_bt = jnp.dot(pt_ref[...], g_tm, preferred_element_type=F32)
    o_ref[...] = g_bt.astype(BF16).reshape(bsz, chunk, width)


def _s5_scan(u3, perm, perm_t, bmat, a_re, a_im, cmat, dvec):
    bsz, seq, width = u3.shape
    rows = bsz * SSM_CHUNK
    blk = pl.BlockSpec((bsz, SSM_CHUNK, width), lambda c: (0, c, 0))
    return pl.pallas_call(
        _s5_kernel,
        grid=(seq // SSM_CHUNK,),
        in_specs=[
            blk,
            _const_spec((rows, rows)),
            _const_spec((rows, rows)),
            _const_spec((width, 2 * N_STATE)),
            _const_spec((bsz, N_STATE)),
            _const_spec((bsz, N_STATE)),
            _const_spec((2 * N_STATE, width)),
            _const_spec((1, width)),
        ],
        out_specs=blk,
        out_shape=jax.ShapeDtypeStruct(u3.shape, BF16),
        scratch_shapes=[
            pltpu.VMEM((rows, 2 * N_STATE), F32),
            pltpu.VMEM((bsz, 2 * N_STATE), F32),
        ],
        compiler_params=pltpu.CompilerParams(
            dimension_semantics=("arbitrary",), vmem_limit_bytes=VMEM_LIMIT),
        name="s5_scan",
    )(u3, perm, perm_t, bmat, a_re, a_im, cmat, dvec)


def _merge_kernel(x_ref, oa_ref, gy_ref, g1_ref, wg_ref, woa_ref, wglu_ref, wout_ref,
                  g2_ref, wrh_ref, wrl_ref, br_ref, x2_ref, h2_ref, route_ref, cnt_ref):
    for half in range(TM_MERGE // TM_DISP):
        rs = slice(half * TM_DISP, (half + 1) * TM_DISP)
        _merge_rows(x_ref[rs, :], oa_ref[rs, :], gy_ref[rs, :], g1_ref, wg_ref, woa_ref,
                    wglu_ref, wout_ref, g2_ref, wrh_ref, wrl_ref, br_ref,
                    x2_ref.at[rs, :], h2_ref.at[rs, :], route_ref.at[rs, :],
                    cnt_ref.at[half])


def _merge_rows(x, o_in, gy, g1_ref, wg_ref, woa_ref, wglu_ref, wout_ref, g2_ref, wrh_ref,
                wrl_ref, br_ref, x2_ref, h2_ref, route_ref, cnt_ref):
    ms = jnp.mean(x * x, axis=-1, keepdims=True)
    h = (x * lax.rsqrt(ms + EPS) * g1_ref[...]).astype(BF16)
    o_a = jnp.dot(o_in, woa_ref[...], preferred_element_type=F32)
    z_lin = jnp.dot(gy, wglu_ref[:, 0:D_MODEL], preferred_element_type=F32)
    z_gate = jnp.dot(gy, wglu_ref[:, D_MODEL:2 * D_MODEL], preferred_element_type=F32)
    o_s = z_lin * jax.nn.sigmoid(z_gate)
    gate_a = jax.nn.sigmoid(jnp.dot(h, wg_ref[:, 0:D_MODEL], preferred_element_type=F32))
    merged = gate_a * o_a
    gate_s = jax.nn.sigmoid(
        jnp.dot(h, wg_ref[:, D_MODEL:2 * D_MODEL], preferred_element_type=F32))
    merged = merged + gate_s * o_s
    x2 = x + jnp.dot(merged.astype(BF16), wout_ref[...], preferred_element_type=F32)
    x2_ref[...] = x2

    ms2 = jnp.mean(x2 * x2, axis=-1, keepdims=True)
    h2 = x2 * lax.rsqrt(ms2 + EPS) * g2_ref[...]
    h2_hi = h2.astype(BF16)
    h2_lo = (h2 - h2_hi.astype(F32)).astype(BF16)
    h2_ref[...] = h2_hi

    logits = (jnp.dot(h2_hi, wrh_ref[...], preferred_element_type=F32)
              + jnp.dot(h2_lo, wrh_ref[...], preferred_element_type=F32)
              + jnp.dot(h2_hi, wrl_ref[...], preferred_element_type=F32)
              + br_ref[...])
    lane = lax.broadcasted_iota(jnp.int32, logits.shape, 1).astype(F32)
    is_grp = (lane >= N_EXPERTS) & (lane < N_EXPERTS + N_EXPERT_GROUPS)
    gl = jnp.where(is_grp, logits, NEG_BIG)
    gmax = jnp.max(gl, axis=-1, keepdims=True)
    gsum = jnp.sum(jnp.where(is_grp, jnp.exp(gl - gmax), 0.0), axis=-1, keepdims=True)
    p_grp = 1.0 / gsum
    big = float(4 * LANES)
    grp = jnp.min(jnp.where(is_grp & (gl == gmax), lane, big), axis=-1,
                  keepdims=True) - N_EXPERTS
    sel = logits
    for g in range(1, N_EXPERT_GROUPS):
        rolled = pltpu.roll(logits, LANES - g * EXPERTS_PER_GROUP, axis=1)
        sel = jnp.where(grp == g, rolled, sel)
    in_grp = lane < EXPERTS_PER_GROUP
    es = jnp.where(in_grp, sel, NEG_BIG)
    top1 = jnp.max(es, axis=-1, keepdims=True)
    i1 = jnp.min(jnp.where(in_grp & (es == top1), lane, big), axis=-1, keepdims=True)
    es2 = jnp.where(lane == i1, NEG_BIG, es)
    top2 = jnp.max(es2, axis=-1, keepdims=True)
    i2 = jnp.min(jnp.where(in_grp & (lane != i1) & (es2 == top2), lane, big), axis=-1,
                 keepdims=True)
    e2 = jnp.exp(top2 - top1)
    w1 = p_grp / (1.0 + e2)
    w2 = p_grp * e2 / (1.0 + e2)
    route_ref[...] = (jnp.where(lane == i1, w1, 0.0) + jnp.where(lane == i2, w2, 0.0)
                      + jnp.where(lane == ROUTE_GROUP_LANE, grp, 0.0))
    grp_onehot = jnp.where((lane == grp) & (lane < N_EXPERT_GROUPS), 1.0, 0.0)
    cnt_ref[...] = jnp.sum(grp_onehot, axis=0, keepdims=True)


def _merge(x2d, o_attn, gy, g1, w_gates, w_oa, w_glu, w_out, g2, wr_hi, wr_lo, b_r):
    t = x2d.shape[0]
    tok = lambda i: (i, 0)
    return pl.pallas_call(
        _merge_kernel,
        grid=(t // TM_MERGE,),
        in_specs=[
            pl.BlockSpec((TM_MERGE, D_MODEL), tok),
            pl.BlockSpec((TM_MERGE, ATTN_WIDTH), tok),
            pl.BlockSpec((TM_MERGE, SSM_WIDTH), tok),
            _const_spec((1, D_MODEL)),
            _const_spec((D_MODEL, 2 * D_MODEL)),
            _const_spec((ATTN_WIDTH, D_MODEL)),
            _const_spec((SSM_WIDTH, 2 * D_MODEL)),
            _const_spec((D_MODEL, D_MODEL)),
            _const_spec((1, D_MODEL)),
            _const_spec((D_MODEL, LANES)),
            _const_spec((D_MODEL, LANES)),
            _const_spec((1, LANES)),
        ],
        out_specs=[
            pl.BlockSpec((TM_MERGE, D_MODEL), tok),
            pl.BlockSpec((TM_MERGE, D_MODEL), tok),
            pl.BlockSpec((TM_MERGE, LANES), tok),
            pl.BlockSpec((TM_MERGE // TM_DISP, 1, LANES), lambda i: (i, 0, 0)),
        ],
        out_shape=[
            jax.ShapeDtypeStruct((t, D_MODEL), F32),
            jax.ShapeDtypeStruct((t, D_MODEL), BF16),
            jax.ShapeDtypeStruct((t, LANES), F32),
            jax.ShapeDtypeStruct((t // TM_DISP, 1, LANES), F32),
        ],
        compiler_params=pltpu.CompilerParams(
            dimension_semantics=("arbitrary",), vmem_limit_bytes=VMEM_LIMIT),
        name="merge",
    )(x2d, o_attn, gy, g1, w_gates, w_oa, w_glu, w_out, g2, wr_hi, wr_lo, b_r)


def _moe_layout(t):
    ntiles = t // TM_DISP
    pad_rows = ntiles * (ROW_ALIGN - 1)
    cap = -(-(t + pad_rows + TM_DISP + TS_MOE) // TS_MOE) * TS_MOE
    nsteps = -(-(t + N_EXPERT_GROUPS * (pad_rows + TM_DISP)) // TS_MOE) + N_EXPERT_GROUPS
    return ntiles, cap, nsteps


def _dispatch_kernel(base_ref, tail_ref, h_ref, route_ref, tri_ref, xs_ref, xr_ref,
                     slot_ref, buf_ref, rbuf_ref, zbuf_ref, zrbuf_ref, sem_ref, tsem_ref):
    i = pl.program_id(0)
    last = pl.num_programs(0) - 1
    cur = i % 2
    tm = TM_DISP
    route = route_ref[...]
    lane = lax.broadcasted_iota(jnp.int32, route.shape, 1)
    lane_f = lane.astype(F32)
    g_col = jnp.sum(jnp.where(lane == ROUTE_GROUP_LANE, route, 0.0), axis=-1, keepdims=True)
    onehot = jnp.where((lane_f == g_col) & (lane < N_EXPERT_GROUPS), 1.0, 0.0).astype(BF16)
    rank_all = jnp.dot(tri_ref[...], onehot, preferred_element_type=F32)
    r_col = jnp.sum(jnp.where(lane_f == g_col, rank_all, 0.0), axis=-1, keepdims=True)
    slot_ref[...] = jnp.where(lane == 0, g_col * tm + r_col, 0.0)
    rg = (jnp.where(lane == 0, r_col, 0.0) + jnp.where(lane == 1, g_col, 0.0)).astype(BF16)
    eye8 = (lax.broadcasted_iota(jnp.int32, (SUBLANES, LANES), 0)
            == lax.broadcasted_iota(jnp.int32, (SUBLANES, LANES), 1)).astype(BF16)
    rows = lax.dot_general(eye8, rg, (((1,), (1,)), ((), ())), preferred_element_type=F32)
    r_row = rows[0:1, :]
    g_row = rows[1:2, :]
    sub = lax.broadcasted_iota(jnp.int32, (tm, tm), 0).astype(F32)
    r_hi = route.astype(BF16)
    rem = route - r_hi.astype(F32)
    r_mid = rem.astype(BF16)
    r_lo = (rem - r_mid.astype(F32)).astype(BF16)
    h = h_ref[...]
    for g in range(N_EXPERT_GROUPS):
        perm = jnp.where((r_row == sub) & (g_row == float(g)), 1.0, 0.0).astype(BF16)
        rs = slice(g * tm, (g + 1) * tm)
        buf_ref[cur, rs, :] = jnp.dot(perm, h, preferred_element_type=F32).astype(BF16)
        rbuf_ref[cur, rs, :] = (jnp.dot(perm, r_hi, preferred_element_type=F32)
                                + jnp.dot(perm, r_mid, preferred_element_type=F32)
                                + jnp.dot(perm, r_lo, preferred_element_type=F32))

    def copies(step, slot):
        out = []
        for g in range(N_EXPERT_GROUPS):
            dst = pl.multiple_of(base_ref[step * N_EXPERT_GROUPS + g], ROW_ALIGN)
            out.append(pltpu.make_async_copy(
                buf_ref.at[slot, pl.ds(g * tm, tm)], xs_ref.at[pl.ds(dst, tm)],
                sem_ref.at[slot, g]))
            out.append(pltpu.make_async_copy(
                rbuf_ref.at[slot, pl.ds(g * tm, tm)], xr_ref.at[pl.ds(dst, tm)],
                sem_ref.at[slot, N_EXPERT_GROUPS + g]))
        return out

    @pl.when(i > 0)
    def _wait_prev():
        for c in copies(i - 1, 1 - cur):
            c.wait()

    for c in copies(i, cur):
        c.start()

    @pl.when(i == last)
    def _finish():
        for c in copies(i, cur):
            c.wait()
        zbuf_ref[...] = jnp.zeros(zbuf_ref.shape, BF16)
        zrbuf_ref[...] = jnp.zeros(zrbuf_ref.shape, F32)
        tails = []
        for g in range(N_EXPERT_GROUPS):
            dst = pl.multiple_of(tail_ref[g], ROW_ALIGN)
            tails.append(pltpu.make_async_copy(zbuf_ref, xs_ref.at[pl.ds(dst, TS_MOE)],
                                               tsem_ref.at[g]))
            tails.append(pltpu.make_async_copy(zrbuf_ref, xr_ref.at[pl.ds(dst, TS_MOE)],
                                               tsem_ref.at[N_EXPERT_GROUPS + g]))
        for c in tails:
            c.start()
        for c in tails:
            c.wait()


def _dispatch(h2, route, base, tail, tri, rows):
    t = h2.shape[0]
    tm = TM_DISP
    tok = lambda i, *_: (i, 0)
    grid_spec = pltpu.PrefetchScalarGridSpec(
        num_scalar_prefetch=2,
        grid=(t // tm,),
        in_specs=[
            pl.BlockSpec((tm, D_MODEL), tok),
            pl.BlockSpec((tm, LANES), tok),
            pl.BlockSpec((tm, tm), lambda i, *_: (0, 0)),
        ],
        out_specs=[
            pl.BlockSpec(memory_space=pl.ANY),
            pl.BlockSpec(memory_space=pl.ANY),
            pl.BlockSpec((tm, LANES), tok),
        ],
        scratch_shapes=[
            pltpu.VMEM((2, N_EXPERT_GROUPS * tm, D_MODEL), BF16),
            pltpu.VMEM((2, N_EXPERT_GROUPS * tm, LANES), F32),
            pltpu.VMEM((TS_MOE, D_MODEL), BF16),
            pltpu.VMEM((TS_MOE, LANES), F32),
            pltpu.SemaphoreType.DMA((2, 2 * N_EXPERT_GROUPS)),
            pltpu.SemaphoreType.DMA((2 * N_EXPERT_GROUPS,)),
        ],
    )
    return pl.pallas_call(
        _dispatch_kernel,
        grid_spec=grid_spec,
        out_shape=[
            jax.ShapeDtypeStruct((rows, D_MODEL), BF16),
            jax.ShapeDtypeStruct((rows, LANES), F32),
            jax.ShapeDtypeStruct((t, LANES), F32),
        ],
        compiler_params=pltpu.CompilerParams(
            dimension_semantics=("arbitrary",), vmem_limit_bytes=VMEM_LIMIT),
        name="dispatch",
    )(base, tail, h2, route, tri)


def _moe_kernel(blk_ref, grp_ref, nvalid_ref, xs_ref, xr_ref, wg_ref, wu_ref, wd_ref,
                ys_ref, a_ref):
    s = pl.program_id(0)

    @pl.when(s < nvalid_ref[0])
    def _compute():
        t = xs_ref[...]
        comb = xr_ref[...]
        lane = lax.broadcasted_iota(jnp.int32, comb.shape, 1)
        for e in range(EXPERTS_PER_GROUP):
            sl = slice(e * D_EXPERT, (e + 1) * D_EXPERT)
            hg = jnp.dot(t, wg_ref[e], preferred_element_type=F32)
            hu = jnp.dot(t, wu_ref[e], preferred_element_type=F32)
            cw = jnp.sum(jnp.where(lane == e, comb, 0.0), axis=-1, keepdims=True)
            a_ref[:, sl] = (hg * jax.nn.sigmoid(hg) * hu * cw).astype(BF16)
        ys_ref[...] = jnp.dot(a_ref[...], wd_ref[...],
                              preferred_element_type=F32).astype(BF16)

    @pl.when(s >= nvalid_ref[0])
    def _spare():
        ys_ref[...] = jnp.zeros(ys_ref.shape, BF16)


def _moe(xs, xr, blk, grp, nvalid, wg, wu, wd):
    ge = EXPERTS_PER_GROUP * D_EXPERT
    row = lambda s, blk, grp, nv: (blk[s], 0)
    wsel = lambda s, blk, grp, nv: (grp[s], 0, 0)
    grid_spec = pltpu.PrefetchScalarGridSpec(
        num_scalar_prefetch=3,
        grid=(blk.shape[0],),
        in_specs=[
            pl.BlockSpec((TS_MOE, D_MODEL), row),
            pl.BlockSpec((TS_MOE, LANES), row),
            pl.BlockSpec((None, EXPERTS_PER_GROUP, D_MODEL, D_EXPERT),
                         lambda s, blk, grp, nv: (grp[s], 0, 0, 0)),
            pl.BlockSpec((None, EXPERTS_PER_GROUP, D_MODEL, D_EXPERT),
                         lambda s, blk, grp, nv: (grp[s], 0, 0, 0)),
            pl.BlockSpec((None, ge, D_MODEL), wsel),
        ],
        out_specs=pl.BlockSpec((TS_MOE, D_MODEL), row),
        scratch_shapes=[pltpu.VMEM((TS_MOE, ge), BF16)],
    )
    return pl.pallas_call(
        _moe_kernel,
        grid_spec=grid_spec,
        out_shape=jax.ShapeDtypeStruct(xs.shape, BF16),
        compiler_params=pltpu.CompilerParams(
            dimension_semantics=("arbitrary",), vmem_limit_bytes=VMEM_LIMIT),
        name="moe",
    )(blk, grp, nvalid, xs, xr, wg, wu, wd)


def _combine_kernel(base_ref, x2_ref, slot_ref, ys_ref, o_ref, stage_ref, sem_ref):
    i = pl.program_id(0)
    n = pl.num_programs(0)
    cur = i % 2
    tm = TM_DISP

    def copies(step, slot):
        out = []
        for g in range(N_EXPERT_GROUPS):
            src = pl.multiple_of(base_ref[step * N_EXPERT_GROUPS + g], ROW_ALIGN)
            out.append(pltpu.make_async_copy(
                ys_ref.at[pl.ds(src, tm)], stage_ref.at[slot, pl.ds(g * tm, tm)],
                sem_ref.at[slot, g]))
        return out

    @pl.when(i == 0)
    def _first():
        for c in copies(0, 0):
            c.start()

    @pl.when(i + 1 < n)
    def _prefetch():
        for c in copies(i + 1, 1 - cur):
            c.start()

    for c in copies(i, cur):
        c.wait()
    slot = slot_ref[...]
    lane128 = lax.broadcasted_iota(jnp.int32, slot.shape, 1)
    slot_col = jnp.sum(jnp.where(lane128 == 0, slot, 0.0), axis=-1, keepdims=True)
    lane = lax.broadcasted_iota(jnp.int32, (tm, N_EXPERT_GROUPS * tm), 1).astype(F32)
    pick = jnp.where(lane == slot_col, 1.0, 0.0).astype(BF16)
    o_ref[...] = x2_ref[...] + jnp.dot(pick, stage_ref[cur], preferred_element_type=F32)


def _combine(x2, slot, ys, base):
    t = x2.shape[0]
    tm = TM_DISP
    tok = lambda i, *_: (i, 0)
    grid_spec = pltpu.PrefetchScalarGridSpec(
        num_scalar_prefetch=1,
        grid=(t // tm,),
        in_specs=[
            pl.BlockSpec((tm, D_MODEL), tok),
            pl.BlockSpec((tm, LANES), tok),
            pl.BlockSpec(memory_space=pl.ANY),
        ],
        out_specs=pl.BlockSpec((tm, D_MODEL), tok),
        scratch_shapes=[
            pltpu.VMEM((2, N_EXPERT_GROUPS * tm, D_MODEL), BF16),
            pltpu.SemaphoreType.DMA((2, N_EXPERT_GROUPS)),
        ],
    )
    return pl.pallas_call(
        _combine_kernel,
        grid_spec=grid_spec,
        out_shape=jax.ShapeDtypeStruct((t, D_MODEL), F32),
        compiler_params=pltpu.CompilerParams(
            dimension_semantics=("arbitrary",), vmem_limit_bytes=VMEM_LIMIT),
        name="combine",
    )(base, x2, slot, ys)


def _sparse_moe(h2, route, cnt, x2, wg, wu, wd):
    t = h2.shape[0]
    ntiles, cap, nsteps = _moe_layout(t)
    rows = N_EXPERT_GROUPS * cap + TS_MOE
    counts = cnt[:, 0, :N_EXPERT_GROUPS].astype(jnp.int32)
    padded = (counts + (ROW_ALIGN - 1)) // ROW_ALIGN * ROW_ALIGN
    region = jnp.arange(N_EXPERT_GROUPS, dtype=jnp.int32) * cap
    start = jnp.cumsum(padded, axis=0) - padded
    base = (start + region[None, :]).reshape(-1)
    written = start[-1] + TM_DISP
    tail = written + region
    ntile_g = (written + (TS_MOE - 1)) // TS_MOE
    first = jnp.cumsum(ntile_g) - ntile_g
    nvalid = jnp.sum(ntile_g).reshape(1)
    s = jnp.arange(nsteps, dtype=jnp.int32)
    grp = jnp.minimum(jnp.sum(s[:, None] >= (first + ntile_g)[None, :], axis=1),
                      N_EXPERT_GROUPS - 1).astype(jnp.int32)
    blk = jnp.where(s < nvalid[0], region[grp] // TS_MOE + s - first[grp],
                    N_EXPERT_GROUPS * cap // TS_MOE).astype(jnp.int32)
    tri = (jnp.arange(TM_DISP)[:, None] > jnp.arange(TM_DISP)[None, :]).astype(BF16)
    xs, xr, slot = _dispatch(h2, route, base, tail, tri, rows)
    ys = _moe(xs, xr, blk, grp, nvalid, wg, wu, wd)
    return _combine(x2, slot, ys, base)


def _rope_tables(positions):
    inv = ROPE_THETA ** (-jnp.arange(0, ROPE_DIM, 2, dtype=F32) / ROPE_DIM)
    ang = positions.astype(F32).reshape(-1, 1) * inv
    cos, sin = jnp.cos(ang), jnp.sin(ang)
    t = ang.shape[0]
    ones = jnp.ones((t, HEAD_DIM - ROPE_DIM), F32)
    zeros = jnp.zeros((t, HEAD_DIM - ROPE_DIM), F32)
    cos64 = jnp.concatenate([cos, cos, ones], axis=-1)
    sin64 = jnp.concatenate([-sin, sin, zeros], axis=-1)
    return jnp.tile(cos64, (1, 2)), jnp.tile(sin64, (1, 2))


def _s5_params(lam_re, lam_im, log_dt, b_re, b_im, c_re, c_im, bsz):
    dt = jnp.exp(log_dt)[:, None]
    mag = jnp.exp(lam_re * dt)
    lb_re = mag * jnp.cos(lam_im * dt)
    lb_im = mag * jnp.sin(lam_im * dt)
    den = lam_re * lam_re + lam_im * lam_im
    k_re = ((lb_re - 1.0) * lam_re + lb_im * lam_im) / den
    k_im = (lb_im * lam_re - (lb_re - 1.0) * lam_im) / den
    bb_re = k_re[..., None] * b_re - k_im[..., None] * b_im
    bb_im = k_re[..., None] * b_im + k_im[..., None] * b_re
    eye = jnp.eye(SSM_GROUPS, dtype=F32)
    blk_b = lambda m: jnp.einsum('gph,gk->ghkp', m, eye).reshape(SSM_WIDTH, N_STATE)
    blk_c = lambda m: jnp.einsum('ghp,gk->gpkh', m, eye).reshape(N_STATE, SSM_WIDTH)
    bmat = jnp.concatenate([blk_b(bb_re), blk_b(bb_im)], axis=1).astype(BF16)
    cmat = jnp.concatenate([blk_c(c_re), blk_c(-c_im)], axis=0).astype(BF16)
    a_re = jnp.broadcast_to(lb_re.reshape(1, N_STATE), (bsz, N_STATE))
    a_im = jnp.broadcast_to(lb_im.reshape(1, N_STATE), (bsz, N_STATE))
    return bmat, a_re, a_im, cmat


def _time_major_perm(bsz, chunk):
    r = jnp.arange(bsz * chunk)
    src = (r % bsz) * chunk + r // bsz
    perm = (src[:, None] == jnp.arange(bsz * chunk)[None, :]).astype(BF16)
    return perm, perm.T


def kernel(x, positions, norm_mix_g, w_in, q_norm_g, k_norm_g, lambda_q1, lambda_k1, lambda_q2, lambda_k2, subln_g, w_o_attn, ssm_lambda_re, ssm_lambda_im, ssm_log_dt, ssm_b_re, ssm_b_im, ssm_c_re, ssm_c_im, ssm_d, w_glu, w_out, norm_ffn_g, w_router_group, b_router_group, w_router_expert, b_router_expert, w_expert_gate, w_expert_up, w_expert_down):
    bsz, seq, _ = x.shape
    assert bsz == SUBLANES and seq % TQ == 0 and seq % SSM_CHUNK == 0
    assert norm_mix_g.shape[0] == 1
    t = bsz * seq
    x2d = x.reshape(t, D_MODEL)
    l = 0

    cos_t, sin_t = _rope_tables(positions)
    w_qkvu = w_in[l][:, :QKVU_COLS].astype(BF16)
    w_gates = w_in[l][:, QKVU_COLS:].astype(BF16)
    qg = jnp.tile(q_norm_g[l].reshape(1, HEAD_DIM), (1, 2))
    kg = jnp.tile(k_norm_g[l].reshape(1, HEAD_DIM), (1, 2))
    q, k, v, u = _in_proj(x2d, norm_mix_g[l].reshape(1, D_MODEL), w_qkvu, cos_t, sin_t, qg, kg)

    lam = (jnp.exp(jnp.sum(lambda_q1[l] * lambda_k1[l]))
           - jnp.exp(jnp.sum(lambda_q2[l] * lambda_k2[l])) + LAM_INIT).reshape(1)
    o_attn = _diff_attn(q, k, v, lam, subln_g[l].reshape(1, V_DIM), bsz, seq)

    bmat, a_re, a_im, cmat = _s5_params(
        ssm_lambda_re[l], ssm_lambda_im[l], ssm_log_dt[l], ssm_b_re[l], ssm_b_im[l],
        ssm_c_re[l], ssm_c_im[l], bsz)
    perm, perm_t = _time_major_perm(bsz, SSM_CHUNK)
    gy = _s5_scan(u.reshape(bsz, seq, SSM_WIDTH), perm, perm_t, bmat, a_re, a_im, cmat,
                  ssm_d[l].reshape(1, SSM_WIDTH)).reshape(t, SSM_WIDTH)

    w_r = jnp.concatenate(
        [w_router_expert[l].reshape(D_MODEL, N_EXPERTS), w_router_group[l],
         jnp.zeros((D_MODEL, LANES - N_EXPERTS - N_EXPERT_GROUPS), F32)], axis=1)
    b_r = jnp.concatenate(
        [b_router_expert[l].reshape(N_EXPERTS), b_router_group[l],
         jnp.zeros((LANES - N_EXPERTS - N_EXPERT_GROUPS,), F32)]).reshape(1, LANES)
    wr_hi = w_r.astype(BF16)
    wr_lo = (w_r - wr_hi.astype(F32)).astype(BF16)
    x2, h2, route, cnt = _merge(
        x2d, o_attn, gy, norm_mix_g[l].reshape(1, D_MODEL), w_gates,
        w_o_attn[l].astype(BF16), w_glu[l].astype(BF16), w_out[l].astype(BF16),
        norm_ffn_g[l].reshape(1, D_MODEL), wr_hi, wr_lo, b_r)

    ge = EXPERTS_PER_GROUP * D_EXPERT
    wg = w_expert_gate[l].astype(BF16)
    wu = w_expert_up[l].astype(BF16)
    wd = w_expert_down[l].astype(BF16).reshape(N_EXPERT_GROUPS, ge, D_MODEL)
    out = _sparse_moe(h2, route, cnt, x2, wg, wu, wd)
    return out.reshape(bsz, seq, D_MODEL)
```

```python
import functools
import math

import jax
import jax.numpy as jnp
from jax import lax
from jax.experimental import pallas as pl
from jax.experimental.pallas import tpu as pltpu

F32 = jnp.float32
BF16 = jnp.bfloat16

D_MODEL = 1024
N_HEADS = 4
HEAD_DIM = 64
V_DIM = 2 * HEAD_DIM
ATTN_WIDTH = N_HEADS * V_DIM
ROPE_THETA = 500000.0
ROPE_DIM = HEAD_DIM // 4
ROPE_HALF = ROPE_DIM // 2
SSM_WIDTH = D_MODEL // 2
SSM_GROUP = 16
SSM_GROUPS = SSM_WIDTH // SSM_GROUP
SSM_STATE = 64
N_STATE = SSM_GROUPS * SSM_STATE
N_EXPERT_GROUPS = 4
EXPERTS_PER_GROUP = 8
N_EXPERTS = N_EXPERT_GROUPS * EXPERTS_PER_GROUP
D_EXPERT = D_MODEL // 4
EPS = 1e-6
LAM_INIT = 0.8 - 0.6 * math.exp(-0.3 * 0)
QKVU_COLS = 4 * ATTN_WIDTH
LANES = 128
SUBLANES = 8
NEG_BIG = -1e30
LOG2_E = math.log2(math.e)

VMEM_LIMIT = 48 * 1024 * 1024

TM_IN = 512
TQ = 512
SSM_CHUNK = 64
SCAN_LANES = 512
TM_MERGE = 512
TM_DISP = 256
TS_MOE = 512
ROW_ALIGN = 16


def _const_spec(shape):
    return pl.BlockSpec(shape, lambda *_: (0,) * len(shape))


def _in_proj_kernel(x_ref, g_ref, w_ref, cos_ref, sin_ref, qg_ref, kg_ref,
                    q_ref, k_ref, v_ref, u_ref):
    x = x_ref[...]
    ms = jnp.mean(x * x, axis=-1, keepdims=True)
    h = (x * lax.rsqrt(ms + EPS) * g_ref[...]).astype(BF16)
    cos_t = cos_ref[...]
    sin_t = sin_ref[...]
    lane = lax.broadcasted_iota(jnp.int32, (x.shape[0], LANES), 1)
    first_comp = lane < HEAD_DIM
    low_half = (lane % HEAD_DIM) < ROPE_HALF

    def norm_rope(blk, gain, scale):
        sq = blk * blk
        s_all = jnp.sum(sq, axis=-1, keepdims=True)
        s_lo = jnp.sum(jnp.where(first_comp, sq, 0.0), axis=-1, keepdims=True)
        ssum = jnp.where(first_comp, s_lo, s_all - s_lo)
        nb = blk * lax.rsqrt(ssum * (1.0 / HEAD_DIM) + EPS) * gain
        up = pltpu.roll(nb, LANES - ROPE_HALF, axis=1)
        dn = pltpu.roll(nb, ROPE_HALF, axis=1)
        partner = jnp.where(low_half, up, dn)
        return ((nb * cos_t + partner * sin_t) * scale).astype(BF16)

    for j in range(0, N_HEADS, 2):
        sl2 = slice(j * LANES, (j + 2) * LANES)
        qb = jnp.dot(h, w_ref[:, sl2], preferred_element_type=F32)
        kb = jnp.dot(h, w_ref[:, ATTN_WIDTH + j * LANES:ATTN_WIDTH + (j + 2) * LANES],
                     preferred_element_type=F32)
        for jj in range(2):
            sl = slice((j + jj) * LANES, (j + jj + 1) * LANES)
            half = slice(jj * LANES, (jj + 1) * LANES)
            q_ref[:, sl] = norm_rope(qb[:, half], qg_ref[...], LOG2_E * HEAD_DIM ** -0.5)
            k_ref[:, sl] = norm_rope(kb[:, half], kg_ref[...], 1.0)
    v_ref[...] = jnp.dot(h, w_ref[:, 2 * ATTN_WIDTH:3 * ATTN_WIDTH],
                         preferred_element_type=F32).astype(BF16)
    u_ref[...] = jnp.dot(h, w_ref[:, 3 * ATTN_WIDTH:4 * ATTN_WIDTH],
                         preferred_element_type=F32).astype(BF16)


def _in_proj(x2d, g, w, cos_t, sin_t, qg, kg):
    t = x2d.shape[0]
    tok = lambda i: (i, 0)
    out = jax.ShapeDtypeStruct((t, ATTN_WIDTH), BF16)
    return pl.pallas_call(
        _in_proj_kernel,
        grid=(t // TM_IN,),
        in_specs=[
            pl.BlockSpec((TM_IN, D_MODEL), tok),
            _const_spec((1, D_MODEL)),
            _const_spec((D_MODEL, QKVU_COLS)),
            pl.BlockSpec((TM_IN, LANES), tok),
            pl.BlockSpec((TM_IN, LANES), tok),
            _const_spec((1, LANES)),
            _const_spec((1, LANES)),
        ],
        out_specs=[pl.BlockSpec((TM_IN, ATTN_WIDTH), tok)] * 4,
        out_shape=[out] * 4,
        compiler_params=pltpu.CompilerParams(
            dimension_semantics=("arbitrary",), vmem_limit_bytes=VMEM_LIMIT),
        name="in_proj",
    )(x2d, g, w, cos_t, sin_t, qg, kg)


def _attn_kernel(qi_ref, kj_ref, lam_ref, q_ref, k_ref, v_ref, sg_ref, o_ref,
                 qs_ref, m_ref, l_ref, acc_ref):
    step = pl.program_id(1)
    qi = qi_ref[step]
    kj = kj_ref[step]

    @pl.when(kj == 0)
    def _init():
        for h in range(N_HEADS):
            q = q_ref[:, h * LANES:(h + 1) * LANES]
            lane = lax.broadcasted_iota(jnp.int32, q.shape, 1)
            zero = jnp.zeros_like(q)
            qs_ref[h, 0:TQ, :] = jnp.where(lane < HEAD_DIM, q, zero)
            qs_ref[h, TQ:2 * TQ, :] = jnp.where(lane < HEAD_DIM, zero, q)
        m_ref[...] = jnp.full(m_ref.shape, NEG_BIG, F32)
        l_ref[...] = jnp.zeros(l_ref.shape, F32)
        acc_ref[...] = jnp.zeros(acc_ref.shape, F32)

    def update(h, masked):
        hs = slice(h * LANES, (h + 1) * LANES)
        s = lax.dot_general(qs_ref[h], k_ref[:, hs], (((1,), (1,)), ((), ())),
                            preferred_element_type=F32)
        if masked:
            row = lax.broadcasted_iota(jnp.int32, s.shape, 0) % TQ
            col = lax.broadcasted_iota(jnp.int32, s.shape, 1)
            s = jnp.where(col <= row, s, NEG_BIG)
        m_old = m_ref[h]
        m_new = jnp.maximum(m_old, jnp.max(s, axis=-1, keepdims=True))
        alpha = jnp.exp2(m_old - m_new)
        p = jnp.exp2(s - jnp.concatenate([m_new] * (TQ // LANES), axis=1))
        psum = p[:, 0:LANES]
        for c in range(1, TQ // LANES):
            psum = psum + p[:, c * LANES:(c + 1) * LANES]
        l_ref[h] = alpha * l_ref[h] + psum
        acc_ref[h] = alpha * acc_ref[h] + jnp.dot(
            p.astype(BF16), v_ref[:, hs], preferred_element_type=F32)
        m_ref[h] = m_new

    @pl.when(kj < qi)
    def _off_diag():
        for h in range(N_HEADS):
            update(h, False)

    @pl.when(kj == qi)
    def _diag():
        for h in range(N_HEADS):
            update(h, True)
            o = acc_ref[h] / jnp.sum(l_ref[h], axis=-1, keepdims=True)
            d = o[0:TQ, :] - lam_ref[0] * o[TQ:2 * TQ, :]
            ms = jnp.mean(d * d, axis=-1, keepdims=True)
            d = d * lax.rsqrt(ms + EPS) * sg_ref[...] * (1.0 - LAM_INIT)
            o_ref[:, h * LANES:(h + 1) * LANES] = d.astype(BF16)


def _diff_attn(q, k, v, lam, subln_g, bsz, seq):
    nq = seq // TQ
    pairs = [(i, j) for i in range(nq) for j in range(i + 1)]
    qi = jnp.asarray([p[0] for p in pairs], jnp.int32)
    kj = jnp.asarray([p[1] for p in pairs], jnp.int32)
    q_map = lambda b, s, qi, kj: (b * nq + qi[s], 0)
    k_map = lambda b, s, qi, kj: (b * nq + kj[s], 0)
    grid_spec = pltpu.PrefetchScalarGridSpec(
        num_scalar_prefetch=2,
        grid=(bsz, len(pairs)),
        in_specs=[
            pl.BlockSpec(memory_space=pltpu.SMEM),
            pl.BlockSpec((TQ, ATTN_WIDTH), q_map),
            pl.BlockSpec((TQ, ATTN_WIDTH), k_map),
            pl.BlockSpec((TQ, ATTN_WIDTH), k_map),
            pl.BlockSpec((1, LANES), lambda b, s, qi, kj: (0, 0)),
        ],
        out_specs=pl.BlockSpec((TQ, ATTN_WIDTH), q_map),
        scratch_shapes=[
            pltpu.VMEM((N_HEADS, 2 * TQ, LANES), BF16),
            pltpu.VMEM((N_HEADS, 2 * TQ, LANES), F32),
            pltpu.VMEM((N_HEADS, 2 * TQ, LANES), F32),
            pltpu.VMEM((N_HEADS, 2 * TQ, LANES), F32),
        ],
    )
    return pl.pallas_call(
        _attn_kernel,
        grid_spec=grid_spec,
        out_shape=jax.ShapeDtypeStruct((bsz * seq, ATTN_WIDTH), BF16),
        compiler_params=pltpu.CompilerParams(
            dimension_semantics=("arbitrary", "arbitrary"),
            vmem_limit_bytes=VMEM_LIMIT),
        name="diff_attn",
    )(qi, kj, lam, q, k, v, subln_g)


def _gelu_tanh(x):
    c = math.sqrt(2.0 / math.pi)
    return 0.5 * x * (1.0 + jnp.tanh(c * (x + 0.044715 * (x * x * x))))


def _s5_kernel(u_ref, p_ref, pt_ref, b_ref, are_ref, aim_ref, c_ref, d_ref, o_ref,
               st_ref, state_ref):
    bsz, chunk, width = u_ref.shape
    rows = bsz * chunk

    @pl.when(pl.program_id(0) == 0)
    def _init():
        state_ref[...] = jnp.zeros(state_ref.shape, F32)

    u_bt = u_ref[...].reshape(rows, width)
    u_tm = jnp.dot(p_ref[...], u_bt, preferred_element_type=F32)
    u_tm_b = u_tm.astype(BF16)

    slab = SCAN_LANES // SSM_STATE * SSM_GROUP
    g_parts = []
    for ch in range(N_STATE // SCAN_LANES):
        re = slice(ch * SCAN_LANES, (ch + 1) * SCAN_LANES)
        im = slice(N_STATE + ch * SCAN_LANES, N_STATE + (ch + 1) * SCAN_LANES)
        cs = slice(ch * slab, (ch + 1) * slab)
        st_ref[:, re] = jnp.dot(u_tm_b[:, cs], b_ref[cs, re], preferred_element_type=F32)
        st_ref[:, im] = jnp.dot(u_tm_b[:, cs], b_ref[cs, im], preferred_element_type=F32)
        a_re = are_ref[:, re]
        a_im = aim_ref[:, re]

        def body(t, carry, re=re, im=im, a_re=a_re, a_im=a_im):
            s_re, s_im = carry
            r0 = pl.multiple_of(t * SUBLANES, SUBLANES)
            n_re = a_re * s_re - a_im * s_im + st_ref[pl.ds(r0, SUBLANES), re]
            n_im = a_re * s_im + a_im * s_re + st_ref[pl.ds(r0, SUBLANES), im]
            st_ref[pl.ds(r0, SUBLANES), re] = n_re
            st_ref[pl.ds(r0, SUBLANES), im] = n_im
            return n_re, n_im

        s_re, s_im = lax.fori_loop(0, chunk, body, (state_ref[:, re], state_ref[:, im]),
                                   unroll=True)
        state_ref[:, re] = s_re
        state_ref[:, im] = s_im
        y = (jnp.dot(st_ref[:, re].astype(BF16), c_ref[re, cs], preferred_element_type=F32)
             + jnp.dot(st_ref[:, im].astype(BF16), c_ref[im, cs], preferred_element_type=F32))
        y = y + d_ref[:, cs] * u_tm[:, cs]
        g_parts.append(_gelu_tanh(y).astype(BF16))

    g_tm = jnp.concatenate(g_parts, axis=1)
    g_bt = jnp.dot(pt_ref[...], g_tm, preferred_element_type=F32)
    o_ref[...] = g_bt.astype(BF16).reshape(bsz, chunk, width)


def _s5_scan(u3, perm, perm_t, bmat, a_re, a_im, cmat, dvec):
    bsz, seq, width = u3.shape
    rows = bsz * SSM_CHUNK
    blk = pl.BlockSpec((bsz, SSM_CHUNK, width), lambda c: (0, c, 0))
    return pl.pallas_call(
        _s5_kernel,
        grid=(seq // SSM_CHUNK,),
        in_specs=[
            blk,
            _const_spec((rows, rows)),
            _const_spec((rows, rows)),
            _const_spec((width, 2 * N_STATE)),
            _const_spec((bsz, N_STATE)),
            _const_spec((bsz, N_STATE)),
            _const_spec((2 * N_STATE, width)),
            _const_spec((1, width)),
        ],
        out_specs=blk,
        out_shape=jax.ShapeDtypeStruct(u3.shape, BF16),
        scratch_shapes=[
            pltpu.VMEM((rows, 2 * N_STATE), F32),
            pltpu.VMEM((bsz, 2 * N_STATE), F32),
        ],
        compiler_params=pltpu.CompilerParams(
            dimension_semantics=("arbitrary",), vmem_limit_bytes=VMEM_LIMIT),
        name="s5_scan",
    )(u3, perm, perm_t, bmat, a_re, a_im, cmat, dvec)


def _merge_kernel(x_ref, oa_ref, gy_ref, g1_ref, wg_ref, woa_ref, wglu_ref, wout_ref,
                  g2_ref, wrh_ref, wrl_ref, br_ref, x2_ref, h2_ref, route_ref, cnt_ref):
    for half in range(TM_MERGE // TM_DISP):
        rs = slice(half * TM_DISP, (half + 1) * TM_DISP)
        _merge_rows(x_ref[rs, :], oa_ref[rs, :], gy_ref[rs, :], g1_ref, wg_ref, woa_ref,
                    wglu_ref, wout_ref, g2_ref, wrh_ref, wrl_ref, br_ref,
                    x2_ref.at[rs, :], h2_ref.at[rs, :], route_ref.at[rs, :],
                    cnt_ref.at[half])


def _merge_rows(x, o_in, gy, g1_ref, wg_ref, woa_ref, wglu_ref, wout_ref, g2_ref, wrh_ref,
                wrl_ref, br_ref, x2_ref, h2_ref, route_ref, cnt_ref):
    ms = jnp.mean(x * x, axis=-1, keepdims=True)
    h = (x * lax.rsqrt(ms + EPS) * g1_ref[...]).astype(BF16)
    o_a = jnp.dot(o_in, woa_ref[...], preferred_element_type=F32)
    z_lin = jnp.dot(gy, wglu_ref[:, 0:D_MODEL], preferred_element_type=F32)
    z_gate = jnp.dot(gy, wglu_ref[:, D_MODEL:2 * D_MODEL], preferred_element_type=F32)
    o_s = z_lin * jax.nn.sigmoid(z_gate)
    gate_a = jax.nn.sigmoid(jnp.dot(h, wg_ref[:, 0:D_MODEL], preferred_element_type=F32))
    merged = gate_a * o_a
    gate_s = jax.nn.sigmoid(
        jnp.dot(h, wg_ref[:, D_MODEL:2 * D_MODEL], preferred_element_type=F32))
    merged = merged + gate_s * o_s
    x2 = x + jnp.dot(merged.astype(BF16), wout_ref[...], preferred_element_type=F32)
    x2_ref[...] = x2

    ms2 = jnp.mean(x2 * x2, axis=-1, keepdims=True)
    h2 = x2 * lax.rsqrt(ms2 + EPS) * g2_ref[...]
    h2_hi = h2.astype(BF16)
    h2_lo = (h2 - h2_hi.astype(F32)).astype(BF16)
    h2_ref[...] = h2_hi

    logits = (jnp.dot(h2_hi, wrh_ref[...], preferred_element_type=F32)
              + jnp.dot(h2_lo, wrh_ref[...], preferred_element_type=F32)
              + jnp.dot(h2_hi, wrl_ref[...], preferred_element_type=F32)
              + br_ref[...])
    lane = lax.broadcasted_iota(jnp.int32, logits.shape, 1).astype(F32)
    is_grp = (lane >= N_EXPERTS) & (lane < N_EXPERTS + N_EXPERT_GROUPS)
    gl = jnp.where(is_grp, logits, NEG_BIG)
    gmax = jnp.max(gl, axis=-1, keepdims=True)
    gsum = jnp.sum(jnp.where(is_grp, jnp.exp(gl - gmax), 0.0), axis=-1, keepdims=True)
    p_grp = 1.0 / gsum
    big = float(4 * LANES)
    grp = jnp.min(jnp.where(is_grp & (gl == gmax), lane, big), axis=-1,
                  keepdims=True) - N_EXPERTS
    sel = logits
    for g in range(1, N_EXPERT_GROUPS):
        rolled = pltpu.roll(logits, LANES - g * EXPERTS_PER_GROUP, axis=1)
        sel = jnp.where(grp == g, rolled, sel)
    in_grp = lane < EXPERTS_PER_GROUP
    es = jnp.where(in_grp, sel, NEG_BIG)
    top1 = jnp.max(es, axis=-1, keepdims=True)
    i1 = jnp.min(jnp.where(in_grp & (es == top1), lane, big), axis=-1, keepdims=True)
    es2 = jnp.where(lane == i1, NEG_BIG, es)
    top2 = jnp.max(es2, axis=-1, keepdims=True)
    i2 = jnp.min(jnp.where(in_grp & (lane != i1) & (es2 == top2), lane, big), axis=-1,
                 keepdims=True)
    e2 = jnp.exp(top2 - top1)
    w1 = p_grp / (1.0 + e2)
    w2 = p_grp * e2 / (1.0 + e2)
    e1 = grp * EXPERTS_PER_GROUP + i1
    e2x = grp * EXPERTS_PER_GROUP + i2
    route_ref[...] = (jnp.where(lane == 0.0, e1, 0.0) + jnp.where(lane == 1.0, e2x, 0.0)
                      + jnp.where(lane == 2.0, w1, 0.0) + jnp.where(lane == 3.0, w2, 0.0))
    picked = jnp.where((lane == e1) | (lane == e2x), 1.0, 0.0)
    cnt_ref[...] = jnp.sum(picked, axis=0, keepdims=True)


def _merge(x2d, o_attn, gy, g1, w_gates, w_oa, w_glu, w_out, g2, wr_hi, wr_lo, b_r):
    t = x2d.shape[0]
    tok = lambda i: (i, 0)
    return pl.pallas_call(
        _merge_kernel,
        grid=(t // TM_MERGE,),
        in_specs=[
            pl.BlockSpec((TM_MERGE, D_MODEL), tok),
            pl.BlockSpec((TM_MERGE, ATTN_WIDTH), tok),
            pl.BlockSpec((TM_MERGE, SSM_WIDTH), tok),
            _const_spec((1, D_MODEL)),
            _const_spec((D_MODEL, 2 * D_MODEL)),
            _const_spec((ATTN_WIDTH, D_MODEL)),
            _const_spec((SSM_WIDTH, 2 * D_MODEL)),
            _const_spec((D_MODEL, D_MODEL)),
            _const_spec((1, D_MODEL)),
            _const_spec((D_MODEL, LANES)),
            _const_spec((D_MODEL, LANES)),
            _const_spec((1, LANES)),
        ],
        out_specs=[
            pl.BlockSpec((TM_MERGE, D_MODEL), tok),
            pl.BlockSpec((TM_MERGE, D_MODEL), tok),
            pl.BlockSpec((TM_MERGE, LANES), tok),
            pl.BlockSpec((TM_MERGE // TM_DISP, 1, LANES), lambda i: (i, 0, 0)),
        ],
        out_shape=[
            jax.ShapeDtypeStruct((t, D_MODEL), F32),
            jax.ShapeDtypeStruct((t, D_MODEL), BF16),
            jax.ShapeDtypeStruct((t, LANES), F32),
            jax.ShapeDtypeStruct((t // TM_DISP, 1, LANES), F32),
        ],
        compiler_params=pltpu.CompilerParams(
            dimension_semantics=("arbitrary",), vmem_limit_bytes=VMEM_LIMIT),
        name="merge",
    )(x2d, o_attn, gy, g1, w_gates, w_oa, w_glu, w_out, g2, wr_hi, wr_lo, b_r)


XS_COLS = D_MODEL + LANES
SORT_ROWS = 2 * TM_DISP + N_EXPERTS * ROW_ALIGN
MAX_PIECES = SORT_ROWS // ROW_ALIGN
POS_RADIX = 32


def _moe_layout(t):
    ntiles = t // TM_DISP
    max_rows = 2 * t + ntiles * N_EXPERTS * (ROW_ALIGN - 1)
    nsteps = -(-max_rows // TS_MOE) + N_EXPERTS
    return ntiles, nsteps


def _lane_col(arr, lane, k):
    return jnp.sum(jnp.where(lane == k, arr, 0.0), axis=-1, keepdims=True)


def _dispatch_kernel(dst_ref, npiece_ref, tail_ref, ntail_ref, h_ref, route_ref, tri_ref,
                     upper_ref, xs_ref, pos_ref, buf_ref, zbuf_ref, sem_ref, tsem_ref):
    i = pl.program_id(0)
    last = pl.num_programs(0) - 1
    cur = i % 2
    tm = TM_DISP
    route = route_ref[...]
    lane = lax.broadcasted_iota(jnp.int32, route.shape, 1)
    lane_f = lane.astype(F32)
    e1 = _lane_col(route, lane, 0)
    e2 = _lane_col(route, lane, 1)
    is1 = lane_f == e1
    is2 = lane_f == e2
    picked = jnp.where(is1 | is2, 1.0, 0.0)
    rank_all = jnp.dot(tri_ref[...], picked.astype(BF16), preferred_element_type=F32)
    cnt_row = jnp.sum(picked, axis=0, keepdims=True)
    pad_row = jnp.floor((cnt_row + (ROW_ALIGN - 1)) * (1.0 / ROW_ALIGN)) * ROW_ALIGN
    off = jnp.dot(jnp.broadcast_to(pad_row, (SUBLANES, LANES)).astype(BF16), upper_ref[...],
                  preferred_element_type=F32)[0:1, :]
    posmat = rank_all + off
    pos1 = jnp.sum(jnp.where(is1, posmat, 0.0), axis=-1, keepdims=True)
    pos2 = jnp.sum(jnp.where(is2, posmat, 0.0), axis=-1, keepdims=True)
    pos_ref[...] = jnp.where(lane == 0, pos1, 0.0) + jnp.where(lane == 1, pos2, 0.0)

    def digits(p):
        hi = jnp.floor(p * (1.0 / POS_RADIX))
        return hi, p - POS_RADIX * hi

    d1h, d1l = digits(pos1)
    d2h, d2l = digits(pos2)
    dig = (jnp.where(lane == 0, d1h, 0.0) + jnp.where(lane == 1, d1l, 0.0)
           + jnp.where(lane == 2, d2h, 0.0) + jnp.where(lane == 3, d2l, 0.0)).astype(BF16)
    eye8 = (lax.broadcasted_iota(jnp.int32, (SUBLANES, LANES), 0)
            == lax.broadcasted_iota(jnp.int32, (SUBLANES, LANES), 1)).astype(BF16)
    rows = lax.dot_general(eye8, dig, (((1,), (1,)), ((), ())), preferred_element_type=F32)
    p1_row = POS_RADIX * rows[0:1, :] + rows[1:2, :]
    p2_row = POS_RADIX * rows[2:3, :] + rows[3:4, :]
    sub = lax.broadcasted_iota(jnp.int32, (SORT_ROWS, tm), 0).astype(F32)
    perm = jnp.where((sub == p1_row) | (sub == p2_row), 1.0, 0.0).astype(BF16)

    def hi_lo(w):
        hi = w.astype(BF16).astype(F32)
        return hi, (w - hi).astype(BF16).astype(F32)

    w1h, w1l = hi_lo(_lane_col(route, lane, 2))
    w2h, w2l = hi_lo(_lane_col(route, lane, 3))
    aux = (jnp.where(lane == 0, e1, 0.0) + jnp.where(lane == 1, e2, 0.0)
           + jnp.where(lane == 2, w1h, 0.0) + jnp.where(lane == 3, w1l, 0.0)
           + jnp.where(lane == 4, w2h, 0.0) + jnp.where(lane == 5, w2l, 0.0)).astype(BF16)
    payload = jnp.concatenate([h_ref[...], aux], axis=1)
    buf_ref[cur] = jnp.dot(perm, payload, preferred_element_type=F32).astype(BF16)

    def piece(step, slot, p):
        src = pl.multiple_of(p * ROW_ALIGN, ROW_ALIGN)
        dst = pl.multiple_of(dst_ref[step * MAX_PIECES + p], ROW_ALIGN)
        return pltpu.make_async_copy(buf_ref.at[slot, pl.ds(src, ROW_ALIGN)],
                                     xs_ref.at[pl.ds(dst, ROW_ALIGN)], sem_ref.at[slot])

    def start_all(step, slot):
        def body(p, c):
            piece(step, slot, p).start()
            return c
        lax.fori_loop(0, npiece_ref[step], body, 0)

    def wait_all(step, slot):
        def body(p, c):
            piece(step, slot, p).wait()
            return c
        lax.fori_loop(0, npiece_ref[step], body, 0)

    start_all(i, cur)

    @pl.when(i > 0)
    def _wait_prev():
        wait_all(i - 1, 1 - cur)

    @pl.when(i == last)
    def _finish():
        wait_all(i, cur)
        zbuf_ref[...] = jnp.zeros(zbuf_ref.shape, BF16)

        def tail(p):
            dst = pl.multiple_of(tail_ref[p], ROW_ALIGN)
            return pltpu.make_async_copy(zbuf_ref, xs_ref.at[pl.ds(dst, ROW_ALIGN)],
                                         tsem_ref.at[0])

        def start_tail(p, c):
            tail(p).start()
            return c

        def wait_tail(p, c):
            tail(p).wait()
            return c

        lax.fori_loop(0, ntail_ref[0], start_tail, 0)
        lax.fori_loop(0, ntail_ref[0], wait_tail, 0)


def _dispatch(h2, route, dst, npiece, tail, ntail, tri, upper, rows):
    t = h2.shape[0]
    tm = TM_DISP
    tok = lambda i, *_: (i, 0)
    const = lambda i, *_: (0, 0)
    grid_spec = pltpu.PrefetchScalarGridSpec(
        num_scalar_prefetch=4,
        grid=(t // tm,),
        in_specs=[
            pl.BlockSpec((tm, D_MODEL), tok),
            pl.BlockSpec((tm, LANES), tok),
            pl.BlockSpec((tm, tm), const),
            pl.BlockSpec((LANES, LANES), const),
        ],
        out_specs=[
            pl.BlockSpec(memory_space=pl.ANY),
            pl.BlockSpec((tm, LANES), tok),
        ],
        scratch_shapes=[
            pltpu.VMEM((2, SORT_ROWS, XS_COLS), BF16),
            pltpu.VMEM((ROW_ALIGN, XS_COLS), BF16),
            pltpu.SemaphoreType.DMA((2,)),
            pltpu.SemaphoreType.DMA((1,)),
        ],
    )
    return pl.pallas_call(
        _dispatch_kernel,
        grid_spec=grid_spec,
        out_shape=[
            jax.ShapeDtypeStruct((rows, XS_COLS), BF16),
            jax.ShapeDtypeStruct((t, LANES), F32),
        ],
        compiler_params=pltpu.CompilerParams(
            dimension_semantics=("arbitrary",), vmem_limit_bytes=VMEM_LIMIT),
        name="dispatch",
    )(dst, npiece, tail, ntail, h2, route, tri, upper)


def _moe_kernel(blk_ref, exp_ref, nvalid_ref, xs_ref, wg_ref, wu_ref, wd_ref, ys_ref):
    s = pl.program_id(0)

    @pl.when(s < nvalid_ref[0])
    def _compute():
        t = xs_ref[:, 0:D_MODEL]
        aux = xs_ref[:, D_MODEL:XS_COLS].astype(F32)
        lane = lax.broadcasted_iota(jnp.int32, aux.shape, 1)
        first = _lane_col(aux, lane, 0) == exp_ref[s].astype(F32)
        w = jnp.where(first, _lane_col(aux, lane, 2) + _lane_col(aux, lane, 3),
                      _lane_col(aux, lane, 4) + _lane_col(aux, lane, 5))
        hg = jnp.dot(t, wg_ref[...], preferred_element_type=F32)
        hu = jnp.dot(t, wu_ref[...], preferred_element_type=F32)
        a = (hg * jax.nn.sigmoid(hg) * hu * w).astype(BF16)
        ys_ref[...] = jnp.dot(a, wd_ref[...], preferred_element_type=F32).astype(BF16)

    @pl.when(s >= nvalid_ref[0])
    def _spare():
        ys_ref[...] = jnp.zeros(ys_ref.shape, BF16)


def _moe(xs, blk, exp, nvalid, wg, wu, wd):
    row = lambda s, blk, exp, nv: (blk[s], 0)
    wsel = lambda s, blk, exp, nv: (exp[s], 0, 0)
    grid_spec = pltpu.PrefetchScalarGridSpec(
        num_scalar_prefetch=3,
        grid=(blk.shape[0],),
        in_specs=[
            pl.BlockSpec((TS_MOE, XS_COLS), row),
            pl.BlockSpec((None, D_MODEL, D_EXPERT), wsel),
            pl.BlockSpec((None, D_MODEL, D_EXPERT), wsel),
            pl.BlockSpec((None, D_EXPERT, D_MODEL), wsel),
        ],
        out_specs=pl.BlockSpec((TS_MOE, D_MODEL), row),
    )
    return pl.pallas_call(
        _moe_kernel,
        grid_spec=grid_spec,
        out_shape=jax.ShapeDtypeStruct((xs.shape[0], D_MODEL), BF16),
        compiler_params=pltpu.CompilerParams(
            dimension_semantics=("arbitrary",), vmem_limit_bytes=VMEM_LIMIT),
        name="moe",
    )(blk, exp, nvalid, xs, wg, wu, wd)


def _combine_kernel(dst_ref, npiece_ref, x2_ref, pos_ref, ys_ref, o_ref, stage_ref, sem_ref):
    i = pl.program_id(0)
    n = pl.num_programs(0)
    cur = i % 2
    tm = TM_DISP

    def piece(step, slot, p):
        row = pl.multiple_of(p * ROW_ALIGN, ROW_ALIGN)
        src = pl.multiple_of(dst_ref[step * MAX_PIECES + p], ROW_ALIGN)
        return pltpu.make_async_copy(ys_ref.at[pl.ds(src, ROW_ALIGN)],
                                     stage_ref.at[slot, pl.ds(row, ROW_ALIGN)],
                                     sem_ref.at[slot])

    def start_all(step, slot):
        def body(p, c):
            piece(step, slot, p).start()
            return c
        lax.fori_loop(0, npiece_ref[step], body, 0)

    @pl.when(i == 0)
    def _first():
        stage_ref[...] = jnp.zeros(stage_ref.shape, BF16)
        start_all(0, 0)

    @pl.when(i + 1 < n)
    def _prefetch():
        start_all(i + 1, 1 - cur)

    def wait_body(p, c):
        piece(i, cur, p).wait()
        return c

    lax.fori_loop(0, npiece_ref[i], wait_body, 0)
    pos = pos_ref[...]
    lane128 = lax.broadcasted_iota(jnp.int32, pos.shape, 1)
    p1 = _lane_col(pos, lane128, 0)
    p2 = _lane_col(pos, lane128, 1)
    lane = lax.broadcasted_iota(jnp.int32, (tm, SORT_ROWS), 1).astype(F32)
    pick = jnp.where((lane == p1) | (lane == p2), 1.0, 0.0).astype(BF16)
    o_ref[...] = x2_ref[...] + jnp.dot(pick, stage_ref[cur], preferred_element_type=F32)


def _combine(x2, pos, ys, dst, npiece):
    t = x2.shape[0]
    tm = TM_DISP
    tok = lambda i, *_: (i, 0)
    grid_spec = pltpu.PrefetchScalarGridSpec(
        num_scalar_prefetch=2,
        grid=(t // tm,),
        in_specs=[
            pl.BlockSpec((tm, D_MODEL), tok),
            pl.BlockSpec((tm, LANES), tok),
            pl.BlockSpec(memory_space=pl.ANY),
        ],
        out_specs=pl.BlockSpec((tm, D_MODEL), tok),
        scratch_shapes=[
            pltpu.VMEM((2, SORT_ROWS, D_MODEL), BF16),
            pltpu.SemaphoreType.DMA((2,)),
        ],
    )
    return pl.pallas_call(
        _combine_kernel,
        grid_spec=grid_spec,
        out_shape=jax.ShapeDtypeStruct((t, D_MODEL), F32),
        compiler_params=pltpu.CompilerParams(
            dimension_semantics=("arbitrary",), vmem_limit_bytes=VMEM_LIMIT),
        name="combine",
    )(dst, npiece, x2, pos, ys)


def _sparse_moe(h2, route, cnt, x2, wg, wu, wd):
    t = h2.shape[0]
    ntiles, nsteps = _moe_layout(t)
    rows = (nsteps + 1) * TS_MOE
    i32 = jnp.int32
    counts = cnt[:, 0, :N_EXPERTS].astype(i32)
    padded = (counts + (ROW_ALIGN - 1)) // ROW_ALIGN * ROW_ALIGN
    total = jnp.sum(padded, axis=0)
    reg_rows = (total + (TS_MOE - 1)) // TS_MOE * TS_MOE
    region = jnp.cumsum(reg_rows) - reg_rows
    base = region[None, :] + jnp.cumsum(padded, axis=0) - padded
    np_e = padded // ROW_ALIGN
    cum = jnp.cumsum(np_e, axis=1)
    npiece = cum[:, -1].astype(i32)
    p = jnp.arange(MAX_PIECES, dtype=i32)
    e_of_p = jnp.minimum(jnp.sum(p[None, :, None] >= cum[:, None, :], axis=2), N_EXPERTS - 1)
    first_p = jnp.take_along_axis(cum - np_e, e_of_p, axis=1)
    dst = (jnp.take_along_axis(base, e_of_p, axis=1) + (p[None, :] - first_p) * ROW_ALIGN)
    dst = dst.reshape(-1).astype(i32)
    nt_e = (reg_rows - total) // ROW_ALIGN
    cumt = jnp.cumsum(nt_e)
    q = jnp.arange(N_EXPERTS * (TS_MOE // ROW_ALIGN), dtype=i32)
    e_of_q = jnp.minimum(jnp.sum(q[:, None] >= cumt[None, :], axis=1), N_EXPERTS - 1)
    tail = (region[e_of_q] + total[e_of_q] + (q - (cumt - nt_e)[e_of_q]) * ROW_ALIGN).astype(i32)
    ntail = cumt[-1:].astype(i32)
    ntile_e = reg_rows // TS_MOE
    first_t = jnp.cumsum(ntile_e) - ntile_e
    nvalid = jnp.sum(ntile_e).reshape(1).astype(i32)
    s = jnp.arange(nsteps, dtype=i32)
    exp = jnp.minimum(jnp.sum(s[:, None] >= (first_t + ntile_e)[None, :], axis=1),
                      N_EXPERTS - 1).astype(i32)
    blk = jnp.where(s < nvalid[0], region[exp] // TS_MOE + s - first_t[exp], nsteps).astype(i32)
    tri = (jnp.arange(TM_DISP)[:, None] > jnp.arange(TM_DISP)[None, :]).astype(BF16)
    upper = (jnp.arange(LANES)[:, None] < jnp.arange(LANES)[None, :]).astype(BF16)
    xs, pos = _dispatch(h2, route, dst, npiece, tail, ntail, tri, upper, rows)
    ys = _moe(xs, blk, exp, nvalid, wg, wu, wd)
    return _combine(x2, pos, ys, dst, npiece)


def _rope_tables(positions):
    inv = ROPE_THETA ** (-jnp.arange(0, ROPE_DIM, 2, dtype=F32) / ROPE_DIM)
    ang = positions.astype(F32).reshape(-1, 1) * inv
    cos, sin = jnp.cos(ang), jnp.sin(ang)
    t = ang.shape[0]
    ones = jnp.ones((t, HEAD_DIM - ROPE_DIM), F32)
    zeros = jnp.zeros((t, HEAD_DIM - ROPE_DIM), F32)
    cos64 = jnp.concatenate([cos, cos, ones], axis=-1)
    sin64 = jnp.concatenate([-sin, sin, zeros], axis=-1)
    return jnp.tile(cos64, (1, 2)), jnp.tile(sin64, (1, 2))


def _s5_params(lam_re, lam_im, log_dt, b_re, b_im, c_re, c_im, bsz):
    dt = jnp.exp(log_dt)[:, None]
    mag = jnp.exp(lam_re * dt)
    lb_re = mag * jnp.cos(lam_im * dt)
    lb_im = mag * jnp.sin(lam_im * dt)
    den = lam_re * lam_re + lam_im * lam_im
    k_re = ((lb_re - 1.0) * lam_re + lb_im * lam_im) / den
    k_im = (lb_im * lam_re - (lb_re - 1.0) * lam_im) / den
    bb_re = k_re[..., None] * b_re - k_im[..., None] * b_im
    bb_im = k_re[..., None] * b_im + k_im[..., None] * b_re
    eye = jnp.eye(SSM_GROUPS, dtype=F32)
    blk_b = lambda m: jnp.einsum('gph,gk->ghkp', m, eye).reshape(SSM_WIDTH, N_STATE)
    blk_c = lambda m: jnp.einsum('ghp,gk->gpkh', m, eye).reshape(N_STATE, SSM_WIDTH)
    bmat = jnp.concatenate([blk_b(bb_re), blk_b(bb_im)], axis=1).astype(BF16)
    cmat = jnp.concatenate([blk_c(c_re), blk_c(-c_im)], axis=0).astype(BF16)
    a_re = jnp.broadcast_to(lb_re.reshape(1, N_STATE), (bsz, N_STATE))
    a_im = jnp.broadcast_to(lb_im.reshape(1, N_STATE), (bsz, N_STATE))
    return bmat, a_re, a_im, cmat


def _time_major_perm(bsz, chunk):
    r = jnp.arange(bsz * chunk)
    src = (r % bsz) * chunk + r // bsz
    perm = (src[:, None] == jnp.arange(bsz * chunk)[None, :]).astype(BF16)
    return perm, perm.T


def kernel(x, positions, norm_mix_g, w_in, q_norm_g, k_norm_g, lambda_q1, lambda_k1, lambda_q2, lambda_k2, subln_g, w_o_attn, ssm_lambda_re, ssm_lambda_im, ssm_log_dt, ssm_b_re, ssm_b_im, ssm_c_re, ssm_c_im, ssm_d, w_glu, w_out, norm_ffn_g, w_router_group, b_router_group, w_router_expert, b_router_expert, w_expert_gate, w_expert_up, w_expert_down):
    bsz, seq, _ = x.shape
    assert bsz == SUBLANES and seq % TQ == 0 and seq % SSM_CHUNK == 0
    assert norm_mix_g.shape[0] == 1
    t = bsz * seq
    x2d = x.reshape(t, D_MODEL)
    l = 0

    cos_t, sin_t = _rope_tables(positions)
    w_qkvu = w_in[l][:, :QKVU_COLS].astype(BF16)
    w_gates = w_in[l][:, QKVU_COLS:].astype(BF16)
    qg = jnp.tile(q_norm_g[l].reshape(1, HEAD_DIM), (1, 2))
    kg = jnp.tile(k_norm_g[l].reshape(1, HEAD_DIM), (1, 2))
    q, k, v, u = _in_proj(x2d, norm_mix_g[l].reshape(1, D_MODEL), w_qkvu, cos_t, sin_t, qg, kg)

    lam = (jnp.exp(jnp.sum(lambda_q1[l] * lambda_k1[l]))
           - jnp.exp(jnp.sum(lambda_q2[l] * lambda_k2[l])) + LAM_INIT).reshape(1)
    o_attn = _diff_attn(q, k, v, lam, subln_g[l].reshape(1, V_DIM), bsz, seq)

    bmat, a_re, a_im, cmat = _s5_params(
        ssm_lambda_re[l], ssm_lambda_im[l], ssm_log_dt[l], ssm_b_re[l], ssm_b_im[l],
        ssm_c_re[l], ssm_c_im[l], bsz)
    perm, perm_t = _time_major_perm(bsz, SSM_CHUNK)
    gy = _s5_scan(u.reshape(bsz, seq, SSM_WIDTH), perm, perm_t, bmat, a_re, a_im, cmat,
                  ssm_d[l].reshape(1, SSM_WIDTH)).reshape(t, SSM_WIDTH)

    w_r = jnp.concatenate(
        [w_router_expert[l].reshape(D_MODEL, N_EXPERTS), w_router_group[l],
         jnp.zeros((D_MODEL, LANES - N_EXPERTS - N_EXPERT_GROUPS), F32)], axis=1)
    b_r = jnp.concatenate(
        [b_router_expert[l].reshape(N_EXPERTS), b_router_group[l],
         jnp.zeros((LANES - N_EXPERTS - N_EXPERT_GROUPS,), F32)]).reshape(1, LANES)
    wr_hi = w_r.astype(BF16)
    wr_lo = (w_r - wr_hi.astype(F32)).astype(BF16)
    x2, h2, route, cnt = _merge(
        x2d, o_attn, gy, norm_mix_g[l].reshape(1, D_MODEL), w_gates,
        w_o_attn[l].astype(BF16), w_glu[l].astype(BF16), w_out[l].astype(BF16),
        norm_ffn_g[l].reshape(1, D_MODEL), wr_hi, wr_lo, b_r)

    wg = w_expert_gate[l].astype(BF16).reshape(N_EXPERTS, D_MODEL, D_EXPERT)
    wu = w_expert_up[l].astype(BF16).reshape(N_EXPERTS, D_MODEL, D_EXPERT)
    wd = w_expert_down[l].astype(BF16).reshape(N_EXPERTS, D_EXPERT, D_MODEL)
    out = _sparse_moe(h2, route, cnt, x2, wg, wu, wd)
    return out.reshape(bsz, seq, D_MODEL)
```

```python
import functools
import math

import jax
import jax.numpy as jnp
from jax import lax
from jax.experimental import pallas as pl
from jax.experimental.pallas import tpu as pltpu

F32 = jnp.float32
BF16 = jnp.bfloat16

D_MODEL = 1024
N_HEADS = 4
HEAD_DIM = 64
V_DIM = 2 * HEAD_DIM
ATTN_WIDTH = N_HEADS * V_DIM
ROPE_THETA = 500000.0
ROPE_DIM = HEAD_DIM // 4
ROPE_HALF = ROPE_DIM // 2
SSM_WIDTH = D_MODEL // 2
SSM_GROUP = 16
SSM_GROUPS = SSM_WIDTH // SSM_GROUP
SSM_STATE = 64
N_STATE = SSM_GROUPS * SSM_STATE
N_EXPERT_GROUPS = 4
EXPERTS_PER_GROUP = 8
N_EXPERTS = N_EXPERT_GROUPS * EXPERTS_PER_GROUP
D_EXPERT = D_MODEL // 4
EPS = 1e-6
LAM_INIT = 0.8 - 0.6 * math.exp(-0.3 * 0)
QKVU_COLS = 4 * ATTN_WIDTH
LANES = 128
SUBLANES = 8
NEG_BIG = -1e30
LOG2_E = math.log2(math.e)

VMEM_LIMIT = 48 * 1024 * 1024

TM_IN = 512
TQ = 512
SSM_CHUNK = 64
SCAN_LANES = 512
TM_MERGE = 512
TM_DISP = 256
DISP_SUB = 2
TS_MOE = 1024
ROW_ALIGN = 16


def _const_spec(shape):
    return pl.BlockSpec(shape, lambda *_: (0,) * len(shape))


def _in_proj_kernel(x_ref, g_ref, w_ref, cos_ref, sin_ref, qg_ref, kg_ref,
                    q_ref, k_ref, v_ref, u_ref):
    x = x_ref[...]
    ms = jnp.mean(x * x, axis=-1, keepdims=True)
    h = (x * lax.rsqrt(ms + EPS) * g_ref[...]).astype(BF16)
    cos_t = cos_ref[...]
    sin_t = sin_ref[...]
    lane = lax.broadcasted_iota(jnp.int32, (x.shape[0], LANES), 1)
    first_comp = lane < HEAD_DIM
    low_half = (lane % HEAD_DIM) < ROPE_HALF

    def norm_rope(blk, gain, scale):
        sq = blk * blk
        s_all = jnp.sum(sq, axis=-1, keepdims=True)
        s_lo = jnp.sum(jnp.where(first_comp, sq, 0.0), axis=-1, keepdims=True)
        ssum = jnp.where(first_comp, s_lo, s_all - s_lo)
        nb = blk * lax.rsqrt(ssum * (1.0 / HEAD_DIM) + EPS) * gain
        up = pltpu.roll(nb, LANES - ROPE_HALF, axis=1)
        dn = pltpu.roll(nb, ROPE_HALF, axis=1)
        partner = jnp.where(low_half, up, dn)
        return ((nb * cos_t + partner * sin_t) * scale).astype(BF16)

    for j in range(0, N_HEADS, 2):
        sl2 = slice(j * LANES, (j + 2) * LANES)
        qb = jnp.dot(h, w_ref[:, sl2], preferred_element_type=F32)
        kb = jnp.dot(h, w_ref[:, ATTN_WIDTH + j * LANES:ATTN_WIDTH + (j + 2) * LANES],
                     preferred_element_type=F32)
        for jj in range(2):
            sl = slice((j + jj) * LANES, (j + jj + 1) * LANES)
            half = slice(jj * LANES, (jj + 1) * LANES)
            q_ref[:, sl] = norm_rope(qb[:, half], qg_ref[...], LOG2_E * HEAD_DIM ** -0.5)
            k_ref[:, sl] = norm_rope(kb[:, half], kg_ref[...], 1.0)
    v_ref[...] = jnp.dot(h, w_ref[:, 2 * ATTN_WIDTH:3 * ATTN_WIDTH],
                         preferred_element_type=F32).astype(BF16)
    u_ref[...] = jnp.dot(h, w_ref[:, 3 * ATTN_WIDTH:4 * ATTN_WIDTH],
                         preferred_element_type=F32).astype(BF16)


def _in_proj(x2d, g, w, cos_t, sin_t, qg, kg):
    t = x2d.shape[0]
    tok = lambda i: (i, 0)
    out = jax.ShapeDtypeStruct((t, ATTN_WIDTH), BF16)
    return pl.pallas_call(
        _in_proj_kernel,
        grid=(t // TM_IN,),
        in_specs=[
            pl.BlockSpec((TM_IN, D_MODEL), tok),
            _const_spec((1, D_MODEL)),
            _const_spec((D_MODEL, QKVU_COLS)),
            pl.BlockSpec((TM_IN, LANES), tok),
            pl.BlockSpec((TM_IN, LANES), tok),
            _const_spec((1, LANES)),
            _const_spec((1, LANES)),
        ],
        out_specs=[pl.BlockSpec((TM_IN, ATTN_WIDTH), tok)] * 4,
        out_shape=[out] * 4,
        compiler_params=pltpu.CompilerParams(
            dimension_semantics=("arbitrary",), vmem_limit_bytes=VMEM_LIMIT),
        name="in_proj",
    )(x2d, g, w, cos_t, sin_t, qg, kg)


def _attn_kernel(qi_ref, kj_ref, lam_ref, q_ref, k_ref, v_ref, sg_ref, o_ref,
                 qs_ref, m_ref, l_ref, acc_ref):
    step = pl.program_id(1)
    qi = qi_ref[step]
    kj = kj_ref[step]

    @pl.when(kj == 0)
    def _init():
        for h in range(N_HEADS):
            q = q_ref[:, h * LANES:(h + 1) * LANES]
            lane = lax.broadcasted_iota(jnp.int32, q.shape, 1)
            zero = jnp.zeros_like(q)
            qs_ref[h, 0:TQ, :] = jnp.where(lane < HEAD_DIM, q, zero)
            qs_ref[h, TQ:2 * TQ, :] = jnp.where(lane < HEAD_DIM, zero, q)
        m_ref[...] = jnp.full(m_ref.shape, NEG_BIG, F32)
        l_ref[...] = jnp.zeros(l_ref.shape, F32)
        acc_ref[...] = jnp.zeros(acc_ref.shape, F32)

    def update(h, masked):
        hs = slice(h * LANES, (h + 1) * LANES)
        s = lax.dot_general(qs_ref[h], k_ref[:, hs], (((1,), (1,)), ((), ())),
                            preferred_element_type=F32)
        if masked:
            row = lax.broadcasted_iota(jnp.int32, s.shape, 0) % TQ
            col = lax.broadcasted_iota(jnp.int32, s.shape, 1)
            s = jnp.where(col <= row, s, NEG_BIG)
        m_old = m_ref[h]
        m_new = jnp.maximum(m_old, jnp.max(s, axis=-1, keepdims=True))
        alpha = jnp.exp2(m_old - m_new)
        p = jnp.exp2(s - jnp.concatenate([m_new] * (TQ // LANES), axis=1))
        psum = p[:, 0:LANES]
        for c in range(1, TQ // LANES):
            psum = psum + p[:, c * LANES:(c + 1) * LANES]
        l_ref[h] = alpha * l_ref[h] + psum
        acc_ref[h] = alpha * acc_ref[h] + jnp.dot(
            p.astype(BF16), v_ref[:, hs], preferred_element_type=F32)
        m_ref[h] = m_new

    @pl.when(kj < qi)
    def _off_diag():
        for h in range(N_HEADS):
            update(h, False)

    @pl.when(kj == qi)
    def _diag():
        for h in range(N_HEADS):
            update(h, True)
            o = acc_ref[h] / jnp.sum(l_ref[h], axis=-1, keepdims=True)
            d = o[0:TQ, :] - lam_ref[0] * o[TQ:2 * TQ, :]
            ms = jnp.mean(d * d, axis=-1, keepdims=True)
            d = d * lax.rsqrt(ms + EPS) * sg_ref[...] * (1.0 - LAM_INIT)
            o_ref[:, h * LANES:(h + 1) * LANES] = d.astype(BF16)


def _diff_attn(q, k, v, lam, subln_g, bsz, seq):
    nq = seq // TQ
    pairs = [(i, j) for i in range(nq) for j in range(i + 1)]
    qi = jnp.asarray([p[0] for p in pairs], jnp.int32)
    kj = jnp.asarray([p[1] for p in pairs], jnp.int32)
    q_map = lambda b, s, qi, kj: (b * nq + qi[s], 0)
    k_map = lambda b, s, qi, kj: (b * nq + kj[s], 0)
    grid_spec = pltpu.PrefetchScalarGridSpec(
        num_scalar_prefetch=2,
        grid=(bsz, len(pairs)),
        in_specs=[
            pl.BlockSpec(memory_space=pltpu.SMEM),
            pl.BlockSpec((TQ, ATTN_WIDTH), q_map),
            pl.BlockSpec((TQ, ATTN_WIDTH), k_map),
            pl.BlockSpec((TQ, ATTN_WIDTH), k_map),
            pl.BlockSpec((1, LANES), lambda b, s, qi, kj: (0, 0)),
        ],
        out_specs=pl.BlockSpec((TQ, ATTN_WIDTH), q_map),
        scratch_shapes=[
            pltpu.VMEM((N_HEADS, 2 * TQ, LANES), BF16),
            pltpu.VMEM((N_HEADS, 2 * TQ, LANES), F32),
            pltpu.VMEM((N_HEADS, 2 * TQ, LANES), F32),
            pltpu.VMEM((N_HEADS, 2 * TQ, LANES), F32),
        ],
    )
    return pl.pallas_call(
        _attn_kernel,
        grid_spec=grid_spec,
        out_shape=jax.ShapeDtypeStruct((bsz * seq, ATTN_WIDTH), BF16),
        compiler_params=pltpu.CompilerParams(
            dimension_semantics=("arbitrary", "arbitrary"),
            vmem_limit_bytes=VMEM_LIMIT),
        name="diff_attn",
    )(qi, kj, lam, q, k, v, subln_g)


def _gelu_tanh(x):
    c = math.sqrt(2.0 / math.pi)
    return 0.5 * x * (1.0 + jnp.tanh(c * (x + 0.044715 * (x * x * x))))


def _s5_kernel(u_ref, p_ref, pt_ref, b_ref, are_ref, aim_ref, c_ref, d_ref, o_ref,
               st_ref, state_ref):
    bsz, chunk, width = u_ref.shape
    rows = bsz * chunk

    @pl.when(pl.program_id(0) == 0)
    def _init():
        state_ref[...] = jnp.zeros(state_ref.shape, F32)

    u_bt = u_ref[...].reshape(rows, width)
    u_tm = jnp.dot(p_ref[...], u_bt, preferred_element_type=F32)
    u_tm_b = u_tm.astype(BF16)

    slab = SCAN_LANES // SSM_STATE * SSM_GROUP
    g_parts = []
    for ch in range(N_STATE // SCAN_LANES):
        re = slice(ch * SCAN_LANES, (ch + 1) * SCAN_LANES)
        im = slice(N_STATE + ch * SCAN_LANES, N_STATE + (ch + 1) * SCAN_LANES)
        cs = slice(ch * slab, (ch + 1) * slab)
        st_ref[:, re] = jnp.dot(u_tm_b[:, cs], b_ref[cs, re], preferred_element_type=F32)
        st_ref[:, im] = jnp.dot(u_tm_b[:, cs], b_ref[cs, im], preferred_element_type=F32)
        a_re = are_ref[:, re]
        a_im = aim_ref[:, re]

        def body(t, carry, re=re, im=im, a_re=a_re, a_im=a_im):
            s_re, s_im = carry
            r0 = pl.multiple_of(t * SUBLANES, SUBLANES)
            n_re = a_re * s_re - a_im * s_im + st_ref[pl.ds(r0, SUBLANES), re]
            n_im = a_re * s_im + a_im * s_re + st_ref[pl.ds(r0, SUBLANES), im]
            st_ref[pl.ds(r0, SUBLANES), re] = n_re
            st_ref[pl.ds(r0, SUBLANES), im] = n_im
            return n_re, n_im

        s_re, s_im = lax.fori_loop(0, chunk, body, (state_ref[:, re], state_ref[:, im]),
                                   unroll=True)
        state_ref[:, re] = s_re
        state_ref[:, im] = s_im
        y = (jnp.dot(st_ref[:, re].astype(BF16), c_ref[re, cs], preferred_element_type=F32)
             + jnp.dot(st_ref[:, im].astype(BF16), c_ref[im, cs], preferred_element_type=F32))
        y = y + d_ref[:, cs] * u_tm[:, cs]
        g_parts.append(_gelu_tanh(y).astype(BF16))

    g_tm = jnp.concatenate(g_parts, axis=1)
    g_bt = jnp.dot(pt_ref[...], g_tm, preferred_element_type=F32)
    o_ref[...] = g_bt.astype(BF16).reshape(bsz, chunk, width)


def _s5_scan(u3, perm, perm_t, bmat, a_re, a_im, cmat, dvec):
    bsz, seq, width = u3.shape
    rows = bsz * SSM_CHUNK
    blk = pl.BlockSpec((bsz, SSM_CHUNK, width), lambda c: (0, c, 0))
    return pl.pallas_call(
        _s5_kernel,
        grid=(seq // SSM_CHUNK,),
        in_specs=[
            blk,
            _const_spec((rows, rows)),
            _const_spec((rows, rows)),
            _const_spec((width, 2 * N_STATE)),
            _const_spec((bsz, N_STATE)),
            _const_spec((bsz, N_STATE)),
            _const_spec((2 * N_STATE, width)),
            _const_spec((1, width)),
        ],
        out_specs=blk,
        out_shape=jax.ShapeDtypeStruct(u3.shape, BF16),
        scratch_shapes=[
            pltpu.VMEM((rows, 2 * N_STATE), F32),
            pltpu.VMEM((bsz, 2 * N_STATE), F32),
        ],
        compiler_params=pltpu.CompilerParams(
            dimension_semantics=("arbitrary",), vmem_limit_bytes=VMEM_LIMIT),
        name="s5_scan",
    )(u3, perm, perm_t, bmat, a_re, a_im, cmat, dvec)


def _merge_kernel(x_ref, oa_ref, gy_ref, g1_ref, wg_ref, woa_ref, wglu_ref, wout_ref,
                  g2_ref, wrh_ref, wrl_ref, br_ref, x2_ref, h2_ref, route_ref, cnt_ref):
    for half in range(TM_MERGE // TM_DISP):
        rs = slice(half * TM_DISP, (half + 1) * TM_DISP)
        _merge_rows(x_ref[rs, :], oa_ref[rs, :], gy_ref[rs, :], g1_ref, wg_ref, woa_ref,
                    wglu_ref, wout_ref, g2_ref, wrh_ref, wrl_ref, br_ref,
                    x2_ref.at[rs, :], h2_ref.at[rs, :], route_ref.at[rs, :],
                    cnt_ref.at[half])


def _merge_rows(x, o_in, gy, g1_ref, wg_ref, woa_ref, wglu_ref, wout_ref, g2_ref, wrh_ref,
                wrl_ref, br_ref, x2_ref, h2_ref, route_ref, cnt_ref):
    ms = jnp.mean(x * x, axis=-1, keepdims=True)
    h = (x * lax.rsqrt(ms + EPS) * g1_ref[...]).astype(BF16)
    o_a = jnp.dot(o_in, woa_ref[...], preferred_element_type=F32)
    z_lin = jnp.dot(gy, wglu_ref[:, 0:D_MODEL], preferred_element_type=F32)
    z_gate = jnp.dot(gy, wglu_ref[:, D_MODEL:2 * D_MODEL], preferred_element_type=F32)
    o_s = z_lin * jax.nn.sigmoid(z_gate)
    gate_a = jax.nn.sigmoid(jnp.dot(h, wg_ref[:, 0:D_MODEL], preferred_element_type=F32))
    merged = gate_a * o_a
    gate_s = jax.nn.sigmoid(
        jnp.dot(h, wg_ref[:, D_MODEL:2 * D_MODEL], preferred_element_type=F32))
    merged = merged + gate_s * o_s
    x2 = x + jnp.dot(merged.astype(BF16), wout_ref[...], preferred_element_type=F32)
    x2_ref[...] = x2

    ms2 = jnp.mean(x2 * x2, axis=-1, keepdims=True)
    h2 = x2 * lax.rsqrt(ms2 + EPS) * g2_ref[...]
    h2_hi = h2.astype(BF16)
    h2_lo = (h2 - h2_hi.astype(F32)).astype(BF16)
    h2_ref[...] = h2_hi

    logits = (jnp.dot(h2_hi, wrh_ref[...], preferred_element_type=F32)
              + jnp.dot(h2_lo, wrh_ref[...], preferred_element_type=F32)
              + jnp.dot(h2_hi, wrl_ref[...], preferred_element_type=F32)
              + br_ref[...])
    lane = lax.broadcasted_iota(jnp.int32, logits.shape, 1).astype(F32)
    is_grp = (lane >= N_EXPERTS) & (lane < N_EXPERTS + N_EXPERT_GROUPS)
    gl = jnp.where(is_grp, logits, NEG_BIG)
    gmax = jnp.max(gl, axis=-1, keepdims=True)
    gsum = jnp.sum(jnp.where(is_grp, jnp.exp(gl - gmax), 0.0), axis=-1, keepdims=True)
    p_grp = 1.0 / gsum
    big = float(4 * LANES)
    grp = jnp.min(jnp.where(is_grp & (gl == gmax), lane, big), axis=-1,
                  keepdims=True) - N_EXPERTS
    sel = logits
    for g in range(1, N_EXPERT_GROUPS):
        rolled = pltpu.roll(logits, LANES - g * EXPERTS_PER_GROUP, axis=1)
        sel = jnp.where(grp == g, rolled, sel)
    in_grp = lane < EXPERTS_PER_GROUP
    es = jnp.where(in_grp, sel, NEG_BIG)
    top1 = jnp.max(es, axis=-1, keepdims=True)
    i1 = jnp.min(jnp.where(in_grp & (es == top1), lane, big), axis=-1, keepdims=True)
    es2 = jnp.where(lane == i1, NEG_BIG, es)
    top2 = jnp.max(es2, axis=-1, keepdims=True)
    i2 = jnp.min(jnp.where(in_grp & (lane != i1) & (es2 == top2), lane, big), axis=-1,
                 keepdims=True)
    e2 = jnp.exp(top2 - top1)
    w1 = p_grp / (1.0 + e2)
    w2 = p_grp * e2 / (1.0 + e2)
    e1 = grp * EXPERTS_PER_GROUP + i1
    e2x = grp * EXPERTS_PER_GROUP + i2
    route_ref[...] = (jnp.where(lane == 0.0, e1, 0.0) + jnp.where(lane == 1.0, e2x, 0.0)
                      + jnp.where(lane == 2.0, w1, 0.0) + jnp.where(lane == 3.0, w2, 0.0))
    picked = jnp.where((lane == e1) | (lane == e2x), 1.0, 0.0)
    cnt_ref[...] = jnp.sum(picked, axis=0, keepdims=True)


def _merge(x2d, o_attn, gy, g1, w_gates, w_oa, w_glu, w_out, g2, wr_hi, wr_lo, b_r):
    t = x2d.shape[0]
    tok = lambda i: (i, 0)
    return pl.pallas_call(
        _merge_kernel,
        grid=(t // TM_MERGE,),
        in_specs=[
            pl.BlockSpec((TM_MERGE, D_MODEL), tok),
            pl.BlockSpec((TM_MERGE, ATTN_WIDTH), tok),
            pl.BlockSpec((TM_MERGE, SSM_WIDTH), tok),
            _const_spec((1, D_MODEL)),
            _const_spec((D_MODEL, 2 * D_MODEL)),
            _const_spec((ATTN_WIDTH, D_MODEL)),
            _const_spec((SSM_WIDTH, 2 * D_MODEL)),
            _const_spec((D_MODEL, D_MODEL)),
            _const_spec((1, D_MODEL)),
            _const_spec((D_MODEL, LANES)),
            _const_spec((D_MODEL, LANES)),
            _const_spec((1, LANES)),
        ],
        out_specs=[
            pl.BlockSpec((TM_MERGE, D_MODEL), tok),
            pl.BlockSpec((TM_MERGE, D_MODEL), tok),
            pl.BlockSpec((TM_MERGE, LANES), tok),
            pl.BlockSpec((TM_MERGE // TM_DISP, 1, LANES), lambda i: (i, 0, 0)),
        ],
        out_shape=[
            jax.ShapeDtypeStruct((t, D_MODEL), F32),
            jax.ShapeDtypeStruct((t, D_MODEL), BF16),
            jax.ShapeDtypeStruct((t, LANES), F32),
            jax.ShapeDtypeStruct((t // TM_DISP, 1, LANES), F32),
        ],
        compiler_params=pltpu.CompilerParams(
            dimension_semantics=("arbitrary",), vmem_limit_bytes=VMEM_LIMIT),
        name="merge",
    )(x2d, o_attn, gy, g1, w_gates, w_oa, w_glu, w_out, g2, wr_hi, wr_lo, b_r)


XS_COLS = D_MODEL + LANES
SORT_ROWS = 2 * TM_DISP + N_EXPERTS * ROW_ALIGN
MAX_PIECES = SORT_ROWS // ROW_ALIGN
POS_RADIX = 32


def _moe_layout(t):
    ntiles = t // TM_DISP
    max_rows = 2 * t + ntiles * N_EXPERTS * (ROW_ALIGN - 1)
    nsteps = -(-max_rows // TS_MOE) + N_EXPERTS
    return ntiles, nsteps


def _lane_col(arr, lane, k):
    return jnp.sum(jnp.where(lane == k, arr, 0.0), axis=-1, keepdims=True)


def _compact_tile(h, route, tri_ref, upper_ref, pos_ref):
    tm = TM_DISP
    lane = lax.broadcasted_iota(jnp.int32, route.shape, 1)
    lane_f = lane.astype(F32)
    e1 = _lane_col(route, lane, 0)
    e2 = _lane_col(route, lane, 1)
    is1 = lane_f == e1
    is2 = lane_f == e2
    picked = jnp.where(is1 | is2, 1.0, 0.0)
    rank_all = jnp.dot(tri_ref[...], picked.astype(BF16), preferred_element_type=F32)
    cnt_row = jnp.sum(picked, axis=0, keepdims=True)
    pad_row = jnp.floor((cnt_row + (ROW_ALIGN - 1)) * (1.0 / ROW_ALIGN)) * ROW_ALIGN
    off = jnp.dot(jnp.broadcast_to(pad_row, (SUBLANES, LANES)).astype(BF16), upper_ref[...],
                  preferred_element_type=F32)[0:1, :]
    posmat = rank_all + off
    pos1 = jnp.sum(jnp.where(is1, posmat, 0.0), axis=-1, keepdims=True)
    pos2 = jnp.sum(jnp.where(is2, posmat, 0.0), axis=-1, keepdims=True)
    pos_ref[...] = jnp.where(lane == 0, pos1, 0.0) + jnp.where(lane == 1, pos2, 0.0)

    def digits(p):
        hi = jnp.floor(p * (1.0 / POS_RADIX))
        return hi, p - POS_RADIX * hi

    d1h, d1l = digits(pos1)
    d2h, d2l = digits(pos2)
    dig = (jnp.where(lane == 0, d1h, 0.0) + jnp.where(lane == 1, d1l, 0.0)
           + jnp.where(lane == 2, d2h, 0.0) + jnp.where(lane == 3, d2l, 0.0)).astype(BF16)
    eye8 = (lax.broadcasted_iota(jnp.int32, (SUBLANES, LANES), 0)
            == lax.broadcasted_iota(jnp.int32, (SUBLANES, LANES), 1)).astype(BF16)
    rows = lax.dot_general(eye8, dig, (((1,), (1,)), ((), ())), preferred_element_type=F32)
    p1_row = POS_RADIX * rows[0:1, :] + rows[1:2, :]
    p2_row = POS_RADIX * rows[2:3, :] + rows[3:4, :]
    sub = lax.broadcasted_iota(jnp.int32, (SORT_ROWS, tm), 0).astype(F32)
    perm = jnp.where((sub == p1_row) | (sub == p2_row), 1.0, 0.0).astype(BF16)

    def hi_lo(w):
        hi = w.astype(BF16).astype(F32)
        return hi, (w - hi).astype(BF16).astype(F32)

    w1h, w1l = hi_lo(_lane_col(route, lane, 2))
    w2h, w2l = hi_lo(_lane_col(route, lane, 3))
    aux = (jnp.where(lane == 0, e1, 0.0) + jnp.where(lane == 1, e2, 0.0)
           + jnp.where(lane == 2, w1h, 0.0) + jnp.where(lane == 3, w1l, 0.0)
           + jnp.where(lane == 4, w2h, 0.0) + jnp.where(lane == 5, w2l, 0.0)).astype(BF16)
    payload = jnp.concatenate([h, aux], axis=1)
    return jnp.dot(perm, payload, preferred_element_type=F32).astype(BF16)


def _dispatch_kernel(dst_ref, npiece_ref, tail_ref, ntail_ref, h_ref, route_ref, tri_ref,
                     upper_ref, xs_ref, pos_ref, buf_ref, zbuf_ref, sem_ref, tsem_ref):
    i = pl.program_id(0)
    last = pl.num_programs(0) - 1
    cur = i % 2

    def piece(tile, slot, sub, p):
        src = pl.multiple_of(p * ROW_ALIGN, ROW_ALIGN)
        dst = pl.multiple_of(dst_ref[tile * MAX_PIECES + p], ROW_ALIGN)
        return pltpu.make_async_copy(buf_ref.at[slot, sub, pl.ds(src, ROW_ALIGN)],
                                     xs_ref.at[pl.ds(dst, ROW_ALIGN)], sem_ref.at[slot, sub])

    def start_all(tile, slot, sub):
        def body(p, c):
            piece(tile, slot, sub, p).start()
            return c
        lax.fori_loop(0, npiece_ref[tile], body, 0)

    def wait_all(tile, slot, sub):
        def body(p, c):
            piece(tile, slot, sub, p).wait()
            return c
        lax.fori_loop(0, npiece_ref[tile], body, 0)

    for sub in range(DISP_SUB):
        rs = slice(sub * TM_DISP, (sub + 1) * TM_DISP)
        buf_ref[cur, sub] = _compact_tile(h_ref[rs, :], route_ref[rs, :], tri_ref, upper_ref,
                                          pos_ref.at[rs, :])
        start_all(i * DISP_SUB + sub, cur, sub)

    @pl.when(i > 0)
    def _wait_prev():
        for sub in range(DISP_SUB):
            wait_all((i - 1) * DISP_SUB + sub, 1 - cur, sub)

    @pl.when(i == last)
    def _finish():
        for sub in range(DISP_SUB):
            wait_all(i * DISP_SUB + sub, cur, sub)
        zbuf_ref[...] = jnp.zeros(zbuf_ref.shape, BF16)

        def tail(p):
            dst = pl.multiple_of(tail_ref[p], ROW_ALIGN)
            return pltpu.make_async_copy(zbuf_ref, xs_ref.at[pl.ds(dst, ROW_ALIGN)],
                                         tsem_ref.at[0])

        def start_tail(p, c):
            tail(p).start()
            return c

        def wait_tail(p, c):
            tail(p).wait()
            return c

        lax.fori_loop(0, ntail_ref[0], start_tail, 0)
        lax.fori_loop(0, ntail_ref[0], wait_tail, 0)


def _dispatch(h2, route, dst, npiece, tail, ntail, tri, upper, rows):
    t = h2.shape[0]
    tm = TM_DISP * DISP_SUB
    tok = lambda i, *_: (i, 0)
    const = lambda i, *_: (0, 0)
    grid_spec = pltpu.PrefetchScalarGridSpec(
        num_scalar_prefetch=4,
        grid=(t // tm,),
        in_specs=[
            pl.BlockSpec((tm, D_MODEL), tok),
            pl.BlockSpec((tm, LANES), tok),
            pl.BlockSpec((TM_DISP, TM_DISP), const),
            pl.BlockSpec((LANES, LANES), const),
        ],
        out_specs=[
            pl.BlockSpec(memory_space=pl.ANY),
            pl.BlockSpec((tm, LANES), tok),
        ],
        scratch_shapes=[
            pltpu.VMEM((2, DISP_SUB, SORT_ROWS, XS_COLS), BF16),
            pltpu.VMEM((ROW_ALIGN, XS_COLS), BF16),
            pltpu.SemaphoreType.DMA((2, DISP_SUB)),
            pltpu.SemaphoreType.DMA((1,)),
        ],
    )
    return pl.pallas_call(
        _dispatch_kernel,
        grid_spec=grid_spec,
        out_shape=[
            jax.ShapeDtypeStruct((rows, XS_COLS), BF16),
            jax.ShapeDtypeStruct((t, LANES), F32),
        ],
        compiler_params=pltpu.CompilerParams(
            dimension_semantics=("arbitrary",), vmem_limit_bytes=VMEM_LIMIT),
        name="dispatch",
    )(dst, npiece, tail, ntail, h2, route, tri, upper)


def _moe_kernel(blk_ref, exp_ref, nvalid_ref, xs_ref, wg_ref, wu_ref, wd_ref, ys_ref):
    s = pl.program_id(0)

    @pl.when(s < nvalid_ref[0])
    def _compute():
        t = xs_ref[:, 0:D_MODEL]
        aux = xs_ref[:, D_MODEL:XS_COLS].astype(F32)
        lane = lax.broadcasted_iota(jnp.int32, aux.shape, 1)
        first = _lane_col(aux, lane, 0) == exp_ref[s].astype(F32)
        w = jnp.where(first, _lane_col(aux, lane, 2) + _lane_col(aux, lane, 3),
                      _lane_col(aux, lane, 4) + _lane_col(aux, lane, 5))
        hg = jnp.dot(t, wg_ref[...], preferred_element_type=F32)
        hu = jnp.dot(t, wu_ref[...], preferred_element_type=F32)
        a = (hg * jax.nn.sigmoid(hg) * hu * w).astype(BF16)
        ys_ref[...] = jnp.dot(a, wd_ref[...], preferred_element_type=F32).astype(BF16)

    @pl.when(s >= nvalid_ref[0])
    def _spare():
        ys_ref[...] = jnp.zeros(ys_ref.shape, BF16)


def _moe(xs, blk, exp, nvalid, wg, wu, wd):
    row = lambda s, blk, exp, nv: (blk[s], 0)
    wsel = lambda s, blk, exp, nv: (exp[s], 0, 0)
    grid_spec = pltpu.PrefetchScalarGridSpec(
        num_scalar_prefetch=3,
        grid=(blk.shape[0],),
        in_specs=[
            pl.BlockSpec((TS_MOE, XS_COLS), row),
            pl.BlockSpec((None, D_MODEL, D_EXPERT), wsel),
            pl.BlockSpec((None, D_MODEL, D_EXPERT), wsel),
            pl.BlockSpec((None, D_EXPERT, D_MODEL), wsel),
        ],
        out_specs=pl.BlockSpec((TS_MOE, D_MODEL), row),
    )
    return pl.pallas_call(
        _moe_kernel,
        grid_spec=grid_spec,
        out_shape=jax.ShapeDtypeStruct((xs.shape[0], D_MODEL), BF16),
        compiler_params=pltpu.CompilerParams(
            dimension_semantics=("arbitrary",), vmem_limit_bytes=VMEM_LIMIT),
        name="moe",
    )(blk, exp, nvalid, xs, wg, wu, wd)


def _combine_kernel(dst_ref, npiece_ref, x2_ref, pos_ref, ys_ref, o_ref, stage_ref, sem_ref):
    i = pl.program_id(0)
    n = pl.num_programs(0)
    cur = i % 2
    tm = TM_DISP

    def piece(tile, slot, sub, p):
        row = pl.multiple_of(p * ROW_ALIGN, ROW_ALIGN)
        src = pl.multiple_of(dst_ref[tile * MAX_PIECES + p], ROW_ALIGN)
        return pltpu.make_async_copy(ys_ref.at[pl.ds(src, ROW_ALIGN)],
                                     stage_ref.at[slot, sub, pl.ds(row, ROW_ALIGN)],
                                     sem_ref.at[slot, sub])

    def start_step(step, slot):
        for sub in range(DISP_SUB):
            tile = step * DISP_SUB + sub

            def body(p, c, tile=tile, sub=sub):
                piece(tile, slot, sub, p).start()
                return c
            lax.fori_loop(0, npiece_ref[tile], body, 0)

    @pl.when(i == 0)
    def _first():
        stage_ref[...] = jnp.zeros(stage_ref.shape, BF16)
        start_step(0, 0)

    @pl.when(i + 1 < n)
    def _prefetch():
        start_step(i + 1, 1 - cur)

    for sub in range(DISP_SUB):
        tile = i * DISP_SUB + sub
        rs = slice(sub * tm, (sub + 1) * tm)

        def wait_body(p, c, tile=tile, sub=sub):
            piece(tile, cur, sub, p).wait()
            return c

        lax.fori_loop(0, npiece_ref[tile], wait_body, 0)
        pos = pos_ref[rs, :]
        lane128 = lax.broadcasted_iota(jnp.int32, pos.shape, 1)
        p1 = _lane_col(pos, lane128, 0)
        p2 = _lane_col(pos, lane128, 1)
        lane = lax.broadcasted_iota(jnp.int32, (tm, SORT_ROWS), 1).astype(F32)
        pick = jnp.where((lane == p1) | (lane == p2), 1.0, 0.0).astype(BF16)
        o_ref[rs, :] = x2_ref[rs, :] + jnp.dot(pick, stage_ref[cur, sub],
                                               preferred_element_type=F32)


def _combine(x2, pos, ys, dst, npiece):
    t = x2.shape[0]
    tm = TM_DISP * DISP_SUB
    tok = lambda i, *_: (i, 0)
    grid_spec = pltpu.PrefetchScalarGridSpec(
        num_scalar_prefetch=2,
        grid=(t // tm,),
        in_specs=[
            pl.BlockSpec((tm, D_MODEL), tok),
            pl.BlockSpec((tm, LANES), tok),
            pl.BlockSpec(memory_space=pl.ANY),
        ],
        out_specs=pl.BlockSpec((tm, D_MODEL), tok),
        scratch_shapes=[
            pltpu.VMEM((2, DISP_SUB, SORT_ROWS, D_MODEL), BF16),
            pltpu.SemaphoreType.DMA((2, DISP_SUB)),
        ],
    )
    return pl.pallas_call(
        _combine_kernel,
        grid_spec=grid_spec,
        out_shape=jax.ShapeDtypeStruct((t, D_MODEL), F32),
        compiler_params=pltpu.CompilerParams(
            dimension_semantics=("arbitrary",), vmem_limit_bytes=VMEM_LIMIT),
        name="combine",
    )(dst, npiece, x2, pos, ys)


def _sparse_moe(h2, route, cnt, x2, wg, wu, wd):
    t = h2.shape[0]
    ntiles, nsteps = _moe_layout(t)
    rows = (nsteps + 1) * TS_MOE
    i32 = jnp.int32
    counts = cnt[:, 0, :N_EXPERTS].astype(i32)
    padded = (counts + (ROW_ALIGN - 1)) // ROW_ALIGN * ROW_ALIGN
    total = jnp.sum(padded, axis=0)
    reg_rows = (total + (TS_MOE - 1)) // TS_MOE * TS_MOE
    region = jnp.cumsum(reg_rows) - reg_rows
    base = region[None, :] + jnp.cumsum(padded, axis=0) - padded
    np_e = padded // ROW_ALIGN
    cum = jnp.cumsum(np_e, axis=1)
    npiece = cum[:, -1].astype(i32)
    p = jnp.arange(MAX_PIECES, dtype=i32)
    owns = (p[None, :, None] >= (cum - np_e)[:, None, :]) & (p[None, :, None] < cum[:, None, :])
    piece_dst = base[:, None, :] + (p[None, :, None] - (cum - np_e)[:, None, :]) * ROW_ALIGN
    dst = jnp.sum(jnp.where(owns, piece_dst, 0), axis=2).reshape(-1).astype(i32)
    nt_e = (reg_rows - total) // ROW_ALIGN
    cumt = jnp.cumsum(nt_e)
    q = jnp.arange(N_EXPERTS * (TS_MOE // ROW_ALIGN), dtype=i32)
    owns_q = (q[:, None] >= (cumt - nt_e)[None, :]) & (q[:, None] < cumt[None, :])
    tail_dst = (region + total)[None, :] + (q[:, None] - (cumt - nt_e)[None, :]) * ROW_ALIGN
    tail = jnp.sum(jnp.where(owns_q, tail_dst, 0), axis=1).astype(i32)
    ntail = cumt[-1:].astype(i32)
    ntile_e = reg_rows // TS_MOE
    first_t = jnp.cumsum(ntile_e) - ntile_e
    nvalid = jnp.sum(ntile_e).reshape(1).astype(i32)
    s = jnp.arange(nsteps, dtype=i32)
    owns_s = (s[:, None] >= first_t[None, :]) & (s[:, None] < (first_t + ntile_e)[None, :])
    spare = s >= nvalid[0]
    exp = jnp.where(spare, N_EXPERTS - 1, jnp.sum(
        jnp.where(owns_s, jnp.arange(N_EXPERTS, dtype=i32)[None, :], 0), axis=1)).astype(i32)
    blk = jnp.where(spare, nsteps, jnp.sum(
        jnp.where(owns_s, (region // TS_MOE - first_t)[None, :] + s[:, None], 0),
        axis=1)).astype(i32)
    tri = (jnp.arange(TM_DISP)[:, None] > jnp.arange(TM_DISP)[None, :]).astype(BF16)
    upper = (jnp.arange(LANES)[:, None] < jnp.arange(LANES)[None, :]).astype(BF16)
    xs, pos = _dispatch(h2, route, dst, npiece, tail, ntail, tri, upper, rows)
    ys = _moe(xs, blk, exp, nvalid, wg, wu, wd)
    return _combine(x2, pos, ys, dst, npiece)


def _rope_tables(positions):
    inv = ROPE_THETA ** (-jnp.arange(0, ROPE_DIM, 2, dtype=F32) / ROPE_DIM)
    ang = positions.astype(F32).reshape(-1, 1) * inv
    cos, sin = jnp.cos(ang), jnp.sin(ang)
    t = ang.shape[0]
    ones = jnp.ones((t, HEAD_DIM - ROPE_DIM), F32)
    zeros = jnp.zeros((t, HEAD_DIM - ROPE_DIM), F32)
    cos64 = jnp.concatenate([cos, cos, ones], axis=-1)
    sin64 = jnp.concatenate([-sin, sin, zeros], axis=-1)
    return jnp.tile(cos64, (1, 2)), jnp.tile(sin64, (1, 2))


def _s5_params(lam_re, lam_im, log_dt, b_re, b_im, c_re, c_im, bsz):
    dt = jnp.exp(log_dt)[:, None]
    mag = jnp.exp(lam_re * dt)
    lb_re = mag * jnp.cos(lam_im * dt)
    lb_im = mag * jnp.sin(lam_im * dt)
    den = lam_re * lam_re + lam_im * lam_im
    k_re = ((lb_re - 1.0) * lam_re + lb_im * lam_im) / den
    k_im = (lb_im * lam_re - (lb_re - 1.0) * lam_im) / den
    bb_re = k_re[..., None] * b_re - k_im[..., None] * b_im
    bb_im = k_re[..., None] * b_im + k_im[..., None] * b_re
    eye = jnp.eye(SSM_GROUPS, dtype=F32)
    blk_b = lambda m: jnp.einsum('gph,gk->ghkp', m, eye).reshape(SSM_WIDTH, N_STATE)
    blk_c = lambda m: jnp.einsum('ghp,gk->gpkh', m, eye).reshape(N_STATE, SSM_WIDTH)
    bmat = jnp.concatenate([blk_b(bb_re), blk_b(bb_im)], axis=1).astype(BF16)
    cmat = jnp.concatenate([blk_c(c_re), blk_c(-c_im)], axis=0).astype(BF16)
    a_re = jnp.broadcast_to(lb_re.reshape(1, N_STATE), (bsz, N_STATE))
    a_im = jnp.broadcast_to(lb_im.reshape(1, N_STATE), (bsz, N_STATE))
    return bmat, a_re, a_im, cmat


def _time_major_perm(bsz, chunk):
    r = jnp.arange(bsz * chunk)
    src = (r % bsz) * chunk + r // bsz
    perm = (src[:, None] == jnp.arange(bsz * chunk)[None, :]).astype(BF16)
    return perm, perm.T


def kernel(x, positions, norm_mix_g, w_in, q_norm_g, k_norm_g, lambda_q1, lambda_k1, lambda_q2, lambda_k2, subln_g, w_o_attn, ssm_lambda_re, ssm_lambda_im, ssm_log_dt, ssm_b_re, ssm_b_im, ssm_c_re, ssm_c_im, ssm_d, w_glu, w_out, norm_ffn_g, w_router_group, b_router_group, w_router_expert, b_router_expert, w_expert_gate, w_expert_up, w_expert_down):
    bsz, seq, _ = x.shape
    assert bsz == SUBLANES and seq % TQ == 0 and seq % SSM_CHUNK == 0
    assert norm_mix_g.shape[0] == 1
    t = bsz * seq
    x2d = x.reshape(t, D_MODEL)
    l = 0

    cos_t, sin_t = _rope_tables(positions)
    w_qkvu = w_in[l][:, :QKVU_COLS].astype(BF16)
    w_gates = w_in[l][:, QKVU_COLS:].astype(BF16)
    qg = jnp.tile(q_norm_g[l].reshape(1, HEAD_DIM), (1, 2))
    kg = jnp.tile(k_norm_g[l].reshape(1, HEAD_DIM), (1, 2))
    q, k, v, u = _in_proj(x2d, norm_mix_g[l].reshape(1, D_MODEL), w_qkvu, cos_t, sin_t, qg, kg)

    lam = (jnp.exp(jnp.sum(lambda_q1[l] * lambda_k1[l]))
           - jnp.exp(jnp.sum(lambda_q2[l] * lambda_k2[l])) + LAM_INIT).reshape(1)
    o_attn = _diff_attn(q, k, v, lam, subln_g[l].reshape(1, V_DIM), bsz, seq)

    bmat, a_re, a_im, cmat = _s5_params(
        ssm_lambda_re[l], ssm_lambda_im[l], ssm_log_dt[l], ssm_b_re[l], ssm_b_im[l],
        ssm_c_re[l], ssm_c_im[l], bsz)
    perm, perm_t = _time_major_perm(bsz, SSM_CHUNK)
    gy = _s5_scan(u.reshape(bsz, seq, SSM_WIDTH), perm, perm_t, bmat, a_re, a_im, cmat,
                  ssm_d[l].reshape(1, SSM_WIDTH)).reshape(t, SSM_WIDTH)

    w_r = jnp.concatenate(
        [w_router_expert[l].reshape(D_MODEL, N_EXPERTS), w_router_group[l],
         jnp.zeros((D_MODEL, LANES - N_EXPERTS - N_EXPERT_GROUPS), F32)], axis=1)
    b_r = jnp.concatenate(
        [b_router_expert[l].reshape(N_EXPERTS), b_router_group[l],
         jnp.zeros((LANES - N_EXPERTS - N_EXPERT_GROUPS,), F32)]).reshape(1, LANES)
    wr_hi = w_r.astype(BF16)
    wr_lo = (w_r - wr_hi.astype(F32)).astype(BF16)
    x2, h2, route, cnt = _merge(
        x2d, o_attn, gy, norm_mix_g[l].reshape(1, D_MODEL), w_gates,
        w_o_attn[l].astype(BF16), w_glu[l].astype(BF16), w_out[l].astype(BF16),
        norm_ffn_g[l].reshape(1, D_MODEL), wr_hi, wr_lo, b_r)

    wg = w_expert_gate[l].astype(BF16).reshape(N_EXPERTS, D_MODEL, D_EXPERT)
    wu = w_expert_up[l].astype(BF16).reshape(N_EXPERTS, D_MODEL, D_EXPERT)
    wd = w_expert_down[l].astype(BF16).reshape(N_EXPERTS, D_EXPERT, D_MODEL)
    out = _sparse_moe(h2, route, cnt, x2, wg, wu, wd)
    return out.reshape(bsz, seq, D_MODEL)
```

```python
import functools
import math

import jax
import jax.numpy as jnp
from jax import lax
from jax.experimental import pallas as pl
from jax.experimental.pallas import tpu as pltpu

F32 = jnp.float32
BF16 = jnp.bfloat16

D_MODEL = 1024
N_HEADS = 4
HEAD_DIM = 64
V_DIM = 2 * HEAD_DIM
ATTN_WIDTH = N_HEADS * V_DIM
ROPE_THETA = 500000.0
ROPE_DIM = HEAD_DIM // 4
ROPE_HALF = ROPE_DIM // 2
SSM_WIDTH = D_MODEL // 2
SSM_GROUP = 16
SSM_GROUPS = SSM_WIDTH // SSM_GROUP
SSM_STATE = 64
N_STATE = SSM_GROUPS * SSM_STATE
N_EXPERT_GROUPS = 4
EXPERTS_PER_GROUP = 8
N_EXPERTS = N_EXPERT_GROUPS * EXPERTS_PER_GROUP
D_EXPERT = D_MODEL // 4
EPS = 1e-6
LAM_INIT = 0.8 - 0.6 * math.exp(-0.3 * 0)
QKVU_COLS = 4 * ATTN_WIDTH
LANES = 128
SUBLANES = 8
NEG_BIG = -1e30
LOG2_E = math.log2(math.e)

VMEM_LIMIT = 48 * 1024 * 1024

TM_IN = 512
TQ = 512
SSM_CHUNK = 64
SCAN_LANES = 512
TM_MERGE = 512
TM_DISP = 256
DISP_SUB = 2
TS_MOE = 1024
ROW_ALIGN = 16


def _const_spec(shape):
    return pl.BlockSpec(shape, lambda *_: (0,) * len(shape))


def _in_proj_kernel(x_ref, g_ref, w_ref, cos_ref, sin_ref, qg_ref, kg_ref,
                    q_ref, k_ref, v_ref, u_ref):
    x = x_ref[...]
    ms = jnp.mean(x * x, axis=-1, keepdims=True)
    h = (x * lax.rsqrt(ms + EPS) * g_ref[...]).astype(BF16)
    cos_t = cos_ref[...]
    sin_t = sin_ref[...]
    lane = lax.broadcasted_iota(jnp.int32, (x.shape[0], LANES), 1)
    first_comp = lane < HEAD_DIM
    low_half = (lane % HEAD_DIM) < ROPE_HALF

    def norm_rope(blk, gain, scale):
        sq = blk * blk
        s_all = jnp.sum(sq, axis=-1, keepdims=True)
        s_lo = jnp.sum(jnp.where(first_comp, sq, 0.0), axis=-1, keepdims=True)
        ssum = jnp.where(first_comp, s_lo, s_all - s_lo)
        nb = blk * lax.rsqrt(ssum * (1.0 / HEAD_DIM) + EPS) * gain
        up = pltpu.roll(nb, LANES - ROPE_HALF, axis=1)
        dn = pltpu.roll(nb, ROPE_HALF, axis=1)
        partner = jnp.where(low_half, up, dn)
        return ((nb * cos_t + partner * sin_t) * scale).astype(BF16)

    for j in range(0, N_HEADS, 2):
        sl2 = slice(j * LANES, (j + 2) * LANES)
        qb = jnp.dot(h, w_ref[:, sl2], preferred_element_type=F32)
        kb = jnp.dot(h, w_ref[:, ATTN_WIDTH + j * LANES:ATTN_WIDTH + (j + 2) * LANES],
                     preferred_element_type=F32)
        for jj in range(2):
            sl = slice((j + jj) * LANES, (j + jj + 1) * LANES)
            half = slice(jj * LANES, (jj + 1) * LANES)
            q_ref[:, sl] = norm_rope(qb[:, half], qg_ref[...], LOG2_E * HEAD_DIM ** -0.5)
            k_ref[:, sl] = norm_rope(kb[:, half], kg_ref[...], 1.0)
    v_ref[...] = jnp.dot(h, w_ref[:, 2 * ATTN_WIDTH:3 * ATTN_WIDTH],
                         preferred_element_type=F32).astype(BF16)
    u_ref[...] = jnp.dot(h, w_ref[:, 3 * ATTN_WIDTH:4 * ATTN_WIDTH],
                         preferred_element_type=F32).astype(BF16)


def _in_proj(x2d, g, w, cos_t, sin_t, qg, kg):
    t = x2d.shape[0]
    tok = lambda i: (i, 0)
    out = jax.ShapeDtypeStruct((t, ATTN_WIDTH), BF16)
    return pl.pallas_call(
        _in_proj_kernel,
        grid=(t // TM_IN,),
        in_specs=[
            pl.BlockSpec((TM_IN, D_MODEL), tok),
            _const_spec((1, D_MODEL)),
            _const_spec((D_MODEL, QKVU_COLS)),
            pl.BlockSpec((TM_IN, LANES), tok),
            pl.BlockSpec((TM_IN, LANES), tok),
            _const_spec((1, LANES)),
            _const_spec((1, LANES)),
        ],
        out_specs=[pl.BlockSpec((TM_IN, ATTN_WIDTH), tok)] * 4,
        out_shape=[out] * 4,
        compiler_params=pltpu.CompilerParams(
            dimension_semantics=("arbitrary",), vmem_limit_bytes=VMEM_LIMIT),
        name="in_proj",
    )(x2d, g, w, cos_t, sin_t, qg, kg)


def _attn_kernel(qi_ref, kj_ref, lam_ref, q_ref, k_ref, v_ref, sg_ref, o_ref,
                 qs_ref, m_ref, l_ref, acc_ref):
    step = pl.program_id(1)
    qi = qi_ref[step]
    kj = kj_ref[step]

    @pl.when(kj == 0)
    def _init():
        for h in range(N_HEADS):
            q = q_ref[:, h * LANES:(h + 1) * LANES]
            lane = lax.broadcasted_iota(jnp.int32, q.shape, 1)
            zero = jnp.zeros_like(q)
            qs_ref[h, 0:TQ, :] = jnp.where(lane < HEAD_DIM, q, zero)
            qs_ref[h, TQ:2 * TQ, :] = jnp.where(lane < HEAD_DIM, zero, q)
        m_ref[...] = jnp.full(m_ref.shape, NEG_BIG, F32)
        l_ref[...] = jnp.zeros(l_ref.shape, F32)
        acc_ref[...] = jnp.zeros(acc_ref.shape, F32)

    def update(h, masked):
        hs = slice(h * LANES, (h + 1) * LANES)
        s = lax.dot_general(qs_ref[h], k_ref[:, hs], (((1,), (1,)), ((), ())),
                            preferred_element_type=F32)
        if masked:
            row = lax.broadcasted_iota(jnp.int32, s.shape, 0) % TQ
            col = lax.broadcasted_iota(jnp.int32, s.shape, 1)
            s = jnp.where(col <= row, s, NEG_BIG)
        m_old = m_ref[h]
        m_new = jnp.maximum(m_old, jnp.max(s, axis=-1, keepdims=True))
        alpha = jnp.exp2(m_old - m_new)
        p = jnp.exp2(s - jnp.concatenate([m_new] * (TQ // LANES), axis=1))
        psum = p[:, 0:LANES]
        for c in range(1, TQ // LANES):
            psum = psum + p[:, c * LANES:(c + 1) * LANES]
        l_ref[h] = alpha * l_ref[h] + psum
        acc_ref[h] = alpha * acc_ref[h] + jnp.dot(
            p.astype(BF16), v_ref[:, hs], preferred_element_type=F32)
        m_ref[h] = m_new

    @pl.when(kj < qi)
    def _off_diag():
        for h in range(N_HEADS):
            update(h, False)

    @pl.when(kj == qi)
    def _diag():
        for h in range(N_HEADS):
            update(h, True)
            o = acc_ref[h] / jnp.sum(l_ref[h], axis=-1, keepdims=True)
            d = o[0:TQ, :] - lam_ref[0] * o[TQ:2 * TQ, :]
            ms = jnp.mean(d * d, axis=-1, keepdims=True)
            d = d * lax.rsqrt(ms + EPS) * sg_ref[...] * (1.0 - LAM_INIT)
            o_ref[:, h * LANES:(h + 1) * LANES] = d.astype(BF16)


def _diff_attn(q, k, v, lam, subln_g, bsz, seq):
    nq = seq // TQ
    pairs = [(i, j) for i in range(nq) for j in range(i + 1)]
    qi = jnp.asarray([p[0] for p in pairs], jnp.int32)
    kj = jnp.asarray([p[1] for p in pairs], jnp.int32)
    q_map = lambda b, s, qi, kj: (b * nq + qi[s], 0)
    k_map = lambda b, s, qi, kj: (b * nq + kj[s], 0)
    grid_spec = pltpu.PrefetchScalarGridSpec(
        num_scalar_prefetch=2,
        grid=(bsz, len(pairs)),
        in_specs=[
            pl.BlockSpec(memory_space=pltpu.SMEM),
            pl.BlockSpec((TQ, ATTN_WIDTH), q_map),
            pl.BlockSpec((TQ, ATTN_WIDTH), k_map),
            pl.BlockSpec((TQ, ATTN_WIDTH), k_map),
            pl.BlockSpec((1, LANES), lambda b, s, qi, kj: (0, 0)),
        ],
        out_specs=pl.BlockSpec((TQ, ATTN_WIDTH), q_map),
        scratch_shapes=[
            pltpu.VMEM((N_HEADS, 2 * TQ, LANES), BF16),
            pltpu.VMEM((N_HEADS, 2 * TQ, LANES), F32),
            pltpu.VMEM((N_HEADS, 2 * TQ, LANES), F32),
            pltpu.VMEM((N_HEADS, 2 * TQ, LANES), F32),
        ],
    )
    return pl.pallas_call(
        _attn_kernel,
        grid_spec=grid_spec,
        out_shape=jax.ShapeDtypeStruct((bsz * seq, ATTN_WIDTH), BF16),
        compiler_params=pltpu.CompilerParams(
            dimension_semantics=("arbitrary", "arbitrary"),
            vmem_limit_bytes=VMEM_LIMIT),
        name="diff_attn",
    )(qi, kj, lam, q, k, v, subln_g)


def _gelu_tanh(x):
    c = math.sqrt(2.0 / math.pi)
    return 0.5 * x * (1.0 + jnp.tanh(c * (x + 0.044715 * (x * x * x))))


def _s5_kernel(u_ref, p_ref, pt_ref, b_ref, are_ref, aim_ref, c_ref, d_ref, o_ref,
               st_ref, state_ref):
    bsz, chunk, width = u_ref.shape
    rows = bsz * chunk

    @pl.when(pl.program_id(0) == 0)
    def _init():
        state_ref[...] = jnp.zeros(state_ref.shape, F32)

    u_bt = u_ref[...].reshape(rows, width)
    u_tm = jnp.dot(p_ref[...], u_bt, preferred_element_type=F32)
    u_tm_b = u_tm.astype(BF16)

    slab = SCAN_LANES // SSM_STATE * SSM_GROUP
    g_parts = []
    for ch in range(N_STATE // SCAN_LANES):
        re = slice(ch * SCAN_LANES, (ch + 1) * SCAN_LANES)
        im = slice(N_STATE + ch * SCAN_LANES, N_STATE + (ch + 1) * SCAN_LANES)
        cs = slice(ch * slab, (ch + 1) * slab)
        st_ref[:, re] = jnp.dot(u_tm_b[:, cs], b_ref[cs, re], preferred_element_type=F32)
        st_ref[:, im] = jnp.dot(u_tm_b[:, cs], b_ref[cs, im], preferred_element_type=F32)
        a_re = are_ref[:, re]
        a_im = aim_ref[:, re]

        def body(t, carry, re=re, im=im, a_re=a_re, a_im=a_im):
            s_re, s_im = carry
            r0 = pl.multiple_of(t * SUBLANES, SUBLANES)
            n_re = a_re * s_re - a_im * s_im + st_ref[pl.ds(r0, SUBLANES), re]
            n_im = a_re * s_im + a_im * s_re + st_ref[pl.ds(r0, SUBLANES), im]
            st_ref[pl.ds(r0, SUBLANES), re] = n_re
            st_ref[pl.ds(r0, SUBLANES), im] = n_im
            return n_re, n_im

        s_re, s_im = lax.fori_loop(0, chunk, body, (state_ref[:, re], state_ref[:, im]),
                                   unroll=True)
        state_ref[:, re] = s_re
        state_ref[:, im] = s_im
        y = (jnp.dot(st_ref[:, re].astype(BF16), c_ref[re, cs], preferred_element_type=F32)
             + jnp.dot(st_ref[:, im].astype(BF16), c_ref[im, cs], preferred_element_type=F32))
        y = y + d_ref[:, cs] * u_tm[:, cs]
        g_parts.append(_gelu_tanh(y).astype(BF16))

    g_tm = jnp.concatenate(g_parts, axis=1)
    g_bt = jnp.dot(pt_ref[...], g_tm, preferred_element_type=F32)
    o_ref[...] = g_bt.astype(BF16).reshape(bsz, chunk, width)


def _s5_scan(u3, perm, perm_t, bmat, a_re, a_im, cmat, dvec):
    bsz, seq, width = u3.shape
    rows = bsz * SSM_CHUNK
    blk = pl.BlockSpec((bsz, SSM_CHUNK, width), lambda c: (0, c, 0))
    return pl.pallas_call(
        _s5_kernel,
        grid=(seq // SSM_CHUNK,),
        in_specs=[
            blk,
            _const_spec((rows, rows)),
            _const_spec((rows, rows)),
            _const_spec((width, 2 * N_STATE)),
            _const_spec((bsz, N_STATE)),
            _const_spec((bsz, N_STATE)),
            _const_spec((2 * N_STATE, width)),
            _const_spec((1, width)),
        ],
        out_specs=blk,
        out_shape=jax.ShapeDtypeStruct(u3.shape, BF16),
        scratch_shapes=[
            pltpu.VMEM((rows, 2 * N_STATE), F32),
            pltpu.VMEM((bsz, 2 * N_STATE), F32),
        ],
        compiler_params=pltpu.CompilerParams(
            dimension_semantics=("arbitrary",), vmem_limit_bytes=VMEM_LIMIT),
        name="s5_scan",
    )(u3, perm, perm_t, bmat, a_re, a_im, cmat, dvec)


def _merge_kernel(x_ref, oa_ref, gy_ref, g1_ref, wg_ref, woa_ref, wglu_ref, wout_ref,
                  g2_ref, wr_ref, br_ref, x2_ref, h2_ref, route_ref, cnt_ref):
    for half in range(TM_MERGE // TM_DISP):
        rs = slice(half * TM_DISP, (half + 1) * TM_DISP)
        _merge_rows(x_ref[rs, :], oa_ref[rs, :], gy_ref[rs, :], g1_ref, wg_ref, woa_ref,
                    wglu_ref, wout_ref, g2_ref, wr_ref, br_ref,
                    x2_ref.at[rs, :], h2_ref.at[rs, :], route_ref.at[rs, :],
                    cnt_ref.at[half])


def _merge_rows(x, o_in, gy, g1_ref, wg_ref, woa_ref, wglu_ref, wout_ref, g2_ref, wr_ref,
                br_ref, x2_ref, h2_ref, route_ref, cnt_ref):
    ms = jnp.mean(x * x, axis=-1, keepdims=True)
    h = (x * lax.rsqrt(ms + EPS) * g1_ref[...]).astype(BF16)
    o_a = jnp.dot(o_in, woa_ref[...], preferred_element_type=F32)
    z_lin = jnp.dot(gy, wglu_ref[:, 0:D_MODEL], preferred_element_type=F32)
    z_gate = jnp.dot(gy, wglu_ref[:, D_MODEL:2 * D_MODEL], preferred_element_type=F32)
    o_s = z_lin * jax.nn.sigmoid(z_gate)
    gate_a = jax.nn.sigmoid(jnp.dot(h, wg_ref[:, 0:D_MODEL], preferred_element_type=F32))
    merged = gate_a * o_a
    gate_s = jax.nn.sigmoid(
        jnp.dot(h, wg_ref[:, D_MODEL:2 * D_MODEL], preferred_element_type=F32))
    merged = merged + gate_s * o_s
    x2 = x + jnp.dot(merged.astype(BF16), wout_ref[...], preferred_element_type=F32)
    x2_ref[...] = x2

    ms2 = jnp.mean(x2 * x2, axis=-1, keepdims=True)
    h2 = x2 * lax.rsqrt(ms2 + EPS) * g2_ref[...]
    h2_hi = h2.astype(BF16)
    h2_lo = (h2 - h2_hi.astype(F32)).astype(BF16)
    h2_ref[...] = h2_hi

    both = jnp.dot(h2_hi, wr_ref[...], preferred_element_type=F32)
    logits = (both[:, 0:LANES] + both[:, LANES:2 * LANES]
              + jnp.dot(h2_lo, wr_ref[:, 0:LANES], preferred_element_type=F32)
              + br_ref[...])
    lane = lax.broadcasted_iota(jnp.int32, logits.shape, 1).astype(F32)
    is_grp = (lane >= N_EXPERTS) & (lane < N_EXPERTS + N_EXPERT_GROUPS)
    gl = jnp.where(is_grp, logits, NEG_BIG)
    gmax = jnp.max(gl, axis=-1, keepdims=True)
    gsum = jnp.sum(jnp.where(is_grp, jnp.exp(gl - gmax), 0.0), axis=-1, keepdims=True)
    p_grp = 1.0 / gsum
    big = float(4 * LANES)
    grp = jnp.min(jnp.where(is_grp & (gl == gmax), lane, big), axis=-1,
                  keepdims=True) - N_EXPERTS
    sel = logits
    for g in range(1, N_EXPERT_GROUPS):
        rolled = pltpu.roll(logits, LANES - g * EXPERTS_PER_GROUP, axis=1)
        sel = jnp.where(grp == g, rolled, sel)
    in_grp = lane < EXPERTS_PER_GROUP
    es = jnp.where(in_grp, sel, NEG_BIG)
    top1 = jnp.max(es, axis=-1, keepdims=True)
    i1 = jnp.min(jnp.where(in_grp & (es == top1), lane, big), axis=-1, keepdims=True)
    es2 = jnp.where(lane == i1, NEG_BIG, es)
    top2 = jnp.max(es2, axis=-1, keepdims=True)
    i2 = jnp.min(jnp.where(in_grp & (lane != i1) & (es2 == top2), lane, big), axis=-1,
                 keepdims=True)
    e2 = jnp.exp(top2 - top1)
    w1 = p_grp / (1.0 + e2)
    w2 = p_grp * e2 / (1.0 + e2)
    e1 = grp * EXPERTS_PER_GROUP + i1
    e2x = grp * EXPERTS_PER_GROUP + i2
    route_ref[...] = (jnp.where(lane == 0.0, e1, 0.0) + jnp.where(lane == 1.0, e2x, 0.0)
                      + jnp.where(lane == 2.0, w1, 0.0) + jnp.where(lane == 3.0, w2, 0.0))
    picked = jnp.where((lane == e1) | (lane == e2x), 1.0, 0.0)
    cnt_ref[...] = jnp.sum(picked, axis=0, keepdims=True)


def _merge(x2d, o_attn, gy, g1, w_gates, w_oa, w_glu, w_out, g2, w_router, b_r):
    t = x2d.shape[0]
    tok = lambda i: (i, 0)
    return pl.pallas_call(
        _merge_kernel,
        grid=(t // TM_MERGE,),
        in_specs=[
            pl.BlockSpec((TM_MERGE, D_MODEL), tok),
            pl.BlockSpec((TM_MERGE, ATTN_WIDTH), tok),
            pl.BlockSpec((TM_MERGE, SSM_WIDTH), tok),
            _const_spec((1, D_MODEL)),
            _const_spec((D_MODEL, 2 * D_MODEL)),
            _const_spec((ATTN_WIDTH, D_MODEL)),
            _const_spec((SSM_WIDTH, 2 * D_MODEL)),
            _const_spec((D_MODEL, D_MODEL)),
            _const_spec((1, D_MODEL)),
            _const_spec((D_MODEL, 2 * LANES)),
            _const_spec((1, LANES)),
        ],
        out_specs=[
            pl.BlockSpec((TM_MERGE, D_MODEL), tok),
            pl.BlockSpec((TM_MERGE, D_MODEL), tok),
            pl.BlockSpec((TM_MERGE, LANES), tok),
            pl.BlockSpec((TM_MERGE // TM_DISP, 1, LANES), lambda i: (i, 0, 0)),
        ],
        out_shape=[
            jax.ShapeDtypeStruct((t, D_MODEL), F32),
            jax.ShapeDtypeStruct((t, D_MODEL), BF16),
            jax.ShapeDtypeStruct((t, LANES), F32),
            jax.ShapeDtypeStruct((t // TM_DISP, 1, LANES), F32),
        ],
        compiler_params=pltpu.CompilerParams(
            dimension_semantics=("arbitrary",), vmem_limit_bytes=VMEM_LIMIT),
        name="merge",
    )(x2d, o_attn, gy, g1, w_gates, w_oa, w_glu, w_out, g2, w_router, b_r)


XS_COLS = D_MODEL + LANES
SORT_ROWS = 2 * TM_DISP + N_EXPERTS * ROW_ALIGN
MAX_PIECES = SORT_ROWS // ROW_ALIGN
POS_RADIX = 32


def _moe_layout(t):
    ntiles = t // TM_DISP
    max_rows = 2 * t + ntiles * N_EXPERTS * (ROW_ALIGN - 1)
    nsteps = -(-max_rows // TS_MOE) + N_EXPERTS
    return ntiles, nsteps


def _lane_col(arr, lane, k):
    return jnp.sum(jnp.where(lane == k, arr, 0.0), axis=-1, keepdims=True)


def _compact_tile(h, route, tri_ref, upper_ref, pos_ref):
    tm = TM_DISP
    lane = lax.broadcasted_iota(jnp.int32, route.shape, 1)
    lane_f = lane.astype(F32)
    e1 = _lane_col(route, lane, 0)
    e2 = _lane_col(route, lane, 1)
    is1 = lane_f == e1
    is2 = lane_f == e2
    picked = jnp.where(is1 | is2, 1.0, 0.0)
    rank_all = jnp.dot(tri_ref[...], picked.astype(BF16), preferred_element_type=F32)
    cnt_row = jnp.sum(picked, axis=0, keepdims=True)
    pad_row = jnp.floor((cnt_row + (ROW_ALIGN - 1)) * (1.0 / ROW_ALIGN)) * ROW_ALIGN
    off = jnp.dot(jnp.broadcast_to(pad_row, (SUBLANES, LANES)).astype(BF16), upper_ref[...],
                  preferred_element_type=F32)[0:1, :]
    posmat = rank_all + off
    pos1 = jnp.sum(jnp.where(is1, posmat, 0.0), axis=-1, keepdims=True)
    pos2 = jnp.sum(jnp.where(is2, posmat, 0.0), axis=-1, keepdims=True)
    pos_ref[...] = jnp.where(lane == 0, pos1, 0.0) + jnp.where(lane == 1, pos2, 0.0)

    def digits(p):
        hi = jnp.floor(p * (1.0 / POS_RADIX))
        return hi, p - POS_RADIX * hi

    d1h, d1l = digits(pos1)
    d2h, d2l = digits(pos2)
    dig = (jnp.where(lane == 0, d1h, 0.0) + jnp.where(lane == 1, d1l, 0.0)
           + jnp.where(lane == 2, d2h, 0.0) + jnp.where(lane == 3, d2l, 0.0)).astype(BF16)
    eye8 = (lax.broadcasted_iota(jnp.int32, (SUBLANES, LANES), 0)
            == lax.broadcasted_iota(jnp.int32, (SUBLANES, LANES), 1)).astype(BF16)
    rows = lax.dot_general(eye8, dig, (((1,), (1,)), ((), ())), preferred_element_type=F32)
    p1_row = POS_RADIX * rows[0:1, :] + rows[1:2, :]
    p2_row = POS_RADIX * rows[2:3, :] + rows[3:4, :]
    sub = lax.broadcasted_iota(jnp.int32, (SORT_ROWS, tm), 0).astype(F32)
    perm = jnp.where((sub == p1_row) | (sub == p2_row), 1.0, 0.0).astype(BF16)

    def hi_lo(w):
        hi = w.astype(BF16).astype(F32)
        return hi, (w - hi).astype(BF16).astype(F32)

    w1h, w1l = hi_lo(_lane_col(route, lane, 2))
    w2h, w2l = hi_lo(_lane_col(route, lane, 3))
    aux = (jnp.where(lane == 0, e1, 0.0) + jnp.where(lane == 1, e2, 0.0)
           + jnp.where(lane == 2, w1h, 0.0) + jnp.where(lane == 3, w1l, 0.0)
           + jnp.where(lane == 4, w2h, 0.0) + jnp.where(lane == 5, w2l, 0.0)).astype(BF16)
    payload = jnp.concatenate([h, aux], axis=1)
    return jnp.dot(perm, payload, preferred_element_type=F32).astype(BF16)


def _dispatch_kernel(dst_ref, npiece_ref, tail_ref, ntail_ref, h_ref, route_ref, tri_ref,
                     upper_ref, xs_ref, pos_ref, buf_ref, zbuf_ref, sem_ref, tsem_ref):
    i = pl.program_id(0)
    last = pl.num_programs(0) - 1
    cur = i % 2

    def piece(tile, slot, sub, p):
        src = pl.multiple_of(p * ROW_ALIGN, ROW_ALIGN)
        dst = pl.multiple_of(dst_ref[tile * MAX_PIECES + p], ROW_ALIGN)
        return pltpu.make_async_copy(buf_ref.at[slot, sub, pl.ds(src, ROW_ALIGN)],
                                     xs_ref.at[pl.ds(dst, ROW_ALIGN)], sem_ref.at[slot, sub])

    def start_all(tile, slot, sub):
        def body(p, c):
            piece(tile, slot, sub, p).start()
            return c
        lax.fori_loop(0, npiece_ref[tile], body, 0)

    def wait_all(tile, slot, sub):
        def body(p, c):
            piece(tile, slot, sub, p).wait()
            return c
        lax.fori_loop(0, npiece_ref[tile], body, 0)

    for sub in range(DISP_SUB):
        rs = slice(sub * TM_DISP, (sub + 1) * TM_DISP)
        buf_ref[cur, sub] = _compact_tile(h_ref[rs, :], route_ref[rs, :], tri_ref, upper_ref,
                                          pos_ref.at[rs, :])
    for sub in range(DISP_SUB):
        start_all(i * DISP_SUB + sub, cur, sub)

    @pl.when(i > 0)
    def _wait_prev():
        for sub in range(DISP_SUB):
            wait_all((i - 1) * DISP_SUB + sub, 1 - cur, sub)

    @pl.when(i == last)
    def _finish():
        for sub in range(DISP_SUB):
            wait_all(i * DISP_SUB + sub, cur, sub)
        zbuf_ref[...] = jnp.zeros(zbuf_ref.shape, BF16)

        def tail(p):
            dst = pl.multiple_of(tail_ref[p], ROW_ALIGN)
            return pltpu.make_async_copy(zbuf_ref, xs_ref.at[pl.ds(dst, ROW_ALIGN)],
                                         tsem_ref.at[0])

        def start_tail(p, c):
            tail(p).start()
            return c

        def wait_tail(p, c):
            tail(p).wait()
            return c

        lax.fori_loop(0, ntail_ref[0], start_tail, 0)
        lax.fori_loop(0, ntail_ref[0], wait_tail, 0)


def _dispatch(h2, route, dst, npiece, tail, ntail, tri, upper, rows):
    t = h2.shape[0]
    tm = TM_DISP * DISP_SUB
    tok = lambda i, *_: (i, 0)
    const = lambda i, *_: (0, 0)
    grid_spec = pltpu.PrefetchScalarGridSpec(
        num_scalar_prefetch=4,
        grid=(t // tm,),
        in_specs=[
            pl.BlockSpec((tm, D_MODEL), tok),
            pl.BlockSpec((tm, LANES), tok),
            pl.BlockSpec((TM_DISP, TM_DISP), const),
            pl.BlockSpec((LANES, LANES), const),
        ],
        out_specs=[
            pl.BlockSpec(memory_space=pl.ANY),
            pl.BlockSpec((tm, LANES), tok),
        ],
        scratch_shapes=[
            pltpu.VMEM((2, DISP_SUB, SORT_ROWS, XS_COLS), BF16),
            pltpu.VMEM((ROW_ALIGN, XS_COLS), BF16),
            pltpu.SemaphoreType.DMA((2, DISP_SUB)),
            pltpu.SemaphoreType.DMA((1,)),
        ],
    )
    return pl.pallas_call(
        _dispatch_kernel,
        grid_spec=grid_spec,
        out_shape=[
            jax.ShapeDtypeStruct((rows, XS_COLS), BF16),
            jax.ShapeDtypeStruct((t, LANES), F32),
        ],
        compiler_params=pltpu.CompilerParams(
            dimension_semantics=("arbitrary",), vmem_limit_bytes=VMEM_LIMIT),
        name="dispatch",
    )(dst, npiece, tail, ntail, h2, route, tri, upper)


def _moe_kernel(blk_ref, exp_ref, nvalid_ref, fresh_ref, xs_ref, wg_ref, wu_ref, wd_ref,
                ys_ref, wgb_ref, wub_ref, wdb_ref):
    s = pl.program_id(0)

    @pl.when(fresh_ref[s] == 1)
    def _cast_weights():
        wgb_ref[...] = wg_ref[...].astype(BF16)
        wub_ref[...] = wu_ref[...].astype(BF16)
        wdb_ref[...] = wd_ref[...].astype(BF16)

    @pl.when(s < nvalid_ref[0])
    def _compute():
        t = xs_ref[:, 0:D_MODEL]
        aux = xs_ref[:, D_MODEL:XS_COLS].astype(F32)
        lane = lax.broadcasted_iota(jnp.int32, aux.shape, 1)
        first = _lane_col(aux, lane, 0) == exp_ref[s].astype(F32)
        w = jnp.where(first, _lane_col(aux, lane, 2) + _lane_col(aux, lane, 3),
                      _lane_col(aux, lane, 4) + _lane_col(aux, lane, 5))
        hg = jnp.dot(t, wgb_ref[...], preferred_element_type=F32)
        hu = jnp.dot(t, wub_ref[...], preferred_element_type=F32)
        a = (hg * jax.nn.sigmoid(hg) * hu * w).astype(BF16)
        ys_ref[...] = jnp.dot(a, wdb_ref[...], preferred_element_type=F32).astype(BF16)

    @pl.when(s >= nvalid_ref[0])
    def _spare():
        ys_ref[...] = jnp.zeros(ys_ref.shape, BF16)


def _moe(xs, blk, exp, nvalid, fresh, wg, wu, wd):
    row = lambda s, blk, exp, nv, fr: (blk[s], 0)
    wsel = lambda s, blk, exp, nv, fr: (exp[s], 0, 0)
    grid_spec = pltpu.PrefetchScalarGridSpec(
        num_scalar_prefetch=4,
        grid=(blk.shape[0],),
        in_specs=[
            pl.BlockSpec((TS_MOE, XS_COLS), row),
            pl.BlockSpec((None, D_MODEL, D_EXPERT), wsel),
            pl.BlockSpec((None, D_MODEL, D_EXPERT), wsel),
            pl.BlockSpec((None, D_EXPERT, D_MODEL), wsel),
        ],
        out_specs=pl.BlockSpec((TS_MOE, D_MODEL), row),
        scratch_shapes=[
            pltpu.VMEM((D_MODEL, D_EXPERT), BF16),
            pltpu.VMEM((D_MODEL, D_EXPERT), BF16),
            pltpu.VMEM((D_EXPERT, D_MODEL), BF16),
        ],
    )
    return pl.pallas_call(
        _moe_kernel,
        grid_spec=grid_spec,
        out_shape=jax.ShapeDtypeStruct((xs.shape[0], D_MODEL), BF16),
        compiler_params=pltpu.CompilerParams(
            dimension_semantics=("arbitrary",), vmem_limit_bytes=VMEM_LIMIT),
        name="moe",
    )(blk, exp, nvalid, fresh, xs, wg, wu, wd)


def _combine_kernel(dst_ref, npiece_ref, x2_ref, pos_ref, ys_ref, o_ref, stage_ref, sem_ref):
    i = pl.program_id(0)
    n = pl.num_programs(0)
    cur = i % 2
    tm = TM_DISP

    def piece(tile, slot, sub, p):
        row = pl.multiple_of(p * ROW_ALIGN, ROW_ALIGN)
        src = pl.multiple_of(dst_ref[tile * MAX_PIECES + p], ROW_ALIGN)
        return pltpu.make_async_copy(ys_ref.at[pl.ds(src, ROW_ALIGN)],
                                     stage_ref.at[slot, sub, pl.ds(row, ROW_ALIGN)],
                                     sem_ref.at[slot, sub])

    def start_step(step, slot):
        for sub in range(DISP_SUB):
            tile = step * DISP_SUB + sub

            def body(p, c, tile=tile, sub=sub):
                piece(tile, slot, sub, p).start()
                return c
            lax.fori_loop(0, npiece_ref[tile], body, 0)

    @pl.when(i == 0)
    def _first():
        stage_ref[...] = jnp.zeros(stage_ref.shape, BF16)
        start_step(0, 0)

    @pl.when(i + 1 < n)
    def _prefetch():
        start_step(i + 1, 1 - cur)

    for sub in range(DISP_SUB):
        tile = i * DISP_SUB + sub

        def wait_body(p, c, tile=tile, sub=sub):
            piece(tile, cur, sub, p).wait()
            return c

        lax.fori_loop(0, npiece_ref[tile], wait_body, 0)

    for sub in range(DISP_SUB):
        rs = slice(sub * tm, (sub + 1) * tm)
        pos = pos_ref[rs, :]
        lane128 = lax.broadcasted_iota(jnp.int32, pos.shape, 1)
        p1 = _lane_col(pos, lane128, 0)
        p2 = _lane_col(pos, lane128, 1)
        lane = lax.broadcasted_iota(jnp.int32, (tm, SORT_ROWS), 1).astype(F32)
        pick = jnp.where((lane == p1) | (lane == p2), 1.0, 0.0).astype(BF16)
        o_ref[rs, :] = x2_ref[rs, :] + jnp.dot(pick, stage_ref[cur, sub],
                                               preferred_element_type=F32)


def _combine(x2, pos, ys, dst, npiece):
    t = x2.shape[0]
    tm = TM_DISP * DISP_SUB
    tok = lambda i, *_: (i, 0)
    grid_spec = pltpu.PrefetchScalarGridSpec(
        num_scalar_prefetch=2,
        grid=(t // tm,),
        in_specs=[
            pl.BlockSpec((tm, D_MODEL), tok),
            pl.BlockSpec((tm, LANES), tok),
            pl.BlockSpec(memory_space=pl.ANY),
        ],
        out_specs=pl.BlockSpec((tm, D_MODEL), tok),
        scratch_shapes=[
            pltpu.VMEM((2, DISP_SUB, SORT_ROWS, D_MODEL), BF16),
            pltpu.SemaphoreType.DMA((2, DISP_SUB)),
        ],
    )
    return pl.pallas_call(
        _combine_kernel,
        grid_spec=grid_spec,
        out_shape=jax.ShapeDtypeStruct((t, D_MODEL), F32),
        compiler_params=pltpu.CompilerParams(
            dimension_semantics=("arbitrary",), vmem_limit_bytes=VMEM_LIMIT),
        name="combine",
    )(dst, npiece, x2, pos, ys)


def _sparse_moe(h2, route, cnt, x2, wg, wu, wd):
    t = h2.shape[0]
    ntiles, nsteps = _moe_layout(t)
    rows = (nsteps + 1) * TS_MOE
    i32 = jnp.int32
    counts = cnt[:, 0, :N_EXPERTS].astype(i32)
    padded = (counts + (ROW_ALIGN - 1)) // ROW_ALIGN * ROW_ALIGN
    total = jnp.sum(padded, axis=0)
    reg_rows = (total + (TS_MOE - 1)) // TS_MOE * TS_MOE
    region = jnp.cumsum(reg_rows) - reg_rows
    base = region[None, :] + jnp.cumsum(padded, axis=0) - padded
    np_e = padded // ROW_ALIGN
    cum = jnp.cumsum(np_e, axis=1)
    npiece = cum[:, -1].astype(i32)
    p = jnp.arange(MAX_PIECES, dtype=i32)
    owns = (p[None, :, None] >= (cum - np_e)[:, None, :]) & (p[None, :, None] < cum[:, None, :])
    piece_dst = base[:, None, :] + (p[None, :, None] - (cum - np_e)[:, None, :]) * ROW_ALIGN
    dst = jnp.sum(jnp.where(owns, piece_dst, 0), axis=2).reshape(-1).astype(i32)
    nt_e = (reg_rows - total) // ROW_ALIGN
    cumt = jnp.cumsum(nt_e)
    q = jnp.arange(N_EXPERTS * (TS_MOE // ROW_ALIGN), dtype=i32)
    owns_q = (q[:, None] >= (cumt - nt_e)[None, :]) & (q[:, None] < cumt[None, :])
    tail_dst = (region + total)[None, :] + (q[:, None] - (cumt - nt_e)[None, :]) * ROW_ALIGN
    tail = jnp.sum(jnp.where(owns_q, tail_dst, 0), axis=1).astype(i32)
    ntail = cumt[-1:].astype(i32)
    ntile_e = reg_rows // TS_MOE
    first_t = jnp.cumsum(ntile_e) - ntile_e
    nvalid = jnp.sum(ntile_e).reshape(1).astype(i32)
    s = jnp.arange(nsteps, dtype=i32)
    owns_s = (s[:, None] >= first_t[None, :]) & (s[:, None] < (first_t + ntile_e)[None, :])
    spare = s >= nvalid[0]
    fresh = jnp.any(owns_s & (s[:, None] == first_t[None, :]), axis=1).astype(i32)
    exp = jnp.where(spare, N_EXPERTS - 1, jnp.sum(
        jnp.where(owns_s, jnp.arange(N_EXPERTS, dtype=i32)[None, :], 0), axis=1)).astype(i32)
    blk = jnp.where(spare, nsteps, jnp.sum(
        jnp.where(owns_s, (region // TS_MOE - first_t)[None, :] + s[:, None], 0),
        axis=1)).astype(i32)
    tri = (jnp.arange(TM_DISP)[:, None] > jnp.arange(TM_DISP)[None, :]).astype(BF16)
    upper = (jnp.arange(LANES)[:, None] < jnp.arange(LANES)[None, :]).astype(BF16)
    xs, pos = _dispatch(h2, route, dst, npiece, tail, ntail, tri, upper, rows)
    ys = _moe(xs, blk, exp, nvalid, fresh, wg, wu, wd)
    return _combine(x2, pos, ys, dst, npiece)


def _rope_tables(positions):
    inv = ROPE_THETA ** (-jnp.arange(0, ROPE_DIM, 2, dtype=F32) / ROPE_DIM)
    ang = positions.astype(F32).reshape(-1, 1) * inv
    cos, sin = jnp.cos(ang), jnp.sin(ang)
    d = jnp.arange(LANES) % HEAD_DIM
    in_rope = d < ROPE_DIM
    place = (((d % ROPE_HALF)[None, :] == jnp.arange(ROPE_HALF)[:, None])
             & in_rope[None, :]).astype(F32)
    sign = jnp.where(d < ROPE_HALF, -1.0, 1.0)
    hp = lax.Precision.HIGHEST
    cos_t = jnp.dot(cos, place, precision=hp) + (1.0 - in_rope.astype(F32))
    sin_t = jnp.dot(sin, place * sign, precision=hp)
    return cos_t, sin_t


def _s5_params(lam_re, lam_im, log_dt, b_re, b_im, c_re, c_im, bsz):
    dt = jnp.exp(log_dt)[:, None]
    mag = jnp.exp(lam_re * dt)
    lb_re = mag * jnp.cos(lam_im * dt)
    lb_im = mag * jnp.sin(lam_im * dt)
    den = lam_re * lam_re + lam_im * lam_im
    k_re = ((lb_re - 1.0) * lam_re + lb_im * lam_im) / den
    k_im = (lb_im * lam_re - (lb_re - 1.0) * lam_im) / den
    bb_re = k_re[..., None] * b_re - k_im[..., None] * b_im
    bb_im = k_re[..., None] * b_im + k_im[..., None] * b_re
    eye = jnp.eye(SSM_GROUPS, dtype=F32)
    blk_b = lambda m: jnp.einsum('gph,gk->ghkp', m, eye).reshape(SSM_WIDTH, N_STATE)
    blk_c = lambda m: jnp.einsum('ghp,gk->gpkh', m, eye).reshape(N_STATE, SSM_WIDTH)
    bmat = jnp.concatenate([blk_b(bb_re), blk_b(bb_im)], axis=1).astype(BF16)
    cmat = jnp.concatenate([blk_c(c_re), blk_c(-c_im)], axis=0).astype(BF16)
    a_re = jnp.broadcast_to(lb_re.reshape(1, N_STATE), (bsz, N_STATE))
    a_im = jnp.broadcast_to(lb_im.reshape(1, N_STATE), (bsz, N_STATE))
    return bmat, a_re, a_im, cmat


def _time_major_perm(bsz, chunk):
    r = jnp.arange(bsz * chunk)
    src = (r % bsz) * chunk + r // bsz
    perm = (src[:, None] == jnp.arange(bsz * chunk)[None, :]).astype(BF16)
    return perm, perm.T


def kernel(x, positions, norm_mix_g, w_in, q_norm_g, k_norm_g, lambda_q1, lambda_k1, lambda_q2, lambda_k2, subln_g, w_o_attn, ssm_lambda_re, ssm_lambda_im, ssm_log_dt, ssm_b_re, ssm_b_im, ssm_c_re, ssm_c_im, ssm_d, w_glu, w_out, norm_ffn_g, w_router_group, b_router_group, w_router_expert, b_router_expert, w_expert_gate, w_expert_up, w_expert_down):
    bsz, seq, _ = x.shape
    assert bsz == SUBLANES and seq % TQ == 0 and seq % SSM_CHUNK == 0
    assert norm_mix_g.shape[0] == 1
    t = bsz * seq
    x2d = x.reshape(t, D_MODEL)
    l = 0

    cos_t, sin_t = _rope_tables(positions)
    w_qkvu = w_in[l][:, :QKVU_COLS].astype(BF16)
    w_gates = w_in[l][:, QKVU_COLS:].astype(BF16)
    qg = jnp.tile(q_norm_g[l].reshape(1, HEAD_DIM), (1, 2))
    kg = jnp.tile(k_norm_g[l].reshape(1, HEAD_DIM), (1, 2))
    q, k, v, u = _in_proj(x2d, norm_mix_g[l].reshape(1, D_MODEL), w_qkvu, cos_t, sin_t, qg, kg)

    lam = (jnp.exp(jnp.sum(lambda_q1[l] * lambda_k1[l]))
           - jnp.exp(jnp.sum(lambda_q2[l] * lambda_k2[l])) + LAM_INIT).reshape(1)
    o_attn = _diff_attn(q, k, v, lam, subln_g[l].reshape(1, V_DIM), bsz, seq)

    bmat, a_re, a_im, cmat = _s5_params(
        ssm_lambda_re[l], ssm_lambda_im[l], ssm_log_dt[l], ssm_b_re[l], ssm_b_im[l],
        ssm_c_re[l], ssm_c_im[l], bsz)
    perm, perm_t = _time_major_perm(bsz, SSM_CHUNK)
    gy = _s5_scan(u.reshape(bsz, seq, SSM_WIDTH), perm, perm_t, bmat, a_re, a_im, cmat,
                  ssm_d[l].reshape(1, SSM_WIDTH)).reshape(t, SSM_WIDTH)

    w_r = jnp.concatenate(
        [w_router_expert[l].reshape(D_MODEL, N_EXPERTS), w_router_group[l],
         jnp.zeros((D_MODEL, LANES - N_EXPERTS - N_EXPERT_GROUPS), F32)], axis=1)
    b_r = jnp.concatenate(
        [b_router_expert[l].reshape(N_EXPERTS), b_router_group[l],
         jnp.zeros((LANES - N_EXPERTS - N_EXPERT_GROUPS,), F32)]).reshape(1, LANES)
    wr_hi = w_r.astype(BF16)
    wr_lo = (w_r - wr_hi.astype(F32)).astype(BF16)
    x2, h2, route, cnt = _merge(
        x2d, o_attn, gy, norm_mix_g[l].reshape(1, D_MODEL), w_gates,
        w_o_attn[l].astype(BF16), w_glu[l].astype(BF16), w_out[l].astype(BF16),
        norm_ffn_g[l].reshape(1, D_MODEL), jnp.concatenate([wr_hi, wr_lo], axis=1), b_r)

    wg = w_expert_gate[l].reshape(N_EXPERTS, D_MODEL, D_EXPERT)
    wu = w_expert_up[l].reshape(N_EXPERTS, D_MODEL, D_EXPERT)
    wd = w_expert_down[l].reshape(N_EXPERTS, D_EXPERT, D_MODEL)
    out = _sparse_moe(h2, route, cnt, x2, wg, wu, wd)
    return out.reshape(bsz, seq, D_MODEL)
```

```python
import functools
import math

import jax
import jax.numpy as jnp
from jax import lax
from jax.experimental import pallas as pl
from jax.experimental.pallas import tpu as pltpu

F32 = jnp.float32
BF16 = jnp.bfloat16

D_MODEL = 1024
N_HEADS = 4
HEAD_DIM = 64
V_DIM = 2 * HEAD_DIM
ATTN_WIDTH = N_HEADS * V_DIM
ROPE_THETA = 500000.0
ROPE_DIM = HEAD_DIM // 4
ROPE_HALF = ROPE_DIM // 2
SSM_WIDTH = D_MODEL // 2
SSM_GROUP = 16
SSM_GROUPS = SSM_WIDTH // SSM_GROUP
SSM_STATE = 64
N_STATE = SSM_GROUPS * SSM_STATE
N_EXPERT_GROUPS = 4
EXPERTS_PER_GROUP = 8
N_EXPERTS = N_EXPERT_GROUPS * EXPERTS_PER_GROUP
D_EXPERT = D_MODEL // 4
EPS = 1e-6
LAM_INIT = 0.8 - 0.6 * math.exp(-0.3 * 0)
QKVU_COLS = 4 * ATTN_WIDTH
LANES = 128
SUBLANES = 8
NEG_BIG = -1e30
LOG2_E = math.log2(math.e)

VMEM_LIMIT = 48 * 1024 * 1024

TM_IN = 512
TQ = 512
SSM_CHUNK = 64
SCAN_LANES = 512
TM_MERGE = 1024
TM_DISP = 256
DISP_SUB = 2
TS_MOE = 1024
ROW_ALIGN = 16


def _const_spec(shape):
    return pl.BlockSpec(shape, lambda *_: (0,) * len(shape))


def _in_proj_kernel(x_ref, g_ref, w_ref, cos_ref, sin_ref, qg_ref, kg_ref,
                    q_ref, k_ref, v_ref, u_ref):
    x = x_ref[...]
    ms = jnp.mean(x * x, axis=-1, keepdims=True)
    h = (x * lax.rsqrt(ms + EPS) * g_ref[...]).astype(BF16)
    cos_t = cos_ref[...]
    sin_t = sin_ref[...]
    lane = lax.broadcasted_iota(jnp.int32, (x.shape[0], LANES), 1)
    first_comp = lane < HEAD_DIM
    low_half = (lane % HEAD_DIM) < ROPE_HALF

    def norm_rope(blk, gain, scale):
        sq = blk * blk
        s_all = jnp.sum(sq, axis=-1, keepdims=True)
        s_lo = jnp.sum(jnp.where(first_comp, sq, 0.0), axis=-1, keepdims=True)
        ssum = jnp.where(first_comp, s_lo, s_all - s_lo)
        nb = blk * lax.rsqrt(ssum * (1.0 / HEAD_DIM) + EPS) * gain
        up = pltpu.roll(nb, LANES - ROPE_HALF, axis=1)
        dn = pltpu.roll(nb, ROPE_HALF, axis=1)
        partner = jnp.where(low_half, up, dn)
        return ((nb * cos_t + partner * sin_t) * scale).astype(BF16)

    for j in range(0, N_HEADS, 2):
        sl2 = slice(j * LANES, (j + 2) * LANES)
        qb = jnp.dot(h, w_ref[:, sl2], preferred_element_type=F32)
        kb = jnp.dot(h, w_ref[:, ATTN_WIDTH + j * LANES:ATTN_WIDTH + (j + 2) * LANES],
                     preferred_element_type=F32)
        for jj in range(2):
            sl = slice((j + jj) * LANES, (j + jj + 1) * LANES)
            half = slice(jj * LANES, (jj + 1) * LANES)
            q_ref[:, sl] = norm_rope(qb[:, half], qg_ref[...], LOG2_E * HEAD_DIM ** -0.5)
            k_ref[:, sl] = norm_rope(kb[:, half], kg_ref[...], 1.0)
    v_ref[...] = jnp.dot(h, w_ref[:, 2 * ATTN_WIDTH:3 * ATTN_WIDTH],
                         preferred_element_type=F32).astype(BF16)
    u_ref[...] = jnp.dot(h, w_ref[:, 3 * ATTN_WIDTH:4 * ATTN_WIDTH],
                         preferred_element_type=F32).astype(BF16)


def _in_proj(x2d, g, w, cos_t, sin_t, qg, kg):
    t = x2d.shape[0]
    tok = lambda i: (i, 0)
    out = jax.ShapeDtypeStruct((t, ATTN_WIDTH), BF16)
    return pl.pallas_call(
        _in_proj_kernel,
        grid=(t // TM_IN,),
        in_specs=[
            pl.BlockSpec((TM_IN, D_MODEL), tok),
            _const_spec((1, D_MODEL)),
            _const_spec((D_MODEL, QKVU_COLS)),
            pl.BlockSpec((TM_IN, LANES), tok),
            pl.BlockSpec((TM_IN, LANES), tok),
            _const_spec((1, LANES)),
            _const_spec((1, LANES)),
        ],
        out_specs=[pl.BlockSpec((TM_IN, ATTN_WIDTH), tok)] * 4,
        out_shape=[out] * 4,
        compiler_params=pltpu.CompilerParams(
            dimension_semantics=("arbitrary",), vmem_limit_bytes=VMEM_LIMIT),
        name="in_proj",
    )(x2d, g, w, cos_t, sin_t, qg, kg)


def _attn_kernel(qi_ref, kp_ref, mode_ref, lam_ref, q_ref, k_ref, v_ref, sg_ref, o_ref,
                 qs_ref, m_ref, l_ref, acc_ref):
    step = pl.program_id(1)
    kp = kp_ref[step]
    mode = mode_ref[step]

    @pl.when(kp == 0)
    def _init():
        for h in range(N_HEADS):
            q = q_ref[:, h * LANES:(h + 1) * LANES]
            lane = lax.broadcasted_iota(jnp.int32, q.shape, 1)
            zero = jnp.zeros_like(q)
            qs_ref[h, 0:TQ, :] = jnp.where(lane < HEAD_DIM, q, zero)
            qs_ref[h, TQ:2 * TQ, :] = jnp.where(lane < HEAD_DIM, zero, q)
        m_ref[...] = jnp.full(m_ref.shape, NEG_BIG, F32)
        l_ref[...] = jnp.zeros(l_ref.shape, F32)
        acc_ref[...] = jnp.zeros(acc_ref.shape, F32)

    def update(h, half, masked):
        hs = slice(h * LANES, (h + 1) * LANES)
        ks = slice(half * TQ, (half + 1) * TQ)
        s = lax.dot_general(qs_ref[h], k_ref[ks, hs], (((1,), (1,)), ((), ())),
                            preferred_element_type=F32)
        if masked:
            row = lax.broadcasted_iota(jnp.int32, s.shape, 0) % TQ
            col = lax.broadcasted_iota(jnp.int32, s.shape, 1)
            s = jnp.where(col <= row, s, NEG_BIG)
        m_old = m_ref[h]
        m_new = jnp.maximum(m_old, jnp.max(s, axis=-1, keepdims=True))
        alpha = jnp.exp2(m_old - m_new)
        p = jnp.exp2(s - jnp.concatenate([m_new] * (TQ // LANES), axis=1))
        v_ones = jnp.concatenate([v_ref[ks, hs], jnp.ones((TQ, LANES), BF16)], axis=1)
        pv = jnp.dot(p.astype(BF16), v_ones, preferred_element_type=F32)
        l_ref[h] = alpha * l_ref[h] + pv[:, LANES:2 * LANES]
        acc_ref[h] = alpha * acc_ref[h] + pv[:, 0:LANES]
        m_ref[h] = m_new

    def finish(h):
        o = acc_ref[h] / l_ref[h]
        d = o[0:TQ, :] - lam_ref[0] * o[TQ:2 * TQ, :]
        ms = jnp.mean(d * d, axis=-1, keepdims=True)
        d = d * lax.rsqrt(ms + EPS) * sg_ref[...] * (1.0 - LAM_INIT)
        o_ref[:, h * LANES:(h + 1) * LANES] = d.astype(BF16)

    @pl.when(mode == 0)
    def _below():
        for h in range(N_HEADS):
            update(h, 0, False)
            update(h, 1, False)

    @pl.when(mode == 1)
    def _below_then_diag():
        for h in range(N_HEADS):
            update(h, 0, False)
            update(h, 1, True)
            finish(h)

    @pl.when(mode == 2)
    def _diag_only():
        for h in range(N_HEADS):
            update(h, 0, True)
            finish(h)


def _diff_attn(q, k, v, lam, subln_g, bsz, seq):
    nq = seq // TQ
    steps = []
    for i in range(nq):
        for p in range(i // 2 + 1):
            mode = 0 if 2 * p + 1 < i else (1 if 2 * p + 1 == i else 2)
            steps.append((i, p, mode))
    qi = jnp.asarray([s[0] for s in steps], jnp.int32)
    kp = jnp.asarray([s[1] for s in steps], jnp.int32)
    mode = jnp.asarray([s[2] for s in steps], jnp.int32)
    q_map = lambda b, s, qi, kp, mode: (b * nq + qi[s], 0)
    k_map = lambda b, s, qi, kp, mode: (b * (nq // 2) + kp[s], 0)
    grid_spec = pltpu.PrefetchScalarGridSpec(
        num_scalar_prefetch=3,
        grid=(bsz, len(steps)),
        in_specs=[
            pl.BlockSpec(memory_space=pltpu.SMEM),
            pl.BlockSpec((TQ, ATTN_WIDTH), q_map),
            pl.BlockSpec((2 * TQ, ATTN_WIDTH), k_map),
            pl.BlockSpec((2 * TQ, ATTN_WIDTH), k_map),
            pl.BlockSpec((1, LANES), lambda b, s, qi, kp, mode: (0, 0)),
        ],
        out_specs=pl.BlockSpec((TQ, ATTN_WIDTH), q_map),
        scratch_shapes=[
            pltpu.VMEM((N_HEADS, 2 * TQ, LANES), BF16),
            pltpu.VMEM((N_HEADS, 2 * TQ, LANES), F32),
            pltpu.VMEM((N_HEADS, 2 * TQ, LANES), F32),
            pltpu.VMEM((N_HEADS, 2 * TQ, LANES), F32),
        ],
    )
    return pl.pallas_call(
        _attn_kernel,
        grid_spec=grid_spec,
        out_shape=jax.ShapeDtypeStruct((bsz * seq, ATTN_WIDTH), BF16),
        compiler_params=pltpu.CompilerParams(
            dimension_semantics=("arbitrary", "arbitrary"),
            vmem_limit_bytes=VMEM_LIMIT),
        name="diff_attn",
    )(qi, kp, mode, lam, q, k, v, subln_g)


def _gelu_tanh(x):
    c = math.sqrt(2.0 / math.pi)
    return 0.5 * x * (1.0 + jnp.tanh(c * (x + 0.044715 * (x * x * x))))


def _s5_kernel(u_ref, p_ref, pt_ref, b_ref, are_ref, aim_ref, c_ref, d_ref, o_ref,
               st_ref, state_ref):
    bsz, chunk, width = u_ref.shape
    rows = bsz * chunk

    @pl.when(pl.program_id(0) == 0)
    def _init():
        state_ref[...] = jnp.zeros(state_ref.shape, F32)

    u_bt = u_ref[...].reshape(rows, width)
    u_tm = jnp.dot(p_ref[...], u_bt, preferred_element_type=F32)
    u_tm_b = u_tm.astype(BF16)

    slab = SCAN_LANES // SSM_STATE * SSM_GROUP
    g_parts = []
    for ch in range(N_STATE // SCAN_LANES):
        re = slice(ch * SCAN_LANES, (ch + 1) * SCAN_LANES)
        im = slice(N_STATE + ch * SCAN_LANES, N_STATE + (ch + 1) * SCAN_LANES)
        cs = slice(ch * slab, (ch + 1) * slab)
        st_ref[:, re] = jnp.dot(u_tm_b[:, cs], b_ref[cs, re], preferred_element_type=F32)
        st_ref[:, im] = jnp.dot(u_tm_b[:, cs], b_ref[cs, im], preferred_element_type=F32)
        a_re = are_ref[:, re]
        a_im = aim_ref[:, re]

        def body(t, carry, re=re, im=im, a_re=a_re, a_im=a_im):
            s_re, s_im = carry
            r0 = pl.multiple_of(t * SUBLANES, SUBLANES)
            n_re = a_re * s_re - a_im * s_im + st_ref[pl.ds(r0, SUBLANES), re]
            n_im = a_re * s_im + a_im * s_re + st_ref[pl.ds(r0, SUBLANES), im]
            st_ref[pl.ds(r0, SUBLANES), re] = n_re
            st_ref[pl.ds(r0, SUBLANES), im] = n_im
            return n_re, n_im

        s_re, s_im = lax.fori_loop(0, chunk, body, (state_ref[:, re], state_ref[:, im]),
                                   unroll=True)
        state_ref[:, re] = s_re
        state_ref[:, im] = s_im
        y = (jnp.dot(st_ref[:, re].astype(BF16), c_ref[re, cs], preferred_element_type=F32)
             + jnp.dot(st_ref[:, im].astype(BF16), c_ref[im, cs], preferred_element_type=F32))
        y = y + d_ref[:, cs] * u_tm[:, cs]
        g_parts.append(_gelu_tanh(y).astype(BF16))

    g_tm = jnp.concatenate(g_parts, axis=1)
    g_bt = jnp.dot(pt_ref[...], g_tm, preferred_element_type=F32)
    o_ref[...] = g_bt.astype(BF16).reshape(bsz, chunk, width)


def _s5_scan(u3, perm, perm_t, bmat, a_re, a_im, cmat, dvec):
    bsz, seq, width = u3.shape
    rows = bsz * SSM_CHUNK
    blk = pl.BlockSpec((bsz, SSM_CHUNK, width), lambda c: (0, c, 0))
    return pl.pallas_call(
        _s5_kernel,
        grid=(seq // SSM_CHUNK,),
        in_specs=[
            blk,
            _const_spec((rows, rows)),
            _const_spec((rows, rows)),
            _const_spec((width, 2 * N_STATE)),
            _const_spec((bsz, N_STATE)),
            _const_spec((bsz, N_STATE)),
            _const_spec((2 * N_STATE, width)),
            _const_spec((1, width)),
        ],
        out_specs=blk,
        out_shape=jax.ShapeDtypeStruct(u3.shape, BF16),
        scratch_shapes=[
            pltpu.VMEM((rows, 2 * N_STATE), F32),
            pltpu.VMEM((bsz, 2 * N_STATE), F32),
        ],
        compiler_params=pltpu.CompilerParams(
            dimension_semantics=("arbitrary",), vmem_limit_bytes=VMEM_LIMIT),
        name="s5_scan",
    )(u3, perm, perm_t, bmat, a_re, a_im, cmat, dvec)


def _merge_kernel(x_ref, oa_ref, gy_ref, g1_ref, wg_ref, woa_ref, wglu_ref, wout_ref,
                  g2_ref, wr_ref, br_ref, x2_ref, h2_ref, route_ref, cnt_ref):
    for half in range(TM_MERGE // TM_DISP):
        rs = slice(half * TM_DISP, (half + 1) * TM_DISP)
        _merge_rows(x_ref[rs, :], oa_ref[rs, :], gy_ref[rs, :], g1_ref, wg_ref, woa_ref,
                    wglu_ref, wout_ref, g2_ref, wr_ref, br_ref,
                    x2_ref.at[rs, :], h2_ref.at[rs, :], route_ref.at[rs, :],
                    cnt_ref.at[half])


def _merge_rows(x, o_in, gy, g1_ref, wg_ref, woa_ref, wglu_ref, wout_ref, g2_ref, wr_ref,
                br_ref, x2_ref, h2_ref, route_ref, cnt_ref):
    ms = jnp.mean(x * x, axis=-1, keepdims=True)
    h = (x * lax.rsqrt(ms + EPS) * g1_ref[...]).astype(BF16)
    o_a = jnp.dot(o_in, woa_ref[...], preferred_element_type=F32)
    z_lin = jnp.dot(gy, wglu_ref[:, 0:D_MODEL], preferred_element_type=F32)
    z_gate = jnp.dot(gy, wglu_ref[:, D_MODEL:2 * D_MODEL], preferred_element_type=F32)
    o_s = z_lin * jax.nn.sigmoid(z_gate)
    gate_a = jax.nn.sigmoid(jnp.dot(h, wg_ref[:, 0:D_MODEL], preferred_element_type=F32))
    merged = gate_a * o_a
    gate_s = jax.nn.sigmoid(
        jnp.dot(h, wg_ref[:, D_MODEL:2 * D_MODEL], preferred_element_type=F32))
    merged = merged + gate_s * o_s
    x2 = x + jnp.dot(merged.astype(BF16), wout_ref[...], preferred_element_type=F32)
    x2_ref[...] = x2

    ms2 = jnp.mean(x2 * x2, axis=-1, keepdims=True)
    h2 = x2 * lax.rsqrt(ms2 + EPS) * g2_ref[...]
    h2_hi = h2.astype(BF16)
    h2_lo = (h2 - h2_hi.astype(F32)).astype(BF16)
    h2_ref[...] = h2_hi

    both = jnp.dot(h2_hi, wr_ref[...], preferred_element_type=F32)
    logits = (both[:, 0:LANES] + both[:, LANES:2 * LANES]
              + jnp.dot(h2_lo, wr_ref[:, 0:LANES], preferred_element_type=F32)
              + br_ref[...])
    lane = lax.broadcasted_iota(jnp.int32, logits.shape, 1).astype(F32)
    is_grp = (lane >= N_EXPERTS) & (lane < N_EXPERTS + N_EXPERT_GROUPS)
    gl = jnp.where(is_grp, logits, NEG_BIG)
    gmax = jnp.max(gl, axis=-1, keepdims=True)
    gsum = jnp.sum(jnp.where(is_grp, jnp.exp(gl - gmax), 0.0), axis=-1, keepdims=True)
    p_grp = 1.0 / gsum
    big = float(4 * LANES)
    grp = jnp.min(jnp.where(is_grp & (gl == gmax), lane, big), axis=-1,
                  keepdims=True) - N_EXPERTS
    sel = logits
    for g in range(1, N_EXPERT_GROUPS):
        rolled = pltpu.roll(logits, LANES - g * EXPERTS_PER_GROUP, axis=1)
        sel = jnp.where(grp == g, rolled, sel)
    in_grp = lane < EXPERTS_PER_GROUP
    es = jnp.where(in_grp, sel, NEG_BIG)
    top1 = jnp.max(es, axis=-1, keepdims=True)
    i1 = jnp.min(jnp.where(in_grp & (es == top1), lane, big), axis=-1, keepdims=True)
    es2 = jnp.where(lane == i1, NEG_BIG, es)
    top2 = jnp.max(es2, axis=-1, keepdims=True)
    i2 = jnp.min(jnp.where(in_grp & (lane != i1) & (es2 == top2), lane, big), axis=-1,
                 keepdims=True)
    e2 = jnp.exp(top2 - top1)
    w1 = p_grp / (1.0 + e2)
    w2 = p_grp * e2 / (1.0 + e2)
    e1 = grp * EXPERTS_PER_GROUP + i1
    e2x = grp * EXPERTS_PER_GROUP + i2
    route_ref[...] = (jnp.where(lane == 0.0, e1, 0.0) + jnp.where(lane == 1.0, e2x, 0.0)
                      + jnp.where(lane == 2.0, w1, 0.0) + jnp.where(lane == 3.0, w2, 0.0))
    picked = jnp.where((lane == e1) | (lane == e2x), 1.0, 0.0)
    cnt_ref[...] = jnp.sum(picked, axis=0, keepdims=True)


def _merge(x2d, o_attn, gy, g1, w_gates, w_oa, w_glu, w_out, g2, w_router, b_r):
    t = x2d.shape[0]
    tok = lambda i: (i, 0)
    return pl.pallas_call(
        _merge_kernel,
        grid=(t // TM_MERGE,),
        in_specs=[
            pl.BlockSpec((TM_MERGE, D_MODEL), tok),
            pl.BlockSpec((TM_MERGE, ATTN_WIDTH), tok),
            pl.BlockSpec((TM_MERGE, SSM_WIDTH), tok),
            _const_spec((1, D_MODEL)),
            _const_spec((D_MODEL, 2 * D_MODEL)),
            _const_spec((ATTN_WIDTH, D_MODEL)),
            _const_spec((SSM_WIDTH, 2 * D_MODEL)),
            _const_spec((D_MODEL, D_MODEL)),
            _const_spec((1, D_MODEL)),
            _const_spec((D_MODEL, 2 * LANES)),
            _const_spec((1, LANES)),
        ],
        out_specs=[
            pl.BlockSpec((TM_MERGE, D_MODEL), tok),
            pl.BlockSpec((TM_MERGE, D_MODEL), tok),
            pl.BlockSpec((TM_MERGE, LANES), tok),
            pl.BlockSpec((TM_MERGE // TM_DISP, 1, LANES), lambda i: (i, 0, 0)),
        ],
        out_shape=[
            jax.ShapeDtypeStruct((t, D_MODEL), F32),
            jax.ShapeDtypeStruct((t, D_MODEL), BF16),
            jax.ShapeDtypeStruct((t, LANES), F32),
            jax.ShapeDtypeStruct((t // TM_DISP, 1, LANES), F32),
        ],
        compiler_params=pltpu.CompilerParams(
            dimension_semantics=("arbitrary",), vmem_limit_bytes=VMEM_LIMIT),
        name="merge",
    )(x2d, o_attn, gy, g1, w_gates, w_oa, w_glu, w_out, g2, w_router, b_r)


XS_COLS = D_MODEL + LANES
SORT_ROWS = 2 * TM_DISP + N_EXPERTS * ROW_ALIGN
MAX_PIECES = SORT_ROWS // ROW_ALIGN
POS_RADIX = 32


def _moe_layout(t):
    ntiles = t // TM_DISP
    max_rows = 2 * t + ntiles * N_EXPERTS * (ROW_ALIGN - 1)
    nsteps = -(-max_rows // TS_MOE) + N_EXPERTS
    return ntiles, nsteps


def _lane_col(arr, lane, k):
    return jnp.sum(jnp.where(lane == k, arr, 0.0), axis=-1, keepdims=True)


def _compact_tile(h, route, tri_ref, upper_ref, pos_ref):
    tm = TM_DISP
    lane = lax.broadcasted_iota(jnp.int32, route.shape, 1)
    lane_f = lane.astype(F32)
    e1 = _lane_col(route, lane, 0)
    e2 = _lane_col(route, lane, 1)
    is1 = lane_f == e1
    is2 = lane_f == e2
    picked = jnp.where(is1 | is2, 1.0, 0.0)
    rank_all = jnp.dot(tri_ref[...], picked.astype(BF16), preferred_element_type=F32)
    cnt_row = jnp.sum(picked, axis=0, keepdims=True)
    pad_row = jnp.floor((cnt_row + (ROW_ALIGN - 1)) * (1.0 / ROW_ALIGN)) * ROW_ALIGN
    off = jnp.dot(jnp.broadcast_to(pad_row, (SUBLANES, LANES)).astype(BF16), upper_ref[...],
                  preferred_element_type=F32)[0:1, :]
    posmat = rank_all + off
    pos1 = jnp.sum(jnp.where(is1, posmat, 0.0), axis=-1, keepdims=True)
    pos2 = jnp.sum(jnp.where(is2, posmat, 0.0), axis=-1, keepdims=True)
    pos_ref[...] = jnp.where(lane == 0, pos1, 0.0) + jnp.where(lane == 1, pos2, 0.0)

    def digits(p):
        hi = jnp.floor(p * (1.0 / POS_RADIX))
        return hi, p - POS_RADIX * hi

    d1h, d1l = digits(pos1)
    d2h, d2l = digits(pos2)
    dig = (jnp.where(lane == 0, d1h, 0.0) + jnp.where(lane == 1, d1l, 0.0)
           + jnp.where(lane == 2, d2h, 0.0) + jnp.where(lane == 3, d2l, 0.0)).astype(BF16)
    eye8 = (lax.broadcasted_iota(jnp.int32, (SUBLANES, LANES), 0)
            == lax.broadcasted_iota(jnp.int32, (SUBLANES, LANES), 1)).astype(BF16)
    rows = lax.dot_general(eye8, dig, (((1,), (1,)), ((), ())), preferred_element_type=F32)
    p1_row = POS_RADIX * rows[0:1, :] + rows[1:2, :]
    p2_row = POS_RADIX * rows[2:3, :] + rows[3:4, :]
    sub = lax.broadcasted_iota(jnp.int32, (SORT_ROWS, tm), 0).astype(F32)
    perm = jnp.where((sub == p1_row) | (sub == p2_row), 1.0, 0.0).astype(BF16)

    def hi_lo(w):
        hi = w.astype(BF16).astype(F32)
        return hi, (w - hi).astype(BF16).astype(F32)

    w1h, w1l = hi_lo(_lane_col(route, lane, 2))
    w2h, w2l = hi_lo(_lane_col(route, lane, 3))
    aux = (jnp.where(lane == 0, e1, 0.0) + jnp.where(lane == 1, e2, 0.0)
           + jnp.where(lane == 2, w1h, 0.0) + jnp.where(lane == 3, w1l, 0.0)
           + jnp.where(lane == 4, w2h, 0.0) + jnp.where(lane == 5, w2l, 0.0)).astype(BF16)
    payload = jnp.concatenate([h, aux], axis=1)
    return jnp.dot(perm, payload, preferred_element_type=F32).astype(BF16)


def _dispatch_kernel(dst_ref, npiece_ref, tail_ref, ntail_ref, h_ref, route_ref, tri_ref,
                     upper_ref, xs_ref, pos_ref, buf_ref, zbuf_ref, sem_ref, tsem_ref):
    i = pl.program_id(0)
    last = pl.num_programs(0) - 1
    cur = i % 2

    def piece(tile, slot, sub, p):
        src = pl.multiple_of(p * ROW_ALIGN, ROW_ALIGN)
        dst = pl.multiple_of(dst_ref[tile * MAX_PIECES + p], ROW_ALIGN)
        return pltpu.make_async_copy(buf_ref.at[slot, sub, pl.ds(src, ROW_ALIGN)],
                                     xs_ref.at[pl.ds(dst, ROW_ALIGN)], sem_ref.at[slot, sub])

    def start_all(tile, slot, sub):
        def body(p, c):
            piece(tile, slot, sub, p).start()
            return c
        lax.fori_loop(0, npiece_ref[tile], body, 0)

    def wait_all(tile, slot, sub):
        def body(p, c):
            piece(tile, slot, sub, p).wait()
            return c
        lax.fori_loop(0, npiece_ref[tile], body, 0)

    for sub in range(DISP_SUB):
        rs = slice(sub * TM_DISP, (sub + 1) * TM_DISP)
        buf_ref[cur, sub] = _compact_tile(h_ref[rs, :], route_ref[rs, :], tri_ref, upper_ref,
                                          pos_ref.at[rs, :])
    for sub in range(DISP_SUB):
        start_all(i * DISP_SUB + sub, cur, sub)

    @pl.when(i > 0)
    def _wait_prev():
        for sub in range(DISP_SUB):
            wait_all((i - 1) * DISP_SUB + sub, 1 - cur, sub)

    @pl.when(i == last)
    def _finish():
        for sub in range(DISP_SUB):
            wait_all(i * DISP_SUB + sub, cur, sub)
        zbuf_ref[...] = jnp.zeros(zbuf_ref.shape, BF16)

        def tail(p):
            dst = pl.multiple_of(tail_ref[p], ROW_ALIGN)
            return pltpu.make_async_copy(zbuf_ref, xs_ref.at[pl.ds(dst, ROW_ALIGN)],
                                         tsem_ref.at[0])

        def start_tail(p, c):
            tail(p).start()
            return c

        def wait_tail(p, c):
            tail(p).wait()
            return c

        lax.fori_loop(0, ntail_ref[0], start_tail, 0)
        lax.fori_loop(0, ntail_ref[0], wait_tail, 0)


def _dispatch(h2, route, dst, npiece, tail, ntail, tri, upper, rows):
    t = h2.shape[0]
    tm = TM_DISP * DISP_SUB
    tok = lambda i, *_: (i, 0)
    const = lambda i, *_: (0, 0)
    grid_spec = pltpu.PrefetchScalarGridSpec(
        num_scalar_prefetch=4,
        grid=(t // tm,),
        in_specs=[
            pl.BlockSpec((tm, D_MODEL), tok),
            pl.BlockSpec((tm, LANES), tok),
            pl.BlockSpec((TM_DISP, TM_DISP), const),
            pl.BlockSpec((LANES, LANES), const),
        ],
        out_specs=[
            pl.BlockSpec(memory_space=pl.ANY),
            pl.BlockSpec((tm, LANES), tok),
        ],
        scratch_shapes=[
            pltpu.VMEM((2, DISP_SUB, SORT_ROWS, XS_COLS), BF16),
            pltpu.VMEM((ROW_ALIGN, XS_COLS), BF16),
            pltpu.SemaphoreType.DMA((2, DISP_SUB)),
            pltpu.SemaphoreType.DMA((1,)),
        ],
    )
    return pl.pallas_call(
        _dispatch_kernel,
        grid_spec=grid_spec,
        out_shape=[
            jax.ShapeDtypeStruct((rows, XS_COLS), BF16),
            jax.ShapeDtypeStruct((t, LANES), F32),
        ],
        compiler_params=pltpu.CompilerParams(
            dimension_semantics=("arbitrary",), vmem_limit_bytes=VMEM_LIMIT),
        name="dispatch",
    )(dst, npiece, tail, ntail, h2, route, tri, upper)


def _moe_kernel(blk_ref, exp_ref, nvalid_ref, fresh_ref, xs_ref, wg_ref, wu_ref, wd_ref,
                ys_ref, wgb_ref, wub_ref, wdb_ref):
    s = pl.program_id(0)

    @pl.when(fresh_ref[s] == 1)
    def _cast_weights():
        wgb_ref[...] = wg_ref[...].astype(BF16)
        wub_ref[...] = wu_ref[...].astype(BF16)
        wdb_ref[...] = wd_ref[...].astype(BF16)

    @pl.when(s < nvalid_ref[0])
    def _compute():
        t = xs_ref[:, 0:D_MODEL]
        aux = xs_ref[:, D_MODEL:XS_COLS].astype(F32)
        lane = lax.broadcasted_iota(jnp.int32, aux.shape, 1)
        first = _lane_col(aux, lane, 0) == exp_ref[s].astype(F32)
        w = jnp.where(first, _lane_col(aux, lane, 2) + _lane_col(aux, lane, 3),
                      _lane_col(aux, lane, 4) + _lane_col(aux, lane, 5))
        hg = jnp.dot(t, wgb_ref[...], preferred_element_type=F32)
        hu = jnp.dot(t, wub_ref[...], preferred_element_type=F32)
        a = (hg * jax.nn.sigmoid(hg) * hu * w).astype(BF16)
        ys_ref[...] = jnp.dot(a, wdb_ref[...], preferred_element_type=F32).astype(BF16)

    @pl.when(s >= nvalid_ref[0])
    def _spare():
        ys_ref[...] = jnp.zeros(ys_ref.shape, BF16)


def _moe(xs, blk, exp, nvalid, fresh, wg, wu, wd):
    row = lambda s, blk, exp, nv, fr: (blk[s], 0)
    wsel = lambda s, blk, exp, nv, fr: (exp[s], 0, 0)
    grid_spec = pltpu.PrefetchScalarGridSpec(
        num_scalar_prefetch=4,
        grid=(blk.shape[0],),
        in_specs=[
            pl.BlockSpec((TS_MOE, XS_COLS), row),
            pl.BlockSpec((None, D_MODEL, D_EXPERT), wsel),
            pl.BlockSpec((None, D_MODEL, D_EXPERT), wsel),
            pl.BlockSpec((None, D_EXPERT, D_MODEL), wsel),
        ],
        out_specs=pl.BlockSpec((TS_MOE, D_MODEL), row),
        scratch_shapes=[
            pltpu.VMEM((D_MODEL, D_EXPERT), BF16),
            pltpu.VMEM((D_MODEL, D_EXPERT), BF16),
            pltpu.VMEM((D_EXPERT, D_MODEL), BF16),
        ],
    )
    return pl.pallas_call(
        _moe_kernel,
        grid_spec=grid_spec,
        out_shape=jax.ShapeDtypeStruct((xs.shape[0], D_MODEL), BF16),
        compiler_params=pltpu.CompilerParams(
            dimension_semantics=("arbitrary",), vmem_limit_bytes=VMEM_LIMIT),
        name="moe",
    )(blk, exp, nvalid, fresh, xs, wg, wu, wd)


def _combine_kernel(dst_ref, npiece_ref, x2_ref, pos_ref, ys_ref, o_ref, stage_ref, sem_ref):
    i = pl.program_id(0)
    n = pl.num_programs(0)
    cur = i % 2
    tm = TM_DISP

    def piece(tile, slot, sub, p):
        row = pl.multiple_of(p * ROW_ALIGN, ROW_ALIGN)
        src = pl.multiple_of(dst_ref[tile * MAX_PIECES + p], ROW_ALIGN)
        return pltpu.make_async_copy(ys_ref.at[pl.ds(src, ROW_ALIGN)],
                                     stage_ref.at[slot, sub, pl.ds(row, ROW_ALIGN)],
                                     sem_ref.at[slot, sub])

    def start_step(step, slot):
        for sub in range(DISP_SUB):
            tile = step * DISP_SUB + sub

            def body(p, c, tile=tile, sub=sub):
                piece(tile, slot, sub, p).start()
                return c
            lax.fori_loop(0, npiece_ref[tile], body, 0)

    @pl.when(i == 0)
    def _first():
        stage_ref[...] = jnp.zeros(stage_ref.shape, BF16)
        start_step(0, 0)

    @pl.when(i + 1 < n)
    def _prefetch():
        start_step(i + 1, 1 - cur)

    for sub in range(DISP_SUB):
        tile = i * DISP_SUB + sub

        def wait_body(p, c, tile=tile, sub=sub):
            piece(tile, cur, sub, p).wait()
            return c

        lax.fori_loop(0, npiece_ref[tile], wait_body, 0)

    for sub in range(DISP_SUB):
        rs = slice(sub * tm, (sub + 1) * tm)
        pos = pos_ref[rs, :]
        lane128 = lax.broadcasted_iota(jnp.int32, pos.shape, 1)
        p1 = _lane_col(pos, lane128, 0)
        p2 = _lane_col(pos, lane128, 1)
        lane = lax.broadcasted_iota(jnp.int32, (tm, SORT_ROWS), 1).astype(F32)
        pick = jnp.where((lane == p1) | (lane == p2), 1.0, 0.0).astype(BF16)
        o_ref[rs, :] = x2_ref[rs, :] + jnp.dot(pick, stage_ref[cur, sub],
                                               preferred_element_type=F32)


def _combine(x2, pos, ys, dst, npiece):
    t = x2.shape[0]
    tm = TM_DISP * DISP_SUB
    tok = lambda i, *_: (i, 0)
    grid_spec = pltpu.PrefetchScalarGridSpec(
        num_scalar_prefetch=2,
        grid=(t // tm,),
        in_specs=[
            pl.BlockSpec((tm, D_MODEL), tok),
            pl.BlockSpec((tm, LANES), tok),
            pl.BlockSpec(memory_space=pl.ANY),
        ],
        out_specs=pl.BlockSpec((tm, D_MODEL), tok),
        scratch_shapes=[
            pltpu.VMEM((2, DISP_SUB, SORT_ROWS, D_MODEL), BF16),
            pltpu.SemaphoreType.DMA((2, DISP_SUB)),
        ],
    )
    return pl.pallas_call(
        _combine_kernel,
        grid_spec=grid_spec,
        out_shape=jax.ShapeDtypeStruct((t, D_MODEL), F32),
        compiler_params=pltpu.CompilerParams(
            dimension_semantics=("arbitrary",), vmem_limit_bytes=VMEM_LIMIT),
        name="combine",
    )(dst, npiece, x2, pos, ys)


def _sparse_moe(h2, route, cnt, x2, wg, wu, wd):
    t = h2.shape[0]
    ntiles, nsteps = _moe_layout(t)
    rows = (nsteps + 1) * TS_MOE
    i32 = jnp.int32
    counts = cnt[:, 0, :N_EXPERTS].astype(i32)
    padded = (counts + (ROW_ALIGN - 1)) // ROW_ALIGN * ROW_ALIGN
    total = jnp.sum(padded, axis=0)
    reg_rows = (total + (TS_MOE - 1)) // TS_MOE * TS_MOE
    region = jnp.cumsum(reg_rows) - reg_rows
    base = region[None, :] + jnp.cumsum(padded, axis=0) - padded
    np_e = padded // ROW_ALIGN
    cum = jnp.cumsum(np_e, axis=1)
    npiece = cum[:, -1].astype(i32)
    p = jnp.arange(MAX_PIECES, dtype=i32)
    owns = (p[None, :, None] >= (cum - np_e)[:, None, :]) & (p[None, :, None] < cum[:, None, :])
    piece_dst = base[:, None, :] + (p[None, :, None] - (cum - np_e)[:, None, :]) * ROW_ALIGN
    dst = jnp.sum(jnp.where(owns, piece_dst, 0), axis=2).reshape(-1).astype(i32)
    nt_e = (reg_rows - total) // ROW_ALIGN
    cumt = jnp.cumsum(nt_e)
    q = jnp.arange(N_EXPERTS * (TS_MOE // ROW_ALIGN), dtype=i32)
    owns_q = (q[:, None] >= (cumt - nt_e)[None, :]) & (q[:, None] < cumt[None, :])
    tail_dst = (region + total)[None, :] + (q[:, None] - (cumt - nt_e)[None, :]) * ROW_ALIGN
    tail = jnp.sum(jnp.where(owns_q, tail_dst, 0), axis=1).astype(i32)
    ntail = cumt[-1:].astype(i32)
    ntile_e = reg_rows // TS_MOE
    first_t = jnp.cumsum(ntile_e) - ntile_e
    nvalid = jnp.sum(ntile_e).reshape(1).astype(i32)
    s = jnp.arange(nsteps, dtype=i32)
    owns_s = (s[:, None] >= first_t[None, :]) & (s[:, None] < (first_t + ntile_e)[None, :])
    spare = s >= nvalid[0]
    fresh = jnp.any(owns_s & (s[:, None] == first_t[None, :]), axis=1).astype(i32)
    exp = jnp.where(spare, N_EXPERTS - 1, jnp.sum(
        jnp.where(owns_s, jnp.arange(N_EXPERTS, dtype=i32)[None, :], 0), axis=1)).astype(i32)
    blk = jnp.where(spare, nsteps, jnp.sum(
        jnp.where(owns_s, (region // TS_MOE - first_t)[None, :] + s[:, None], 0),
        axis=1)).astype(i32)
    tri = (jnp.arange(TM_DISP)[:, None] > jnp.arange(TM_DISP)[None, :]).astype(BF16)
    upper = (jnp.arange(LANES)[:, None] < jnp.arange(LANES)[None, :]).astype(BF16)
    xs, pos = _dispatch(h2, route, dst, npiece, tail, ntail, tri, upper, rows)
    ys = _moe(xs, blk, exp, nvalid, fresh, wg, wu, wd)
    return _combine(x2, pos, ys, dst, npiece)


def _rope_tables(positions):
    inv = ROPE_THETA ** (-jnp.arange(0, ROPE_DIM, 2, dtype=F32) / ROPE_DIM)
    ang = positions.astype(F32).reshape(-1, 1) * inv
    cos, sin = jnp.cos(ang), jnp.sin(ang)
    d = jnp.arange(LANES) % HEAD_DIM
    in_rope = d < ROPE_DIM
    place = (((d % ROPE_HALF)[None, :] == jnp.arange(ROPE_HALF)[:, None])
             & in_rope[None, :]).astype(F32)
    sign = jnp.where(d < ROPE_HALF, -1.0, 1.0)
    hp = lax.Precision.HIGHEST
    cos_t = jnp.dot(cos, place, precision=hp) + (1.0 - in_rope.astype(F32))
    sin_t = jnp.dot(sin, place * sign, precision=hp)
    return cos_t, sin_t


def _s5_params(lam_re, lam_im, log_dt, b_re, b_im, c_re, c_im, bsz):
    dt = jnp.exp(log_dt)[:, None]
    mag = jnp.exp(lam_re * dt)
    lb_re = mag * jnp.cos(lam_im * dt)
    lb_im = mag * jnp.sin(lam_im * dt)
    den = lam_re * lam_re + lam_im * lam_im
    k_re = ((lb_re - 1.0) * lam_re + lb_im * lam_im) / den
    k_im = (lb_im * lam_re - (lb_re - 1.0) * lam_im) / den
    bb_re = k_re[..., None] * b_re - k_im[..., None] * b_im
    bb_im = k_re[..., None] * b_im + k_im[..., None] * b_re
    eye = jnp.eye(SSM_GROUPS, dtype=F32)
    blk_b = lambda m: jnp.einsum('gph,gk->ghkp', m, eye).reshape(SSM_WIDTH, N_STATE)
    blk_c = lambda m: jnp.einsum('ghp,gk->gpkh', m, eye).reshape(N_STATE, SSM_WIDTH)
    bmat = jnp.concatenate([blk_b(bb_re), blk_b(bb_im)], axis=1).astype(BF16)
    cmat = jnp.concatenate([blk_c(c_re), blk_c(-c_im)], axis=0).astype(BF16)
    a_re = jnp.broadcast_to(lb_re.reshape(1, N_STATE), (bsz, N_STATE))
    a_im = jnp.broadcast_to(lb_im.reshape(1, N_STATE), (bsz, N_STATE))
    return bmat, a_re, a_im, cmat


def _time_major_perm(bsz, chunk):
    r = jnp.arange(bsz * chunk)
    src = (r % bsz) * chunk + r // bsz
    perm = (src[:, None] == jnp.arange(bsz * chunk)[None, :]).astype(BF16)
    return perm, perm.T


def kernel(x, positions, norm_mix_g, w_in, q_norm_g, k_norm_g, lambda_q1, lambda_k1, lambda_q2, lambda_k2, subln_g, w_o_attn, ssm_lambda_re, ssm_lambda_im, ssm_log_dt, ssm_b_re, ssm_b_im, ssm_c_re, ssm_c_im, ssm_d, w_glu, w_out, norm_ffn_g, w_router_group, b_router_group, w_router_expert, b_router_expert, w_expert_gate, w_expert_up, w_expert_down):
    bsz, seq, _ = x.shape
    assert bsz == SUBLANES and seq % (2 * TQ) == 0 and seq % SSM_CHUNK == 0
    assert norm_mix_g.shape[0] == 1
    t = bsz * seq
    x2d = x.reshape(t, D_MODEL)
    l = 0

    cos_t, sin_t = _rope_tables(positions)
    w_qkvu = w_in[l][:, :QKVU_COLS].astype(BF16)
    w_gates = w_in[l][:, QKVU_COLS:].astype(BF16)
    qg = jnp.tile(q_norm_g[l].reshape(1, HEAD_DIM), (1, 2))
    kg = jnp.tile(k_norm_g[l].reshape(1, HEAD_DIM), (1, 2))
    q, k, v, u = _in_proj(x2d, norm_mix_g[l].reshape(1, D_MODEL), w_qkvu, cos_t, sin_t, qg, kg)

    lam = (jnp.exp(jnp.sum(lambda_q1[l] * lambda_k1[l]))
           - jnp.exp(jnp.sum(lambda_q2[l] * lambda_k2[l])) + LAM_INIT).reshape(1)
    o_attn = _diff_attn(q, k, v, lam, subln_g[l].reshape(1, V_DIM), bsz, seq)

    bmat, a_re, a_im, cmat = _s5_params(
        ssm_lambda_re[l], ssm_lambda_im[l], ssm_log_dt[l], ssm_b_re[l], ssm_b_im[l],
        ssm_c_re[l], ssm_c_im[l], bsz)
    perm, perm_t = _time_major_perm(bsz, SSM_CHUNK)
    gy = _s5_scan(u.reshape(bsz, seq, SSM_WIDTH), perm, perm_t, bmat, a_re, a_im, cmat,
                  ssm_d[l].reshape(1, SSM_WIDTH)).reshape(t, SSM_WIDTH)

    w_r = jnp.concatenate(
        [w_router_expert[l].reshape(D_MODEL, N_EXPERTS), w_router_group[l],
         jnp.zeros((D_MODEL, LANES - N_EXPERTS - N_EXPERT_GROUPS), F32)], axis=1)
    b_r = jnp.concatenate(
        [b_router_expert[l].reshape(N_EXPERTS), b_router_group[l],
         jnp.zeros((LANES - N_EXPERTS - N_EXPERT_GROUPS,), F32)]).reshape(1, LANES)
    wr_hi = w_r.astype(BF16)
    wr_lo = (w_r - wr_hi.astype(F32)).astype(BF16)
    x2, h2, route, cnt = _merge(
        x2d, o_attn, gy, norm_mix_g[l].reshape(1, D_MODEL), w_gates,
        w_o_attn[l].astype(BF16), w_glu[l].astype(BF16), w_out[l].astype(BF16),
        norm_ffn_g[l].reshape(1, D_MODEL), jnp.concatenate([wr_hi, wr_lo], axis=1), b_r)

    wg = w_expert_gate[l].reshape(N_EXPERTS, D_MODEL, D_EXPERT)
    wu = w_expert_up[l].reshape(N_EXPERTS, D_MODEL, D_EXPERT)
    wd = w_expert_down[l].reshape(N_EXPERTS, D_EXPERT, D_MODEL)
    out = _sparse_moe(h2, route, cnt, x2, wg, wu, wd)
    return out.reshape(bsz, seq, D_MODEL)
```

```python
import functools
import math

import jax
import jax.numpy as jnp
from jax import lax
from jax.experimental import pallas as pl
from jax.experimental.pallas import tpu as pltpu

F32 = jnp.float32
BF16 = jnp.bfloat16

D_MODEL = 1024
N_HEADS = 4
HEAD_DIM = 64
V_DIM = 2 * HEAD_DIM
ATTN_WIDTH = N_HEADS * V_DIM
ROPE_THETA = 500000.0
ROPE_DIM = HEAD_DIM // 4
ROPE_HALF = ROPE_DIM // 2
SSM_WIDTH = D_MODEL // 2
SSM_GROUP = 16
SSM_GROUPS = SSM_WIDTH // SSM_GROUP
SSM_STATE = 64
N_STATE = SSM_GROUPS * SSM_STATE
N_EXPERT_GROUPS = 4
EXPERTS_PER_GROUP = 8
N_EXPERTS = N_EXPERT_GROUPS * EXPERTS_PER_GROUP
D_EXPERT = D_MODEL // 4
EPS = 1e-6
LAM_INIT = 0.8 - 0.6 * math.exp(-0.3 * 0)
QKVU_COLS = 4 * ATTN_WIDTH
LANES = 128
SUBLANES = 8
NEG_BIG = -1e30
LOG2_E = math.log2(math.e)

VMEM_LIMIT = 48 * 1024 * 1024

TM_IN = 512
TQ = 512
SSM_CHUNK = 64
SCAN_LANES = 512
TM_MERGE = 1024
TM_DISP = 256
DISP_SUB = 2
TS_MOE = 1024
ROW_ALIGN = 16


def _const_spec(shape):
    return pl.BlockSpec(shape, lambda *_: (0,) * len(shape))


def _in_proj_kernel(x_ref, g_ref, w_ref, cos_ref, sin_ref, qg_ref, kg_ref,
                    q_ref, k_ref, v_ref, u_ref):
    x = x_ref[...]
    ms = jnp.mean(x * x, axis=-1, keepdims=True)
    h = (x * lax.rsqrt(ms + EPS) * g_ref[...]).astype(BF16)
    cos_t = cos_ref[...]
    sin_t = sin_ref[...]
    lane = lax.broadcasted_iota(jnp.int32, (x.shape[0], LANES), 1)
    low_half = (lane % HEAD_DIM) < ROPE_HALF
    same_comp = (lax.broadcasted_iota(jnp.int32, (LANES, LANES), 0) // HEAD_DIM
                 == lax.broadcasted_iota(jnp.int32, (LANES, LANES), 1) // HEAD_DIM
                 ).astype(BF16)

    def norm_rope(blk, gain, scale):
        ssum = jnp.dot((blk * blk).astype(BF16), same_comp, preferred_element_type=F32)
        nb = blk * lax.rsqrt(ssum * (1.0 / HEAD_DIM) + EPS) * gain
        up = pltpu.roll(nb, LANES - ROPE_HALF, axis=1)
        dn = pltpu.roll(nb, ROPE_HALF, axis=1)
        partner = jnp.where(low_half, up, dn)
        return ((nb * cos_t + partner * sin_t) * scale).astype(BF16)

    for j in range(0, N_HEADS, 2):
        sl2 = slice(j * LANES, (j + 2) * LANES)
        qb = jnp.dot(h, w_ref[:, sl2], preferred_element_type=F32)
        kb = jnp.dot(h, w_ref[:, ATTN_WIDTH + j * LANES:ATTN_WIDTH + (j + 2) * LANES],
                     preferred_element_type=F32)
        for jj in range(2):
            sl = slice((j + jj) * LANES, (j + jj + 1) * LANES)
            half = slice(jj * LANES, (jj + 1) * LANES)
            q_ref[:, sl] = norm_rope(qb[:, half], qg_ref[...], LOG2_E * HEAD_DIM ** -0.5)
            k_ref[:, sl] = norm_rope(kb[:, half], kg_ref[...], 1.0)
    v_ref[...] = jnp.dot(h, w_ref[:, 2 * ATTN_WIDTH:3 * ATTN_WIDTH],
                         preferred_element_type=F32).astype(BF16)
    u_ref[...] = jnp.dot(h, w_ref[:, 3 * ATTN_WIDTH:4 * ATTN_WIDTH],
                         preferred_element_type=F32).astype(BF16)


def _in_proj(x2d, g, w, cos_t, sin_t, qg, kg):
    t = x2d.shape[0]
    tok = lambda i: (i, 0)
    out = jax.ShapeDtypeStruct((t, ATTN_WIDTH), BF16)
    return pl.pallas_call(
        _in_proj_kernel,
        grid=(t // TM_IN,),
        in_specs=[
            pl.BlockSpec((TM_IN, D_MODEL), tok),
            _const_spec((1, D_MODEL)),
            _const_spec((D_MODEL, QKVU_COLS)),
            pl.BlockSpec((TM_IN, LANES), tok),
            pl.BlockSpec((TM_IN, LANES), tok),
            _const_spec((1, LANES)),
            _const_spec((1, LANES)),
        ],
        out_specs=[pl.BlockSpec((TM_IN, ATTN_WIDTH), tok)] * 4,
        out_shape=[out] * 4,
        compiler_params=pltpu.CompilerParams(
            dimension_semantics=("arbitrary",), vmem_limit_bytes=VMEM_LIMIT),
        name="in_proj",
    )(x2d, g, w, cos_t, sin_t, qg, kg)


def _attn_kernel(qi_ref, kp_ref, mode_ref, lam_ref, q_ref, k_ref, v_ref, sg_ref, o_ref,
                 qs_ref, m_ref, l_ref, acc_ref):
    step = pl.program_id(1)
    kp = kp_ref[step]
    mode = mode_ref[step]

    hq = TQ // 2

    @pl.when(kp == 0)
    def _init():
        for h in range(N_HEADS):
            q = q_ref[:, h * LANES:(h + 1) * LANES]
            lane = lax.broadcasted_iota(jnp.int32, q.shape, 1)
            zero = jnp.zeros_like(q)
            only0 = jnp.where(lane < HEAD_DIM, q, zero)
            only1 = jnp.where(lane < HEAD_DIM, zero, q)
            for half in range(2):
                rows = slice(half * hq, (half + 1) * hq)
                qs_ref[h, 2 * half * hq:(2 * half + 1) * hq, :] = only0[rows, :]
                qs_ref[h, (2 * half + 1) * hq:(2 * half + 2) * hq, :] = only1[rows, :]
        m_ref[...] = jnp.full(m_ref.shape, NEG_BIG, F32)
        l_ref[...] = jnp.zeros(l_ref.shape, F32)
        acc_ref[...] = jnp.zeros(acc_ref.shape, F32)

    def update(h, rows, key0, nkeys, diag_offset=None):
        hs = slice(h * LANES, (h + 1) * LANES)
        ks = slice(key0, key0 + nkeys)
        s = lax.dot_general(qs_ref[h, rows, :], k_ref[ks, hs], (((1,), (1,)), ((), ())),
                            preferred_element_type=F32)
        if diag_offset is not None:
            row = lax.broadcasted_iota(jnp.int32, s.shape, 0)
            row = row % hq + row // TQ * hq + diag_offset
            col = lax.broadcasted_iota(jnp.int32, s.shape, 1)
            s = jnp.where(col <= row, s, NEG_BIG)
        m_old = m_ref[h, rows, :]
        m_new = jnp.maximum(m_old, jnp.max(s, axis=-1, keepdims=True))
        alpha = jnp.exp2(m_old - m_new)
        p = jnp.exp2(s - jnp.concatenate([m_new] * (nkeys // LANES), axis=1))
        v_ones = jnp.concatenate([v_ref[ks, hs], jnp.ones((nkeys, LANES), BF16)], axis=1)
        pv = jnp.dot(p.astype(BF16), v_ones, preferred_element_type=F32)
        l_ref[h, rows, :] = alpha * l_ref[h, rows, :] + pv[:, LANES:2 * LANES]
        acc_ref[h, rows, :] = alpha * acc_ref[h, rows, :] + pv[:, 0:LANES]
        m_ref[h, rows, :] = m_new

    all_rows = slice(0, 2 * TQ)

    def update_diag(h, key0):
        update(h, slice(0, TQ), key0, hq, diag_offset=0)
        update(h, slice(TQ, 2 * TQ), key0, TQ, diag_offset=hq)

    def finish(h):
        o = acc_ref[h] / l_ref[h]
        d = jnp.concatenate(
            [o[2 * half * hq:(2 * half + 1) * hq, :]
             - lam_ref[0] * o[(2 * half + 1) * hq:(2 * half + 2) * hq, :]
             for half in range(2)], axis=0)
        ms = jnp.mean(d * d, axis=-1, keepdims=True)
        d = d * lax.rsqrt(ms + EPS) * sg_ref[...] * (1.0 - LAM_INIT)
        o_ref[:, h * LANES:(h + 1) * LANES] = d.astype(BF16)

    @pl.when(mode == 0)
    def _below():
        for h in range(N_HEADS):
            update(h, all_rows, 0, TQ)
            update(h, all_rows, TQ, TQ)

    @pl.when(mode == 1)
    def _below_then_diag():
        for h in range(N_HEADS):
            update(h, all_rows, 0, TQ)
            update(h, all_rows, TQ, TQ, diag_offset=0)
            finish(h)

    @pl.when(mode == 2)
    def _diag_only():
        for h in range(N_HEADS):
            update_diag(h, 0)
            finish(h)


def _diff_attn(q, k, v, lam, subln_g, bsz, seq):
    nq = seq // TQ
    steps = []
    for i in range(nq):
        for p in range(i // 2 + 1):
            mode = 0 if 2 * p + 1 < i else (1 if 2 * p + 1 == i else 2)
            steps.append((i, p, mode))
    qi = jnp.asarray([s[0] for s in steps], jnp.int32)
    kp = jnp.asarray([s[1] for s in steps], jnp.int32)
    mode = jnp.asarray([s[2] for s in steps], jnp.int32)
    q_map = lambda b, s, qi, kp, mode: (b * nq + qi[s], 0)
    k_map = lambda b, s, qi, kp, mode: (b * (nq // 2) + kp[s], 0)
    grid_spec = pltpu.PrefetchScalarGridSpec(
        num_scalar_prefetch=3,
        grid=(bsz, len(steps)),
        in_specs=[
            pl.BlockSpec(memory_space=pltpu.SMEM),
            pl.BlockSpec((TQ, ATTN_WIDTH), q_map),
            pl.BlockSpec((2 * TQ, ATTN_WIDTH), k_map),
            pl.BlockSpec((2 * TQ, ATTN_WIDTH), k_map),
            pl.BlockSpec((1, LANES), lambda b, s, qi, kp, mode: (0, 0)),
        ],
        out_specs=pl.BlockSpec((TQ, ATTN_WIDTH), q_map),
        scratch_shapes=[
            pltpu.VMEM((N_HEADS, 2 * TQ, LANES), BF16),
            pltpu.VMEM((N_HEADS, 2 * TQ, LANES), F32),
            pltpu.VMEM((N_HEADS, 2 * TQ, LANES), F32),
            pltpu.VMEM((N_HEADS, 2 * TQ, LANES), F32),
        ],
    )
    return pl.pallas_call(
        _attn_kernel,
        grid_spec=grid_spec,
        out_shape=jax.ShapeDtypeStruct((bsz * seq, ATTN_WIDTH), BF16),
        compiler_params=pltpu.CompilerParams(
            dimension_semantics=("arbitrary", "arbitrary"),
            vmem_limit_bytes=VMEM_LIMIT),
        name="diff_attn",
    )(qi, kp, mode, lam, q, k, v, subln_g)


def _gelu_tanh(x):
    c = math.sqrt(2.0 / math.pi)
    return 0.5 * x * (1.0 + jnp.tanh(c * (x + 0.044715 * (x * x * x))))


def _s5_kernel(u_ref, p_ref, pt_ref, b_ref, are_ref, aim_ref, c_ref, d_ref, o_ref,
               st_ref, state_ref):
    bsz, chunk, width = u_ref.shape
    rows = bsz * chunk

    @pl.when(pl.program_id(0) == 0)
    def _init():
        state_ref[...] = jnp.zeros(state_ref.shape, F32)

    u_bt = u_ref[...].reshape(rows, width)
    u_tm = jnp.dot(p_ref[...], u_bt, preferred_element_type=F32)
    u_tm_b = u_tm.astype(BF16)

    slab = SCAN_LANES // SSM_STATE * SSM_GROUP
    g_parts = []
    for ch in range(N_STATE // SCAN_LANES):
        re = slice(ch * SCAN_LANES, (ch + 1) * SCAN_LANES)
        im = slice(N_STATE + ch * SCAN_LANES, N_STATE + (ch + 1) * SCAN_LANES)
        cs = slice(ch * slab, (ch + 1) * slab)
        st_ref[:, re] = jnp.dot(u_tm_b[:, cs], b_ref[cs, re], preferred_element_type=F32)
        st_ref[:, im] = jnp.dot(u_tm_b[:, cs], b_ref[cs, im], preferred_element_type=F32)
        a_re = are_ref[:, re]
        a_im = aim_ref[:, re]

        def body(t, carry, re=re, im=im, a_re=a_re, a_im=a_im):
            s_re, s_im = carry
            r0 = pl.multiple_of(t * SUBLANES, SUBLANES)
            n_re = a_re * s_re - a_im * s_im + st_ref[pl.ds(r0, SUBLANES), re]
            n_im = a_re * s_im + a_im * s_re + st_ref[pl.ds(r0, SUBLANES), im]
            st_ref[pl.ds(r0, SUBLANES), re] = n_re
            st_ref[pl.ds(r0, SUBLANES), im] = n_im
            return n_re, n_im

        s_re, s_im = lax.fori_loop(0, chunk, body, (state_ref[:, re], state_ref[:, im]),
                                   unroll=True)
        state_ref[:, re] = s_re
        state_ref[:, im] = s_im
        y = (jnp.dot(st_ref[:, re].astype(BF16), c_ref[re, cs], preferred_element_type=F32)
             + jnp.dot(st_ref[:, im].astype(BF16), c_ref[im, cs], preferred_element_type=F32))
        y = y + d_ref[:, cs] * u_tm[:, cs]
        g_parts.append(_gelu_tanh(y).astype(BF16))

    g_tm = jnp.concatenate(g_parts, axis=1)
    g_bt = jnp.dot(pt_ref[...], g_tm, preferred_element_type=F32)
    o_ref[...] = g_bt.astype(BF16).reshape(bsz, chunk, width)


def _s5_scan(u3, perm, perm_t, bmat, a_re, a_im, cmat, dvec):
    bsz, seq, width = u3.shape
    rows = bsz * SSM_CHUNK
    blk = pl.BlockSpec((bsz, SSM_CHUNK, width), lambda c: (0, c, 0))
    return pl.pallas_call(
        _s5_kernel,
        grid=(seq // SSM_CHUNK,),
        in_specs=[
            blk,
            _const_spec((rows, rows)),
            _const_spec((rows, rows)),
            _const_spec((width, 2 * N_STATE)),
            _const_spec((bsz, N_STATE)),
            _const_spec((bsz, N_STATE)),
            _const_spec((2 * N_STATE, width)),
            _const_spec((1, width)),
        ],
        out_specs=blk,
        out_shape=jax.ShapeDtypeStruct(u3.shape, BF16),
        scratch_shapes=[
            pltpu.VMEM((rows, 2 * N_STATE), F32),
            pltpu.VMEM((bsz, 2 * N_STATE), F32),
        ],
        compiler_params=pltpu.CompilerParams(
            dimension_semantics=("arbitrary",), vmem_limit_bytes=VMEM_LIMIT),
        name="s5_scan",
    )(u3, perm, perm_t, bmat, a_re, a_im, cmat, dvec)


def _merge_kernel(x_ref, oa_ref, gy_ref, g1_ref, wg_ref, woa_ref, wglu_ref, wout_ref,
                  g2_ref, wr_ref, br_ref, x2_ref, h2_ref, route_ref, cnt_ref):
    for half in range(TM_MERGE // TM_DISP):
        rs = slice(half * TM_DISP, (half + 1) * TM_DISP)
        _merge_rows(x_ref[rs, :], oa_ref[rs, :], gy_ref[rs, :], g1_ref, wg_ref, woa_ref,
                    wglu_ref, wout_ref, g2_ref, wr_ref, br_ref,
                    x2_ref.at[rs, :], h2_ref.at[rs, :], route_ref.at[rs, :],
                    cnt_ref.at[half])


def _merge_rows(x, o_in, gy, g1_ref, wg_ref, woa_ref, wglu_ref, wout_ref, g2_ref, wr_ref,
                br_ref, x2_ref, h2_ref, route_ref, cnt_ref):
    ms = jnp.mean(x * x, axis=-1, keepdims=True)
    h = (x * lax.rsqrt(ms + EPS) * g1_ref[...]).astype(BF16)
    o_a = jnp.dot(o_in, woa_ref[...], preferred_element_type=F32)
    z_lin = jnp.dot(gy, wglu_ref[:, 0:D_MODEL], preferred_element_type=F32)
    z_gate = jnp.dot(gy, wglu_ref[:, D_MODEL:2 * D_MODEL], preferred_element_type=F32)
    o_s = z_lin * jax.nn.sigmoid(z_gate)
    gate_a = jax.nn.sigmoid(jnp.dot(h, wg_ref[:, 0:D_MODEL], preferred_element_type=F32))
    merged = gate_a * o_a
    gate_s = jax.nn.sigmoid(
        jnp.dot(h, wg_ref[:, D_MODEL:2 * D_MODEL], preferred_element_type=F32))
    merged = merged + gate_s * o_s
    x2 = x + jnp.dot(merged.astype(BF16), wout_ref[...], preferred_element_type=F32)
    x2_ref[...] = x2

    ms2 = jnp.mean(x2 * x2, axis=-1, keepdims=True)
    h2 = x2 * lax.rsqrt(ms2 + EPS) * g2_ref[...]
    h2_hi = h2.astype(BF16)
    h2_lo = (h2 - h2_hi.astype(F32)).astype(BF16)
    h2_ref[...] = h2_hi

    both = jnp.dot(h2_hi, wr_ref[...], preferred_element_type=F32)
    logits = (both[:, 0:LANES] + both[:, LANES:2 * LANES]
              + jnp.dot(h2_lo, wr_ref[:, 0:LANES], preferred_element_type=F32)
              + br_ref[...])
    lane = lax.broadcasted_iota(jnp.int32, logits.shape, 1).astype(F32)
    is_grp = (lane >= N_EXPERTS) & (lane < N_EXPERTS + N_EXPERT_GROUPS)
    gl = jnp.where(is_grp, logits, NEG_BIG)
    gmax = jnp.max(gl, axis=-1, keepdims=True)
    gsum = jnp.sum(jnp.where(is_grp, jnp.exp(gl - gmax), 0.0), axis=-1, keepdims=True)
    p_grp = 1.0 / gsum
    big = float(4 * LANES)
    grp = jnp.min(jnp.where(is_grp & (gl == gmax), lane, big), axis=-1,
                  keepdims=True) - N_EXPERTS
    sel = logits
    for g in range(1, N_EXPERT_GROUPS):
        rolled = pltpu.roll(logits, LANES - g * EXPERTS_PER_GROUP, axis=1)
        sel = jnp.where(grp == g, rolled, sel)
    in_grp = lane < EXPERTS_PER_GROUP
    es = jnp.where(in_grp, sel, NEG_BIG)
    top1 = jnp.max(es, axis=-1, keepdims=True)
    i1 = jnp.min(jnp.where(in_grp & (es == top1), lane, big), axis=-1, keepdims=True)
    es2 = jnp.where(lane == i1, NEG_BIG, es)
    top2 = jnp.max(es2, axis=-1, keepdims=True)
    i2 = jnp.min(jnp.where(in_grp & (lane != i1) & (es2 == top2), lane, big), axis=-1,
                 keepdims=True)
    e2 = jnp.exp(top2 - top1)
    w1 = p_grp / (1.0 + e2)
    w2 = p_grp * e2 / (1.0 + e2)
    e1 = grp * EXPERTS_PER_GROUP + i1
    e2x = grp * EXPERTS_PER_GROUP + i2
    route_ref[...] = (jnp.where(lane == 0.0, e1, 0.0) + jnp.where(lane == 1.0, e2x, 0.0)
                      + jnp.where(lane == 2.0, w1, 0.0) + jnp.where(lane == 3.0, w2, 0.0))
    picked = jnp.where((lane == e1) | (lane == e2x), 1.0, 0.0)
    cnt_ref[...] = jnp.sum(picked, axis=0, keepdims=True)


def _merge(x2d, o_attn, gy, g1, w_gates, w_oa, w_glu, w_out, g2, w_router, b_r):
    t = x2d.shape[0]
    tok = lambda i: (i, 0)
    return pl.pallas_call(
        _merge_kernel,
        grid=(t // TM_MERGE,),
        in_specs=[
            pl.BlockSpec((TM_MERGE, D_MODEL), tok),
            pl.BlockSpec((TM_MERGE, ATTN_WIDTH), tok),
            pl.BlockSpec((TM_MERGE, SSM_WIDTH), tok),
            _const_spec((1, D_MODEL)),
            _const_spec((D_MODEL, 2 * D_MODEL)),
            _const_spec((ATTN_WIDTH, D_MODEL)),
            _const_spec((SSM_WIDTH, 2 * D_MODEL)),
            _const_spec((D_MODEL, D_MODEL)),
            _const_spec((1, D_MODEL)),
            _const_spec((D_MODEL, 2 * LANES)),
            _const_spec((1, LANES)),
        ],
        out_specs=[
            pl.BlockSpec((TM_MERGE, D_MODEL), tok),
            pl.BlockSpec((TM_MERGE, D_MODEL), tok),
            pl.BlockSpec((TM_MERGE, LANES), tok),
            pl.BlockSpec((TM_MERGE // TM_DISP, 1, LANES), lambda i: (i, 0, 0)),
        ],
        out_shape=[
            jax.ShapeDtypeStruct((t, D_MODEL), F32),
            jax.ShapeDtypeStruct((t, D_MODEL), BF16),
            jax.ShapeDtypeStruct((t, LANES), F32),
            jax.ShapeDtypeStruct((t // TM_DISP, 1, LANES), F32),
        ],
        compiler_params=pltpu.CompilerParams(
            dimension_semantics=("arbitrary",), vmem_limit_bytes=VMEM_LIMIT),
        name="merge",
    )(x2d, o_attn, gy, g1, w_gates, w_oa, w_glu, w_out, g2, w_router, b_r)


XS_COLS = D_MODEL + LANES
SORT_ROWS = 2 * TM_DISP + N_EXPERTS * ROW_ALIGN
MAX_PIECES = SORT_ROWS // ROW_ALIGN
POS_RADIX = 32


def _moe_layout(t):
    ntiles = t // TM_DISP
    max_rows = 2 * t + ntiles * N_EXPERTS * (ROW_ALIGN - 1)
    nsteps = -(-max_rows // TS_MOE) + N_EXPERTS
    return ntiles, nsteps


def _lane_col(arr, lane, k):
    return jnp.sum(jnp.where(lane == k, arr, 0.0), axis=-1, keepdims=True)


def _compact_tile(h, route, tri_ref, upper_ref, pos_ref):
    tm = TM_DISP
    lane = lax.broadcasted_iota(jnp.int32, route.shape, 1)
    lane_f = lane.astype(F32)
    e1 = _lane_col(route, lane, 0)
    e2 = _lane_col(route, lane, 1)
    is1 = lane_f == e1
    is2 = lane_f == e2
    picked = jnp.where(is1 | is2, 1.0, 0.0)
    rank_all = jnp.dot(tri_ref[...], picked.astype(BF16), preferred_element_type=F32)
    cnt_row = jnp.sum(picked, axis=0, keepdims=True)
    pad_row = jnp.floor((cnt_row + (ROW_ALIGN - 1)) * (1.0 / ROW_ALIGN)) * ROW_ALIGN
    off = jnp.dot(jnp.broadcast_to(pad_row, (SUBLANES, LANES)).astype(BF16), upper_ref[...],
                  preferred_element_type=F32)[0:1, :]
    posmat = rank_all + off
    pos1 = jnp.sum(jnp.where(is1, posmat, 0.0), axis=-1, keepdims=True)
    pos2 = jnp.sum(jnp.where(is2, posmat, 0.0), axis=-1, keepdims=True)
    pos_ref[...] = jnp.where(lane == 0, pos1, 0.0) + jnp.where(lane == 1, pos2, 0.0)

    def digits(p):
        hi = jnp.floor(p * (1.0 / POS_RADIX))
        return hi, p - POS_RADIX * hi

    d1h, d1l = digits(pos1)
    d2h, d2l = digits(pos2)
    dig = (jnp.where(lane == 0, d1h, 0.0) + jnp.where(lane == 1, d1l, 0.0)
           + jnp.where(lane == 2, d2h, 0.0) + jnp.where(lane == 3, d2l, 0.0)).astype(BF16)
    eye8 = (lax.broadcasted_iota(jnp.int32, (SUBLANES, LANES), 0)
            == lax.broadcasted_iota(jnp.int32, (SUBLANES, LANES), 1)).astype(BF16)
    rows = lax.dot_general(eye8, dig, (((1,), (1,)), ((), ())), preferred_element_type=F32)
    p1_row = POS_RADIX * rows[0:1, :] + rows[1:2, :]
    p2_row = POS_RADIX * rows[2:3, :] + rows[3:4, :]
    sub = lax.broadcasted_iota(jnp.int32, (SORT_ROWS, tm), 0).astype(F32)
    perm = jnp.where((sub == p1_row) | (sub == p2_row), 1.0, 0.0).astype(BF16)

    def hi_lo(w):
        hi = w.astype(BF16).astype(F32)
        return hi, (w - hi).astype(BF16).astype(F32)

    w1h, w1l = hi_lo(_lane_col(route, lane, 2))
    w2h, w2l = hi_lo(_lane_col(route, lane, 3))
    aux = (jnp.where(lane == 0, e1, 0.0) + jnp.where(lane == 1, e2, 0.0)
           + jnp.where(lane == 2, w1h, 0.0) + jnp.where(lane == 3, w1l, 0.0)
           + jnp.where(lane == 4, w2h, 0.0) + jnp.where(lane == 5, w2l, 0.0)).astype(BF16)
    payload = jnp.concatenate([h, aux], axis=1)
    return jnp.dot(perm, payload, preferred_element_type=F32).astype(BF16)


def _dispatch_kernel(dst_ref, npiece_ref, tail_ref, ntail_ref, h_ref, route_ref, tri_ref,
                     upper_ref, xs_ref, pos_ref, buf_ref, zbuf_ref, sem_ref, tsem_ref):
    i = pl.program_id(0)
    last = pl.num_programs(0) - 1
    cur = i % 2

    def piece(tile, slot, sub, p):
        src = pl.multiple_of(p * ROW_ALIGN, ROW_ALIGN)
        dst = pl.multiple_of(dst_ref[tile * MAX_PIECES + p], ROW_ALIGN)
        return pltpu.make_async_copy(buf_ref.at[slot, sub, pl.ds(src, ROW_ALIGN)],
                                     xs_ref.at[pl.ds(dst, ROW_ALIGN)], sem_ref.at[slot, sub])

    def start_all(tile, slot, sub):
        def body(p, c):
            piece(tile, slot, sub, p).start()
            return c
        lax.fori_loop(0, npiece_ref[tile], body, 0)

    def wait_all(tile, slot, sub):
        def body(p, c):
            piece(tile, slot, sub, p).wait()
            return c
        lax.fori_loop(0, npiece_ref[tile], body, 0)

    for sub in range(DISP_SUB):
        rs = slice(sub * TM_DISP, (sub + 1) * TM_DISP)
        buf_ref[cur, sub] = _compact_tile(h_ref[rs, :], route_ref[rs, :], tri_ref, upper_ref,
                                          pos_ref.at[rs, :])
    for sub in range(DISP_SUB):
        start_all(i * DISP_SUB + sub, cur, sub)

    @pl.when(i > 0)
    def _wait_prev():
        for sub in range(DISP_SUB):
            wait_all((i - 1) * DISP_SUB + sub, 1 - cur, sub)

    @pl.when(i == last)
    def _finish():
        for sub in range(DISP_SUB):
            wait_all(i * DISP_SUB + sub, cur, sub)
        zbuf_ref[...] = jnp.zeros(zbuf_ref.shape, BF16)

        def tail(p):
            dst = pl.multiple_of(tail_ref[p], ROW_ALIGN)
            return pltpu.make_async_copy(zbuf_ref, xs_ref.at[pl.ds(dst, ROW_ALIGN)],
                                         tsem_ref.at[0])

        def start_tail(p, c):
            tail(p).start()
            return c

        def wait_tail(p, c):
            tail(p).wait()
            return c

        lax.fori_loop(0, ntail_ref[0], start_tail, 0)
        lax.fori_loop(0, ntail_ref[0], wait_tail, 0)


def _dispatch(h2, route, dst, npiece, tail, ntail, tri, upper, rows):
    t = h2.shape[0]
    tm = TM_DISP * DISP_SUB
    tok = lambda i, *_: (i, 0)
    const = lambda i, *_: (0, 0)
    grid_spec = pltpu.PrefetchScalarGridSpec(
        num_scalar_prefetch=4,
        grid=(t // tm,),
        in_specs=[
            pl.BlockSpec((tm, D_MODEL), tok),
            pl.BlockSpec((tm, LANES), tok),
            pl.BlockSpec((TM_DISP, TM_DISP), const),
            pl.BlockSpec((LANES, LANES), const),
        ],
        out_specs=[
            pl.BlockSpec(memory_space=pl.ANY),
            pl.BlockSpec((tm, LANES), tok),
        ],
        scratch_shapes=[
            pltpu.VMEM((2, DISP_SUB, SORT_ROWS, XS_COLS), BF16),
            pltpu.VMEM((ROW_ALIGN, XS_COLS), BF16),
            pltpu.SemaphoreType.DMA((2, DISP_SUB)),
            pltpu.SemaphoreType.DMA((1,)),
        ],
    )
    return pl.pallas_call(
        _dispatch_kernel,
        grid_spec=grid_spec,
        out_shape=[
            jax.ShapeDtypeStruct((rows, XS_COLS), BF16),
            jax.ShapeDtypeStruct((t, LANES), F32),
        ],
        compiler_params=pltpu.CompilerParams(
            dimension_semantics=("arbitrary",), vmem_limit_bytes=VMEM_LIMIT),
        name="dispatch",
    )(dst, npiece, tail, ntail, h2, route, tri, upper)


def _moe_kernel(blk_ref, exp_ref, nvalid_ref, fresh_ref, xs_ref, wg_ref, wu_ref, wd_ref,
                ys_ref, wgb_ref, wub_ref, wdb_ref):
    s = pl.program_id(0)

    @pl.when(fresh_ref[s] == 1)
    def _cast_weights():
        wgb_ref[...] = wg_ref[...].astype(BF16)
        wub_ref[...] = wu_ref[...].astype(BF16)
        wdb_ref[...] = wd_ref[...].astype(BF16)

    @pl.when(s < nvalid_ref[0])
    def _compute():
        t = xs_ref[:, 0:D_MODEL]
        aux = xs_ref[:, D_MODEL:XS_COLS].astype(F32)
        lane = lax.broadcasted_iota(jnp.int32, aux.shape, 1)
        first = _lane_col(aux, lane, 0) == exp_ref[s].astype(F32)
        w = jnp.where(first, _lane_col(aux, lane, 2) + _lane_col(aux, lane, 3),
                      _lane_col(aux, lane, 4) + _lane_col(aux, lane, 5))
        hg = jnp.dot(t, wgb_ref[...], preferred_element_type=F32)
        hu = jnp.dot(t, wub_ref[...], preferred_element_type=F32)
        a = (hg * jax.nn.sigmoid(hg) * hu * w).astype(BF16)
        ys_ref[...] = jnp.dot(a, wdb_ref[...], preferred_element_type=F32).astype(BF16)

    @pl.when(s == nvalid_ref[0])
    def _spare():
        ys_ref[...] = jnp.zeros(ys_ref.shape, BF16)


def _moe(xs, blk, exp, nvalid, fresh, wg, wu, wd):
    row = lambda s, blk, exp, nv, fr: (blk[s], 0)
    wsel = lambda s, blk, exp, nv, fr: (exp[s], 0, 0)
    grid_spec = pltpu.PrefetchScalarGridSpec(
        num_scalar_prefetch=4,
        grid=(blk.shape[0],),
        in_specs=[
            pl.BlockSpec((TS_MOE, XS_COLS), row),
            pl.BlockSpec((None, D_MODEL, D_EXPERT), wsel),
            pl.BlockSpec((None, D_MODEL, D_EXPERT), wsel),
            pl.BlockSpec((None, D_EXPERT, D_MODEL), wsel),
        ],
        out_specs=pl.BlockSpec((TS_MOE, D_MODEL), row),
        scratch_shapes=[
            pltpu.VMEM((D_MODEL, D_EXPERT), BF16),
            pltpu.VMEM((D_MODEL, D_EXPERT), BF16),
            pltpu.VMEM((D_EXPERT, D_MODEL), BF16),
        ],
    )
    return pl.pallas_call(
        _moe_kernel,
        grid_spec=grid_spec,
        out_shape=jax.ShapeDtypeStruct((xs.shape[0], D_MODEL), BF16),
        compiler_params=pltpu.CompilerParams(
            dimension_semantics=("arbitrary",), vmem_limit_bytes=VMEM_LIMIT),
        name="moe",
    )(blk, exp, nvalid, fresh, xs, wg, wu, wd)


def _combine_kernel(dst_ref, npiece_ref, x2_ref, pos_ref, ys_ref, o_ref, stage_ref, sem_ref):
    i = pl.program_id(0)
    n = pl.num_programs(0)
    cur = i % 2
    tm = TM_DISP

    def piece(tile, slot, sub, p):
        row = pl.multiple_of(p * ROW_ALIGN, ROW_ALIGN)
        src = pl.multiple_of(dst_ref[tile * MAX_PIECES + p], ROW_ALIGN)
        return pltpu.make_async_copy(ys_ref.at[pl.ds(src, ROW_ALIGN)],
                                     stage_ref.at[slot, sub, pl.ds(row, ROW_ALIGN)],
                                     sem_ref.at[slot, sub])

    def start_step(step, slot):
        for sub in range(DISP_SUB):
            tile = step * DISP_SUB + sub

            def body(p, c, tile=tile, sub=sub):
                piece(tile, slot, sub, p).start()
                return c
            lax.fori_loop(0, npiece_ref[tile], body, 0)

    @pl.when(i == 0)
    def _first():
        stage_ref[...] = jnp.zeros(stage_ref.shape, BF16)
        start_step(0, 0)

    @pl.when(i + 1 < n)
    def _prefetch():
        start_step(i + 1, 1 - cur)

    for sub in range(DISP_SUB):
        tile = i * DISP_SUB + sub

        def wait_body(p, c, tile=tile, sub=sub):
            piece(tile, cur, sub, p).wait()
            return c

        lax.fori_loop(0, npiece_ref[tile], wait_body, 0)

    for sub in range(DISP_SUB):
        rs = slice(sub * tm, (sub + 1) * tm)
        pos = pos_ref[rs, :]
        lane128 = lax.broadcasted_iota(jnp.int32, pos.shape, 1)
        p1 = _lane_col(pos, lane128, 0)
        p2 = _lane_col(pos, lane128, 1)
        lane = lax.broadcasted_iota(jnp.int32, (tm, SORT_ROWS), 1).astype(F32)
        pick = jnp.where((lane == p1) | (lane == p2), 1.0, 0.0).astype(BF16)
        o_ref[rs, :] = x2_ref[rs, :] + jnp.dot(pick, stage_ref[cur, sub],
                                               preferred_element_type=F32)


def _combine(x2, pos, ys, dst, npiece):
    t = x2.shape[0]
    tm = TM_DISP * DISP_SUB
    tok = lambda i, *_: (i, 0)
    grid_spec = pltpu.PrefetchScalarGridSpec(
        num_scalar_prefetch=2,
        grid=(t // tm,),
        in_specs=[
            pl.BlockSpec((tm, D_MODEL), tok),
            pl.BlockSpec((tm, LANES), tok),
            pl.BlockSpec(memory_space=pl.ANY),
        ],
        out_specs=pl.BlockSpec((tm, D_MODEL), tok),
        scratch_shapes=[
            pltpu.VMEM((2, DISP_SUB, SORT_ROWS, D_MODEL), BF16),
            pltpu.SemaphoreType.DMA((2, DISP_SUB)),
        ],
    )
    return pl.pallas_call(
        _combine_kernel,
        grid_spec=grid_spec,
        out_shape=jax.ShapeDtypeStruct((t, D_MODEL), F32),
        compiler_params=pltpu.CompilerParams(
            dimension_semantics=("arbitrary",), vmem_limit_bytes=VMEM_LIMIT),
        name="combine",
    )(dst, npiece, x2, pos, ys)


def _sparse_moe(h2, route, cnt, x2, wg, wu, wd):
    t = h2.shape[0]
    ntiles, nsteps = _moe_layout(t)
    rows = (nsteps + 1) * TS_MOE
    i32 = jnp.int32
    counts = cnt[:, 0, :N_EXPERTS].astype(i32)
    padded = (counts + (ROW_ALIGN - 1)) // ROW_ALIGN * ROW_ALIGN
    total = jnp.sum(padded, axis=0)
    reg_rows = (total + (TS_MOE - 1)) // TS_MOE * TS_MOE
    region = jnp.cumsum(reg_rows) - reg_rows
    base = region[None, :] + jnp.cumsum(padded, axis=0) - padded
    np_e = padded // ROW_ALIGN
    cum = jnp.cumsum(np_e, axis=1)
    npiece = cum[:, -1].astype(i32)
    p = jnp.arange(MAX_PIECES, dtype=i32)
    owns = (p[None, :, None] >= (cum - np_e)[:, None, :]) & (p[None, :, None] < cum[:, None, :])
    piece_dst = base[:, None, :] + (p[None, :, None] - (cum - np_e)[:, None, :]) * ROW_ALIGN
    dst = jnp.sum(jnp.where(owns, piece_dst, 0), axis=2).reshape(-1).astype(i32)
    nt_e = (reg_rows - total) // ROW_ALIGN
    cumt = jnp.cumsum(nt_e)
    q = jnp.arange(N_EXPERTS * (TS_MOE // ROW_ALIGN), dtype=i32)
    owns_q = (q[:, None] >= (cumt - nt_e)[None, :]) & (q[:, None] < cumt[None, :])
    tail_dst = (region + total)[None, :] + (q[:, None] - (cumt - nt_e)[None, :]) * ROW_ALIGN
    tail = jnp.sum(jnp.where(owns_q, tail_dst, 0), axis=1).astype(i32)
    ntail = cumt[-1:].astype(i32)
    ntile_e = reg_rows // TS_MOE
    first_t = jnp.cumsum(ntile_e) - ntile_e
    nvalid = jnp.sum(ntile_e).reshape(1).astype(i32)
    s = jnp.arange(nsteps, dtype=i32)
    owns_s = (s[:, None] >= first_t[None, :]) & (s[:, None] < (first_t + ntile_e)[None, :])
    spare = s >= nvalid[0]
    fresh = jnp.any(owns_s & (s[:, None] == first_t[None, :]), axis=1).astype(i32)
    exp = jnp.where(spare, N_EXPERTS - 1, jnp.sum(
        jnp.where(owns_s, jnp.arange(N_EXPERTS, dtype=i32)[None, :], 0), axis=1)).astype(i32)
    blk = jnp.where(spare, nsteps, jnp.sum(
        jnp.where(owns_s, (region // TS_MOE - first_t)[None, :] + s[:, None], 0),
        axis=1)).astype(i32)
    tri = (jnp.arange(TM_DISP)[:, None] > jnp.arange(TM_DISP)[None, :]).astype(BF16)
    upper = (jnp.arange(LANES)[:, None] < jnp.arange(LANES)[None, :]).astype(BF16)
    xs, pos = _dispatch(h2, route, dst, npiece, tail, ntail, tri, upper, rows)
    ys = _moe(xs, blk, exp, nvalid, fresh, wg, wu, wd)
    return _combine(x2, pos, ys, dst, npiece)


def _rope_tables(positions):
    inv = ROPE_THETA ** (-jnp.arange(0, ROPE_DIM, 2, dtype=F32) / ROPE_DIM)
    ang = positions.astype(F32).reshape(-1, 1) * inv
    cos, sin = jnp.cos(ang), jnp.sin(ang)
    d = jnp.arange(LANES) % HEAD_DIM
    in_rope = d < ROPE_DIM
    place = (((d % ROPE_HALF)[None, :] == jnp.arange(ROPE_HALF)[:, None])
             & in_rope[None, :]).astype(F32)
    sign = jnp.where(d < ROPE_HALF, -1.0, 1.0)
    hp = lax.Precision.HIGHEST
    cos_t = jnp.dot(cos, place, precision=hp) + (1.0 - in_rope.astype(F32))
    sin_t = jnp.dot(sin, place * sign, precision=hp)
    return cos_t, sin_t


def _s5_params(lam_re, lam_im, log_dt, b_re, b_im, c_re, c_im, bsz):
    dt = jnp.exp(log_dt)[:, None]
    mag = jnp.exp(lam_re * dt)
    lb_re = mag * jnp.cos(lam_im * dt)
    lb_im = mag * jnp.sin(lam_im * dt)
    den = lam_re * lam_re + lam_im * lam_im
    k_re = ((lb_re - 1.0) * lam_re + lb_im * lam_im) / den
    k_im = (lb_im * lam_re - (lb_re - 1.0) * lam_im) / den
    bb_re = k_re[..., None] * b_re - k_im[..., None] * b_im
    bb_im = k_re[..., None] * b_im + k_im[..., None] * b_re
    eye = jnp.eye(SSM_GROUPS, dtype=F32)
    blk_b = lambda m: jnp.einsum('gph,gk->ghkp', m, eye).reshape(SSM_WIDTH, N_STATE)
    blk_c = lambda m: jnp.einsum('ghp,gk->gpkh', m, eye).reshape(N_STATE, SSM_WIDTH)
    bmat = jnp.concatenate([blk_b(bb_re), blk_b(bb_im)], axis=1).astype(BF16)
    cmat = jnp.concatenate([blk_c(c_re), blk_c(-c_im)], axis=0).astype(BF16)
    a_re = jnp.broadcast_to(lb_re.reshape(1, N_STATE), (bsz, N_STATE))
    a_im = jnp.broadcast_to(lb_im.reshape(1, N_STATE), (bsz, N_STATE))
    return bmat, a_re, a_im, cmat


def _time_major_perm(bsz, chunk):
    r = jnp.arange(bsz * chunk)
    src = (r % bsz) * chunk + r // bsz
    perm = (src[:, None] == jnp.arange(bsz * chunk)[None, :]).astype(BF16)
    return perm, perm.T


def kernel(x, positions, norm_mix_g, w_in, q_norm_g, k_norm_g, lambda_q1, lambda_k1, lambda_q2, lambda_k2, subln_g, w_o_attn, ssm_lambda_re, ssm_lambda_im, ssm_log_dt, ssm_b_re, ssm_b_im, ssm_c_re, ssm_c_im, ssm_d, w_glu, w_out, norm_ffn_g, w_router_group, b_router_group, w_router_expert, b_router_expert, w_expert_gate, w_expert_up, w_expert_down):
    bsz, seq, _ = x.shape
    assert bsz == SUBLANES and seq % (2 * TQ) == 0 and seq % SSM_CHUNK == 0
    assert norm_mix_g.shape[0] == 1
    t = bsz * seq
    x2d = x.reshape(t, D_MODEL)
    l = 0

    cos_t, sin_t = _rope_tables(positions)
    w_qkvu = w_in[l][:, :QKVU_COLS].astype(BF16)
    w_gates = w_in[l][:, QKVU_COLS:].astype(BF16)
    qg = jnp.tile(q_norm_g[l].reshape(1, HEAD_DIM), (1, 2))
    kg = jnp.tile(k_norm_g[l].reshape(1, HEAD_DIM), (1, 2))
    q, k, v, u = _in_proj(x2d, norm_mix_g[l].reshape(1, D_MODEL), w_qkvu, cos_t, sin_t, qg, kg)

    lam = (jnp.exp(jnp.sum(lambda_q1[l] * lambda_k1[l]))
           - jnp.exp(jnp.sum(lambda_q2[l] * lambda_k2[l])) + LAM_INIT).reshape(1)
    o_attn = _diff_attn(q, k, v, lam, subln_g[l].reshape(1, V_DIM), bsz, seq)

    bmat, a_re, a_im, cmat = _s5_params(
        ssm_lambda_re[l], ssm_lambda_im[l], ssm_log_dt[l], ssm_b_re[l], ssm_b_im[l],
        ssm_c_re[l], ssm_c_im[l], bsz)
    perm, perm_t = _time_major_perm(bsz, SSM_CHUNK)
    gy = _s5_scan(u.reshape(bsz, seq, SSM_WIDTH), perm, perm_t, bmat, a_re, a_im, cmat,
                  ssm_d[l].reshape(1, SSM_WIDTH)).reshape(t, SSM_WIDTH)

    w_r = jnp.concatenate(
        [w_router_expert[l].reshape(D_MODEL, N_EXPERTS), w_router_group[l],
         jnp.zeros((D_MODEL, LANES - N_EXPERTS - N_EXPERT_GROUPS), F32)], axis=1)
    b_r = jnp.concatenate(
        [b_router_expert[l].reshape(N_EXPERTS), b_router_group[l],
         jnp.zeros((LANES - N_EXPERTS - N_EXPERT_GROUPS,), F32)]).reshape(1, LANES)
    wr_hi = w_r.astype(BF16)
    wr_lo = (w_r - wr_hi.astype(F32)).astype(BF16)
    x2, h2, route, cnt = _merge(
        x2d, o_attn, gy, norm_mix_g[l].reshape(1, D_MODEL), w_gates,
        w_o_attn[l].astype(BF16), w_glu[l].astype(BF16), w_out[l].astype(BF16),
        norm_ffn_g[l].reshape(1, D_MODEL), jnp.concatenate([wr_hi, wr_lo], axis=1), b_r)

    wg = w_expert_gate[l].reshape(N_EXPERTS, D_MODEL, D_EXPERT)
    wu = w_expert_up[l].reshape(N_EXPERTS, D_MODEL, D_EXPERT)
    wd = w_expert_down[l].reshape(N_EXPERTS, D_EXPERT, D_MODEL)
    out = _sparse_moe(h2, route, cnt, x2, wg, wu, wd)
    return out.reshape(bsz, seq, D_MODEL)
```

```python
import functools
import math

import jax
import jax.numpy as jnp
from jax import lax
from jax.experimental import pallas as pl
from jax.experimental.pallas import tpu as pltpu

F32 = jnp.float32
BF16 = jnp.bfloat16

D_MODEL = 1024
N_HEADS = 4
HEAD_DIM = 64
V_DIM = 2 * HEAD_DIM
ATTN_WIDTH = N_HEADS * V_DIM
ROPE_THETA = 500000.0
ROPE_DIM = HEAD_DIM // 4
ROPE_HALF = ROPE_DIM // 2
SSM_WIDTH = D_MODEL // 2
SSM_GROUP = 16
SSM_GROUPS = SSM_WIDTH // SSM_GROUP
SSM_STATE = 64
N_STATE = SSM_GROUPS * SSM_STATE
N_EXPERT_GROUPS = 4
EXPERTS_PER_GROUP = 8
N_EXPERTS = N_EXPERT_GROUPS * EXPERTS_PER_GROUP
D_EXPERT = D_MODEL // 4
EPS = 1e-6
LAM_INIT = 0.8 - 0.6 * math.exp(-0.3 * 0)
QKVU_COLS = 4 * ATTN_WIDTH
LANES = 128
SUBLANES = 8
NEG_BIG = -1e30
LOG2_E = math.log2(math.e)

VMEM_LIMIT = 48 * 1024 * 1024

TM_IN = 512
TQ = 512
SSM_CHUNK = 64
SCAN_LANES = 512
TM_MERGE = 1024
TM_DISP = 256
DISP_SUB = 4
TS_MOE = 1024
ROW_ALIGN = 16
XS_COLS = D_MODEL + LANES
SORT_ROWS = 2 * TM_DISP + N_EXPERTS * ROW_ALIGN
MAX_PIECES = SORT_ROWS // ROW_ALIGN
POS_RADIX = 32


def _const_spec(shape):
    return pl.BlockSpec(shape, lambda *_: (0,) * len(shape))


def _in_proj_kernel(x_ref, g_ref, w_ref, cos_ref, sin_ref, qg_ref, kg_ref,
                    q_ref, k_ref, v_ref, u_ref):
    x = x_ref[...]
    ms = jnp.mean(x * x, axis=-1, keepdims=True)
    h = (x * lax.rsqrt(ms + EPS) * g_ref[...]).astype(BF16)
    cos_t = cos_ref[...]
    sin_t = sin_ref[...]
    lane = lax.broadcasted_iota(jnp.int32, (x.shape[0], LANES), 1)
    low_half = (lane % HEAD_DIM) < ROPE_HALF
    same_comp = (lax.broadcasted_iota(jnp.int32, (LANES, LANES), 0) // HEAD_DIM
                 == lax.broadcasted_iota(jnp.int32, (LANES, LANES), 1) // HEAD_DIM
                 ).astype(BF16)

    def norm_rope(blk, gain, scale):
        ssum = jnp.dot((blk * blk).astype(BF16), same_comp, preferred_element_type=F32)
        nb = blk * lax.rsqrt(ssum * (1.0 / HEAD_DIM) + EPS) * gain
        up = pltpu.roll(nb, LANES - ROPE_HALF, axis=1)
        dn = pltpu.roll(nb, ROPE_HALF, axis=1)
        partner = jnp.where(low_half, up, dn)
        return ((nb * cos_t + partner * sin_t) * scale).astype(BF16)

    for j in range(0, N_HEADS, 2):
        sl2 = slice(j * LANES, (j + 2) * LANES)
        qb = jnp.dot(h, w_ref[:, sl2], preferred_element_type=F32)
        kb = jnp.dot(h, w_ref[:, ATTN_WIDTH + j * LANES:ATTN_WIDTH + (j + 2) * LANES],
                     preferred_element_type=F32)
        for jj in range(2):
            sl = slice((j + jj) * LANES, (j + jj + 1) * LANES)
            half = slice(jj * LANES, (jj + 1) * LANES)
            q_ref[:, sl] = norm_rope(qb[:, half], qg_ref[...], LOG2_E * HEAD_DIM ** -0.5)
            k_ref[:, sl] = norm_rope(kb[:, half], kg_ref[...], 1.0)
    v_ref[...] = jnp.dot(h, w_ref[:, 2 * ATTN_WIDTH:3 * ATTN_WIDTH],
                         preferred_element_type=F32).astype(BF16)
    u_ref[...] = jnp.dot(h, w_ref[:, 3 * ATTN_WIDTH:4 * ATTN_WIDTH],
                         preferred_element_type=F32).astype(BF16)


def _in_proj(x2d, g, w, cos_t, sin_t, qg, kg):
    t = x2d.shape[0]
    tok = lambda i: (i, 0)
    out = jax.ShapeDtypeStruct((t, ATTN_WIDTH), BF16)
    return pl.pallas_call(
        _in_proj_kernel,
        grid=(t // TM_IN,),
        in_specs=[
            pl.BlockSpec((TM_IN, D_MODEL), tok),
            _const_spec((1, D_MODEL)),
            _const_spec((D_MODEL, QKVU_COLS)),
            pl.BlockSpec((TM_IN, LANES), tok),
            pl.BlockSpec((TM_IN, LANES), tok),
            _const_spec((1, LANES)),
            _const_spec((1, LANES)),
        ],
        out_specs=[pl.BlockSpec((TM_IN, ATTN_WIDTH), tok)] * 4,
        out_shape=[out] * 4,
        compiler_params=pltpu.CompilerParams(
            dimension_semantics=("arbitrary",), vmem_limit_bytes=VMEM_LIMIT),
        name="in_proj",
    )(x2d, g, w, cos_t, sin_t, qg, kg)


def _attn_kernel(qi_ref, kp_ref, mode_ref, lam_ref, q_ref, k_ref, v_ref, sg_ref, o_ref,
                 qs_ref, m_ref, l_ref, acc_ref):
    step = pl.program_id(1)
    kp = kp_ref[step]
    mode = mode_ref[step]

    hq = TQ // 2

    @pl.when(kp == 0)
    def _init():
        for h in range(N_HEADS):
            q = q_ref[:, h * LANES:(h + 1) * LANES]
            lane = lax.broadcasted_iota(jnp.int32, q.shape, 1)
            zero = jnp.zeros_like(q)
            only0 = jnp.where(lane < HEAD_DIM, q, zero)
            only1 = jnp.where(lane < HEAD_DIM, zero, q)
            for half in range(2):
                rows = slice(half * hq, (half + 1) * hq)
                qs_ref[h, 2 * half * hq:(2 * half + 1) * hq, :] = only0[rows, :]
                qs_ref[h, (2 * half + 1) * hq:(2 * half + 2) * hq, :] = only1[rows, :]
        m_ref[...] = jnp.full(m_ref.shape, NEG_BIG, F32)
        l_ref[...] = jnp.zeros(l_ref.shape, F32)
        acc_ref[...] = jnp.zeros(acc_ref.shape, F32)

    def update(h, rows, key0, nkeys, diag_offset=None):
        hs = slice(h * LANES, (h + 1) * LANES)
        ks = slice(key0, key0 + nkeys)
        s = lax.dot_general(qs_ref[h, rows, :], k_ref[ks, hs], (((1,), (1,)), ((), ())),
                            preferred_element_type=F32)
        if diag_offset is not None:
            row = lax.broadcasted_iota(jnp.int32, s.shape, 0)
            row = row % hq + row // TQ * hq + diag_offset
            col = lax.broadcasted_iota(jnp.int32, s.shape, 1)
            s = jnp.where(col <= row, s, NEG_BIG)
        m_old = m_ref[h, rows, :]
        m_new = jnp.maximum(m_old, jnp.max(s, axis=-1, keepdims=True))
        alpha = jnp.exp2(m_old - m_new)
        p = jnp.exp2(s - jnp.concatenate([m_new] * (nkeys // LANES), axis=1))
        v_ones = jnp.concatenate([v_ref[ks, hs], jnp.ones((nkeys, LANES), BF16)], axis=1)
        pv = jnp.dot(p.astype(BF16), v_ones, preferred_element_type=F32)
        l_ref[h, rows, :] = alpha * l_ref[h, rows, :] + pv[:, LANES:2 * LANES]
        acc_ref[h, rows, :] = alpha * acc_ref[h, rows, :] + pv[:, 0:LANES]
        m_ref[h, rows, :] = m_new

    all_rows = slice(0, 2 * TQ)

    def update_diag(h, key0):
        update(h, slice(0, TQ), key0, hq, diag_offset=0)
        update(h, slice(TQ, 2 * TQ), key0, TQ, diag_offset=hq)

    def finish(h):
        o = acc_ref[h] / l_ref[h]
        d = jnp.concatenate(
            [o[2 * half * hq:(2 * half + 1) * hq, :]
             - lam_ref[0] * o[(2 * half + 1) * hq:(2 * half + 2) * hq, :]
             for half in range(2)], axis=0)
        ms = jnp.mean(d * d, axis=-1, keepdims=True)
        d = d * lax.rsqrt(ms + EPS) * sg_ref[...] * (1.0 - LAM_INIT)
        o_ref[:, h * LANES:(h + 1) * LANES] = d.astype(BF16)

    @pl.when(mode == 0)
    def _below():
        for h in range(N_HEADS):
            update(h, all_rows, 0, TQ)
            update(h, all_rows, TQ, TQ)

    @pl.when(mode == 1)
    def _below_then_diag():
        for h in range(N_HEADS):
            update(h, all_rows, 0, TQ)
            update(h, all_rows, TQ, TQ, diag_offset=0)
            finish(h)

    @pl.when(mode == 2)
    def _diag_only():
        for h in range(N_HEADS):
            update_diag(h, 0)
            finish(h)


def _diff_attn(q, k, v, lam, subln_g, bsz, seq):
    nq = seq // TQ
    steps = []
    for i in range(nq):
        for p in range(i // 2 + 1):
            mode = 0 if 2 * p + 1 < i else (1 if 2 * p + 1 == i else 2)
            steps.append((i, p, mode))
    qi = jnp.asarray([s[0] for s in steps], jnp.int32)
    kp = jnp.asarray([s[1] for s in steps], jnp.int32)
    mode = jnp.asarray([s[2] for s in steps], jnp.int32)
    q_map = lambda b, s, qi, kp, mode: (b * nq + qi[s], 0)
    k_map = lambda b, s, qi, kp, mode: (b * (nq // 2) + kp[s], 0)
    grid_spec = pltpu.PrefetchScalarGridSpec(
        num_scalar_prefetch=3,
        grid=(bsz, len(steps)),
        in_specs=[
            pl.BlockSpec(memory_space=pltpu.SMEM),
            pl.BlockSpec((TQ, ATTN_WIDTH), q_map),
            pl.BlockSpec((2 * TQ, ATTN_WIDTH), k_map),
            pl.BlockSpec((2 * TQ, ATTN_WIDTH), k_map),
            pl.BlockSpec((1, LANES), lambda b, s, qi, kp, mode: (0, 0)),
        ],
        out_specs=pl.BlockSpec((TQ, ATTN_WIDTH), q_map),
        scratch_shapes=[
            pltpu.VMEM((N_HEADS, 2 * TQ, LANES), BF16),
            pltpu.VMEM((N_HEADS, 2 * TQ, LANES), F32),
            pltpu.VMEM((N_HEADS, 2 * TQ, LANES), F32),
            pltpu.VMEM((N_HEADS, 2 * TQ, LANES), F32),
        ],
    )
    return pl.pallas_call(
        _attn_kernel,
        grid_spec=grid_spec,
        out_shape=jax.ShapeDtypeStruct((bsz * seq, ATTN_WIDTH), BF16),
        compiler_params=pltpu.CompilerParams(
            dimension_semantics=("arbitrary", "arbitrary"),
            vmem_limit_bytes=VMEM_LIMIT),
        name="diff_attn",
    )(qi, kp, mode, lam, q, k, v, subln_g)


def _gelu_tanh(x):
    c = math.sqrt(2.0 / math.pi)
    return 0.5 * x * (1.0 + jnp.tanh(c * (x + 0.044715 * (x * x * x))))


def _s5_kernel(u_ref, p_ref, pt_ref, b_ref, are_ref, aim_ref, c_ref, d_ref, o_ref,
               st_ref, state_ref):
    bsz, chunk, width = u_ref.shape
    rows = bsz * chunk

    @pl.when(pl.program_id(0) == 0)
    def _init():
        state_ref[...] = jnp.zeros(state_ref.shape, F32)

    u_bt = u_ref[...].reshape(rows, width)
    u_tm = jnp.dot(p_ref[...], u_bt, preferred_element_type=F32)
    u_tm_b = u_tm.astype(BF16)

    slab = SCAN_LANES // SSM_STATE * SSM_GROUP
    g_parts = []
    for ch in range(N_STATE // SCAN_LANES):
        re = slice(ch * SCAN_LANES, (ch + 1) * SCAN_LANES)
        im = slice(N_STATE + ch * SCAN_LANES, N_STATE + (ch + 1) * SCAN_LANES)
        cs = slice(ch * slab, (ch + 1) * slab)
        st_ref[:, re] = jnp.dot(u_tm_b[:, cs], b_ref[cs, re], preferred_element_type=F32)
        st_ref[:, im] = jnp.dot(u_tm_b[:, cs], b_ref[cs, im], preferred_element_type=F32)
        a_re = are_ref[:, re]
        a_im = aim_ref[:, re]

        def body(t, carry, re=re, im=im, a_re=a_re, a_im=a_im):
            s_re, s_im = carry
            r0 = pl.multiple_of(t * SUBLANES, SUBLANES)
            n_re = a_re * s_re - a_im * s_im + st_ref[pl.ds(r0, SUBLANES), re]
            n_im = a_re * s_im + a_im * s_re + st_ref[pl.ds(r0, SUBLANES), im]
            st_ref[pl.ds(r0, SUBLANES), re] = n_re
            st_ref[pl.ds(r0, SUBLANES), im] = n_im
            return n_re, n_im

        s_re, s_im = lax.fori_loop(0, chunk, body, (state_ref[:, re], state_ref[:, im]),
                                   unroll=True)
        state_ref[:, re] = s_re
        state_ref[:, im] = s_im
        y = (jnp.dot(st_ref[:, re].astype(BF16), c_ref[re, cs], preferred_element_type=F32)
             + jnp.dot(st_ref[:, im].astype(BF16), c_ref[im, cs], preferred_element_type=F32))
        y = y + d_ref[:, cs] * u_tm[:, cs]
        g_parts.append(_gelu_tanh(y).astype(BF16))

    g_tm = jnp.concatenate(g_parts, axis=1)
    g_bt = jnp.dot(pt_ref[...], g_tm, preferred_element_type=F32)
    o_ref[...] = g_bt.astype(BF16).reshape(bsz, chunk, width)


def _s5_scan(u3, perm, perm_t, bmat, a_re, a_im, cmat, dvec):
    bsz, seq, width = u3.shape
    rows = bsz * SSM_CHUNK
    blk = pl.BlockSpec((bsz, SSM_CHUNK, width), lambda c: (0, c, 0))
    return pl.pallas_call(
        _s5_kernel,
        grid=(seq // SSM_CHUNK,),
        in_specs=[
            blk,
            _const_spec((rows, rows)),
            _const_spec((rows, rows)),
            _const_spec((width, 2 * N_STATE)),
            _const_spec((bsz, N_STATE)),
            _const_spec((bsz, N_STATE)),
            _const_spec((2 * N_STATE, width)),
            _const_spec((1, width)),
        ],
        out_specs=blk,
        out_shape=jax.ShapeDtypeStruct(u3.shape, BF16),
        scratch_shapes=[
            pltpu.VMEM((rows, 2 * N_STATE), F32),
            pltpu.VMEM((bsz, 2 * N_STATE), F32),
        ],
        compiler_params=pltpu.CompilerParams(
            dimension_semantics=("arbitrary",), vmem_limit_bytes=VMEM_LIMIT),
        name="s5_scan",
    )(u3, perm, perm_t, bmat, a_re, a_im, cmat, dvec)


def _merge_kernel(x_ref, oa_ref, gy_ref, g1_ref, wg_ref, woa_ref, wglu_ref, wout_ref,
                  g2_ref, wr_ref, br_ref, x2_ref, pay_ref, cnt_ref):
    n_sub = TM_MERGE // TM_DISP
    logits = [None] * n_sub

    def rows_of(k):
        return slice(k * TM_DISP, (k + 1) * TM_DISP)

    def tail(k):
        _route_rows(logits[k], pay_ref.at[rows_of(k), :], cnt_ref.at[k])

    for k in range(n_sub):
        rs = rows_of(k)
        logits[k] = _merge_rows(x_ref[rs, :], oa_ref[rs, :], gy_ref[rs, :], g1_ref, wg_ref,
                                woa_ref, wglu_ref, wout_ref, g2_ref, wr_ref, br_ref,
                                x2_ref.at[rs, :], pay_ref.at[rs, :])
        if k > 0:
            tail(k - 1)
    tail(n_sub - 1)


def _merge_rows(x, o_in, gy, g1_ref, wg_ref, woa_ref, wglu_ref, wout_ref, g2_ref, wr_ref,
                br_ref, x2_ref, pay_ref):
    ms = jnp.mean(x * x, axis=-1, keepdims=True)
    h = (x * lax.rsqrt(ms + EPS) * g1_ref[...]).astype(BF16)
    o_a = jnp.dot(o_in, woa_ref[...], preferred_element_type=F32)
    z_lin = jnp.dot(gy, wglu_ref[:, 0:D_MODEL], preferred_element_type=F32)
    z_gate = jnp.dot(gy, wglu_ref[:, D_MODEL:2 * D_MODEL], preferred_element_type=F32)
    o_s = z_lin * jax.nn.sigmoid(z_gate)
    gate_a = jax.nn.sigmoid(jnp.dot(h, wg_ref[:, 0:D_MODEL], preferred_element_type=F32))
    merged = gate_a * o_a
    gate_s = jax.nn.sigmoid(
        jnp.dot(h, wg_ref[:, D_MODEL:2 * D_MODEL], preferred_element_type=F32))
    merged = merged + gate_s * o_s
    x2 = x + jnp.dot(merged.astype(BF16), wout_ref[...], preferred_element_type=F32)
    x2_ref[...] = x2

    ms2 = jnp.mean(x2 * x2, axis=-1, keepdims=True)
    h2 = x2 * lax.rsqrt(ms2 + EPS) * g2_ref[...]
    h2_hi = h2.astype(BF16)
    h2_lo = (h2 - h2_hi.astype(F32)).astype(BF16)
    pay_ref[:, 0:D_MODEL] = h2_hi

    both = jnp.dot(h2_hi, wr_ref[...], preferred_element_type=F32)
    return (both[:, 0:LANES] + both[:, LANES:2 * LANES]
            + jnp.dot(h2_lo, wr_ref[:, 0:LANES], preferred_element_type=F32)
            + br_ref[...])


def _route_rows(logits, pay_ref, cnt_ref):
    lane = lax.broadcasted_iota(jnp.int32, logits.shape, 1).astype(F32)
    is_grp = (lane >= N_EXPERTS) & (lane < N_EXPERTS + N_EXPERT_GROUPS)
    gl = jnp.where(is_grp, logits, NEG_BIG)
    gmax = jnp.max(gl, axis=-1, keepdims=True)
    gsum = jnp.sum(jnp.where(is_grp, jnp.exp(gl - gmax), 0.0), axis=-1, keepdims=True)
    p_grp = 1.0 / gsum
    big = float(4 * LANES)
    grp = jnp.min(jnp.where(is_grp & (gl == gmax), lane, big), axis=-1,
                  keepdims=True) - N_EXPERTS
    sel = logits
    for g in range(1, N_EXPERT_GROUPS):
        rolled = pltpu.roll(logits, LANES - g * EXPERTS_PER_GROUP, axis=1)
        sel = jnp.where(grp == g, rolled, sel)
    in_grp = lane < EXPERTS_PER_GROUP
    es = jnp.where(in_grp, sel, NEG_BIG)
    top1 = jnp.max(es, axis=-1, keepdims=True)
    i1 = jnp.min(jnp.where(in_grp & (es == top1), lane, big), axis=-1, keepdims=True)
    es2 = jnp.where(lane == i1, NEG_BIG, es)
    top2 = jnp.max(es2, axis=-1, keepdims=True)
    i2 = jnp.min(jnp.where(in_grp & (lane != i1) & (es2 == top2), lane, big), axis=-1,
                 keepdims=True)
    e2 = jnp.exp(top2 - top1)
    w1 = p_grp / (1.0 + e2)
    w2 = p_grp * e2 / (1.0 + e2)
    e1 = grp * EXPERTS_PER_GROUP + i1
    e2x = grp * EXPERTS_PER_GROUP + i2

    def hi_lo(w):
        hi = w.astype(BF16).astype(F32)
        return hi, (w - hi).astype(BF16).astype(F32)

    w1h, w1l = hi_lo(w1)
    w2h, w2l = hi_lo(w2)
    pay_ref[:, D_MODEL:D_MODEL + LANES] = (
        jnp.where(lane == 0.0, e1, 0.0) + jnp.where(lane == 1.0, e2x, 0.0)
        + jnp.where(lane == 2.0, w1h, 0.0) + jnp.where(lane == 3.0, w1l, 0.0)
        + jnp.where(lane == 4.0, w2h, 0.0) + jnp.where(lane == 5.0, w2l, 0.0)).astype(BF16)

    picked = jnp.where((lane == e1) | (lane == e2x), 1.0, 0.0)
    cnt_ref[...] = jnp.sum(picked, axis=0, keepdims=True)


def _merge(x2d, o_attn, gy, g1, w_gates, w_oa, w_glu, w_out, g2, w_router, b_r):
    t = x2d.shape[0]
    tok = lambda i: (i, 0)
    sub_tiles = TM_MERGE // TM_DISP
    return pl.pallas_call(
        _merge_kernel,
        grid=(t // TM_MERGE,),
        in_specs=[
            pl.BlockSpec((TM_MERGE, D_MODEL), tok),
            pl.BlockSpec((TM_MERGE, ATTN_WIDTH), tok),
            pl.BlockSpec((TM_MERGE, SSM_WIDTH), tok),
            _const_spec((1, D_MODEL)),
            _const_spec((D_MODEL, 2 * D_MODEL)),
            _const_spec((ATTN_WIDTH, D_MODEL)),
            _const_spec((SSM_WIDTH, 2 * D_MODEL)),
            _const_spec((D_MODEL, D_MODEL)),
            _const_spec((1, D_MODEL)),
            _const_spec((D_MODEL, 2 * LANES)),
            _const_spec((1, LANES)),
        ],
        out_specs=[
            pl.BlockSpec((TM_MERGE, D_MODEL), tok),
            pl.BlockSpec((TM_MERGE, XS_COLS), tok),
            pl.BlockSpec((sub_tiles, 1, LANES), lambda i: (i, 0, 0)),
        ],
        out_shape=[
            jax.ShapeDtypeStruct((t, D_MODEL), F32),
            jax.ShapeDtypeStruct((t, XS_COLS), BF16),
            jax.ShapeDtypeStruct((t // TM_DISP, 1, LANES), F32),
        ],
        compiler_params=pltpu.CompilerParams(
            dimension_semantics=("arbitrary",), vmem_limit_bytes=VMEM_LIMIT),
        name="merge",
    )(x2d, o_attn, gy, g1, w_gates, w_oa, w_glu, w_out, g2, w_router, b_r)


def _moe_layout(t):
    ntiles = t // TM_DISP
    max_rows = 2 * t + ntiles * N_EXPERTS * (ROW_ALIGN - 1)
    nsteps = -(-max_rows // TS_MOE) + N_EXPERTS
    return ntiles, nsteps


def _lane_col(arr, lane, k):
    return jnp.sum(jnp.where(lane == k, arr, 0.0), axis=-1, keepdims=True)


def _tile_positions(aux, tri_ref, upper_ref, pos_ref):
    lane = lax.broadcasted_iota(jnp.int32, aux.shape, 1)
    lane_f = lane.astype(F32)
    is1 = lane_f == _lane_col(aux, lane, 0)
    is2 = lane_f == _lane_col(aux, lane, 1)
    picked = jnp.where(is1 | is2, 1.0, 0.0)
    rank_all = jnp.dot(tri_ref[...], picked.astype(BF16), preferred_element_type=F32)
    cnt_row = jnp.sum(picked, axis=0, keepdims=True)
    pad_row = jnp.floor((cnt_row + (ROW_ALIGN - 1)) * (1.0 / ROW_ALIGN)) * ROW_ALIGN
    off = jnp.dot(jnp.broadcast_to(pad_row, (SUBLANES, LANES)).astype(BF16), upper_ref[...],
                  preferred_element_type=F32)[0:1, :]
    posmat = rank_all + off
    pos1 = jnp.sum(jnp.where(is1, posmat, 0.0), axis=-1, keepdims=True)
    pos2 = jnp.sum(jnp.where(is2, posmat, 0.0), axis=-1, keepdims=True)
    pos_ref[...] = jnp.where(lane == 0, pos1, 0.0) + jnp.where(lane == 1, pos2, 0.0)

    def digits(p):
        hi = jnp.floor(p * (1.0 / POS_RADIX))
        return hi, p - POS_RADIX * hi

    d1h, d1l = digits(pos1)
    d2h, d2l = digits(pos2)
    dig = (jnp.where(lane == 0, d1h, 0.0) + jnp.where(lane == 1, d1l, 0.0)
           + jnp.where(lane == 2, d2h, 0.0) + jnp.where(lane == 3, d2l, 0.0)).astype(BF16)
    eye8 = (lax.broadcasted_iota(jnp.int32, (SUBLANES, LANES), 0)
            == lax.broadcasted_iota(jnp.int32, (SUBLANES, LANES), 1)).astype(BF16)
    rows = lax.dot_general(eye8, dig, (((1,), (1,)), ((), ())), preferred_element_type=F32)
    return (POS_RADIX * rows[0:1, :] + rows[1:2, :], POS_RADIX * rows[2:3, :] + rows[3:4, :])


def _compact_tile(payload, p1_row, p2_row):
    sub = lax.broadcasted_iota(jnp.int32, (SORT_ROWS, TM_DISP), 0).astype(F32)
    perm = jnp.where((sub == p1_row) | (sub == p2_row), 1.0, 0.0).astype(BF16)
    return jnp.dot(perm, payload, preferred_element_type=F32).astype(BF16)


def _dispatch_kernel(dst_ref, npiece_ref, tail_ref, ntail_ref, pay_ref, tri_ref, upper_ref,
                     xs_ref, pos_ref, buf_ref, zbuf_ref, sem_ref, tsem_ref):
    i = pl.program_id(0)
    last = pl.num_programs(0) - 1
    cur = i % 2

    def piece(tile, slot, sub, p):
        src = pl.multiple_of(p * ROW_ALIGN, ROW_ALIGN)
        dst = pl.multiple_of(dst_ref[tile * MAX_PIECES + p], ROW_ALIGN)
        return pltpu.make_async_copy(buf_ref.at[slot, sub, pl.ds(src, ROW_ALIGN)],
                                     xs_ref.at[pl.ds(dst, ROW_ALIGN)], sem_ref.at[slot, sub])

    def start_all(tile, slot, sub):
        def body(p, c):
            piece(tile, slot, sub, p).start()
            return c
        lax.fori_loop(0, npiece_ref[tile], body, 0)

    def wait_all(tile, slot, sub):
        def body(p, c):
            piece(tile, slot, sub, p).wait()
            return c
        lax.fori_loop(0, npiece_ref[tile], body, 0)

    places = []
    for sub in range(DISP_SUB):
        rs = slice(sub * TM_DISP, (sub + 1) * TM_DISP)
        places.append(_tile_positions(pay_ref[rs, D_MODEL:XS_COLS].astype(F32), tri_ref,
                                      upper_ref, pos_ref.at[rs, :]))
    for sub in range(DISP_SUB):
        rs = slice(sub * TM_DISP, (sub + 1) * TM_DISP)
        buf_ref[cur, sub] = _compact_tile(pay_ref[rs, :], *places[sub])
    for sub in range(DISP_SUB):
        start_all(i * DISP_SUB + sub, cur, sub)

    @pl.when(i > 0)
    def _wait_prev():
        for sub in range(DISP_SUB):
            wait_all((i - 1) * DISP_SUB + sub, 1 - cur, sub)

    @pl.when(i == last)
    def _finish():
        for sub in range(DISP_SUB):
            wait_all(i * DISP_SUB + sub, cur, sub)
        zbuf_ref[...] = jnp.zeros(zbuf_ref.shape, BF16)

        def tail(p):
            dst = pl.multiple_of(tail_ref[p], ROW_ALIGN)
            return pltpu.make_async_copy(zbuf_ref, xs_ref.at[pl.ds(dst, ROW_ALIGN)],
                                         tsem_ref.at[0])

        def start_tail(p, c):
            tail(p).start()
            return c

        def wait_tail(p, c):
            tail(p).wait()
            return c

        lax.fori_loop(0, ntail_ref[0], start_tail, 0)
        lax.fori_loop(0, ntail_ref[0], wait_tail, 0)


def _dispatch(payload, dst, npiece, tail, ntail, rows):
    t = payload.shape[0]
    tm = TM_DISP * DISP_SUB
    tok = lambda i, *_: (i, 0)
    const = lambda i, *_: (0, 0)
    tri = (jnp.arange(TM_DISP)[:, None] > jnp.arange(TM_DISP)[None, :]).astype(BF16)
    upper = (jnp.arange(LANES)[:, None] < jnp.arange(LANES)[None, :]).astype(BF16)
    grid_spec = pltpu.PrefetchScalarGridSpec(
        num_scalar_prefetch=4,
        grid=(t // tm,),
        in_specs=[
            pl.BlockSpec((tm, XS_COLS), tok),
            pl.BlockSpec((TM_DISP, TM_DISP), const),
            pl.BlockSpec((LANES, LANES), const),
        ],
        out_specs=[
            pl.BlockSpec(memory_space=pl.ANY),
            pl.BlockSpec((tm, LANES), tok),
        ],
        scratch_shapes=[
            pltpu.VMEM((2, DISP_SUB, SORT_ROWS, XS_COLS), BF16),
            pltpu.VMEM((ROW_ALIGN, XS_COLS), BF16),
            pltpu.SemaphoreType.DMA((2, DISP_SUB)),
            pltpu.SemaphoreType.DMA((1,)),
        ],
    )
    return pl.pallas_call(
        _dispatch_kernel,
        grid_spec=grid_spec,
        out_shape=[
            jax.ShapeDtypeStruct((rows, XS_COLS), BF16),
            jax.ShapeDtypeStruct((t, LANES), F32),
        ],
        compiler_params=pltpu.CompilerParams(
            dimension_semantics=("arbitrary",), vmem_limit_bytes=VMEM_LIMIT),
        name="dispatch",
    )(dst, npiece, tail, ntail, payload, tri, upper)


def _moe_kernel(blk_ref, exp_ref, nvalid_ref, fresh_ref, xs_ref, wg_ref, wu_ref, wd_ref,
                ys_ref, wgb_ref, wub_ref, wdb_ref):
    s = pl.program_id(0)

    @pl.when(fresh_ref[s] == 1)
    def _cast_weights():
        wgb_ref[...] = wg_ref[...].astype(BF16)
        wub_ref[...] = wu_ref[...].astype(BF16)
        wdb_ref[...] = wd_ref[...].astype(BF16)

    @pl.when(s < nvalid_ref[0])
    def _compute():
        t = xs_ref[:, 0:D_MODEL]
        aux = xs_ref[:, D_MODEL:XS_COLS].astype(F32)
        lane = lax.broadcasted_iota(jnp.int32, aux.shape, 1)
        first = _lane_col(aux, lane, 0) == exp_ref[s].astype(F32)
        w = jnp.where(first, _lane_col(aux, lane, 2) + _lane_col(aux, lane, 3),
                      _lane_col(aux, lane, 4) + _lane_col(aux, lane, 5))
        hg = jnp.dot(t, wgb_ref[...], preferred_element_type=F32)
        hu = jnp.dot(t, wub_ref[...], preferred_element_type=F32)
        a = (hg * jax.nn.sigmoid(hg) * hu * w).astype(BF16)
        ys_ref[...] = jnp.dot(a, wdb_ref[...], preferred_element_type=F32).astype(BF16)

    @pl.when(s == nvalid_ref[0])
    def _spare():
        ys_ref[...] = jnp.zeros(ys_ref.shape, BF16)


def _moe(xs, blk, exp, nvalid, fresh, wg, wu, wd):
    row = lambda s, blk, exp, nv, fr: (blk[s], 0)
    wsel = lambda s, blk, exp, nv, fr: (exp[s], 0, 0)
    grid_spec = pltpu.PrefetchScalarGridSpec(
        num_scalar_prefetch=4,
        grid=(blk.shape[0],),
        in_specs=[
            pl.BlockSpec((TS_MOE, XS_COLS), row),
            pl.BlockSpec((None, D_MODEL, D_EXPERT), wsel),
            pl.BlockSpec((None, D_MODEL, D_EXPERT), wsel),
            pl.BlockSpec((None, D_EXPERT, D_MODEL), wsel),
        ],
        out_specs=pl.BlockSpec((TS_MOE, D_MODEL), row),
        scratch_shapes=[
            pltpu.VMEM((D_MODEL, D_EXPERT), BF16),
            pltpu.VMEM((D_MODEL, D_EXPERT), BF16),
            pltpu.VMEM((D_EXPERT, D_MODEL), BF16),
        ],
    )
    return pl.pallas_call(
        _moe_kernel,
        grid_spec=grid_spec,
        out_shape=jax.ShapeDtypeStruct((xs.shape[0], D_MODEL), BF16),
        compiler_params=pltpu.CompilerParams(
            dimension_semantics=("arbitrary",), vmem_limit_bytes=VMEM_LIMIT),
        name="moe",
    )(blk, exp, nvalid, fresh, xs, wg, wu, wd)


def _combine_kernel(dst_ref, npiece_ref, x2_ref, pos_ref, ys_ref, o_ref, stage_ref, sem_ref):
    i = pl.program_id(0)
    n = pl.num_programs(0)
    cur = i % 2
    tm = TM_DISP

    def piece(tile, slot, sub, p):
        row = pl.multiple_of(p * ROW_ALIGN, ROW_ALIGN)
        src = pl.multiple_of(dst_ref[tile * MAX_PIECES + p], ROW_ALIGN)
        return pltpu.make_async_copy(ys_ref.at[pl.ds(src, ROW_ALIGN)],
                                     stage_ref.at[slot, sub, pl.ds(row, ROW_ALIGN)],
                                     sem_ref.at[slot, sub])

    def start_step(step, slot):
        for sub in range(DISP_SUB):
            tile = step * DISP_SUB + sub

            def body(p, c, tile=tile, sub=sub):
                piece(tile, slot, sub, p).start()
                return c
            lax.fori_loop(0, npiece_ref[tile], body, 0)

    @pl.when(i == 0)
    def _first():
        stage_ref[...] = jnp.zeros(stage_ref.shape, BF16)
        start_step(0, 0)

    @pl.when(i + 1 < n)
    def _prefetch():
        start_step(i + 1, 1 - cur)

    for sub in range(DISP_SUB):
        tile = i * DISP_SUB + sub

        def wait_body(p, c, tile=tile, sub=sub):
            piece(tile, cur, sub, p).wait()
            return c

        lax.fori_loop(0, npiece_ref[tile], wait_body, 0)

    for sub in range(DISP_SUB):
        rs = slice(sub * tm, (sub + 1) * tm)
        pos = pos_ref[rs, :]
        lane128 = lax.broadcasted_iota(jnp.int32, pos.shape, 1)
        p1 = _lane_col(pos, lane128, 0)
        p2 = _lane_col(pos, lane128, 1)
        lane = lax.broadcasted_iota(jnp.int32, (tm, SORT_ROWS), 1).astype(F32)
        pick = jnp.where((lane == p1) | (lane == p2), 1.0, 0.0).astype(BF16)
        o_ref[rs, :] = x2_ref[rs, :] + jnp.dot(pick, stage_ref[cur, sub],
                                               preferred_element_type=F32)


def _combine(x2, pos, ys, dst, npiece):
    t = x2.shape[0]
    tm = TM_DISP * DISP_SUB
    tok = lambda i, *_: (i, 0)
    grid_spec = pltpu.PrefetchScalarGridSpec(
        num_scalar_prefetch=2,
        grid=(t // tm,),
        in_specs=[
            pl.BlockSpec((tm, D_MODEL), tok),
            pl.BlockSpec((tm, LANES), tok),
            pl.BlockSpec(memory_space=pl.ANY),
        ],
        out_specs=pl.BlockSpec((tm, D_MODEL), tok),
        scratch_shapes=[
            pltpu.VMEM((2, DISP_SUB, SORT_ROWS, D_MODEL), BF16),
            pltpu.SemaphoreType.DMA((2, DISP_SUB)),
        ],
    )
    return pl.pallas_call(
        _combine_kernel,
        grid_spec=grid_spec,
        out_shape=jax.ShapeDtypeStruct((t, D_MODEL), F32),
        compiler_params=pltpu.CompilerParams(
            dimension_semantics=("arbitrary",), vmem_limit_bytes=VMEM_LIMIT),
        name="combine",
    )(dst, npiece, x2, pos, ys)


def _sparse_moe(payload, cnt, x2, wg, wu, wd):
    t = payload.shape[0]
    ntiles, nsteps = _moe_layout(t)
    rows = (nsteps + 1) * TS_MOE
    i32 = jnp.int32
    counts = cnt[:, 0, :N_EXPERTS].astype(i32)
    padded = (counts + (ROW_ALIGN - 1)) // ROW_ALIGN * ROW_ALIGN
    total = jnp.sum(padded, axis=0)
    reg_rows = (total + (TS_MOE - 1)) // TS_MOE * TS_MOE
    region = jnp.cumsum(reg_rows) - reg_rows
    base = region[None, :] + jnp.cumsum(padded, axis=0) - padded
    np_e = padded // ROW_ALIGN
    cum = jnp.cumsum(np_e, axis=1)
    npiece = cum[:, -1].astype(i32)
    p = jnp.arange(MAX_PIECES, dtype=i32)
    owns = (p[None, :, None] >= (cum - np_e)[:, None, :]) & (p[None, :, None] < cum[:, None, :])
    piece_dst = base[:, None, :] + (p[None, :, None] - (cum - np_e)[:, None, :]) * ROW_ALIGN
    dst = jnp.sum(jnp.where(owns, piece_dst, 0), axis=2).reshape(-1).astype(i32)
    nt_e = (reg_rows - total) // ROW_ALIGN
    cumt = jnp.cumsum(nt_e)
    q = jnp.arange(N_EXPERTS * (TS_MOE // ROW_ALIGN), dtype=i32)
    owns_q = (q[:, None] >= (cumt - nt_e)[None, :]) & (q[:, None] < cumt[None, :])
    tail_dst = (region + total)[None, :] + (q[:, None] - (cumt - nt_e)[None, :]) * ROW_ALIGN
    tail = jnp.sum(jnp.where(owns_q, tail_dst, 0), axis=1).astype(i32)
    ntail = cumt[-1:].astype(i32)
    ntile_e = reg_rows // TS_MOE
    first_t = jnp.cumsum(ntile_e) - ntile_e
    nvalid = jnp.sum(ntile_e).reshape(1).astype(i32)
    s = jnp.arange(nsteps, dtype=i32)
    owns_s = (s[:, None] >= first_t[None, :]) & (s[:, None] < (first_t + ntile_e)[None, :])
    spare = s >= nvalid[0]
    fresh = jnp.any(owns_s & (s[:, None] == first_t[None, :]), axis=1).astype(i32)
    exp = jnp.where(spare, N_EXPERTS - 1, jnp.sum(
        jnp.where(owns_s, jnp.arange(N_EXPERTS, dtype=i32)[None, :], 0), axis=1)).astype(i32)
    blk = jnp.where(spare, nsteps, jnp.sum(
        jnp.where(owns_s, (region // TS_MOE - first_t)[None, :] + s[:, None], 0),
        axis=1)).astype(i32)
    xs, pos = _dispatch(payload, dst, npiece, tail, ntail, rows)
    ys = _moe(xs, blk, exp, nvalid, fresh, wg, wu, wd)
    return _combine(x2, pos, ys, dst, npiece)


def _rope_tables(positions):
    inv = ROPE_THETA ** (-jnp.arange(0, ROPE_DIM, 2, dtype=F32) / ROPE_DIM)
    ang = positions.astype(F32).reshape(-1, 1) * inv
    cos, sin = jnp.cos(ang), jnp.sin(ang)
    d = jnp.arange(LANES) % HEAD_DIM
    in_rope = d < ROPE_DIM
    place = (((d % ROPE_HALF)[None, :] == jnp.arange(ROPE_HALF)[:, None])
             & in_rope[None, :]).astype(F32)
    sign = jnp.where(d < ROPE_HALF, -1.0, 1.0)
    hp = lax.Precision.HIGHEST
    cos_t = jnp.dot(cos, place, precision=hp) + (1.0 - in_rope.astype(F32))
    sin_t = jnp.dot(sin, place * sign, precision=hp)
    return cos_t, sin_t


def _s5_params(lam_re, lam_im, log_dt, b_re, b_im, c_re, c_im, bsz):
    dt = jnp.exp(log_dt)[:, None]
    mag = jnp.exp(lam_re * dt)
    lb_re = mag * jnp.cos(lam_im * dt)
    lb_im = mag * jnp.sin(lam_im * dt)
    den = lam_re * lam_re + lam_im * lam_im
    k_re = ((lb_re - 1.0) * lam_re + lb_im * lam_im) / den
    k_im = (lb_im * lam_re - (lb_re - 1.0) * lam_im) / den
    bb_re = k_re[..., None] * b_re - k_im[..., None] * b_im
    bb_im = k_re[..., None] * b_im + k_im[..., None] * b_re
    eye = jnp.eye(SSM_GROUPS, dtype=F32)
    blk_b = lambda m: jnp.einsum('gph,gk->ghkp', m, eye).reshape(SSM_WIDTH, N_STATE)
    blk_c = lambda m: jnp.einsum('ghp,gk->gpkh', m, eye).reshape(N_STATE, SSM_WIDTH)
    bmat = jnp.concatenate([blk_b(bb_re), blk_b(bb_im)], axis=1).astype(BF16)
    cmat = jnp.concatenate([blk_c(c_re), blk_c(-c_im)], axis=0).astype(BF16)
    a_re = jnp.broadcast_to(lb_re.reshape(1, N_STATE), (bsz, N_STATE))
    a_im = jnp.broadcast_to(lb_im.reshape(1, N_STATE), (bsz, N_STATE))
    return bmat, a_re, a_im, cmat


def _time_major_perm(bsz, chunk):
    r = jnp.arange(bsz * chunk)
    src = (r % bsz) * chunk + r // bsz
    perm = (src[:, None] == jnp.arange(bsz * chunk)[None, :]).astype(BF16)
    return perm, perm.T


def kernel(x, positions, norm_mix_g, w_in, q_norm_g, k_norm_g, lambda_q1, lambda_k1, lambda_q2, lambda_k2, subln_g, w_o_attn, ssm_lambda_re, ssm_lambda_im, ssm_log_dt, ssm_b_re, ssm_b_im, ssm_c_re, ssm_c_im, ssm_d, w_glu, w_out, norm_ffn_g, w_router_group, b_router_group, w_router_expert, b_router_expert, w_expert_gate, w_expert_up, w_expert_down):
    bsz, seq, _ = x.shape
    assert bsz == SUBLANES and seq % (2 * TQ) == 0 and seq % SSM_CHUNK == 0
    assert norm_mix_g.shape[0] == 1
    t = bsz * seq
    x2d = x.reshape(t, D_MODEL)
    l = 0

    cos_t, sin_t = _rope_tables(positions)
    w_qkvu = w_in[l][:, :QKVU_COLS].astype(BF16)
    w_gates = w_in[l][:, QKVU_COLS:].astype(BF16)
    qg = jnp.tile(q_norm_g[l].reshape(1, HEAD_DIM), (1, 2))
    kg = jnp.tile(k_norm_g[l].reshape(1, HEAD_DIM), (1, 2))
    q, k, v, u = _in_proj(x2d, norm_mix_g[l].reshape(1, D_MODEL), w_qkvu, cos_t, sin_t, qg, kg)

    lam = (jnp.exp(jnp.sum(lambda_q1[l] * lambda_k1[l]))
           - jnp.exp(jnp.sum(lambda_q2[l] * lambda_k2[l])) + LAM_INIT).reshape(1)
    o_attn = _diff_attn(q, k, v, lam, subln_g[l].reshape(1, V_DIM), bsz, seq)

    bmat, a_re, a_im, cmat = _s5_params(
        ssm_lambda_re[l], ssm_lambda_im[l], ssm_log_dt[l], ssm_b_re[l], ssm_b_im[l],
        ssm_c_re[l], ssm_c_im[l], bsz)
    perm, perm_t = _time_major_perm(bsz, SSM_CHUNK)
    gy = _s5_scan(u.reshape(bsz, seq, SSM_WIDTH), perm, perm_t, bmat, a_re, a_im, cmat,
                  ssm_d[l].reshape(1, SSM_WIDTH)).reshape(t, SSM_WIDTH)

    w_r = jnp.concatenate(
        [w_router_expert[l].reshape(D_MODEL, N_EXPERTS), w_router_group[l],
         jnp.zeros((D_MODEL, LANES - N_EXPERTS - N_EXPERT_GROUPS), F32)], axis=1)
    b_r = jnp.concatenate(
        [b_router_expert[l].reshape(N_EXPERTS), b_router_group[l],
         jnp.zeros((LANES - N_EXPERTS - N_EXPERT_GROUPS,), F32)]).reshape(1, LANES)
    wr_hi = w_r.astype(BF16)
    wr_lo = (w_r - wr_hi.astype(F32)).astype(BF16)
    x2, payload, cnt = _merge(
        x2d, o_attn, gy, norm_mix_g[l].reshape(1, D_MODEL), w_gates,
        w_o_attn[l].astype(BF16), w_glu[l].astype(BF16), w_out[l].astype(BF16),
        norm_ffn_g[l].reshape(1, D_MODEL), jnp.concatenate([wr_hi, wr_lo], axis=1), b_r)

    wg = w_expert_gate[l].reshape(N_EXPERTS, D_MODEL, D_EXPERT)
    wu = w_expert_up[l].reshape(N_EXPERTS, D_MODEL, D_EXPERT)
    wd = w_expert_down[l].reshape(N_EXPERTS, D_EXPERT, D_MODEL)
    out = _sparse_moe(payload, cnt, x2, wg, wu, wd)
    return out.reshape(bsz, seq, D_MODEL)
```

```python
import functools
import math

import jax
import jax.numpy as jnp
from jax import lax
from jax.experimental import pallas as pl
from jax.experimental.pallas import tpu as pltpu

F32 = jnp.float32
BF16 = jnp.bfloat16

D_MODEL = 1024
N_HEADS = 4
HEAD_DIM = 64
V_DIM = 2 * HEAD_DIM
ATTN_WIDTH = N_HEADS * V_DIM
ROPE_THETA = 500000.0
ROPE_DIM = HEAD_DIM // 4
ROPE_HALF = ROPE_DIM // 2
SSM_WIDTH = D_MODEL // 2
SSM_GROUP = 16
SSM_GROUPS = SSM_WIDTH // SSM_GROUP
SSM_STATE = 64
N_STATE = SSM_GROUPS * SSM_STATE
N_EXPERT_GROUPS = 4
EXPERTS_PER_GROUP = 8
N_EXPERTS = N_EXPERT_GROUPS * EXPERTS_PER_GROUP
D_EXPERT = D_MODEL // 4
EPS = 1e-6
LAM_INIT = 0.8 - 0.6 * math.exp(-0.3 * 0)
QKVU_COLS = 4 * ATTN_WIDTH
LANES = 128
SUBLANES = 8
NEG_BIG = -1e30
LOG2_E = math.log2(math.e)

VMEM_LIMIT = 48 * 1024 * 1024

TM_IN = 1024
TQ = 512
SSM_CHUNK = 64
SCAN_LANES = 512
TM_MERGE = 1024
TM_DISP = 256
DISP_SUB = 4
TS_MOE = 1024
ROW_ALIGN = 16
XS_COLS = D_MODEL + LANES
SORT_ROWS = 2 * TM_DISP + N_EXPERTS * ROW_ALIGN
MAX_PIECES = SORT_ROWS // ROW_ALIGN
POS_RADIX = 32


def _const_spec(shape):
    return pl.BlockSpec(shape, lambda *_: (0,) * len(shape))


def _in_proj_kernel(x_ref, g_ref, w_ref, cos_ref, sin_ref, qg_ref, kg_ref,
                    q_ref, k_ref, v_ref, u_ref):
    x = x_ref[...]
    ms = jnp.mean(x * x, axis=-1, keepdims=True)
    h = (x * lax.rsqrt(ms + EPS) * g_ref[...]).astype(BF16)
    cos_t = cos_ref[...]
    sin_t = sin_ref[...]
    lane = lax.broadcasted_iota(jnp.int32, (x.shape[0], LANES), 1)
    low_half = (lane % HEAD_DIM) < ROPE_HALF
    same_comp = (lax.broadcasted_iota(jnp.int32, (LANES, LANES), 0) // HEAD_DIM
                 == lax.broadcasted_iota(jnp.int32, (LANES, LANES), 1) // HEAD_DIM
                 ).astype(BF16)

    def norm_rope(blk, gain, scale):
        ssum = jnp.dot((blk * blk).astype(BF16), same_comp, preferred_element_type=F32)
        nb = blk * lax.rsqrt(ssum * (1.0 / HEAD_DIM) + EPS) * gain
        up = pltpu.roll(nb, LANES - ROPE_HALF, axis=1)
        dn = pltpu.roll(nb, ROPE_HALF, axis=1)
        partner = jnp.where(low_half, up, dn)
        return ((nb * cos_t + partner * sin_t) * scale).astype(BF16)

    for j in range(0, N_HEADS, 2):
        sl2 = slice(j * LANES, (j + 2) * LANES)
        qb = jnp.dot(h, w_ref[:, sl2], preferred_element_type=F32)
        kb = jnp.dot(h, w_ref[:, ATTN_WIDTH + j * LANES:ATTN_WIDTH + (j + 2) * LANES],
                     preferred_element_type=F32)
        for jj in range(2):
            sl = slice((j + jj) * LANES, (j + jj + 1) * LANES)
            half = slice(jj * LANES, (jj + 1) * LANES)
            q_ref[:, sl] = norm_rope(qb[:, half], qg_ref[...], LOG2_E * HEAD_DIM ** -0.5)
            k_ref[:, sl] = norm_rope(kb[:, half], kg_ref[...], 1.0)
    v_ref[...] = jnp.dot(h, w_ref[:, 2 * ATTN_WIDTH:3 * ATTN_WIDTH],
                         preferred_element_type=F32).astype(BF16)
    u_ref[...] = jnp.dot(h, w_ref[:, 3 * ATTN_WIDTH:4 * ATTN_WIDTH],
                         preferred_element_type=F32).astype(BF16)


def _in_proj(x2d, g, w, cos_t, sin_t, qg, kg):
    t = x2d.shape[0]
    tok = lambda i: (i, 0)
    out = jax.ShapeDtypeStruct((t, ATTN_WIDTH), BF16)
    return pl.pallas_call(
        _in_proj_kernel,
        grid=(t // TM_IN,),
        in_specs=[
            pl.BlockSpec((TM_IN, D_MODEL), tok),
            _const_spec((1, D_MODEL)),
            _const_spec((D_MODEL, QKVU_COLS)),
            pl.BlockSpec((TM_IN, LANES), tok),
            pl.BlockSpec((TM_IN, LANES), tok),
            _const_spec((1, LANES)),
            _const_spec((1, LANES)),
        ],
        out_specs=[pl.BlockSpec((TM_IN, ATTN_WIDTH), tok)] * 4,
        out_shape=[out] * 4,
        compiler_params=pltpu.CompilerParams(
            dimension_semantics=("arbitrary",), vmem_limit_bytes=VMEM_LIMIT),
        name="in_proj",
    )(x2d, g, w, cos_t, sin_t, qg, kg)


def _attn_kernel(qi_ref, kp_ref, mode_ref, lam_ref, q_ref, k_ref, v_ref, sg_ref, o_ref,
                 qs_ref, m_ref, l_ref, acc_ref):
    step = pl.program_id(1)
    kp = kp_ref[step]
    mode = mode_ref[step]

    hq = TQ // 2

    @pl.when(kp == 0)
    def _init():
        for h in range(N_HEADS):
            q = q_ref[:, h * LANES:(h + 1) * LANES]
            lane = lax.broadcasted_iota(jnp.int32, q.shape, 1)
            zero = jnp.zeros_like(q)
            only0 = jnp.where(lane < HEAD_DIM, q, zero)
            only1 = jnp.where(lane < HEAD_DIM, zero, q)
            for half in range(2):
                rows = slice(half * hq, (half + 1) * hq)
                qs_ref[h, 2 * half * hq:(2 * half + 1) * hq, :] = only0[rows, :]
                qs_ref[h, (2 * half + 1) * hq:(2 * half + 2) * hq, :] = only1[rows, :]
        m_ref[...] = jnp.full(m_ref.shape, NEG_BIG, F32)
        l_ref[...] = jnp.zeros(l_ref.shape, F32)
        acc_ref[...] = jnp.zeros(acc_ref.shape, F32)

    def update(h, rows, key0, nkeys, diag_offset=None):
        hs = slice(h * LANES, (h + 1) * LANES)
        ks = slice(key0, key0 + nkeys)
        s = lax.dot_general(qs_ref[h, rows, :], k_ref[ks, hs], (((1,), (1,)), ((), ())),
                            preferred_element_type=F32)
        if diag_offset is not None:
            row = lax.broadcasted_iota(jnp.int32, s.shape, 0)
            row = row % hq + row // TQ * hq + diag_offset
            col = lax.broadcasted_iota(jnp.int32, s.shape, 1)
            s = jnp.where(col <= row, s, NEG_BIG)
        m_old = m_ref[h, rows, :]
        m_new = jnp.maximum(m_old, jnp.max(s, axis=-1, keepdims=True))
        alpha = jnp.exp2(m_old - m_new)
        p = jnp.exp2(s - jnp.concatenate([m_new] * (nkeys // LANES), axis=1))
        v_ones = jnp.concatenate([v_ref[ks, hs], jnp.ones((nkeys, LANES), BF16)], axis=1)
        pv = jnp.dot(p.astype(BF16), v_ones, preferred_element_type=F32)
        l_ref[h, rows, :] = alpha * l_ref[h, rows, :] + pv[:, LANES:2 * LANES]
        acc_ref[h, rows, :] = alpha * acc_ref[h, rows, :] + pv[:, 0:LANES]
        m_ref[h, rows, :] = m_new

    all_rows = slice(0, 2 * TQ)

    def update_diag(h, key0):
        update(h, slice(0, TQ), key0, hq, diag_offset=0)
        update(h, slice(TQ, 2 * TQ), key0, TQ, diag_offset=hq)

    def finish(h):
        o = acc_ref[h] / l_ref[h]
        d = jnp.concatenate(
            [o[2 * half * hq:(2 * half + 1) * hq, :]
             - lam_ref[0] * o[(2 * half + 1) * hq:(2 * half + 2) * hq, :]
             for half in range(2)], axis=0)
        ms = jnp.mean(d * d, axis=-1, keepdims=True)
        d = d * lax.rsqrt(ms + EPS) * sg_ref[...] * (1.0 - LAM_INIT)
        o_ref[:, h * LANES:(h + 1) * LANES] = d.astype(BF16)

    @pl.when(mode == 0)
    def _below():
        for h in range(N_HEADS):
            update(h, all_rows, 0, TQ)
            update(h, all_rows, TQ, TQ)

    @pl.when(mode == 1)
    def _below_then_diag():
        for h in range(N_HEADS):
            update(h, all_rows, 0, TQ)
            update(h, all_rows, TQ, TQ, diag_offset=0)
            finish(h)

    @pl.when(mode == 2)
    def _diag_only():
        for h in range(N_HEADS):
            update_diag(h, 0)
            finish(h)


def _diff_attn(q, k, v, lam, subln_g, bsz, seq):
    nq = seq // TQ
    steps = []
    for i in range(nq):
        for p in range(i // 2 + 1):
            mode = 0 if 2 * p + 1 < i else (1 if 2 * p + 1 == i else 2)
            steps.append((i, p, mode))
    qi = jnp.asarray([s[0] for s in steps], jnp.int32)
    kp = jnp.asarray([s[1] for s in steps], jnp.int32)
    mode = jnp.asarray([s[2] for s in steps], jnp.int32)
    q_map = lambda b, s, qi, kp, mode: (b * nq + qi[s], 0)
    k_map = lambda b, s, qi, kp, mode: (b * (nq // 2) + kp[s], 0)
    grid_spec = pltpu.PrefetchScalarGridSpec(
        num_scalar_prefetch=3,
        grid=(bsz, len(steps)),
        in_specs=[
            pl.BlockSpec(memory_space=pltpu.SMEM),
            pl.BlockSpec((TQ, ATTN_WIDTH), q_map),
            pl.BlockSpec((2 * TQ, ATTN_WIDTH), k_map),
            pl.BlockSpec((2 * TQ, ATTN_WIDTH), k_map),
            pl.BlockSpec((1, LANES), lambda b, s, qi, kp, mode: (0, 0)),
        ],
        out_specs=pl.BlockSpec((TQ, ATTN_WIDTH), q_map),
        scratch_shapes=[
            pltpu.VMEM((N_HEADS, 2 * TQ, LANES), BF16),
            pltpu.VMEM((N_HEADS, 2 * TQ, LANES), F32),
            pltpu.VMEM((N_HEADS, 2 * TQ, LANES), F32),
            pltpu.VMEM((N_HEADS, 2 * TQ, LANES), F32),
        ],
    )
    return pl.pallas_call(
        _attn_kernel,
        grid_spec=grid_spec,
        out_shape=jax.ShapeDtypeStruct((bsz * seq, ATTN_WIDTH), BF16),
        compiler_params=pltpu.CompilerParams(
            dimension_semantics=("arbitrary", "arbitrary"),
            vmem_limit_bytes=VMEM_LIMIT),
        name="diff_attn",
    )(qi, kp, mode, lam, q, k, v, subln_g)


def _gelu_tanh(x):
    c = math.sqrt(2.0 / math.pi)
    return 0.5 * x * (1.0 + jnp.tanh(c * (x + 0.044715 * (x * x * x))))


def _s5_kernel(u_ref, p_ref, pt_ref, b_ref, are_ref, aim_ref, c_ref, d_ref, o_ref,
               st_ref, state_ref):
    bsz, chunk, width = u_ref.shape
    rows = bsz * chunk

    @pl.when(pl.program_id(0) == 0)
    def _init():
        state_ref[...] = jnp.zeros(state_ref.shape, F32)

    u_bt = u_ref[...].reshape(rows, width)
    u_tm = jnp.dot(p_ref[...], u_bt, preferred_element_type=F32)
    u_tm_b = u_tm.astype(BF16)

    slab = SCAN_LANES // SSM_STATE * SSM_GROUP
    g_parts = []
    for ch in range(N_STATE // SCAN_LANES):
        re = slice(ch * SCAN_LANES, (ch + 1) * SCAN_LANES)
        im = slice(N_STATE + ch * SCAN_LANES, N_STATE + (ch + 1) * SCAN_LANES)
        cs = slice(ch * slab, (ch + 1) * slab)
        st_ref[:, re] = jnp.dot(u_tm_b[:, cs], b_ref[cs, re], preferred_element_type=F32)
        st_ref[:, im] = jnp.dot(u_tm_b[:, cs], b_ref[cs, im], preferred_element_type=F32)
        a_re = are_ref[:, re]
        a_im = aim_ref[:, re]

        def body(t, carry, re=re, im=im, a_re=a_re, a_im=a_im):
            s_re, s_im = carry
            r0 = pl.multiple_of(t * SUBLANES, SUBLANES)
            n_re = a_re * s_re - a_im * s_im + st_ref[pl.ds(r0, SUBLANES), re]
            n_im = a_re * s_im + a_im * s_re + st_ref[pl.ds(r0, SUBLANES), im]
            st_ref[pl.ds(r0, SUBLANES), re] = n_re
            st_ref[pl.ds(r0, SUBLANES), im] = n_im
            return n_re, n_im

        s_re, s_im = lax.fori_loop(0, chunk, body, (state_ref[:, re], state_ref[:, im]),
                                   unroll=True)
        state_ref[:, re] = s_re
        state_ref[:, im] = s_im
        y = (jnp.dot(st_ref[:, re].astype(BF16), c_ref[re, cs], preferred_element_type=F32)
             + jnp.dot(st_ref[:, im].astype(BF16), c_ref[im, cs], preferred_element_type=F32))
        y = y + d_ref[:, cs] * u_tm[:, cs]
        g_parts.append(_gelu_tanh(y).astype(BF16))

    g_tm = jnp.concatenate(g_parts, axis=1)
    g_bt = jnp.dot(pt_ref[...], g_tm, preferred_element_type=F32)
    o_ref[...] = g_bt.astype(BF16).reshape(bsz, chunk, width)


def _s5_scan(u3, perm, perm_t, bmat, a_re, a_im, cmat, dvec):
    bsz, seq, width = u3.shape
    rows = bsz * SSM_CHUNK
    blk = pl.BlockSpec((bsz, SSM_CHUNK, width), lambda c: (0, c, 0))
    return pl.pallas_call(
        _s5_kernel,
        grid=(seq // SSM_CHUNK,),
        in_specs=[
            blk,
            _const_spec((rows, rows)),
            _const_spec((rows, rows)),
            _const_spec((width, 2 * N_STATE)),
            _const_spec((bsz, N_STATE)),
            _const_spec((bsz, N_STATE)),
            _const_spec((2 * N_STATE, width)),
            _const_spec((1, width)),
        ],
        out_specs=blk,
        out_shape=jax.ShapeDtypeStruct(u3.shape, BF16),
        scratch_shapes=[
            pltpu.VMEM((rows, 2 * N_STATE), F32),
            pltpu.VMEM((bsz, 2 * N_STATE), F32),
        ],
        compiler_params=pltpu.CompilerParams(
            dimension_semantics=("arbitrary",), vmem_limit_bytes=VMEM_LIMIT),
        name="s5_scan",
    )(u3, perm, perm_t, bmat, a_re, a_im, cmat, dvec)


def _merge_kernel(x_ref, oa_ref, gy_ref, g1_ref, wg_ref, woa_ref, wglu_ref, wout_ref,
                  g2_ref, wr_ref, br_ref, x2_ref, pay_ref, cnt_ref):
    n_sub = TM_MERGE // TM_DISP
    logits = [None] * n_sub

    def rows_of(k):
        return slice(k * TM_DISP, (k + 1) * TM_DISP)

    def tail(k):
        _route_rows(logits[k], pay_ref.at[rows_of(k), :], cnt_ref.at[k])

    for k in range(n_sub):
        rs = rows_of(k)
        logits[k] = _merge_rows(x_ref[rs, :], oa_ref[rs, :], gy_ref[rs, :], g1_ref, wg_ref,
                                woa_ref, wglu_ref, wout_ref, g2_ref, wr_ref, br_ref,
                                x2_ref.at[rs, :], pay_ref.at[rs, :])
        if k > 0:
            tail(k - 1)
    tail(n_sub - 1)


def _merge_rows(x, o_in, gy, g1_ref, wg_ref, woa_ref, wglu_ref, wout_ref, g2_ref, wr_ref,
                br_ref, x2_ref, pay_ref):
    ms = jnp.mean(x * x, axis=-1, keepdims=True)
    h = (x * lax.rsqrt(ms + EPS) * g1_ref[...]).astype(BF16)
    o_a = jnp.dot(o_in, woa_ref[...], preferred_element_type=F32)
    z_lin = jnp.dot(gy, wglu_ref[:, 0:D_MODEL], preferred_element_type=F32)
    z_gate = jnp.dot(gy, wglu_ref[:, D_MODEL:2 * D_MODEL], preferred_element_type=F32)
    o_s = z_lin * jax.nn.sigmoid(z_gate)
    gate_a = jax.nn.sigmoid(jnp.dot(h, wg_ref[:, 0:D_MODEL], preferred_element_type=F32))
    merged = gate_a * o_a
    gate_s = jax.nn.sigmoid(
        jnp.dot(h, wg_ref[:, D_MODEL:2 * D_MODEL], preferred_element_type=F32))
    merged = merged + gate_s * o_s
    x2 = x + jnp.dot(merged.astype(BF16), wout_ref[...], preferred_element_type=F32)
    x2_ref[...] = x2

    ms2 = jnp.mean(x2 * x2, axis=-1, keepdims=True)
    h2 = x2 * lax.rsqrt(ms2 + EPS) * g2_ref[...]
    h2_hi = h2.astype(BF16)
    h2_lo = (h2 - h2_hi.astype(F32)).astype(BF16)
    pay_ref[:, 0:D_MODEL] = h2_hi

    both = jnp.dot(h2_hi, wr_ref[...], preferred_element_type=F32)
    return (both[:, 0:LANES] + both[:, LANES:2 * LANES]
            + jnp.dot(h2_lo, wr_ref[:, 0:LANES], preferred_element_type=F32)
            + br_ref[...])


def _route_rows(logits, pay_ref, cnt_ref):
    lane = lax.broadcasted_iota(jnp.int32, logits.shape, 1).astype(F32)
    is_grp = (lane >= N_EXPERTS) & (lane < N_EXPERTS + N_EXPERT_GROUPS)
    gl = jnp.where(is_grp, logits, NEG_BIG)
    gmax = jnp.max(gl, axis=-1, keepdims=True)
    gsum = jnp.sum(jnp.where(is_grp, jnp.exp(gl - gmax), 0.0), axis=-1, keepdims=True)
    p_grp = 1.0 / gsum
    big = float(4 * LANES)
    grp = jnp.min(jnp.where(is_grp & (gl == gmax), lane, big), axis=-1,
                  keepdims=True) - N_EXPERTS
    sel = logits
    for g in range(1, N_EXPERT_GROUPS):
        rolled = pltpu.roll(logits, LANES - g * EXPERTS_PER_GROUP, axis=1)
        sel = jnp.where(grp == g, rolled, sel)
    in_grp = lane < EXPERTS_PER_GROUP
    es = jnp.where(in_grp, sel, NEG_BIG)
    top1 = jnp.max(es, axis=-1, keepdims=True)
    i1 = jnp.min(jnp.where(in_grp & (es == top1), lane, big), axis=-1, keepdims=True)
    es2 = jnp.where(lane == i1, NEG_BIG, es)
    top2 = jnp.max(es2, axis=-1, keepdims=True)
    i2 = jnp.min(jnp.where(in_grp & (lane != i1) & (es2 == top2), lane, big), axis=-1,
                 keepdims=True)
    e2 = jnp.exp(top2 - top1)
    w1 = p_grp / (1.0 + e2)
    w2 = p_grp * e2 / (1.0 + e2)
    e1 = grp * EXPERTS_PER_GROUP + i1
    e2x = grp * EXPERTS_PER_GROUP + i2

    def hi_lo(w):
        hi = w.astype(BF16).astype(F32)
        return hi, (w - hi).astype(BF16).astype(F32)

    w1h, w1l = hi_lo(w1)
    w2h, w2l = hi_lo(w2)
    pay_ref[:, D_MODEL:D_MODEL + LANES] = (
        jnp.where(lane == 0.0, e1, 0.0) + jnp.where(lane == 1.0, e2x, 0.0)
        + jnp.where(lane == 2.0, w1h, 0.0) + jnp.where(lane == 3.0, w1l, 0.0)
        + jnp.where(lane == 4.0, w2h, 0.0) + jnp.where(lane == 5.0, w2l, 0.0)).astype(BF16)

    picked = jnp.where((lane == e1) | (lane == e2x), 1.0, 0.0)
    cnt_ref[...] = jnp.sum(picked, axis=0, keepdims=True)


def _merge(x2d, o_attn, gy, g1, w_gates, w_oa, w_glu, w_out, g2, w_router, b_r):
    t = x2d.shape[0]
    tok = lambda i: (i, 0)
    sub_tiles = TM_MERGE // TM_DISP
    return pl.pallas_call(
        _merge_kernel,
        grid=(t // TM_MERGE,),
        in_specs=[
            pl.BlockSpec((TM_MERGE, D_MODEL), tok),
            pl.BlockSpec((TM_MERGE, ATTN_WIDTH), tok),
            pl.BlockSpec((TM_MERGE, SSM_WIDTH), tok),
            _const_spec((1, D_MODEL)),
            _const_spec((D_MODEL, 2 * D_MODEL)),
            _const_spec((ATTN_WIDTH, D_MODEL)),
            _const_spec((SSM_WIDTH, 2 * D_MODEL)),
            _const_spec((D_MODEL, D_MODEL)),
            _const_spec((1, D_MODEL)),
            _const_spec((D_MODEL, 2 * LANES)),
            _const_spec((1, LANES)),
        ],
        out_specs=[
            pl.BlockSpec((TM_MERGE, D_MODEL), tok),
            pl.BlockSpec((TM_MERGE, XS_COLS), tok),
            pl.BlockSpec((sub_tiles, 1, LANES), lambda i: (i, 0, 0)),
        ],
        out_shape=[
            jax.ShapeDtypeStruct((t, D_MODEL), F32),
            jax.ShapeDtypeStruct((t, XS_COLS), BF16),
            jax.ShapeDtypeStruct((t // TM_DISP, 1, LANES), F32),
        ],
        compiler_params=pltpu.CompilerParams(
            dimension_semantics=("arbitrary",), vmem_limit_bytes=VMEM_LIMIT),
        name="merge",
    )(x2d, o_attn, gy, g1, w_gates, w_oa, w_glu, w_out, g2, w_router, b_r)


def _moe_layout(t):
    ntiles = t // TM_DISP
    max_rows = 2 * t + ntiles * N_EXPERTS * (ROW_ALIGN - 1)
    nsteps = -(-max_rows // TS_MOE) + N_EXPERTS
    return ntiles, nsteps


def _lane_col(arr, lane, k):
    return jnp.sum(jnp.where(lane == k, arr, 0.0), axis=-1, keepdims=True)


def _tile_positions(aux, tri_ref, upper_ref, pos_ref):
    lane = lax.broadcasted_iota(jnp.int32, aux.shape, 1)
    lane_f = lane.astype(F32)
    is1 = lane_f == _lane_col(aux, lane, 0)
    is2 = lane_f == _lane_col(aux, lane, 1)
    picked = jnp.where(is1 | is2, 1.0, 0.0)
    rank_all = jnp.dot(tri_ref[...], picked.astype(BF16), preferred_element_type=F32)
    cnt_row = jnp.sum(picked, axis=0, keepdims=True)
    pad_row = jnp.floor((cnt_row + (ROW_ALIGN - 1)) * (1.0 / ROW_ALIGN)) * ROW_ALIGN
    off = jnp.dot(jnp.broadcast_to(pad_row, (SUBLANES, LANES)).astype(BF16), upper_ref[...],
                  preferred_element_type=F32)[0:1, :]
    posmat = rank_all + off
    pos1 = jnp.sum(jnp.where(is1, posmat, 0.0), axis=-1, keepdims=True)
    pos2 = jnp.sum(jnp.where(is2, posmat, 0.0), axis=-1, keepdims=True)
    pos_ref[...] = jnp.where(lane == 0, pos1, 0.0) + jnp.where(lane == 1, pos2, 0.0)

    def digits(p):
        hi = jnp.floor(p * (1.0 / POS_RADIX))
        return hi, p - POS_RADIX * hi

    d1h, d1l = digits(pos1)
    d2h, d2l = digits(pos2)
    dig = (jnp.where(lane == 0, d1h, 0.0) + jnp.where(lane == 1, d1l, 0.0)
           + jnp.where(lane == 2, d2h, 0.0) + jnp.where(lane == 3, d2l, 0.0)).astype(BF16)
    eye8 = (lax.broadcasted_iota(jnp.int32, (SUBLANES, LANES), 0)
            == lax.broadcasted_iota(jnp.int32, (SUBLANES, LANES), 1)).astype(BF16)
    rows = lax.dot_general(eye8, dig, (((1,), (1,)), ((), ())), preferred_element_type=F32)
    return (POS_RADIX * rows[0:1, :] + rows[1:2, :], POS_RADIX * rows[2:3, :] + rows[3:4, :])


def _compact_tile(payload, p1_row, p2_row):
    sub = lax.broadcasted_iota(jnp.int32, (SORT_ROWS, TM_DISP), 0).astype(F32)
    perm = jnp.where((sub == p1_row) | (sub == p2_row), 1.0, 0.0).astype(BF16)
    return jnp.dot(perm, payload, preferred_element_type=F32).astype(BF16)


def _dispatch_kernel(dst_ref, npiece_ref, tail_ref, ntail_ref, pay_ref, tri_ref, upper_ref,
                     xs_ref, pos_ref, buf_ref, zbuf_ref, sem_ref, tsem_ref):
    i = pl.program_id(0)
    last = pl.num_programs(0) - 1
    cur = i % 2

    def piece(tile, slot, sub, p):
        return pltpu.make_async_copy(buf_ref.at[slot, sub, p],
                                     xs_ref.at[dst_ref[tile * MAX_PIECES + p]],
                                     sem_ref.at[slot, sub])

    def start_all(tile, slot, sub):
        def body(p, c):
            piece(tile, slot, sub, p).start()
            return c
        lax.fori_loop(0, npiece_ref[tile], body, 0)

    def wait_all(tile, slot, sub):
        def body(p, c):
            piece(tile, slot, sub, p).wait()
            return c
        lax.fori_loop(0, npiece_ref[tile], body, 0)

    places = []
    for sub in range(DISP_SUB):
        rs = slice(sub * TM_DISP, (sub + 1) * TM_DISP)
        places.append(_tile_positions(pay_ref[rs, D_MODEL:XS_COLS].astype(F32), tri_ref,
                                      upper_ref, pos_ref.at[rs, :]))
    for sub in range(DISP_SUB):
        rs = slice(sub * TM_DISP, (sub + 1) * TM_DISP)
        buf_ref[cur, sub] = _compact_tile(pay_ref[rs, :], *places[sub]).reshape(
            MAX_PIECES, ROW_ALIGN, XS_COLS)
    for sub in range(DISP_SUB):
        start_all(i * DISP_SUB + sub, cur, sub)

    @pl.when(i > 0)
    def _wait_prev():
        for sub in range(DISP_SUB):
            wait_all((i - 1) * DISP_SUB + sub, 1 - cur, sub)

    @pl.when(i == last)
    def _finish():
        for sub in range(DISP_SUB):
            wait_all(i * DISP_SUB + sub, cur, sub)
        zbuf_ref[...] = jnp.zeros(zbuf_ref.shape, BF16)

        def tail(p):
            return pltpu.make_async_copy(zbuf_ref, xs_ref.at[tail_ref[p]], tsem_ref.at[0])

        def start_tail(p, c):
            tail(p).start()
            return c

        def wait_tail(p, c):
            tail(p).wait()
            return c

        lax.fori_loop(0, ntail_ref[0], start_tail, 0)
        lax.fori_loop(0, ntail_ref[0], wait_tail, 0)


def _dispatch(payload, dst, npiece, tail, ntail, rows):
    t = payload.shape[0]
    tm = TM_DISP * DISP_SUB
    tok = lambda i, *_: (i, 0)
    const = lambda i, *_: (0, 0)
    tri = (jnp.arange(TM_DISP)[:, None] > jnp.arange(TM_DISP)[None, :]).astype(BF16)
    upper = (jnp.arange(LANES)[:, None] < jnp.arange(LANES)[None, :]).astype(BF16)
    grid_spec = pltpu.PrefetchScalarGridSpec(
        num_scalar_prefetch=4,
        grid=(t // tm,),
        in_specs=[
            pl.BlockSpec((tm, XS_COLS), tok),
            pl.BlockSpec((TM_DISP, TM_DISP), const),
            pl.BlockSpec((LANES, LANES), const),
        ],
        out_specs=[
            pl.BlockSpec(memory_space=pl.ANY),
            pl.BlockSpec((tm, LANES), tok),
        ],
        scratch_shapes=[
            pltpu.VMEM((2, DISP_SUB, MAX_PIECES, ROW_ALIGN, XS_COLS), BF16),
            pltpu.VMEM((ROW_ALIGN, XS_COLS), BF16),
            pltpu.SemaphoreType.DMA((2, DISP_SUB)),
            pltpu.SemaphoreType.DMA((1,)),
        ],
    )
    xs, pos = pl.pallas_call(
        _dispatch_kernel,
        grid_spec=grid_spec,
        out_shape=[
            jax.ShapeDtypeStruct((rows // ROW_ALIGN, ROW_ALIGN, XS_COLS), BF16),
            jax.ShapeDtypeStruct((t, LANES), F32),
        ],
        compiler_params=pltpu.CompilerParams(
            dimension_semantics=("arbitrary",), vmem_limit_bytes=VMEM_LIMIT),
        name="dispatch",
    )(dst // ROW_ALIGN, npiece, tail // ROW_ALIGN, ntail, payload, tri, upper)
    return xs.reshape(rows, XS_COLS), pos


def _moe_kernel(blk_ref, exp_ref, nvalid_ref, fresh_ref, xs_ref, wg_ref, wu_ref, wd_ref,
                ys_ref, wgb_ref, wub_ref, wdb_ref):
    s = pl.program_id(0)

    @pl.when(fresh_ref[s] == 1)
    def _cast_weights():
        wgb_ref[...] = wg_ref[...].astype(BF16)
        wub_ref[...] = wu_ref[...].astype(BF16)
        wdb_ref[...] = wd_ref[...].astype(BF16)

    @pl.when(s < nvalid_ref[0])
    def _compute():
        t = xs_ref[:, 0:D_MODEL]
        aux = xs_ref[:, D_MODEL:XS_COLS].astype(F32)
        lane = lax.broadcasted_iota(jnp.int32, aux.shape, 1)
        first = _lane_col(aux, lane, 0) == exp_ref[s].astype(F32)
        w = jnp.where(first, _lane_col(aux, lane, 2) + _lane_col(aux, lane, 3),
                      _lane_col(aux, lane, 4) + _lane_col(aux, lane, 5))
        hg = jnp.dot(t, wgb_ref[...], preferred_element_type=F32)
        hu = jnp.dot(t, wub_ref[...], preferred_element_type=F32)
        a = (hg * jax.nn.sigmoid(hg) * hu * w).astype(BF16)
        ys_ref[...] = jnp.dot(a, wdb_ref[...], preferred_element_type=F32).astype(BF16)

    @pl.when(s == nvalid_ref[0])
    def _spare():
        ys_ref[...] = jnp.zeros(ys_ref.shape, BF16)


def _moe(xs, blk, exp, nvalid, fresh, wg, wu, wd):
    row = lambda s, blk, exp, nv, fr: (blk[s], 0)
    wsel = lambda s, blk, exp, nv, fr: (exp[s], 0, 0)
    grid_spec = pltpu.PrefetchScalarGridSpec(
        num_scalar_prefetch=4,
        grid=(blk.shape[0],),
        in_specs=[
            pl.BlockSpec((TS_MOE, XS_COLS), row),
            pl.BlockSpec((None, D_MODEL, D_EXPERT), wsel),
            pl.BlockSpec((None, D_MODEL, D_EXPERT), wsel),
            pl.BlockSpec((None, D_EXPERT, D_MODEL), wsel),
        ],
        out_specs=pl.BlockSpec((TS_MOE, D_MODEL), row),
        scratch_shapes=[
            pltpu.VMEM((D_MODEL, D_EXPERT), BF16),
            pltpu.VMEM((D_MODEL, D_EXPERT), BF16),
            pltpu.VMEM((D_EXPERT, D_MODEL), BF16),
        ],
    )
    return pl.pallas_call(
        _moe_kernel,
        grid_spec=grid_spec,
        out_shape=jax.ShapeDtypeStruct((xs.shape[0], D_MODEL), BF16),
        compiler_params=pltpu.CompilerParams(
            dimension_semantics=("arbitrary",), vmem_limit_bytes=VMEM_LIMIT),
        name="moe",
    )(blk, exp, nvalid, fresh, xs, wg, wu, wd)


def _combine_kernel(dst_ref, npiece_ref, x2_ref, pos_ref, ys_ref, o_ref, stage_ref, sem_ref):
    i = pl.program_id(0)
    n = pl.num_programs(0)
    cur = i % 2
    tm = TM_DISP

    def piece(tile, slot, sub, p):
        row = pl.multiple_of(p * ROW_ALIGN, ROW_ALIGN)
        src = pl.multiple_of(dst_ref[tile * MAX_PIECES + p], ROW_ALIGN)
        return pltpu.make_async_copy(ys_ref.at[pl.ds(src, ROW_ALIGN)],
                                     stage_ref.at[slot, sub, pl.ds(row, ROW_ALIGN)],
                                     sem_ref.at[slot, sub])

    def start_step(step, slot):
        for sub in range(DISP_SUB):
            tile = step * DISP_SUB + sub

            def body(p, c, tile=tile, sub=sub):
                piece(tile, slot, sub, p).start()
                return c
            lax.fori_loop(0, npiece_ref[tile], body, 0)

    @pl.when(i == 0)
    def _first():
        stage_ref[...] = jnp.zeros(stage_ref.shape, BF16)
        start_step(0, 0)

    @pl.when(i + 1 < n)
    def _prefetch():
        start_step(i + 1, 1 - cur)

    for sub in range(DISP_SUB):
        tile = i * DISP_SUB + sub

        def wait_body(p, c, tile=tile, sub=sub):
            piece(tile, cur, sub, p).wait()
            return c

        lax.fori_loop(0, npiece_ref[tile], wait_body, 0)

    for sub in range(DISP_SUB):
        rs = slice(sub * tm, (sub + 1) * tm)
        pos = pos_ref[rs, :]
        lane128 = lax.broadcasted_iota(jnp.int32, pos.shape, 1)
        p1 = _lane_col(pos, lane128, 0)
        p2 = _lane_col(pos, lane128, 1)
        lane = lax.broadcasted_iota(jnp.int32, (tm, SORT_ROWS), 1).astype(F32)
        pick = jnp.where((lane == p1) | (lane == p2), 1.0, 0.0).astype(BF16)
        o_ref[rs, :] = x2_ref[rs, :] + jnp.dot(pick, stage_ref[cur, sub],
                                               preferred_element_type=F32)


def _combine(x2, pos, ys, dst, npiece):
    t = x2.shape[0]
    tm = TM_DISP * DISP_SUB
    tok = lambda i, *_: (i, 0)
    grid_spec = pltpu.PrefetchScalarGridSpec(
        num_scalar_prefetch=2,
        grid=(t // tm,),
        in_specs=[
            pl.BlockSpec((tm, D_MODEL), tok),
            pl.BlockSpec((tm, LANES), tok),
            pl.BlockSpec(memory_space=pl.ANY),
        ],
        out_specs=pl.BlockSpec((tm, D_MODEL), tok),
        scratch_shapes=[
            pltpu.VMEM((2, DISP_SUB, SORT_ROWS, D_MODEL), BF16),
            pltpu.SemaphoreType.DMA((2, DISP_SUB)),
        ],
    )
    return pl.pallas_call(
        _combine_kernel,
        grid_spec=grid_spec,
        out_shape=jax.ShapeDtypeStruct((t, D_MODEL), F32),
        compiler_params=pltpu.CompilerParams(
            dimension_semantics=("arbitrary",), vmem_limit_bytes=VMEM_LIMIT),
        name="combine",
    )(dst, npiece, x2, pos, ys)


def _sparse_moe(payload, cnt, x2, wg, wu, wd):
    t = payload.shape[0]
    ntiles, nsteps = _moe_layout(t)
    rows = (nsteps + 1) * TS_MOE
    i32 = jnp.int32
    counts = cnt[:, 0, :N_EXPERTS].astype(i32)
    padded = (counts + (ROW_ALIGN - 1)) // ROW_ALIGN * ROW_ALIGN
    total = jnp.sum(padded, axis=0)
    reg_rows = (total + (TS_MOE - 1)) // TS_MOE * TS_MOE
    region = jnp.cumsum(reg_rows) - reg_rows
    base = region[None, :] + jnp.cumsum(padded, axis=0) - padded
    np_e = padded // ROW_ALIGN
    cum = jnp.cumsum(np_e, axis=1)
    npiece = cum[:, -1].astype(i32)
    p = jnp.arange(MAX_PIECES, dtype=i32)
    owns = (p[None, :, None] >= (cum - np_e)[:, None, :]) & (p[None, :, None] < cum[:, None, :])
    piece_dst = base[:, None, :] + (p[None, :, None] - (cum - np_e)[:, None, :]) * ROW_ALIGN
    dst = jnp.sum(jnp.where(owns, piece_dst, 0), axis=2).reshape(-1).astype(i32)
    nt_e = (reg_rows - total) // ROW_ALIGN
    cumt = jnp.cumsum(nt_e)
    q = jnp.arange(N_EXPERTS * (TS_MOE // ROW_ALIGN), dtype=i32)
    owns_q = (q[:, None] >= (cumt - nt_e)[None, :]) & (q[:, None] < cumt[None, :])
    tail_dst = (region + total)[None, :] + (q[:, None] - (cumt - nt_e)[None, :]) * ROW_ALIGN
    tail = jnp.sum(jnp.where(owns_q, tail_dst, 0), axis=1).astype(i32)
    ntail = cumt[-1:].astype(i32)
    ntile_e = reg_rows // TS_MOE
    first_t = jnp.cumsum(ntile_e) - ntile_e
    nvalid = jnp.sum(ntile_e).reshape(1).astype(i32)
    s = jnp.arange(nsteps, dtype=i32)
    owns_s = (s[:, None] >= first_t[None, :]) & (s[:, None] < (first_t + ntile_e)[None, :])
    spare = s >= nvalid[0]
    fresh = jnp.any(owns_s & (s[:, None] == first_t[None, :]), axis=1).astype(i32)
    exp = jnp.where(spare, N_EXPERTS - 1, jnp.sum(
        jnp.where(owns_s, jnp.arange(N_EXPERTS, dtype=i32)[None, :], 0), axis=1)).astype(i32)
    blk = jnp.where(spare, nsteps, jnp.sum(
        jnp.where(owns_s, (region // TS_MOE - first_t)[None, :] + s[:, None], 0),
        axis=1)).astype(i32)
    xs, pos = _dispatch(payload, dst, npiece, tail, ntail, rows)
    ys = _moe(xs, blk, exp, nvalid, fresh, wg, wu, wd)
    return _combine(x2, pos, ys, dst, npiece)


def _rope_tables(positions):
    inv = ROPE_THETA ** (-jnp.arange(0, ROPE_DIM, 2, dtype=F32) / ROPE_DIM)
    ang = positions.astype(F32).reshape(-1, 1) * inv
    cos, sin = jnp.cos(ang), jnp.sin(ang)
    d = jnp.arange(LANES) % HEAD_DIM
    in_rope = d < ROPE_DIM
    place = (((d % ROPE_HALF)[None, :] == jnp.arange(ROPE_HALF)[:, None])
             & in_rope[None, :]).astype(F32)
    sign = jnp.where(d < ROPE_HALF, -1.0, 1.0)
    hp = lax.Precision.HIGHEST
    cos_t = jnp.dot(cos, place, precision=hp) + (1.0 - in_rope.astype(F32))
    sin_t = jnp.dot(sin, place * sign, precision=hp)
    return cos_t, sin_t


def _s5_params(lam_re, lam_im, log_dt, b_re, b_im, c_re, c_im, bsz):
    dt = jnp.exp(log_dt)[:, None]
    mag = jnp.exp(lam_re * dt)
    lb_re = mag * jnp.cos(lam_im * dt)
    lb_im = mag * jnp.sin(lam_im * dt)
    den = lam_re * lam_re + lam_im * lam_im
    k_re = ((lb_re - 1.0) * lam_re + lb_im * lam_im) / den
    k_im = (lb_im * lam_re - (lb_re - 1.0) * lam_im) / den
    bb_re = k_re[..., None] * b_re - k_im[..., None] * b_im
    bb_im = k_re[..., None] * b_im + k_im[..., None] * b_re
    eye = jnp.eye(SSM_GROUPS, dtype=F32)
    blk_b = lambda m: jnp.einsum('gph,gk->ghkp', m, eye).reshape(SSM_WIDTH, N_STATE)
    blk_c = lambda m: jnp.einsum('ghp,gk->gpkh', m, eye).reshape(N_STATE, SSM_WIDTH)
    bmat = jnp.concatenate([blk_b(bb_re), blk_b(bb_im)], axis=1).astype(BF16)
    cmat = jnp.concatenate([blk_c(c_re), blk_c(-c_im)], axis=0).astype(BF16)
    a_re = jnp.broadcast_to(lb_re.reshape(1, N_STATE), (bsz, N_STATE))
    a_im = jnp.broadcast_to(lb_im.reshape(1, N_STATE), (bsz, N_STATE))
    return bmat, a_re, a_im, cmat


def _time_major_perm(bsz, chunk):
    r = jnp.arange(bsz * chunk)
    src = (r % bsz) * chunk + r // bsz
    perm = (src[:, None] == jnp.arange(bsz * chunk)[None, :]).astype(BF16)
    return perm, perm.T


def kernel(x, positions, norm_mix_g, w_in, q_norm_g, k_norm_g, lambda_q1, lambda_k1, lambda_q2, lambda_k2, subln_g, w_o_attn, ssm_lambda_re, ssm_lambda_im, ssm_log_dt, ssm_b_re, ssm_b_im, ssm_c_re, ssm_c_im, ssm_d, w_glu, w_out, norm_ffn_g, w_router_group, b_router_group, w_router_expert, b_router_expert, w_expert_gate, w_expert_up, w_expert_down):
    bsz, seq, _ = x.shape
    assert bsz == SUBLANES and seq % (2 * TQ) == 0 and seq % SSM_CHUNK == 0
    assert norm_mix_g.shape[0] == 1
    t = bsz * seq
    x2d = x.reshape(t, D_MODEL)
    l = 0

    cos_t, sin_t = _rope_tables(positions)
    w_qkvu = w_in[l][:, :QKVU_COLS].astype(BF16)
    w_gates = w_in[l][:, QKVU_COLS:].astype(BF16)
    qg = jnp.tile(q_norm_g[l].reshape(1, HEAD_DIM), (1, 2))
    kg = jnp.tile(k_norm_g[l].reshape(1, HEAD_DIM), (1, 2))
    q, k, v, u = _in_proj(x2d, norm_mix_g[l].reshape(1, D_MODEL), w_qkvu, cos_t, sin_t, qg, kg)

    lam = (jnp.exp(jnp.sum(lambda_q1[l] * lambda_k1[l]))
           - jnp.exp(jnp.sum(lambda_q2[l] * lambda_k2[l])) + LAM_INIT).reshape(1)
    o_attn = _diff_attn(q, k, v, lam, subln_g[l].reshape(1, V_DIM), bsz, seq)

    bmat, a_re, a_im, cmat = _s5_params(
        ssm_lambda_re[l], ssm_lambda_im[l], ssm_log_dt[l], ssm_b_re[l], ssm_b_im[l],
        ssm_c_re[l], ssm_c_im[l], bsz)
    perm, perm_t = _time_major_perm(bsz, SSM_CHUNK)
    gy = _s5_scan(u.reshape(bsz, seq, SSM_WIDTH), perm, perm_t, bmat, a_re, a_im, cmat,
                  ssm_d[l].reshape(1, SSM_WIDTH)).reshape(t, SSM_WIDTH)

    w_r = jnp.concatenate(
        [w_router_expert[l].reshape(D_MODEL, N_EXPERTS), w_router_group[l],
         jnp.zeros((D_MODEL, LANES - N_EXPERTS - N_EXPERT_GROUPS), F32)], axis=1)
    b_r = jnp.concatenate(
        [b_router_expert[l].reshape(N_EXPERTS), b_router_group[l],
         jnp.zeros((LANES - N_EXPERTS - N_EXPERT_GROUPS,), F32)]).reshape(1, LANES)
    wr_hi = w_r.astype(BF16)
    wr_lo = (w_r - wr_hi.astype(F32)).astype(BF16)
    x2, payload, cnt = _merge(
        x2d, o_attn, gy, norm_mix_g[l].reshape(1, D_MODEL), w_gates,
        w_o_attn[l].astype(BF16), w_glu[l].astype(BF16), w_out[l].astype(BF16),
        norm_ffn_g[l].reshape(1, D_MODEL), jnp.concatenate([wr_hi, wr_lo], axis=1), b_r)

    wg = w_expert_gate[l].reshape(N_EXPERTS, D_MODEL, D_EXPERT)
    wu = w_expert_up[l].reshape(N_EXPERTS, D_MODEL, D_EXPERT)
    wd = w_expert_down[l].reshape(N_EXPERTS, D_EXPERT, D_MODEL)
    out = _sparse_moe(payload, cnt, x2, wg, wu, wd)
    return out.reshape(bsz, seq, D_MODEL)
```

```python
import functools
import math

import jax
import jax.numpy as jnp
from jax import lax
from jax.experimental import pallas as pl
from jax.experimental.pallas import tpu as pltpu

F32 = jnp.float32
BF16 = jnp.bfloat16

D_MODEL = 1024
N_HEADS = 4
HEAD_DIM = 64
V_DIM = 2 * HEAD_DIM
ATTN_WIDTH = N_HEADS * V_DIM
ROPE_THETA = 500000.0
ROPE_DIM = HEAD_DIM // 4
ROPE_HALF = ROPE_DIM // 2
SSM_WIDTH = D_MODEL // 2
SSM_GROUP = 16
SSM_GROUPS = SSM_WIDTH // SSM_GROUP
SSM_STATE = 64
N_STATE = SSM_GROUPS * SSM_STATE
N_EXPERT_GROUPS = 4
EXPERTS_PER_GROUP = 8
N_EXPERTS = N_EXPERT_GROUPS * EXPERTS_PER_GROUP
D_EXPERT = D_MODEL // 4
EPS = 1e-6
LAM_INIT = 0.8 - 0.6 * math.exp(-0.3 * 0)
QKVU_COLS = 4 * ATTN_WIDTH
LANES = 128
SUBLANES = 8
NEG_BIG = -1e30
LOG2_E = math.log2(math.e)

VMEM_LIMIT = 48 * 1024 * 1024

TM_IN = 1024
TQ = 512
KEY_GROUP = 4
SSM_CHUNK = 64
SCAN_LANES = 512
TM_MERGE = 1024
TM_DISP = 256
DISP_SUB = 4
TS_MOE = 1024
ROW_ALIGN = 16
XS_COLS = D_MODEL + LANES
SORT_ROWS = 2 * TM_DISP + N_EXPERTS * ROW_ALIGN
MAX_PIECES = SORT_ROWS // ROW_ALIGN
POS_RADIX = 32


def _const_spec(shape):
    return pl.BlockSpec(shape, lambda *_: (0,) * len(shape))


def _in_proj_kernel(x_ref, g_ref, w_ref, cos_ref, sin_ref, qg_ref, kg_ref,
                    q_ref, k_ref, v_ref, u_ref):
    x = x_ref[...]
    ms = jnp.mean(x * x, axis=-1, keepdims=True)
    h = (x * lax.rsqrt(ms + EPS) * g_ref[...]).astype(BF16)
    cos_t = cos_ref[...]
    sin_t = sin_ref[...]
    lane = lax.broadcasted_iota(jnp.int32, (x.shape[0], LANES), 1)
    low_half = (lane % HEAD_DIM) < ROPE_HALF
    same_comp = (lax.broadcasted_iota(jnp.int32, (LANES, LANES), 0) // HEAD_DIM
                 == lax.broadcasted_iota(jnp.int32, (LANES, LANES), 1) // HEAD_DIM
                 ).astype(BF16)

    def norm_rope(blk, gain, scale):
        ssum = jnp.dot((blk * blk).astype(BF16), same_comp, preferred_element_type=F32)
        nb = blk * lax.rsqrt(ssum * (1.0 / HEAD_DIM) + EPS) * gain
        up = pltpu.roll(nb, LANES - ROPE_HALF, axis=1)
        dn = pltpu.roll(nb, ROPE_HALF, axis=1)
        partner = jnp.where(low_half, up, dn)
        return ((nb * cos_t + partner * sin_t) * scale).astype(BF16)

    for j in range(0, N_HEADS, 2):
        sl2 = slice(j * LANES, (j + 2) * LANES)
        qb = jnp.dot(h, w_ref[:, sl2], preferred_element_type=F32)
        kb = jnp.dot(h, w_ref[:, ATTN_WIDTH + j * LANES:ATTN_WIDTH + (j + 2) * LANES],
                     preferred_element_type=F32)
        for jj in range(2):
            sl = slice((j + jj) * LANES, (j + jj + 1) * LANES)
            half = slice(jj * LANES, (jj + 1) * LANES)
            q_ref[:, sl] = norm_rope(qb[:, half], qg_ref[...], LOG2_E * HEAD_DIM ** -0.5)
            k_ref[:, sl] = norm_rope(kb[:, half], kg_ref[...], 1.0)
    v_ref[...] = jnp.dot(h, w_ref[:, 2 * ATTN_WIDTH:3 * ATTN_WIDTH],
                         preferred_element_type=F32).astype(BF16)
    u_ref[...] = jnp.dot(h, w_ref[:, 3 * ATTN_WIDTH:4 * ATTN_WIDTH],
                         preferred_element_type=F32).astype(BF16)


def _in_proj(x2d, g, w, cos_t, sin_t, qg, kg):
    t = x2d.shape[0]
    tok = lambda i: (i, 0)
    out = jax.ShapeDtypeStruct((t, ATTN_WIDTH), BF16)
    return pl.pallas_call(
        _in_proj_kernel,
        grid=(t // TM_IN,),
        in_specs=[
            pl.BlockSpec((TM_IN, D_MODEL), tok),
            _const_spec((1, D_MODEL)),
            _const_spec((D_MODEL, QKVU_COLS)),
            pl.BlockSpec((TM_IN, LANES), tok),
            pl.BlockSpec((TM_IN, LANES), tok),
            _const_spec((1, LANES)),
            _const_spec((1, LANES)),
        ],
        out_specs=[pl.BlockSpec((TM_IN, ATTN_WIDTH), tok)] * 4,
        out_shape=[out] * 4,
        compiler_params=pltpu.CompilerParams(
            dimension_semantics=("arbitrary",), vmem_limit_bytes=VMEM_LIMIT),
        name="in_proj",
    )(x2d, g, w, cos_t, sin_t, qg, kg)


def _attn_kernel(qi_ref, kp_ref, mode_ref, lam_ref, q_ref, k_ref, v_ref, sg_ref, o_ref,
                 qs_ref, m_ref, l_ref, acc_ref):
    step = pl.program_id(1)
    kp = kp_ref[step]
    mode = mode_ref[step]

    hq = TQ // 2

    @pl.when(kp == 0)
    def _init():
        for h in range(N_HEADS):
            q = q_ref[:, h * LANES:(h + 1) * LANES]
            lane = lax.broadcasted_iota(jnp.int32, q.shape, 1)
            zero = jnp.zeros_like(q)
            only0 = jnp.where(lane < HEAD_DIM, q, zero)
            only1 = jnp.where(lane < HEAD_DIM, zero, q)
            for half in range(2):
                rows = slice(half * hq, (half + 1) * hq)
                qs_ref[h, 2 * half * hq:(2 * half + 1) * hq, :] = only0[rows, :]
                qs_ref[h, (2 * half + 1) * hq:(2 * half + 2) * hq, :] = only1[rows, :]
        m_ref[...] = jnp.full(m_ref.shape, NEG_BIG, F32)
        l_ref[...] = jnp.zeros(l_ref.shape, F32)
        acc_ref[...] = jnp.zeros(acc_ref.shape, F32)

    def update(h, rows, key0, nkeys, diag_offset=None):
        hs = slice(h * LANES, (h + 1) * LANES)
        ks = slice(key0, key0 + nkeys)
        s = lax.dot_general(qs_ref[h, rows, :], k_ref[ks, hs], (((1,), (1,)), ((), ())),
                            preferred_element_type=F32)
        if diag_offset is not None:
            row = lax.broadcasted_iota(jnp.int32, s.shape, 0)
            row = row % hq + row // TQ * hq + diag_offset
            col = lax.broadcasted_iota(jnp.int32, s.shape, 1)
            s = jnp.where(col <= row, s, NEG_BIG)
        m_old = m_ref[h, rows, :]
        m_new = jnp.maximum(m_old, jnp.max(s, axis=-1, keepdims=True))
        alpha = jnp.exp2(m_old - m_new)
        p = jnp.exp2(s - jnp.concatenate([m_new] * (nkeys // LANES), axis=1))
        v_ones = jnp.concatenate([v_ref[ks, hs], jnp.ones((nkeys, LANES), BF16)], axis=1)
        pv = jnp.dot(p.astype(BF16), v_ones, preferred_element_type=F32)
        l_ref[h, rows, :] = alpha * l_ref[h, rows, :] + pv[:, LANES:2 * LANES]
        acc_ref[h, rows, :] = alpha * acc_ref[h, rows, :] + pv[:, 0:LANES]
        m_ref[h, rows, :] = m_new

    all_rows = slice(0, 2 * TQ)

    def update_diag(h, key0):
        update(h, slice(0, TQ), key0, hq, diag_offset=0)
        update(h, slice(TQ, 2 * TQ), key0, TQ, diag_offset=hq)

    def finish(h):
        o = acc_ref[h] / l_ref[h]
        d = jnp.concatenate(
            [o[2 * half * hq:(2 * half + 1) * hq, :]
             - lam_ref[0] * o[(2 * half + 1) * hq:(2 * half + 2) * hq, :]
             for half in range(2)], axis=0)
        ms = jnp.mean(d * d, axis=-1, keepdims=True)
        d = d * lax.rsqrt(ms + EPS) * sg_ref[...] * (1.0 - LAM_INIT)
        o_ref[:, h * LANES:(h + 1) * LANES] = d.astype(BF16)

    @pl.when(mode == KEY_GROUP)
    def _all_below():
        for h in range(N_HEADS):
            for g in range(KEY_GROUP):
                update(h, all_rows, g * TQ, TQ)

    for d in range(KEY_GROUP):
        @pl.when(mode == d)
        def _up_to_diag(d=d):
            for h in range(N_HEADS):
                for g in range(d):
                    update(h, all_rows, g * TQ, TQ)
                if d == 0:
                    update_diag(h, 0)
                else:
                    update(h, all_rows, d * TQ, TQ, diag_offset=0)
                finish(h)


def _diff_attn(q, k, v, lam, subln_g, bsz, seq):
    nq = seq // TQ
    steps = []
    for i in range(nq):
        for p in range(i // KEY_GROUP + 1):
            steps.append((i, p, min(i - KEY_GROUP * p, KEY_GROUP)))
    qi = jnp.asarray([s[0] for s in steps], jnp.int32)
    kp = jnp.asarray([s[1] for s in steps], jnp.int32)
    mode = jnp.asarray([s[2] for s in steps], jnp.int32)
    q_map = lambda b, s, qi, kp, mode: (b * nq + qi[s], 0)
    k_map = lambda b, s, qi, kp, mode: (b * (nq // KEY_GROUP) + kp[s], 0)
    grid_spec = pltpu.PrefetchScalarGridSpec(
        num_scalar_prefetch=3,
        grid=(bsz, len(steps)),
        in_specs=[
            pl.BlockSpec(memory_space=pltpu.SMEM),
            pl.BlockSpec((TQ, ATTN_WIDTH), q_map),
            pl.BlockSpec((KEY_GROUP * TQ, ATTN_WIDTH), k_map),
            pl.BlockSpec((KEY_GROUP * TQ, ATTN_WIDTH), k_map),
            pl.BlockSpec((1, LANES), lambda b, s, qi, kp, mode: (0, 0)),
        ],
        out_specs=pl.BlockSpec((TQ, ATTN_WIDTH), q_map),
        scratch_shapes=[
            pltpu.VMEM((N_HEADS, 2 * TQ, LANES), BF16),
            pltpu.VMEM((N_HEADS, 2 * TQ, LANES), F32),
            pltpu.VMEM((N_HEADS, 2 * TQ, LANES), F32),
            pltpu.VMEM((N_HEADS, 2 * TQ, LANES), F32),
        ],
    )
    return pl.pallas_call(
        _attn_kernel,
        grid_spec=grid_spec,
        out_shape=jax.ShapeDtypeStruct((bsz * seq, ATTN_WIDTH), BF16),
        compiler_params=pltpu.CompilerParams(
            dimension_semantics=("arbitrary", "arbitrary"),
            vmem_limit_bytes=VMEM_LIMIT),
        name="diff_attn",
    )(qi, kp, mode, lam, q, k, v, subln_g)


def _gelu_tanh(x):
    c = math.sqrt(2.0 / math.pi)
    return 0.5 * x * (1.0 + jnp.tanh(c * (x + 0.044715 * (x * x * x))))


def _s5_kernel(u_ref, p_ref, pt_ref, b_ref, are_ref, aim_ref, c_ref, d_ref, o_ref,
               st_ref, state_ref):
    bsz, chunk, width = u_ref.shape
    rows = bsz * chunk

    @pl.when(pl.program_id(0) == 0)
    def _init():
        state_ref[...] = jnp.zeros(state_ref.shape, F32)

    u_bt = u_ref[...].reshape(rows, width)
    u_tm = jnp.dot(p_ref[...], u_bt, preferred_element_type=F32)
    u_tm_b = u_tm.astype(BF16)

    slab = SCAN_LANES // SSM_STATE * SSM_GROUP
    g_parts = []
    for ch in range(N_STATE // SCAN_LANES):
        re = slice(ch * SCAN_LANES, (ch + 1) * SCAN_LANES)
        im = slice(N_STATE + ch * SCAN_LANES, N_STATE + (ch + 1) * SCAN_LANES)
        cs = slice(ch * slab, (ch + 1) * slab)
        st_ref[:, re] = jnp.dot(u_tm_b[:, cs], b_ref[cs, re], preferred_element_type=F32)
        st_ref[:, im] = jnp.dot(u_tm_b[:, cs], b_ref[cs, im], preferred_element_type=F32)
        a_re = are_ref[:, re]
        a_im = aim_ref[:, re]

        def body(t, carry, re=re, im=im, a_re=a_re, a_im=a_im):
            s_re, s_im = carry
            r0 = pl.multiple_of(t * SUBLANES, SUBLANES)
            n_re = a_re * s_re - a_im * s_im + st_ref[pl.ds(r0, SUBLANES), re]
            n_im = a_re * s_im + a_im * s_re + st_ref[pl.ds(r0, SUBLANES), im]
            st_ref[pl.ds(r0, SUBLANES), re] = n_re
            st_ref[pl.ds(r0, SUBLANES), im] = n_im
            return n_re, n_im

        s_re, s_im = lax.fori_loop(0, chunk, body, (state_ref[:, re], state_ref[:, im]),
                                   unroll=True)
        state_ref[:, re] = s_re
        state_ref[:, im] = s_im
        y = (jnp.dot(st_ref[:, re].astype(BF16), c_ref[re, cs], preferred_element_type=F32)
             + jnp.dot(st_ref[:, im].astype(BF16), c_ref[im, cs], preferred_element_type=F32))
        y = y + d_ref[:, cs] * u_tm[:, cs]
        g_parts.append(_gelu_tanh(y).astype(BF16))

    g_tm = jnp.concatenate(g_parts, axis=1)
    g_bt = jnp.dot(pt_ref[...], g_tm, preferred_element_type=F32)
    o_ref[...] = g_bt.astype(BF16).reshape(bsz, chunk, width)


def _s5_scan(u3, perm, perm_t, bmat, a_re, a_im, cmat, dvec):
    bsz, seq, width = u3.shape
    rows = bsz * SSM_CHUNK
    blk = pl.BlockSpec((bsz, SSM_CHUNK, width), lambda c: (0, c, 0))
    return pl.pallas_call(
        _s5_kernel,
        grid=(seq // SSM_CHUNK,),
        in_specs=[
            blk,
            _const_spec((rows, rows)),
            _const_spec((rows, rows)),
            _const_spec((width, 2 * N_STATE)),
            _const_spec((bsz, N_STATE)),
            _const_spec((bsz, N_STATE)),
            _const_spec((2 * N_STATE, width)),
            _const_spec((1, width)),
        ],
        out_specs=blk,
        out_shape=jax.ShapeDtypeStruct(u3.shape, BF16),
        scratch_shapes=[
            pltpu.VMEM((rows, 2 * N_STATE), F32),
            pltpu.VMEM((bsz, 2 * N_STATE), F32),
        ],
        compiler_params=pltpu.CompilerParams(
            dimension_semantics=("arbitrary",), vmem_limit_bytes=VMEM_LIMIT),
        name="s5_scan",
    )(u3, perm, perm_t, bmat, a_re, a_im, cmat, dvec)


def _merge_kernel(x_ref, oa_ref, gy_ref, g1_ref, wg_ref, woa_ref, wglu_ref, wout_ref,
                  g2_ref, wr_ref, br_ref, x2_ref, pay_ref, cnt_ref):
    n_sub = TM_MERGE // TM_DISP
    logits = [None] * n_sub

    def rows_of(k):
        return slice(k * TM_DISP, (k + 1) * TM_DISP)

    def tail(k):
        _route_rows(logits[k], pay_ref.at[rows_of(k), :], cnt_ref.at[k])

    for k in range(n_sub):
        rs = rows_of(k)
        logits[k] = _merge_rows(x_ref[rs, :], oa_ref[rs, :], gy_ref[rs, :], g1_ref, wg_ref,
                                woa_ref, wglu_ref, wout_ref, g2_ref, wr_ref, br_ref,
                                x2_ref.at[rs, :], pay_ref.at[rs, :])
        if k > 0:
            tail(k - 1)
    tail(n_sub - 1)


def _merge_rows(x, o_in, gy, g1_ref, wg_ref, woa_ref, wglu_ref, wout_ref, g2_ref, wr_ref,
                br_ref, x2_ref, pay_ref):
    ms = jnp.mean(x * x, axis=-1, keepdims=True)
    h = (x * lax.rsqrt(ms + EPS) * g1_ref[...]).astype(BF16)
    o_a = jnp.dot(o_in, woa_ref[...], preferred_element_type=F32)
    z_lin = jnp.dot(gy, wglu_ref[:, 0:D_MODEL], preferred_element_type=F32)
    z_gate = jnp.dot(gy, wglu_ref[:, D_MODEL:2 * D_MODEL], preferred_element_type=F32)
    o_s = z_lin * jax.nn.sigmoid(z_gate)
    gate_a = jax.nn.sigmoid(jnp.dot(h, wg_ref[:, 0:D_MODEL], preferred_element_type=F32))
    merged = gate_a * o_a
    gate_s = jax.nn.sigmoid(
        jnp.dot(h, wg_ref[:, D_MODEL:2 * D_MODEL], preferred_element_type=F32))
    merged = merged + gate_s * o_s
    x2 = x + jnp.dot(merged.astype(BF16), wout_ref[...], preferred_element_type=F32)
    x2_ref[...] = x2

    ms2 = jnp.mean(x2 * x2, axis=-1, keepdims=True)
    h2 = x2 * lax.rsqrt(ms2 + EPS) * g2_ref[...]
    h2_hi = h2.astype(BF16)
    h2_lo = (h2 - h2_hi.astype(F32)).astype(BF16)
    pay_ref[:, 0:D_MODEL] = h2_hi

    both = jnp.dot(h2_hi, wr_ref[...], preferred_element_type=F32)
    return (both[:, 0:LANES] + both[:, LANES:2 * LANES]
            + jnp.dot(h2_lo, wr_ref[:, 0:LANES], preferred_element_type=F32)
            + br_ref[...])


def _route_rows(logits, pay_ref, cnt_ref):
    lane = lax.broadcasted_iota(jnp.int32, logits.shape, 1).astype(F32)
    is_grp = (lane >= N_EXPERTS) & (lane < N_EXPERTS + N_EXPERT_GROUPS)
    gl = jnp.where(is_grp, logits, NEG_BIG)
    gmax = jnp.max(gl, axis=-1, keepdims=True)
    gsum = jnp.sum(jnp.where(is_grp, jnp.exp(gl - gmax), 0.0), axis=-1, keepdims=True)
    p_grp = 1.0 / gsum
    big = float(4 * LANES)
    grp = jnp.min(jnp.where(is_grp & (gl == gmax), lane, big), axis=-1,
                  keepdims=True) - N_EXPERTS
    sel = logits
    for g in range(1, N_EXPERT_GROUPS):
        rolled = pltpu.roll(logits, LANES - g * EXPERTS_PER_GROUP, axis=1)
        sel = jnp.where(grp == g, rolled, sel)
    in_grp = lane < EXPERTS_PER_GROUP
    es = jnp.where(in_grp, sel, NEG_BIG)
    top1 = jnp.max(es, axis=-1, keepdims=True)
    i1 = jnp.min(jnp.where(in_grp & (es == top1), lane, big), axis=-1, keepdims=True)
    es2 = jnp.where(lane == i1, NEG_BIG, es)
    top2 = jnp.max(es2, axis=-1, keepdims=True)
    i2 = jnp.min(jnp.where(in_grp & (lane != i1) & (es2 == top2), lane, big), axis=-1,
                 keepdims=True)
    e2 = jnp.exp(top2 - top1)
    w1 = p_grp / (1.0 + e2)
    w2 = p_grp * e2 / (1.0 + e2)
    e1 = grp * EXPERTS_PER_GROUP + i1
    e2x = grp * EXPERTS_PER_GROUP + i2

    def hi_lo(w):
        hi = w.astype(BF16).astype(F32)
        return hi, (w - hi).astype(BF16).astype(F32)

    w1h, w1l = hi_lo(w1)
    w2h, w2l = hi_lo(w2)
    pay_ref[:, D_MODEL:D_MODEL + LANES] = (
        jnp.where(lane == 0.0, e1, 0.0) + jnp.where(lane == 1.0, e2x, 0.0)
        + jnp.where(lane == 2.0, w1h, 0.0) + jnp.where(lane == 3.0, w1l, 0.0)
        + jnp.where(lane == 4.0, w2h, 0.0) + jnp.where(lane == 5.0, w2l, 0.0)).astype(BF16)

    picked = jnp.where((lane == e1) | (lane == e2x), 1.0, 0.0)
    cnt_ref[...] = jnp.sum(picked, axis=0, keepdims=True)


def _merge(x2d, o_attn, gy, g1, w_gates, w_oa, w_glu, w_out, g2, w_router, b_r):
    t = x2d.shape[0]
    tok = lambda i: (i, 0)
    sub_tiles = TM_MERGE // TM_DISP
    return pl.pallas_call(
        _merge_kernel,
        grid=(t // TM_MERGE,),
        in_specs=[
            pl.BlockSpec((TM_MERGE, D_MODEL), tok),
            pl.BlockSpec((TM_MERGE, ATTN_WIDTH), tok),
            pl.BlockSpec((TM_MERGE, SSM_WIDTH), tok),
            _const_spec((1, D_MODEL)),
            _const_spec((D_MODEL, 2 * D_MODEL)),
            _const_spec((ATTN_WIDTH, D_MODEL)),
            _const_spec((SSM_WIDTH, 2 * D_MODEL)),
            _const_spec((D_MODEL, D_MODEL)),
            _const_spec((1, D_MODEL)),
            _const_spec((D_MODEL, 2 * LANES)),
            _const_spec((1, LANES)),
        ],
        out_specs=[
            pl.BlockSpec((TM_MERGE, D_MODEL), tok),
            pl.BlockSpec((TM_MERGE, XS_COLS), tok),
            pl.BlockSpec((sub_tiles, 1, LANES), lambda i: (i, 0, 0)),
        ],
        out_shape=[
            jax.ShapeDtypeStruct((t, D_MODEL), F32),
            jax.ShapeDtypeStruct((t, XS_COLS), BF16),
            jax.ShapeDtypeStruct((t // TM_DISP, 1, LANES), F32),
        ],
        compiler_params=pltpu.CompilerParams(
            dimension_semantics=("arbitrary",), vmem_limit_bytes=VMEM_LIMIT),
        name="merge",
    )(x2d, o_attn, gy, g1, w_gates, w_oa, w_glu, w_out, g2, w_router, b_r)


def _moe_layout(t):
    ntiles = t // TM_DISP
    max_rows = 2 * t + ntiles * N_EXPERTS * (ROW_ALIGN - 1)
    nsteps = -(-max_rows // TS_MOE) + N_EXPERTS
    return ntiles, nsteps


def _lane_col(arr, lane, k):
    return jnp.sum(jnp.where(lane == k, arr, 0.0), axis=-1, keepdims=True)


def _tile_positions(aux, tri_ref, upper_ref, pos_ref):
    lane = lax.broadcasted_iota(jnp.int32, aux.shape, 1)
    lane_f = lane.astype(F32)
    is1 = lane_f == _lane_col(aux, lane, 0)
    is2 = lane_f == _lane_col(aux, lane, 1)
    picked = jnp.where(is1 | is2, 1.0, 0.0)
    rank_all = jnp.dot(tri_ref[...], picked.astype(BF16), preferred_element_type=F32)
    cnt_row = jnp.sum(picked, axis=0, keepdims=True)
    pad_row = jnp.floor((cnt_row + (ROW_ALIGN - 1)) * (1.0 / ROW_ALIGN)) * ROW_ALIGN
    off = jnp.dot(jnp.broadcast_to(pad_row, (SUBLANES, LANES)).astype(BF16), upper_ref[...],
                  preferred_element_type=F32)[0:1, :]
    posmat = rank_all + off
    pos1 = jnp.sum(jnp.where(is1, posmat, 0.0), axis=-1, keepdims=True)
    pos2 = jnp.sum(jnp.where(is2, posmat, 0.0), axis=-1, keepdims=True)
    pos_ref[...] = jnp.where(lane == 0, pos1, 0.0) + jnp.where(lane == 1, pos2, 0.0)

    def digits(p):
        hi = jnp.floor(p * (1.0 / POS_RADIX))
        return hi, p - POS_RADIX * hi

    d1h, d1l = digits(pos1)
    d2h, d2l = digits(pos2)
    dig = (jnp.where(lane == 0, d1h, 0.0) + jnp.where(lane == 1, d1l, 0.0)
           + jnp.where(lane == 2, d2h, 0.0) + jnp.where(lane == 3, d2l, 0.0)).astype(BF16)
    eye8 = (lax.broadcasted_iota(jnp.int32, (SUBLANES, LANES), 0)
            == lax.broadcasted_iota(jnp.int32, (SUBLANES, LANES), 1)).astype(BF16)
    rows = lax.dot_general(eye8, dig, (((1,), (1,)), ((), ())), preferred_element_type=F32)
    return (POS_RADIX * rows[0:1, :] + rows[1:2, :], POS_RADIX * rows[2:3, :] + rows[3:4, :])


def _compact_tile(payload, p1_row, p2_row):
    sub = lax.broadcasted_iota(jnp.int32, (SORT_ROWS, TM_DISP), 0).astype(F32)
    perm = jnp.where((sub == p1_row) | (sub == p2_row), 1.0, 0.0).astype(BF16)
    return jnp.dot(perm, payload, preferred_element_type=F32).astype(BF16)


def _dispatch_kernel(dst_ref, npiece_ref, tail_ref, ntail_ref, pay_ref, tri_ref, upper_ref,
                     xs_ref, pos_ref, buf_ref, zbuf_ref, sem_ref, tsem_ref):
    i = pl.program_id(0)
    last = pl.num_programs(0) - 1
    cur = i % 2

    def piece(tile, slot, sub, p):
        return pltpu.make_async_copy(buf_ref.at[slot, sub, p],
                                     xs_ref.at[dst_ref[tile * MAX_PIECES + p]],
                                     sem_ref.at[slot, sub])

    def start_all(tile, slot, sub):
        def body(p, c):
            piece(tile, slot, sub, p).start()
            return c
        lax.fori_loop(0, npiece_ref[tile], body, 0)

    def wait_all(tile, slot, sub):
        def body(p, c):
            piece(tile, slot, sub, p).wait()
            return c
        lax.fori_loop(0, npiece_ref[tile], body, 0)

    places = []
    for sub in range(DISP_SUB):
        rs = slice(sub * TM_DISP, (sub + 1) * TM_DISP)
        places.append(_tile_positions(pay_ref[rs, D_MODEL:XS_COLS].astype(F32), tri_ref,
                                      upper_ref, pos_ref.at[rs, :]))
    for sub in range(DISP_SUB):
        rs = slice(sub * TM_DISP, (sub + 1) * TM_DISP)
        buf_ref[cur, sub] = _compact_tile(pay_ref[rs, :], *places[sub]).reshape(
            MAX_PIECES, ROW_ALIGN, XS_COLS)
    for sub in range(DISP_SUB):
        start_all(i * DISP_SUB + sub, cur, sub)

    @pl.when(i > 0)
    def _wait_prev():
        for sub in range(DISP_SUB):
            wait_all((i - 1) * DISP_SUB + sub, 1 - cur, sub)

    @pl.when(i == last)
    def _finish():
        for sub in range(DISP_SUB):
            wait_all(i * DISP_SUB + sub, cur, sub)
        zbuf_ref[...] = jnp.zeros(zbuf_ref.shape, BF16)

        def tail(p):
            return pltpu.make_async_copy(zbuf_ref, xs_ref.at[tail_ref[p]], tsem_ref.at[0])

        def start_tail(p, c):
            tail(p).start()
            return c

        def wait_tail(p, c):
            tail(p).wait()
            return c

        lax.fori_loop(0, ntail_ref[0], start_tail, 0)
        lax.fori_loop(0, ntail_ref[0], wait_tail, 0)


def _dispatch(payload, dst, npiece, tail, ntail, rows):
    t = payload.shape[0]
    tm = TM_DISP * DISP_SUB
    tok = lambda i, *_: (i, 0)
    const = lambda i, *_: (0, 0)
    tri = (jnp.arange(TM_DISP)[:, None] > jnp.arange(TM_DISP)[None, :]).astype(BF16)
    upper = (jnp.arange(LANES)[:, None] < jnp.arange(LANES)[None, :]).astype(BF16)
    grid_spec = pltpu.PrefetchScalarGridSpec(
        num_scalar_prefetch=4,
        grid=(t // tm,),
        in_specs=[
            pl.BlockSpec((tm, XS_COLS), tok),
            pl.BlockSpec((TM_DISP, TM_DISP), const),
            pl.BlockSpec((LANES, LANES), const),
        ],
        out_specs=[
            pl.BlockSpec(memory_space=pl.ANY),
            pl.BlockSpec((tm, LANES), tok),
        ],
        scratch_shapes=[
            pltpu.VMEM((2, DISP_SUB, MAX_PIECES, ROW_ALIGN, XS_COLS), BF16),
            pltpu.VMEM((ROW_ALIGN, XS_COLS), BF16),
            pltpu.SemaphoreType.DMA((2, DISP_SUB)),
            pltpu.SemaphoreType.DMA((1,)),
        ],
    )
    xs, pos = pl.pallas_call(
        _dispatch_kernel,
        grid_spec=grid_spec,
        out_shape=[
            jax.ShapeDtypeStruct((rows // ROW_ALIGN, ROW_ALIGN, XS_COLS), BF16),
            jax.ShapeDtypeStruct((t, LANES), F32),
        ],
        compiler_params=pltpu.CompilerParams(
            dimension_semantics=("arbitrary",), vmem_limit_bytes=VMEM_LIMIT),
        name="dispatch",
    )(dst // ROW_ALIGN, npiece, tail // ROW_ALIGN, ntail, payload, tri, upper)
    return xs.reshape(rows, XS_COLS), pos


def _moe_kernel(blk_ref, exp_ref, nvalid_ref, fresh_ref, xs_ref, wg_ref, wu_ref, wd_ref,
                ys_ref, wgb_ref, wub_ref, wdb_ref):
    s = pl.program_id(0)

    @pl.when(fresh_ref[s] == 1)
    def _cast_weights():
        wgb_ref[...] = wg_ref[...].astype(BF16)
        wub_ref[...] = wu_ref[...].astype(BF16)
        wdb_ref[...] = wd_ref[...].astype(BF16)

    @pl.when(s < nvalid_ref[0])
    def _compute():
        t = xs_ref[:, 0:D_MODEL]
        aux = xs_ref[:, D_MODEL:XS_COLS].astype(F32)
        lane = lax.broadcasted_iota(jnp.int32, aux.shape, 1)
        first = _lane_col(aux, lane, 0) == exp_ref[s].astype(F32)
        w = jnp.where(first, _lane_col(aux, lane, 2) + _lane_col(aux, lane, 3),
                      _lane_col(aux, lane, 4) + _lane_col(aux, lane, 5))
        hg = jnp.dot(t, wgb_ref[...], preferred_element_type=F32)
        hu = jnp.dot(t, wub_ref[...], preferred_element_type=F32)
        a = (hg * jax.nn.sigmoid(hg) * hu * w).astype(BF16)
        ys_ref[...] = jnp.dot(a, wdb_ref[...], preferred_element_type=F32).astype(BF16)

    @pl.when(s == nvalid_ref[0])
    def _spare():
        ys_ref[...] = jnp.zeros(ys_ref.shape, BF16)


def _moe(xs, blk, exp, nvalid, fresh, wg, wu, wd):
    row = lambda s, blk, exp, nv, fr: (blk[s], 0)
    wsel = lambda s, blk, exp, nv, fr: (exp[s], 0, 0)
    grid_spec = pltpu.PrefetchScalarGridSpec(
        num_scalar_prefetch=4,
        grid=(blk.shape[0],),
        in_specs=[
            pl.BlockSpec((TS_MOE, XS_COLS), row),
            pl.BlockSpec((None, D_MODEL, D_EXPERT), wsel),
            pl.BlockSpec((None, D_MODEL, D_EXPERT), wsel),
            pl.BlockSpec((None, D_EXPERT, D_MODEL), wsel),
        ],
        out_specs=pl.BlockSpec((TS_MOE, D_MODEL), row),
        scratch_shapes=[
            pltpu.VMEM((D_MODEL, D_EXPERT), BF16),
            pltpu.VMEM((D_MODEL, D_EXPERT), BF16),
            pltpu.VMEM((D_EXPERT, D_MODEL), BF16),
        ],
    )
    return pl.pallas_call(
        _moe_kernel,
        grid_spec=grid_spec,
        out_shape=jax.ShapeDtypeStruct((xs.shape[0], D_MODEL), BF16),
        compiler_params=pltpu.CompilerParams(
            dimension_semantics=("arbitrary",), vmem_limit_bytes=VMEM_LIMIT),
        name="moe",
    )(blk, exp, nvalid, fresh, xs, wg, wu, wd)


def _combine_kernel(dst_ref, npiece_ref, x2_ref, pos_ref, ys_ref, o_ref, stage_ref, sem_ref):
    i = pl.program_id(0)
    n = pl.num_programs(0)
    cur = i % 2
    tm = TM_DISP

    def piece(tile, slot, sub, p):
        row = pl.multiple_of(p * ROW_ALIGN, ROW_ALIGN)
        src = pl.multiple_of(dst_ref[tile * MAX_PIECES + p], ROW_ALIGN)
        return pltpu.make_async_copy(ys_ref.at[pl.ds(src, ROW_ALIGN)],
                                     stage_ref.at[slot, sub, pl.ds(row, ROW_ALIGN)],
                                     sem_ref.at[slot, sub])

    def start_step(step, slot):
        for sub in range(DISP_SUB):
            tile = step * DISP_SUB + sub

            def body(p, c, tile=tile, sub=sub):
                piece(tile, slot, sub, p).start()
                return c
            lax.fori_loop(0, npiece_ref[tile], body, 0)

    @pl.when(i == 0)
    def _first():
        stage_ref[...] = jnp.zeros(stage_ref.shape, BF16)
        start_step(0, 0)

    @pl.when(i + 1 < n)
    def _prefetch():
        start_step(i + 1, 1 - cur)

    for sub in range(DISP_SUB):
        tile = i * DISP_SUB + sub

        def wait_body(p, c, tile=tile, sub=sub):
            piece(tile, cur, sub, p).wait()
            return c

        lax.fori_loop(0, npiece_ref[tile], wait_body, 0)

    for sub in range(DISP_SUB):
        rs = slice(sub * tm, (sub + 1) * tm)
        pos = pos_ref[rs, :]
        lane128 = lax.broadcasted_iota(jnp.int32, pos.shape, 1)
        p1 = _lane_col(pos, lane128, 0)
        p2 = _lane_col(pos, lane128, 1)
        lane = lax.broadcasted_iota(jnp.int32, (tm, SORT_ROWS), 1).astype(F32)
        pick = jnp.where((lane == p1) | (lane == p2), 1.0, 0.0).astype(BF16)
        o_ref[rs, :] = x2_ref[rs, :] + jnp.dot(pick, stage_ref[cur, sub],
                                               preferred_element_type=F32)


def _combine(x2, pos, ys, dst, npiece):
    t = x2.shape[0]
    tm = TM_DISP * DISP_SUB
    tok = lambda i, *_: (i, 0)
    grid_spec = pltpu.PrefetchScalarGridSpec(
        num_scalar_prefetch=2,
        grid=(t // tm,),
        in_specs=[
            pl.BlockSpec((tm, D_MODEL), tok),
            pl.BlockSpec((tm, LANES), tok),
            pl.BlockSpec(memory_space=pl.ANY),
        ],
        out_specs=pl.BlockSpec((tm, D_MODEL), tok),
        scratch_shapes=[
            pltpu.VMEM((2, DISP_SUB, SORT_ROWS, D_MODEL), BF16),
            pltpu.SemaphoreType.DMA((2, DISP_SUB)),
        ],
    )
    return pl.pallas_call(
        _combine_kernel,
        grid_spec=grid_spec,
        out_shape=jax.ShapeDtypeStruct((t, D_MODEL), F32),
        compiler_params=pltpu.CompilerParams(
            dimension_semantics=("arbitrary",), vmem_limit_bytes=VMEM_LIMIT),
        name="combine",
    )(dst, npiece, x2, pos, ys)


def _sparse_moe(payload, cnt, x2, wg, wu, wd):
    t = payload.shape[0]
    ntiles, nsteps = _moe_layout(t)
    rows = (nsteps + 1) * TS_MOE
    i32 = jnp.int32
    counts = cnt[:, 0, :N_EXPERTS].astype(i32)
    padded = (counts + (ROW_ALIGN - 1)) // ROW_ALIGN * ROW_ALIGN
    total = jnp.sum(padded, axis=0)
    reg_rows = (total + (TS_MOE - 1)) // TS_MOE * TS_MOE
    region = jnp.cumsum(reg_rows) - reg_rows
    base = region[None, :] + jnp.cumsum(padded, axis=0) - padded
    np_e = padded // ROW_ALIGN
    cum = jnp.cumsum(np_e, axis=1)
    npiece = cum[:, -1].astype(i32)
    p = jnp.arange(MAX_PIECES, dtype=i32)
    owns = (p[None, :, None] >= (cum - np_e)[:, None, :]) & (p[None, :, None] < cum[:, None, :])
    piece_dst = base[:, None, :] + (p[None, :, None] - (cum - np_e)[:, None, :]) * ROW_ALIGN
    dst = jnp.sum(jnp.where(owns, piece_dst, 0), axis=2).reshape(-1).astype(i32)
    nt_e = (reg_rows - total) // ROW_ALIGN
    cumt = jnp.cumsum(nt_e)
    q = jnp.arange(N_EXPERTS * (TS_MOE // ROW_ALIGN), dtype=i32)
    owns_q = (q[:, None] >= (cumt - nt_e)[None, :]) & (q[:, None] < cumt[None, :])
    tail_dst = (region + total)[None, :] + (q[:, None] - (cumt - nt_e)[None, :]) * ROW_ALIGN
    tail = jnp.sum(jnp.where(owns_q, tail_dst, 0), axis=1).astype(i32)
    ntail = cumt[-1:].astype(i32)
    ntile_e = reg_rows // TS_MOE
    first_t = jnp.cumsum(ntile_e) - ntile_e
    nvalid = jnp.sum(ntile_e).reshape(1).astype(i32)
    s = jnp.arange(nsteps, dtype=i32)
    owns_s = (s[:, None] >= first_t[None, :]) & (s[:, None] < (first_t + ntile_e)[None, :])
    spare = s >= nvalid[0]
    fresh = jnp.any(owns_s & (s[:, None] == first_t[None, :]), axis=1).astype(i32)
    exp = jnp.where(spare, N_EXPERTS - 1, jnp.sum(
        jnp.where(owns_s, jnp.arange(N_EXPERTS, dtype=i32)[None, :], 0), axis=1)).astype(i32)
    blk = jnp.where(spare, nsteps, jnp.sum(
        jnp.where(owns_s, (region // TS_MOE - first_t)[None, :] + s[:, None], 0),
        axis=1)).astype(i32)
    xs, pos = _dispatch(payload, dst, npiece, tail, ntail, rows)
    ys = _moe(xs, blk, exp, nvalid, fresh, wg, wu, wd)
    return _combine(x2, pos, ys, dst, npiece)


def _rope_tables(positions):
    inv = ROPE_THETA ** (-jnp.arange(0, ROPE_DIM, 2, dtype=F32) / ROPE_DIM)
    ang = positions.astype(F32).reshape(-1, 1) * inv
    cos, sin = jnp.cos(ang), jnp.sin(ang)
    d = jnp.arange(LANES) % HEAD_DIM
    in_rope = d < ROPE_DIM
    place = (((d % ROPE_HALF)[None, :] == jnp.arange(ROPE_HALF)[:, None])
             & in_rope[None, :]).astype(F32)
    sign = jnp.where(d < ROPE_HALF, -1.0, 1.0)
    hp = lax.Precision.HIGHEST
    cos_t = jnp.dot(cos, place, precision=hp) + (1.0 - in_rope.astype(F32))
    sin_t = jnp.dot(sin, place * sign, precision=hp)
    return cos_t, sin_t


def _s5_params(lam_re, lam_im, log_dt, b_re, b_im, c_re, c_im, bsz):
    dt = jnp.exp(log_dt)[:, None]
    mag = jnp.exp(lam_re * dt)
    lb_re = mag * jnp.cos(lam_im * dt)
    lb_im = mag * jnp.sin(lam_im * dt)
    den = lam_re * lam_re + lam_im * lam_im
    k_re = ((lb_re - 1.0) * lam_re + lb_im * lam_im) / den
    k_im = (lb_im * lam_re - (lb_re - 1.0) * lam_im) / den
    bb_re = k_re[..., None] * b_re - k_im[..., None] * b_im
    bb_im = k_re[..., None] * b_im + k_im[..., None] * b_re
    eye = jnp.eye(SSM_GROUPS, dtype=F32)
    blk_b = lambda m: jnp.einsum('gph,gk->ghkp', m, eye).reshape(SSM_WIDTH, N_STATE)
    blk_c = lambda m: jnp.einsum('ghp,gk->gpkh', m, eye).reshape(N_STATE, SSM_WIDTH)
    bmat = jnp.concatenate([blk_b(bb_re), blk_b(bb_im)], axis=1).astype(BF16)
    cmat = jnp.concatenate([blk_c(c_re), blk_c(-c_im)], axis=0).astype(BF16)
    a_re = jnp.broadcast_to(lb_re.reshape(1, N_STATE), (bsz, N_STATE))
    a_im = jnp.broadcast_to(lb_im.reshape(1, N_STATE), (bsz, N_STATE))
    return bmat, a_re, a_im, cmat


def _time_major_perm(bsz, chunk):
    r = jnp.arange(bsz * chunk)
    src = (r % bsz) * chunk + r // bsz
    perm = (src[:, None] == jnp.arange(bsz * chunk)[None, :]).astype(BF16)
    return perm, perm.T


def kernel(x, positions, norm_mix_g, w_in, q_norm_g, k_norm_g, lambda_q1, lambda_k1, lambda_q2, lambda_k2, subln_g, w_o_attn, ssm_lambda_re, ssm_lambda_im, ssm_log_dt, ssm_b_re, ssm_b_im, ssm_c_re, ssm_c_im, ssm_d, w_glu, w_out, norm_ffn_g, w_router_group, b_router_group, w_router_expert, b_router_expert, w_expert_gate, w_expert_up, w_expert_down):
    bsz, seq, _ = x.shape
    assert bsz == SUBLANES and seq % (KEY_GROUP * TQ) == 0 and seq % SSM_CHUNK == 0
    assert norm_mix_g.shape[0] == 1
    t = bsz * seq
    x2d = x.reshape(t, D_MODEL)
    l = 0

    cos_t, sin_t = _rope_tables(positions)
    w_qkvu = w_in[l][:, :QKVU_COLS].astype(BF16)
    w_gates = w_in[l][:, QKVU_COLS:].astype(BF16)
    qg = jnp.tile(q_norm_g[l].reshape(1, HEAD_DIM), (1, 2))
    kg = jnp.tile(k_norm_g[l].reshape(1, HEAD_DIM), (1, 2))
    q, k, v, u = _in_proj(x2d, norm_mix_g[l].reshape(1, D_MODEL), w_qkvu, cos_t, sin_t, qg, kg)

    lam = (jnp.exp(jnp.sum(lambda_q1[l] * lambda_k1[l]))
           - jnp.exp(jnp.sum(lambda_q2[l] * lambda_k2[l])) + LAM_INIT).reshape(1)
    o_attn = _diff_attn(q, k, v, lam, subln_g[l].reshape(1, V_DIM), bsz, seq)

    bmat, a_re, a_im, cmat = _s5_params(
        ssm_lambda_re[l], ssm_lambda_im[l], ssm_log_dt[l], ssm_b_re[l], ssm_b_im[l],
        ssm_c_re[l], ssm_c_im[l], bsz)
    perm, perm_t = _time_major_perm(bsz, SSM_CHUNK)
    gy = _s5_scan(u.reshape(bsz, seq, SSM_WIDTH), perm, perm_t, bmat, a_re, a_im, cmat,
                  ssm_d[l].reshape(1, SSM_WIDTH)).reshape(t, SSM_WIDTH)

    w_r = jnp.concatenate(
        [w_router_expert[l].reshape(D_MODEL, N_EXPERTS), w_router_group[l],
         jnp.zeros((D_MODEL, LANES - N_EXPERTS - N_EXPERT_GROUPS), F32)], axis=1)
    b_r = jnp.concatenate(
        [b_router_expert[l].reshape(N_EXPERTS), b_router_group[l],
         jnp.zeros((LANES - N_EXPERTS - N_EXPERT_GROUPS,), F32)]).reshape(1, LANES)
    wr_hi = w_r.astype(BF16)
    wr_lo = (w_r - wr_hi.astype(F32)).astype(BF16)
    x2, payload, cnt = _merge(
        x2d, o_attn, gy, norm_mix_g[l].reshape(1, D_MODEL), w_gates,
        w_o_attn[l].astype(BF16), w_glu[l].astype(BF16), w_out[l].astype(BF16),
        norm_ffn_g[l].reshape(1, D_MODEL), jnp.concatenate([wr_hi, wr_lo], axis=1), b_r)

    wg = w_expert_gate[l].reshape(N_EXPERTS, D_MODEL, D_EXPERT)
    wu = w_expert_up[l].reshape(N_EXPERTS, D_MODEL, D_EXPERT)
    wd = w_expert_down[l].reshape(N_EXPERTS, D_EXPERT, D_MODEL)
    out = _sparse_moe(payload, cnt, x2, wg, wu, wd)
    return out.reshape(bsz, seq, D_MODEL)
```

```python
import math

import jax
import jax.numpy as jnp
from jax import lax
from jax.experimental import pallas as pl
from jax.experimental.pallas import tpu as pltpu

F32 = jnp.float32
BF16 = jnp.bfloat16

D_MODEL = 1024
N_HEADS = 4
HEAD_DIM = 64
V_DIM = 2 * HEAD_DIM
ATTN_WIDTH = N_HEADS * V_DIM
ROPE_THETA = 500000.0
ROPE_DIM = HEAD_DIM // 4
ROPE_HALF = ROPE_DIM // 2
SSM_WIDTH = D_MODEL // 2
SSM_GROUP = 16
SSM_GROUPS = SSM_WIDTH // SSM_GROUP
SSM_STATE = 64
N_STATE = SSM_GROUPS * SSM_STATE
N_EXPERT_GROUPS = 4
EXPERTS_PER_GROUP = 8
N_EXPERTS = N_EXPERT_GROUPS * EXPERTS_PER_GROUP
D_EXPERT = D_MODEL // 4
EPS = 1e-6
LAM_INIT = 0.8 - 0.6 * math.exp(-0.3 * 0)
QKVU_COLS = 4 * ATTN_WIDTH
LANES = 128
SUBLANES = 8
NEG_BIG = -1e30
LOG2_E = math.log2(math.e)

VMEM_LIMIT = 48 * 1024 * 1024

TM_IN = 1024
TQ = 512
SSM_CHUNK = 64
SCAN_LANES = 512
TM_MERGE = 1024
TM_DISP = 256
DISP_SUB = 4
TS_MOE = 1024
ROW_ALIGN = 16
XS_COLS = D_MODEL + LANES
SORT_ROWS = 2 * TM_DISP + N_EXPERTS * ROW_ALIGN
MAX_PIECES = SORT_ROWS // ROW_ALIGN
POS_RADIX = 32


def _const_spec(shape):
    return pl.BlockSpec(shape, lambda *_: (0,) * len(shape))


def _in_proj_kernel(x_ref, g_ref, w_ref, cos_ref, sin_ref, qg_ref, kg_ref,
                    q_ref, k_ref, v_ref, u_ref):
    x = x_ref[...]
    ms = jnp.mean(x * x, axis=-1, keepdims=True)
    h = (x * lax.rsqrt(ms + EPS) * g_ref[...]).astype(BF16)
    cos_t = cos_ref[...]
    sin_t = sin_ref[...]
    lane = lax.broadcasted_iota(jnp.int32, (x.shape[0], LANES), 1)
    low_half = (lane % HEAD_DIM) < ROPE_HALF
    same_comp = (lax.broadcasted_iota(jnp.int32, (LANES, LANES), 0) // HEAD_DIM
                 == lax.broadcasted_iota(jnp.int32, (LANES, LANES), 1) // HEAD_DIM
                 ).astype(BF16)

    def norm_rope(blk, gain, scale):
        ssum = jnp.dot((blk * blk).astype(BF16), same_comp, preferred_element_type=F32)
        nb = blk * lax.rsqrt(ssum * (1.0 / HEAD_DIM) + EPS) * gain
        up = pltpu.roll(nb, LANES - ROPE_HALF, axis=1)
        dn = pltpu.roll(nb, ROPE_HALF, axis=1)
        partner = jnp.where(low_half, up, dn)
        return ((nb * cos_t + partner * sin_t) * scale).astype(BF16)

    for j in range(0, N_HEADS, 2):
        sl2 = slice(j * LANES, (j + 2) * LANES)
        qb = jnp.dot(h, w_ref[:, sl2], preferred_element_type=F32)
        kb = jnp.dot(h, w_ref[:, ATTN_WIDTH + j * LANES:ATTN_WIDTH + (j + 2) * LANES],
                     preferred_element_type=F32)
        for jj in range(2):
            sl = slice((j + jj) * LANES, (j + jj + 1) * LANES)
            half = slice(jj * LANES, (jj + 1) * LANES)
            q_ref[:, sl] = norm_rope(qb[:, half], qg_ref[...], LOG2_E * HEAD_DIM ** -0.5)
            k_ref[:, sl] = norm_rope(kb[:, half], kg_ref[...], 1.0)
    v_ref[...] = jnp.dot(h, w_ref[:, 2 * ATTN_WIDTH:3 * ATTN_WIDTH],
                         preferred_element_type=F32).astype(BF16)
    u_ref[...] = jnp.dot(h, w_ref[:, 3 * ATTN_WIDTH:4 * ATTN_WIDTH],
                         preferred_element_type=F32).astype(BF16)


def _in_proj(x2d, g, w, cos_t, sin_t, qg, kg):
    t = x2d.shape[0]
    tok = lambda i: (i, 0)
    out = jax.ShapeDtypeStruct((t, ATTN_WIDTH), BF16)
    return pl.pallas_call(
        _in_proj_kernel,
        grid=(t // TM_IN,),
        in_specs=[
            pl.BlockSpec((TM_IN, D_MODEL), tok),
            _const_spec((1, D_MODEL)),
            _const_spec((D_MODEL, QKVU_COLS)),
            pl.BlockSpec((TM_IN, LANES), tok),
            pl.BlockSpec((TM_IN, LANES), tok),
            _const_spec((1, LANES)),
            _const_spec((1, LANES)),
        ],
        out_specs=[pl.BlockSpec((TM_IN, ATTN_WIDTH), tok)] * 4,
        out_shape=[out] * 4,
        compiler_params=pltpu.CompilerParams(
            dimension_semantics=("arbitrary",), vmem_limit_bytes=VMEM_LIMIT),
        name="in_proj",
    )(x2d, g, w, cos_t, sin_t, qg, kg)


def _attn_kernel(qi_ref, kp_ref, mode_ref, lam_ref, q_ref, k_ref, v_ref, sg_ref, o_ref,
                 qs_ref, m_ref, l_ref, acc_ref):
    step = pl.program_id(1)
    kp = kp_ref[step]
    mode = mode_ref[step]

    hq = TQ // 2

    @pl.when(kp == 0)
    def _init():
        for h in range(N_HEADS):
            q = q_ref[:, h * LANES:(h + 1) * LANES]
            lane = lax.broadcasted_iota(jnp.int32, q.shape, 1)
            zero = jnp.zeros_like(q)
            only0 = jnp.where(lane < HEAD_DIM, q, zero)
            only1 = jnp.where(lane < HEAD_DIM, zero, q)
            for half in range(2):
                rows = slice(half * hq, (half + 1) * hq)
                qs_ref[h, 2 * half * hq:(2 * half + 1) * hq, :] = only0[rows, :]
                qs_ref[h, (2 * half + 1) * hq:(2 * half + 2) * hq, :] = only1[rows, :]
        m_ref[...] = jnp.full(m_ref.shape, NEG_BIG, F32)
        l_ref[...] = jnp.zeros(l_ref.shape, F32)
        acc_ref[...] = jnp.zeros(acc_ref.shape, F32)

    def update(h, rows, key0, nkeys, diag_offset=None):
        hs = slice(h * LANES, (h + 1) * LANES)
        ks = slice(key0, key0 + nkeys)
        s = lax.dot_general(qs_ref[h, rows, :], k_ref[ks, hs], (((1,), (1,)), ((), ())),
                            preferred_element_type=F32)
        if diag_offset is not None:
            row = lax.broadcasted_iota(jnp.int32, s.shape, 0)
            row = row % hq + row // TQ * hq + diag_offset
            col = lax.broadcasted_iota(jnp.int32, s.shape, 1)
            s = jnp.where(col <= row, s, NEG_BIG)
        m_old = m_ref[h, rows, :]
        m_new = jnp.maximum(m_old, jnp.max(s, axis=-1, keepdims=True))
        alpha = jnp.exp2(m_old - m_new)
        p = jnp.exp2(s - jnp.concatenate([m_new] * (nkeys // LANES), axis=1))
        v_ones = jnp.concatenate([v_ref[ks, hs], jnp.ones((nkeys, LANES), BF16)], axis=1)
        pv = jnp.dot(p.astype(BF16), v_ones, preferred_element_type=F32)
        l_ref[h, rows, :] = alpha * l_ref[h, rows, :] + pv[:, LANES:2 * LANES]
        acc_ref[h, rows, :] = alpha * acc_ref[h, rows, :] + pv[:, 0:LANES]
        m_ref[h, rows, :] = m_new

    all_rows = slice(0, 2 * TQ)

    def update_diag(h, key0):
        update(h, slice(0, TQ), key0, hq, diag_offset=0)
        update(h, slice(TQ, 2 * TQ), key0, TQ, diag_offset=hq)

    def finish(h):
        o = acc_ref[h] / l_ref[h]
        d = jnp.concatenate(
            [o[2 * half * hq:(2 * half + 1) * hq, :]
             - lam_ref[0] * o[(2 * half + 1) * hq:(2 * half + 2) * hq, :]
             for half in range(2)], axis=0)
        ms = jnp.mean(d * d, axis=-1, keepdims=True)
        d = d * lax.rsqrt(ms + EPS) * sg_ref[...] * (1.0 - LAM_INIT)
        o_ref[:, h * LANES:(h + 1) * LANES] = d.astype(BF16)

    @pl.when(mode == 0)
    def _below():
        for h in range(N_HEADS):
            update(h, all_rows, 0, TQ)
            update(h, all_rows, TQ, TQ)

    @pl.when(mode == 1)
    def _below_then_diag():
        for h in range(N_HEADS):
            update(h, all_rows, 0, TQ)
            update(h, all_rows, TQ, TQ, diag_offset=0)
            finish(h)

    @pl.when(mode == 2)
    def _diag_only():
        for h in range(N_HEADS):
            update_diag(h, 0)
            finish(h)


def _diff_attn(q, k, v, lam, subln_g, bsz, seq):
    nq = seq // TQ
    steps = []
    for i in range(nq):
        for p in range(i // 2 + 1):
            mode = 0 if 2 * p + 1 < i else (1 if 2 * p + 1 == i else 2)
            steps.append((i, p, mode))
    qi = jnp.asarray([s[0] for s in steps], jnp.int32)
    kp = jnp.asarray([s[1] for s in steps], jnp.int32)
    mode = jnp.asarray([s[2] for s in steps], jnp.int32)
    q_map = lambda b, s, qi, kp, mode: (b * nq + qi[s], 0)
    k_map = lambda b, s, qi, kp, mode: (b * (nq // 2) + kp[s], 0)
    grid_spec = pltpu.PrefetchScalarGridSpec(
        num_scalar_prefetch=3,
        grid=(bsz, len(steps)),
        in_specs=[
            pl.BlockSpec(memory_space=pltpu.SMEM),
            pl.BlockSpec((TQ, ATTN_WIDTH), q_map),
            pl.BlockSpec((2 * TQ, ATTN_WIDTH), k_map),
            pl.BlockSpec((2 * TQ, ATTN_WIDTH), k_map),
            pl.BlockSpec((1, LANES), lambda b, s, qi, kp, mode: (0, 0)),
        ],
        out_specs=pl.BlockSpec((TQ, ATTN_WIDTH), q_map),
        scratch_shapes=[
            pltpu.VMEM((N_HEADS, 2 * TQ, LANES), BF16),
            pltpu.VMEM((N_HEADS, 2 * TQ, LANES), F32),
            pltpu.VMEM((N_HEADS, 2 * TQ, LANES), F32),
            pltpu.VMEM((N_HEADS, 2 * TQ, LANES), F32),
        ],
    )
    return pl.pallas_call(
        _attn_kernel,
        grid_spec=grid_spec,
        out_shape=jax.ShapeDtypeStruct((bsz * seq, ATTN_WIDTH), BF16),
        compiler_params=pltpu.CompilerParams(
            dimension_semantics=("arbitrary", "arbitrary"),
            vmem_limit_bytes=VMEM_LIMIT),
        name="diff_attn",
    )(qi, kp, mode, lam, q, k, v, subln_g)


def _gelu_tanh(x):
    c = math.sqrt(2.0 / math.pi)
    return 0.5 * x * (1.0 + jnp.tanh(c * (x + 0.044715 * (x * x * x))))


def _s5_kernel(u_ref, p_ref, pt_ref, b_ref, are_ref, aim_ref, c_ref, d_ref, o_ref,
               st_ref, state_ref):
    bsz, chunk, width = u_ref.shape
    rows = bsz * chunk

    @pl.when(pl.program_id(0) == 0)
    def _init():
        state_ref[...] = jnp.zeros(state_ref.shape, F32)

    u_bt = u_ref[...].reshape(rows, width)
    u_tm = jnp.dot(p_ref[...], u_bt, preferred_element_type=F32)
    u_tm_b = u_tm.astype(BF16)

    slab = SCAN_LANES // SSM_STATE * SSM_GROUP
    g_parts = []
    for ch in range(N_STATE // SCAN_LANES):
        re = slice(ch * SCAN_LANES, (ch + 1) * SCAN_LANES)
        im = slice(N_STATE + ch * SCAN_LANES, N_STATE + (ch + 1) * SCAN_LANES)
        cs = slice(ch * slab, (ch + 1) * slab)
        st_ref[:, re] = jnp.dot(u_tm_b[:, cs], b_ref[cs, re], preferred_element_type=F32)
        st_ref[:, im] = jnp.dot(u_tm_b[:, cs], b_ref[cs, im], preferred_element_type=F32)
        a_re = are_ref[:, re]
        a_im = aim_ref[:, re]

        def body(t, carry, re=re, im=im, a_re=a_re, a_im=a_im):
            s_re, s_im = carry
            r0 = pl.multiple_of(t * SUBLANES, SUBLANES)
            n_re = a_re * s_re - a_im * s_im + st_ref[pl.ds(r0, SUBLANES), re]
            n_im = a_re * s_im + a_im * s_re + st_ref[pl.ds(r0, SUBLANES), im]
            st_ref[pl.ds(r0, SUBLANES), re] = n_re
            st_ref[pl.ds(r0, SUBLANES), im] = n_im
            return n_re, n_im

        s_re, s_im = lax.fori_loop(0, chunk, body, (state_ref[:, re], state_ref[:, im]),
                                   unroll=True)
        state_ref[:, re] = s_re
        state_ref[:, im] = s_im
        y = (jnp.dot(st_ref[:, re].astype(BF16), c_ref[re, cs], preferred_element_type=F32)
             + jnp.dot(st_ref[:, im].astype(BF16), c_ref[im, cs], preferred_element_type=F32))
        y = y + d_ref[:, cs] * u_tm[:, cs]
        g_parts.append(_gelu_tanh(y).astype(BF16))

    g_tm = jnp.concatenate(g_parts, axis=1)
    g_bt = jnp.dot(pt_ref[...], g_tm, preferred_element_type=F32)
    o_ref[...] = g_bt.astype(BF16).reshape(bsz, chunk, width)


def _s5_scan(u3, perm, perm_t, bmat, a_re, a_im, cmat, dvec):
    bsz, seq, width = u3.shape
    rows = bsz * SSM_CHUNK
    blk = pl.BlockSpec((bsz, SSM_CHUNK, width), lambda c: (0, c, 0))
    return pl.pallas_call(
        _s5_kernel,
        grid=(seq // SSM_CHUNK,),
        in_specs=[
            blk,
            _const_spec((rows, rows)),
            _const_spec((rows, rows)),
            _const_spec((width, 2 * N_STATE)),
            _const_spec((bsz, N_STATE)),
            _const_spec((bsz, N_STATE)),
            _const_spec((2 * N_STATE, width)),
            _const_spec((1, width)),
        ],
        out_specs=blk,
        out_shape=jax.ShapeDtypeStruct(u3.shape, BF16),
        scratch_shapes=[
            pltpu.VMEM((rows, 2 * N_STATE), F32),
            pltpu.VMEM((bsz, 2 * N_STATE), F32),
        ],
        compiler_params=pltpu.CompilerParams(
            dimension_semantics=("arbitrary",), vmem_limit_bytes=VMEM_LIMIT),
        name="s5_scan",
    )(u3, perm, perm_t, bmat, a_re, a_im, cmat, dvec)


def _merge_kernel(x_ref, oa_ref, gy_ref, g1_ref, wg_ref, woa_ref, wglu_ref, wout_ref,
                  g2_ref, wr_ref, br_ref, x2_ref, pay_ref, cnt_ref):
    n_sub = TM_MERGE // TM_DISP
    logits = [None] * n_sub

    def rows_of(k):
        return slice(k * TM_DISP, (k + 1) * TM_DISP)

    def tail(k):
        _route_rows(logits[k], pay_ref.at[rows_of(k), :], cnt_ref.at[k])

    for k in range(n_sub):
        rs = rows_of(k)
        logits[k] = _merge_rows(x_ref[rs, :], oa_ref[rs, :], gy_ref[rs, :], g1_ref, wg_ref,
                                woa_ref, wglu_ref, wout_ref, g2_ref, wr_ref, br_ref,
                                x2_ref.at[rs, :], pay_ref.at[rs, :])
        if k > 0:
            tail(k - 1)
    tail(n_sub - 1)


def _merge_rows(x, o_in, gy, g1_ref, wg_ref, woa_ref, wglu_ref, wout_ref, g2_ref, wr_ref,
                br_ref, x2_ref, pay_ref):
    ms = jnp.mean(x * x, axis=-1, keepdims=True)
    h = (x * lax.rsqrt(ms + EPS) * g1_ref[...]).astype(BF16)
    o_a = jnp.dot(o_in, woa_ref[...], preferred_element_type=F32)
    z_lin = jnp.dot(gy, wglu_ref[:, 0:D_MODEL], preferred_element_type=F32)
    z_gate = jnp.dot(gy, wglu_ref[:, D_MODEL:2 * D_MODEL], preferred_element_type=F32)
    o_s = z_lin * jax.nn.sigmoid(z_gate)
    gate_a = jax.nn.sigmoid(jnp.dot(h, wg_ref[:, 0:D_MODEL], preferred_element_type=F32))
    merged = gate_a * o_a
    gate_s = jax.nn.sigmoid(
        jnp.dot(h, wg_ref[:, D_MODEL:2 * D_MODEL], preferred_element_type=F32))
    merged = merged + gate_s * o_s
    x2 = x + jnp.dot(merged.astype(BF16), wout_ref[...], preferred_element_type=F32)
    x2_ref[...] = x2

    ms2 = jnp.mean(x2 * x2, axis=-1, keepdims=True)
    h2 = x2 * lax.rsqrt(ms2 + EPS) * g2_ref[...]
    h2_hi = h2.astype(BF16)
    h2_lo = (h2 - h2_hi.astype(F32)).astype(BF16)
    pay_ref[:, 0:D_MODEL] = h2_hi

    both = jnp.dot(h2_hi, wr_ref[...], preferred_element_type=F32)
    return (both[:, 0:LANES] + both[:, LANES:2 * LANES]
            + jnp.dot(h2_lo, wr_ref[:, 0:LANES], preferred_element_type=F32)
            + br_ref[...])


def _route_rows(logits, pay_ref, cnt_ref):
    lane = lax.broadcasted_iota(jnp.int32, logits.shape, 1).astype(F32)
    is_grp = (lane >= N_EXPERTS) & (lane < N_EXPERTS + N_EXPERT_GROUPS)
    gl = jnp.where(is_grp, logits, NEG_BIG)
    gmax = jnp.max(gl, axis=-1, keepdims=True)
    gsum = jnp.sum(jnp.where(is_grp, jnp.exp(gl - gmax), 0.0), axis=-1, keepdims=True)
    p_grp = 1.0 / gsum
    big = float(4 * LANES)
    grp = jnp.min(jnp.where(is_grp & (gl == gmax), lane, big), axis=-1,
                  keepdims=True) - N_EXPERTS
    sel = logits
    for g in range(1, N_EXPERT_GROUPS):
        rolled = pltpu.roll(logits, LANES - g * EXPERTS_PER_GROUP, axis=1)
        sel = jnp.where(grp == g, rolled, sel)
    in_grp = lane < EXPERTS_PER_GROUP
    es = jnp.where(in_grp, sel, NEG_BIG)
    top1 = jnp.max(es, axis=-1, keepdims=True)
    i1 = jnp.min(jnp.where(in_grp & (es == top1), lane, big), axis=-1, keepdims=True)
    es2 = jnp.where(lane == i1, NEG_BIG, es)
    top2 = jnp.max(es2, axis=-1, keepdims=True)
    i2 = jnp.min(jnp.where(in_grp & (lane != i1) & (es2 == top2), lane, big), axis=-1,
                 keepdims=True)
    e2 = jnp.exp(top2 - top1)
    w1 = p_grp / (1.0 + e2)
    w2 = p_grp * e2 / (1.0 + e2)
    e1 = grp * EXPERTS_PER_GROUP + i1
    e2x = grp * EXPERTS_PER_GROUP + i2

    def hi_lo(w):
        hi = w.astype(BF16).astype(F32)
        return hi, (w - hi).astype(BF16).astype(F32)

    w1h, w1l = hi_lo(w1)
    w2h, w2l = hi_lo(w2)
    pay_ref[:, D_MODEL:D_MODEL + LANES] = (
        jnp.where(lane == 0.0, e1, 0.0) + jnp.where(lane == 1.0, e2x, 0.0)
        + jnp.where(lane == 2.0, w1h, 0.0) + jnp.where(lane == 3.0, w1l, 0.0)
        + jnp.where(lane == 4.0, w2h, 0.0) + jnp.where(lane == 5.0, w2l, 0.0)).astype(BF16)

    picked = jnp.where((lane == e1) | (lane == e2x), 1.0, 0.0)
    cnt_ref[...] = jnp.sum(picked, axis=0, keepdims=True)


def _merge(x2d, o_attn, gy, g1, w_gates, w_oa, w_glu, w_out, g2, w_router, b_r):
    t = x2d.shape[0]
    tok = lambda i: (i, 0)
    sub_tiles = TM_MERGE // TM_DISP
    return pl.pallas_call(
        _merge_kernel,
        grid=(t // TM_MERGE,),
        in_specs=[
            pl.BlockSpec((TM_MERGE, D_MODEL), tok),
            pl.BlockSpec((TM_MERGE, ATTN_WIDTH), tok),
            pl.BlockSpec((TM_MERGE, SSM_WIDTH), tok),
            _const_spec((1, D_MODEL)),
            _const_spec((D_MODEL, 2 * D_MODEL)),
            _const_spec((ATTN_WIDTH, D_MODEL)),
            _const_spec((SSM_WIDTH, 2 * D_MODEL)),
            _const_spec((D_MODEL, D_MODEL)),
            _const_spec((1, D_MODEL)),
            _const_spec((D_MODEL, 2 * LANES)),
            _const_spec((1, LANES)),
        ],
        out_specs=[
            pl.BlockSpec((TM_MERGE, D_MODEL), tok),
            pl.BlockSpec((TM_MERGE, XS_COLS), tok),
            pl.BlockSpec((sub_tiles, 1, LANES), lambda i: (i, 0, 0)),
        ],
        out_shape=[
            jax.ShapeDtypeStruct((t, D_MODEL), F32),
            jax.ShapeDtypeStruct((t, XS_COLS), BF16),
            jax.ShapeDtypeStruct((t // TM_DISP, 1, LANES), F32),
        ],
        compiler_params=pltpu.CompilerParams(
            dimension_semantics=("arbitrary",), vmem_limit_bytes=VMEM_LIMIT),
        name="merge",
    )(x2d, o_attn, gy, g1, w_gates, w_oa, w_glu, w_out, g2, w_router, b_r)


def _moe_layout(t):
    ntiles = t // TM_DISP
    max_rows = 2 * t + ntiles * N_EXPERTS * (ROW_ALIGN - 1)
    nsteps = -(-max_rows // TS_MOE) + N_EXPERTS
    return ntiles, nsteps


def _lane_col(arr, lane, k):
    return jnp.sum(jnp.where(lane == k, arr, 0.0), axis=-1, keepdims=True)


def _tile_positions(aux, tri_ref, upper_ref, pos_ref):
    lane = lax.broadcasted_iota(jnp.int32, aux.shape, 1)
    lane_f = lane.astype(F32)
    is1 = lane_f == _lane_col(aux, lane, 0)
    is2 = lane_f == _lane_col(aux, lane, 1)
    picked = jnp.where(is1 | is2, 1.0, 0.0)
    rank_all = jnp.dot(tri_ref[...], picked.astype(BF16), preferred_element_type=F32)
    cnt_row = jnp.sum(picked, axis=0, keepdims=True)
    pad_row = jnp.floor((cnt_row + (ROW_ALIGN - 1)) * (1.0 / ROW_ALIGN)) * ROW_ALIGN
    off = jnp.dot(jnp.broadcast_to(pad_row, (SUBLANES, LANES)).astype(BF16), upper_ref[...],
                  preferred_element_type=F32)[0:1, :]
    posmat = rank_all + off
    pos1 = jnp.sum(jnp.where(is1, posmat, 0.0), axis=-1, keepdims=True)
    pos2 = jnp.sum(jnp.where(is2, posmat, 0.0), axis=-1, keepdims=True)
    pos_ref[...] = jnp.where(lane == 0, pos1, 0.0) + jnp.where(lane == 1, pos2, 0.0)

    def digits(p):
        hi = jnp.floor(p * (1.0 / POS_RADIX))
        return hi, p - POS_RADIX * hi

    d1h, d1l = digits(pos1)
    d2h, d2l = digits(pos2)
    dig = (jnp.where(lane == 0, d1h, 0.0) + jnp.where(lane == 1, d1l, 0.0)
           + jnp.where(lane == 2, d2h, 0.0) + jnp.where(lane == 3, d2l, 0.0)).astype(BF16)
    eye8 = (lax.broadcasted_iota(jnp.int32, (SUBLANES, LANES), 0)
            == lax.broadcasted_iota(jnp.int32, (SUBLANES, LANES), 1)).astype(BF16)
    rows = lax.dot_general(eye8, dig, (((1,), (1,)), ((), ())), preferred_element_type=F32)
    return (POS_RADIX * rows[0:1, :] + rows[1:2, :], POS_RADIX * rows[2:3, :] + rows[3:4, :])


def _compact_tile(payload, p1_row, p2_row):
    sub = lax.broadcasted_iota(jnp.int32, (SORT_ROWS, TM_DISP), 0).astype(F32)
    perm = jnp.where((sub == p1_row) | (sub == p2_row), 1.0, 0.0).astype(BF16)
    return jnp.dot(perm, payload, preferred_element_type=F32).astype(BF16)


def _dispatch_kernel(dst_ref, npiece_ref, tail_ref, ntail_ref, pay_ref, tri_ref, upper_ref,
                     xs_ref, pos_ref, buf_ref, zbuf_ref, sem_ref, tsem_ref):
    i = pl.program_id(0)
    last = pl.num_programs(0) - 1
    cur = i % 2

    def piece(tile, slot, sub, p):
        return pltpu.make_async_copy(buf_ref.at[slot, sub, p],
                                     xs_ref.at[dst_ref[tile * MAX_PIECES + p]],
                                     sem_ref.at[slot, sub])

    def start_all(tile, slot, sub):
        def body(p, c):
            piece(tile, slot, sub, p).start()
            return c
        lax.fori_loop(0, npiece_ref[tile], body, 0)

    def wait_all(tile, slot, sub):
        def body(p, c):
            piece(tile, slot, sub, p).wait()
            return c
        lax.fori_loop(0, npiece_ref[tile], body, 0)

    places = []
    for sub in range(DISP_SUB):
        rs = slice(sub * TM_DISP, (sub + 1) * TM_DISP)
        places.append(_tile_positions(pay_ref[rs, D_MODEL:XS_COLS].astype(F32), tri_ref,
                                      upper_ref, pos_ref.at[rs, :]))
    for sub in range(DISP_SUB):
        rs = slice(sub * TM_DISP, (sub + 1) * TM_DISP)
        buf_ref[cur, sub] = _compact_tile(pay_ref[rs, :], *places[sub]).reshape(
            MAX_PIECES, ROW_ALIGN, XS_COLS)
    for sub in range(DISP_SUB):
        start_all(i * DISP_SUB + sub, cur, sub)

    @pl.when(i > 0)
    def _wait_prev():
        for sub in range(DISP_SUB):
            wait_all((i - 1) * DISP_SUB + sub, 1 - cur, sub)

    @pl.when(i == last)
    def _finish():
        for sub in range(DISP_SUB):
            wait_all(i * DISP_SUB + sub, cur, sub)
        zbuf_ref[...] = jnp.zeros(zbuf_ref.shape, BF16)

        def tail(p):
            return pltpu.make_async_copy(zbuf_ref, xs_ref.at[tail_ref[p]], tsem_ref.at[0])

        def start_tail(p, c):
            tail(p).start()
            return c

        def wait_tail(p, c):
            tail(p).wait()
            return c

        lax.fori_loop(0, ntail_ref[0], start_tail, 0)
        lax.fori_loop(0, ntail_ref[0], wait_tail, 0)


def _dispatch(payload, dst, npiece, tail, ntail, rows):
    t = payload.shape[0]
    tm = TM_DISP * DISP_SUB
    tok = lambda i, *_: (i, 0)
    const = lambda i, *_: (0, 0)
    tri = (jnp.arange(TM_DISP)[:, None] > jnp.arange(TM_DISP)[None, :]).astype(BF16)
    upper = (jnp.arange(LANES)[:, None] < jnp.arange(LANES)[None, :]).astype(BF16)
    grid_spec = pltpu.PrefetchScalarGridSpec(
        num_scalar_prefetch=4,
        grid=(t // tm,),
        in_specs=[
            pl.BlockSpec((tm, XS_COLS), tok),
            pl.BlockSpec((TM_DISP, TM_DISP), const),
            pl.BlockSpec((LANES, LANES), const),
        ],
        out_specs=[
            pl.BlockSpec(memory_space=pl.ANY),
            pl.BlockSpec((tm, LANES), tok),
        ],
        scratch_shapes=[
            pltpu.VMEM((2, DISP_SUB, MAX_PIECES, ROW_ALIGN, XS_COLS), BF16),
            pltpu.VMEM((ROW_ALIGN, XS_COLS), BF16),
            pltpu.SemaphoreType.DMA((2, DISP_SUB)),
            pltpu.SemaphoreType.DMA((1,)),
        ],
    )
    xs, pos = pl.pallas_call(
        _dispatch_kernel,
        grid_spec=grid_spec,
        out_shape=[
            jax.ShapeDtypeStruct((rows // ROW_ALIGN, ROW_ALIGN, XS_COLS), BF16),
            jax.ShapeDtypeStruct((t, LANES), F32),
        ],
        compiler_params=pltpu.CompilerParams(
            dimension_semantics=("arbitrary",), vmem_limit_bytes=VMEM_LIMIT),
        name="dispatch",
    )(dst // ROW_ALIGN, npiece, tail // ROW_ALIGN, ntail, payload, tri, upper)
    return xs.reshape(rows, XS_COLS), pos


def _moe_kernel(blk_ref, exp_ref, nvalid_ref, fresh_ref, xs_ref, wg_ref, wu_ref, wd_ref,
                ys_ref, wgb_ref, wub_ref, wdb_ref):
    s = pl.program_id(0)

    @pl.when(fresh_ref[s] == 1)
    def _cast_weights():
        wgb_ref[...] = wg_ref[...].astype(BF16)
        wub_ref[...] = wu_ref[...].astype(BF16)
        wdb_ref[...] = wd_ref[...].astype(BF16)

    @pl.when(s < nvalid_ref[0])
    def _compute():
        t = xs_ref[:, 0:D_MODEL]
        aux = xs_ref[:, D_MODEL:XS_COLS].astype(F32)
        lane = lax.broadcasted_iota(jnp.int32, aux.shape, 1)
        first = _lane_col(aux, lane, 0) == exp_ref[s].astype(F32)
        w = jnp.where(first, _lane_col(aux, lane, 2) + _lane_col(aux, lane, 3),
                      _lane_col(aux, lane, 4) + _lane_col(aux, lane, 5))
        hg = jnp.dot(t, wgb_ref[...], preferred_element_type=F32)
        hu = jnp.dot(t, wub_ref[...], preferred_element_type=F32)
        a = (hg * jax.nn.sigmoid(hg) * hu * w).astype(BF16)
        ys_ref[...] = jnp.dot(a, wdb_ref[...], preferred_element_type=F32).astype(BF16)

    @pl.when(s == nvalid_ref[0])
    def _spare():
        ys_ref[...] = jnp.zeros(ys_ref.shape, BF16)


def _moe(xs, blk, exp, nvalid, fresh, wg, wu, wd):
    row = lambda s, blk, exp, nv, fr: (blk[s], 0)
    wsel = lambda s, blk, exp, nv, fr: (exp[s], 0, 0)
    grid_spec = pltpu.PrefetchScalarGridSpec(
        num_scalar_prefetch=4,
        grid=(blk.shape[0],),
        in_specs=[
            pl.BlockSpec((TS_MOE, XS_COLS), row),
            pl.BlockSpec((None, D_MODEL, D_EXPERT), wsel),
            pl.BlockSpec((None, D_MODEL, D_EXPERT), wsel),
            pl.BlockSpec((None, D_EXPERT, D_MODEL), wsel),
        ],
        out_specs=pl.BlockSpec((TS_MOE, D_MODEL), row),
        scratch_shapes=[
            pltpu.VMEM((D_MODEL, D_EXPERT), BF16),
            pltpu.VMEM((D_MODEL, D_EXPERT), BF16),
            pltpu.VMEM((D_EXPERT, D_MODEL), BF16),
        ],
    )
    return pl.pallas_call(
        _moe_kernel,
        grid_spec=grid_spec,
        out_shape=jax.ShapeDtypeStruct((xs.shape[0], D_MODEL), BF16),
        compiler_params=pltpu.CompilerParams(
            dimension_semantics=("arbitrary",), vmem_limit_bytes=VMEM_LIMIT),
        name="moe",
    )(blk, exp, nvalid, fresh, xs, wg, wu, wd)


def _combine_kernel(dst_ref, npiece_ref, x2_ref, pos_ref, ys_ref, o_ref, stage_ref, sem_ref):
    i = pl.program_id(0)
    n = pl.num_programs(0)
    cur = i % 2
    tm = TM_DISP

    def piece(tile, slot, sub, p):
        row = pl.multiple_of(p * ROW_ALIGN, ROW_ALIGN)
        src = pl.multiple_of(dst_ref[tile * MAX_PIECES + p], ROW_ALIGN)
        return pltpu.make_async_copy(ys_ref.at[pl.ds(src, ROW_ALIGN)],
                                     stage_ref.at[slot, sub, pl.ds(row, ROW_ALIGN)],
                                     sem_ref.at[slot, sub])

    def start_step(step, slot):
        for sub in range(DISP_SUB):
            tile = step * DISP_SUB + sub

            def body(p, c, tile=tile, sub=sub):
                piece(tile, slot, sub, p).start()
                return c
            lax.fori_loop(0, npiece_ref[tile], body, 0)

    @pl.when(i == 0)
    def _first():
        stage_ref[...] = jnp.zeros(stage_ref.shape, BF16)
        start_step(0, 0)

    @pl.when(i + 1 < n)
    def _prefetch():
        start_step(i + 1, 1 - cur)

    for sub in range(DISP_SUB):
        tile = i * DISP_SUB + sub

        def wait_body(p, c, tile=tile, sub=sub):
            piece(tile, cur, sub, p).wait()
            return c

        lax.fori_loop(0, npiece_ref[tile], wait_body, 0)

    for sub in range(DISP_SUB):
        rs = slice(sub * tm, (sub + 1) * tm)
        pos = pos_ref[rs, :]
        lane128 = lax.broadcasted_iota(jnp.int32, pos.shape, 1)
        p1 = _lane_col(pos, lane128, 0)
        p2 = _lane_col(pos, lane128, 1)
        lane = lax.broadcasted_iota(jnp.int32, (tm, SORT_ROWS), 1).astype(F32)
        pick = jnp.where((lane == p1) | (lane == p2), 1.0, 0.0).astype(BF16)
        o_ref[rs, :] = x2_ref[rs, :] + jnp.dot(pick, stage_ref[cur, sub],
                                               preferred_element_type=F32)


def _combine(x2, pos, ys, dst, npiece):
    t = x2.shape[0]
    tm = TM_DISP * DISP_SUB
    tok = lambda i, *_: (i, 0)
    grid_spec = pltpu.PrefetchScalarGridSpec(
        num_scalar_prefetch=2,
        grid=(t // tm,),
        in_specs=[
            pl.BlockSpec((tm, D_MODEL), tok),
            pl.BlockSpec((tm, LANES), tok),
            pl.BlockSpec(memory_space=pl.ANY),
        ],
        out_specs=pl.BlockSpec((tm, D_MODEL), tok),
        scratch_shapes=[
            pltpu.VMEM((2, DISP_SUB, SORT_ROWS, D_MODEL), BF16),
            pltpu.SemaphoreType.DMA((2, DISP_SUB)),
        ],
    )
    return pl.pallas_call(
        _combine_kernel,
        grid_spec=grid_spec,
        out_shape=jax.ShapeDtypeStruct((t, D_MODEL), F32),
        compiler_params=pltpu.CompilerParams(
            dimension_semantics=("arbitrary",), vmem_limit_bytes=VMEM_LIMIT),
        name="combine",
    )(dst, npiece, x2, pos, ys)


def _sparse_moe(payload, cnt, x2, wg, wu, wd):
    t = payload.shape[0]
    ntiles, nsteps = _moe_layout(t)
    rows = (nsteps + 1) * TS_MOE
    i32 = jnp.int32
    counts = cnt[:, 0, :N_EXPERTS].astype(i32)
    padded = (counts + (ROW_ALIGN - 1)) // ROW_ALIGN * ROW_ALIGN
    total = jnp.sum(padded, axis=0)
    reg_rows = (total + (TS_MOE - 1)) // TS_MOE * TS_MOE
    region = jnp.cumsum(reg_rows) - reg_rows
    base = region[None, :] + jnp.cumsum(padded, axis=0) - padded
    np_e = padded // ROW_ALIGN
    cum = jnp.cumsum(np_e, axis=1)
    npiece = cum[:, -1].astype(i32)
    p = jnp.arange(MAX_PIECES, dtype=i32)
    owns = (p[None, :, None] >= (cum - np_e)[:, None, :]) & (p[None, :, None] < cum[:, None, :])
    piece_dst = base[:, None, :] + (p[None, :, None] - (cum - np_e)[:, None, :]) * ROW_ALIGN
    dst = jnp.sum(jnp.where(owns, piece_dst, 0), axis=2).reshape(-1).astype(i32)
    nt_e = (reg_rows - total) // ROW_ALIGN
    cumt = jnp.cumsum(nt_e)
    q = jnp.arange(N_EXPERTS * (TS_MOE // ROW_ALIGN), dtype=i32)
    owns_q = (q[:, None] >= (cumt - nt_e)[None, :]) & (q[:, None] < cumt[None, :])
    tail_dst = (region + total)[None, :] + (q[:, None] - (cumt - nt_e)[None, :]) * ROW_ALIGN
    tail = jnp.sum(jnp.where(owns_q, tail_dst, 0), axis=1).astype(i32)
    ntail = cumt[-1:].astype(i32)
    ntile_e = reg_rows // TS_MOE
    first_t = jnp.cumsum(ntile_e) - ntile_e
    nvalid = jnp.sum(ntile_e).reshape(1).astype(i32)
    s = jnp.arange(nsteps, dtype=i32)
    owns_s = (s[:, None] >= first_t[None, :]) & (s[:, None] < (first_t + ntile_e)[None, :])
    spare = s >= nvalid[0]
    fresh = jnp.any(owns_s & (s[:, None] == first_t[None, :]), axis=1).astype(i32)
    exp = jnp.where(spare, N_EXPERTS - 1, jnp.sum(
        jnp.where(owns_s, jnp.arange(N_EXPERTS, dtype=i32)[None, :], 0), axis=1)).astype(i32)
    blk = jnp.where(spare, nsteps, jnp.sum(
        jnp.where(owns_s, (region // TS_MOE - first_t)[None, :] + s[:, None], 0),
        axis=1)).astype(i32)
    xs, pos = _dispatch(payload, dst, npiece, tail, ntail, rows)
    ys = _moe(xs, blk, exp, nvalid, fresh, wg, wu, wd)
    return _combine(x2, pos, ys, dst, npiece)


def _rope_tables(positions):
    inv = ROPE_THETA ** (-jnp.arange(0, ROPE_DIM, 2, dtype=F32) / ROPE_DIM)
    ang = positions.astype(F32).reshape(-1, 1) * inv
    cos, sin = jnp.cos(ang), jnp.sin(ang)
    d = jnp.arange(LANES) % HEAD_DIM
    in_rope = d < ROPE_DIM
    place = (((d % ROPE_HALF)[None, :] == jnp.arange(ROPE_HALF)[:, None])
             & in_rope[None, :]).astype(F32)
    sign = jnp.where(d < ROPE_HALF, -1.0, 1.0)
    hp = lax.Precision.HIGHEST
    cos_t = jnp.dot(cos, place, precision=hp) + (1.0 - in_rope.astype(F32))
    sin_t = jnp.dot(sin, place * sign, precision=hp)
    return cos_t, sin_t


def _s5_params(lam_re, lam_im, log_dt, b_re, b_im, c_re, c_im, bsz):
    dt = jnp.exp(log_dt)[:, None]
    mag = jnp.exp(lam_re * dt)
    lb_re = mag * jnp.cos(lam_im * dt)
    lb_im = mag * jnp.sin(lam_im * dt)
    den = lam_re * lam_re + lam_im * lam_im
    k_re = ((lb_re - 1.0) * lam_re + lb_im * lam_im) / den
    k_im = (lb_im * lam_re - (lb_re - 1.0) * lam_im) / den
    bb_re = k_re[..., None] * b_re - k_im[..., None] * b_im
    bb_im = k_re[..., None] * b_im + k_im[..., None] * b_re
    eye = jnp.eye(SSM_GROUPS, dtype=F32)
    blk_b = lambda m: jnp.einsum('gph,gk->ghkp', m, eye).reshape(SSM_WIDTH, N_STATE)
    blk_c = lambda m: jnp.einsum('ghp,gk->gpkh', m, eye).reshape(N_STATE, SSM_WIDTH)
    bmat = jnp.concatenate([blk_b(bb_re), blk_b(bb_im)], axis=1).astype(BF16)
    cmat = jnp.concatenate([blk_c(c_re), blk_c(-c_im)], axis=0).astype(BF16)
    a_re = jnp.broadcast_to(lb_re.reshape(1, N_STATE), (bsz, N_STATE))
    a_im = jnp.broadcast_to(lb_im.reshape(1, N_STATE), (bsz, N_STATE))
    return bmat, a_re, a_im, cmat


def _time_major_perm(bsz, chunk):
    r = jnp.arange(bsz * chunk)
    src = (r % bsz) * chunk + r // bsz
    perm = (src[:, None] == jnp.arange(bsz * chunk)[None, :]).astype(BF16)
    return perm, perm.T


def kernel(x, positions, norm_mix_g, w_in, q_norm_g, k_norm_g, lambda_q1, lambda_k1, lambda_q2, lambda_k2, subln_g, w_o_attn, ssm_lambda_re, ssm_lambda_im, ssm_log_dt, ssm_b_re, ssm_b_im, ssm_c_re, ssm_c_im, ssm_d, w_glu, w_out, norm_ffn_g, w_router_group, b_router_group, w_router_expert, b_router_expert, w_expert_gate, w_expert_up, w_expert_down):
    bsz, seq, _ = x.shape
    assert bsz == SUBLANES and seq % (2 * TQ) == 0 and seq % SSM_CHUNK == 0
    assert norm_mix_g.shape[0] == 1
    t = bsz * seq
    x2d = x.reshape(t, D_MODEL)
    l = 0

    cos_t, sin_t = _rope_tables(positions)
    w_qkvu = w_in[l][:, :QKVU_COLS].astype(BF16)
    w_gates = w_in[l][:, QKVU_COLS:].astype(BF16)
    qg = jnp.tile(q_norm_g[l].reshape(1, HEAD_DIM), (1, 2))
    kg = jnp.tile(k_norm_g[l].reshape(1, HEAD_DIM), (1, 2))
    q, k, v, u = _in_proj(x2d, norm_mix_g[l].reshape(1, D_MODEL), w_qkvu, cos_t, sin_t, qg, kg)

    lam = (jnp.exp(jnp.sum(lambda_q1[l] * lambda_k1[l]))
           - jnp.exp(jnp.sum(lambda_q2[l] * lambda_k2[l])) + LAM_INIT).reshape(1)
    o_attn = _diff_attn(q, k, v, lam, subln_g[l].reshape(1, V_DIM), bsz, seq)

    bmat, a_re, a_im, cmat = _s5_params(
        ssm_lambda_re[l], ssm_lambda_im[l], ssm_log_dt[l], ssm_b_re[l], ssm_b_im[l],
        ssm_c_re[l], ssm_c_im[l], bsz)
    perm, perm_t = _time_major_perm(bsz, SSM_CHUNK)
    gy = _s5_scan(u.reshape(bsz, seq, SSM_WIDTH), perm, perm_t, bmat, a_re, a_im, cmat,
                  ssm_d[l].reshape(1, SSM_WIDTH)).reshape(t, SSM_WIDTH)

    w_r = jnp.concatenate(
        [w_router_expert[l].reshape(D_MODEL, N_EXPERTS), w_router_group[l],
         jnp.zeros((D_MODEL, LANES - N_EXPERTS - N_EXPERT_GROUPS), F32)], axis=1)
    b_r = jnp.concatenate(
        [b_router_expert[l].reshape(N_EXPERTS), b_router_group[l],
         jnp.zeros((LANES - N_EXPERTS - N_EXPERT_GROUPS,), F32)]).reshape(1, LANES)
    wr_hi = w_r.astype(BF16)
    wr_lo = (w_r - wr_hi.astype(F32)).astype(BF16)
    x2, payload, cnt = _merge(
        x2d, o_attn, gy, norm_mix_g[l].reshape(1, D_MODEL), w_gates,
        w_o_attn[l].astype(BF16), w_glu[l].astype(BF16), w_out[l].astype(BF16),
        norm_ffn_g[l].reshape(1, D_MODEL), jnp.concatenate([wr_hi, wr_lo], axis=1), b_r)

    wg = w_expert_gate[l].reshape(N_EXPERTS, D_MODEL, D_EXPERT)
    wu = w_expert_up[l].reshape(N_EXPERTS, D_MODEL, D_EXPERT)
    wd = w_expert_down[l].reshape(N_EXPERTS, D_EXPERT, D_MODEL)
    out = _sparse_moe(payload, cnt, x2, wg, wu, wd)
    return out.reshape(bsz, seq, D_MODEL)
```

```python
import math

import jax
import jax.numpy as jnp
from jax import lax
from jax.experimental import pallas as pl
from jax.experimental.pallas import tpu as pltpu

F32 = jnp.float32
BF16 = jnp.bfloat16

D_MODEL = 1024
N_HEADS = 4
HEAD_DIM = 64
V_DIM = 2 * HEAD_DIM
ATTN_WIDTH = N_HEADS * V_DIM
ROPE_THETA = 500000.0
ROPE_DIM = HEAD_DIM // 4
ROPE_HALF = ROPE_DIM // 2
SSM_WIDTH = D_MODEL // 2
SSM_GROUP = 16
SSM_GROUPS = SSM_WIDTH // SSM_GROUP
SSM_STATE = 64
N_STATE = SSM_GROUPS * SSM_STATE
N_EXPERT_GROUPS = 4
EXPERTS_PER_GROUP = 8
N_EXPERTS = N_EXPERT_GROUPS * EXPERTS_PER_GROUP
D_EXPERT = D_MODEL // 4
EPS = 1e-6
LAM_INIT = 0.8 - 0.6 * math.exp(-0.3 * 0)
QKVU_COLS = 4 * ATTN_WIDTH
LANES = 128
SUBLANES = 8
NEG_BIG = -1e30
LOG2_E = math.log2(math.e)

VMEM_LIMIT = 48 * 1024 * 1024

TM_IN = 1024
TQ = 512
SSM_CHUNK = 64
SCAN_LANES = 512
TM_MERGE = 1024
TM_DISP = 256
DISP_SUB = 4
TS_MOE = 1024
XS_AHEAD = 2
ROW_ALIGN = 16
XS_COLS = D_MODEL + LANES
SORT_ROWS = 2 * TM_DISP + N_EXPERTS * ROW_ALIGN
MAX_PIECES = SORT_ROWS // ROW_ALIGN
POS_RADIX = 32


def _const_spec(shape):
    return pl.BlockSpec(shape, lambda *_: (0,) * len(shape))


def _in_proj_kernel(x_ref, g_ref, w_ref, cos_ref, sin_ref, qg_ref, kg_ref,
                    q_ref, k_ref, v_ref, u_ref):
    x = x_ref[...]
    ms = jnp.mean(x * x, axis=-1, keepdims=True)
    h = (x * lax.rsqrt(ms + EPS) * g_ref[...]).astype(BF16)
    cos_t = cos_ref[...]
    sin_t = sin_ref[...]
    lane = lax.broadcasted_iota(jnp.int32, (x.shape[0], LANES), 1)
    low_half = (lane % HEAD_DIM) < ROPE_HALF
    same_comp = (lax.broadcasted_iota(jnp.int32, (LANES, LANES), 0) // HEAD_DIM
                 == lax.broadcasted_iota(jnp.int32, (LANES, LANES), 1) // HEAD_DIM
                 ).astype(BF16)

    def norm_rope(blk, gain, scale):
        ssum = jnp.dot((blk * blk).astype(BF16), same_comp, preferred_element_type=F32)
        nb = blk * lax.rsqrt(ssum * (1.0 / HEAD_DIM) + EPS) * gain
        up = pltpu.roll(nb, LANES - ROPE_HALF, axis=1)
        dn = pltpu.roll(nb, ROPE_HALF, axis=1)
        partner = jnp.where(low_half, up, dn)
        return ((nb * cos_t + partner * sin_t) * scale).astype(BF16)

    for j in range(0, N_HEADS, 2):
        sl2 = slice(j * LANES, (j + 2) * LANES)
        qb = jnp.dot(h, w_ref[:, sl2], preferred_element_type=F32)
        kb = jnp.dot(h, w_ref[:, ATTN_WIDTH + j * LANES:ATTN_WIDTH + (j + 2) * LANES],
                     preferred_element_type=F32)
        for jj in range(2):
            sl = slice((j + jj) * LANES, (j + jj + 1) * LANES)
            half = slice(jj * LANES, (jj + 1) * LANES)
            q_ref[:, sl] = norm_rope(qb[:, half], qg_ref[...], LOG2_E * HEAD_DIM ** -0.5)
            k_ref[:, sl] = norm_rope(kb[:, half], kg_ref[...], 1.0)
    v_ref[...] = jnp.dot(h, w_ref[:, 2 * ATTN_WIDTH:3 * ATTN_WIDTH],
                         preferred_element_type=F32).astype(BF16)
    u_ref[...] = jnp.dot(h, w_ref[:, 3 * ATTN_WIDTH:4 * ATTN_WIDTH],
                         preferred_element_type=F32).astype(BF16)


def _in_proj(x2d, g, w, cos_t, sin_t, qg, kg):
    t = x2d.shape[0]
    tok = lambda i: (i, 0)
    out = jax.ShapeDtypeStruct((t, ATTN_WIDTH), BF16)
    return pl.pallas_call(
        _in_proj_kernel,
        grid=(t // TM_IN,),
        in_specs=[
            pl.BlockSpec((TM_IN, D_MODEL), tok),
            _const_spec((1, D_MODEL)),
            _const_spec((D_MODEL, QKVU_COLS)),
            pl.BlockSpec((TM_IN, LANES), tok),
            pl.BlockSpec((TM_IN, LANES), tok),
            _const_spec((1, LANES)),
            _const_spec((1, LANES)),
        ],
        out_specs=[pl.BlockSpec((TM_IN, ATTN_WIDTH), tok)] * 4,
        out_shape=[out] * 4,
        compiler_params=pltpu.CompilerParams(
            dimension_semantics=("arbitrary",), vmem_limit_bytes=VMEM_LIMIT),
        name="in_proj",
    )(x2d, g, w, cos_t, sin_t, qg, kg)


def _attn_kernel(qi_ref, kp_ref, mode_ref, lam_ref, q_ref, k_ref, v_ref, sg_ref, o_ref,
                 qs_ref, m_ref, l_ref, acc_ref):
    step = pl.program_id(1)
    kp = kp_ref[step]
    mode = mode_ref[step]

    hq = TQ // 2

    @pl.when(kp == 0)
    def _init():
        for h in range(N_HEADS):
            q = q_ref[:, h * LANES:(h + 1) * LANES]
            lane = lax.broadcasted_iota(jnp.int32, q.shape, 1)
            zero = jnp.zeros_like(q)
            only0 = jnp.where(lane < HEAD_DIM, q, zero)
            only1 = jnp.where(lane < HEAD_DIM, zero, q)
            for half in range(2):
                rows = slice(half * hq, (half + 1) * hq)
                qs_ref[h, 2 * half * hq:(2 * half + 1) * hq, :] = only0[rows, :]
                qs_ref[h, (2 * half + 1) * hq:(2 * half + 2) * hq, :] = only1[rows, :]
        m_ref[...] = jnp.full(m_ref.shape, NEG_BIG, F32)
        l_ref[...] = jnp.zeros(l_ref.shape, F32)
        acc_ref[...] = jnp.zeros(acc_ref.shape, F32)

    def update(h, rows, key0, nkeys, diag_offset=None):
        hs = slice(h * LANES, (h + 1) * LANES)
        ks = slice(key0, key0 + nkeys)
        s = lax.dot_general(qs_ref[h, rows, :], k_ref[ks, hs], (((1,), (1,)), ((), ())),
                            preferred_element_type=F32)
        if diag_offset is not None:
            row = lax.broadcasted_iota(jnp.int32, s.shape, 0)
            row = row % hq + row // TQ * hq + diag_offset
            col = lax.broadcasted_iota(jnp.int32, s.shape, 1)
            s = jnp.where(col <= row, s, NEG_BIG)
        m_old = m_ref[h, rows, :]
        m_new = jnp.maximum(m_old, jnp.max(s, axis=-1, keepdims=True))
        alpha = jnp.exp2(m_old - m_new)
        p = jnp.exp2(s - jnp.concatenate([m_new] * (nkeys // LANES), axis=1))
        v_ones = jnp.concatenate([v_ref[ks, hs], jnp.ones((nkeys, LANES), BF16)], axis=1)
        pv = jnp.dot(p.astype(BF16), v_ones, preferred_element_type=F32)
        l_ref[h, rows, :] = alpha * l_ref[h, rows, :] + pv[:, LANES:2 * LANES]
        acc_ref[h, rows, :] = alpha * acc_ref[h, rows, :] + pv[:, 0:LANES]
        m_ref[h, rows, :] = m_new

    all_rows = slice(0, 2 * TQ)

    def update_diag(h, key0):
        update(h, slice(0, TQ), key0, hq, diag_offset=0)
        update(h, slice(TQ, 2 * TQ), key0, TQ, diag_offset=hq)

    def finish(h):
        o = acc_ref[h] / l_ref[h]
        d = jnp.concatenate(
            [o[2 * half * hq:(2 * half + 1) * hq, :]
             - lam_ref[0] * o[(2 * half + 1) * hq:(2 * half + 2) * hq, :]
             for half in range(2)], axis=0)
        ms = jnp.mean(d * d, axis=-1, keepdims=True)
        d = d * lax.rsqrt(ms + EPS) * sg_ref[...] * (1.0 - LAM_INIT)
        o_ref[:, h * LANES:(h + 1) * LANES] = d.astype(BF16)

    @pl.when(mode == 0)
    def _below():
        for h in range(N_HEADS):
            update(h, all_rows, 0, TQ)
            update(h, all_rows, TQ, TQ)

    @pl.when(mode == 1)
    def _below_then_diag():
        for h in range(N_HEADS):
            update(h, all_rows, 0, TQ)
            update(h, all_rows, TQ, TQ, diag_offset=0)
            finish(h)

    @pl.when(mode == 2)
    def _diag_only():
        for h in range(N_HEADS):
            update_diag(h, 0)
            finish(h)


def _diff_attn(q, k, v, lam, subln_g, bsz, seq):
    nq = seq // TQ
    steps = []
    for i in range(nq):
        for p in range(i // 2 + 1):
            mode = 0 if 2 * p + 1 < i else (1 if 2 * p + 1 == i else 2)
            steps.append((i, p, mode))
    qi = jnp.asarray([s[0] for s in steps], jnp.int32)
    kp = jnp.asarray([s[1] for s in steps], jnp.int32)
    mode = jnp.asarray([s[2] for s in steps], jnp.int32)
    q_map = lambda b, s, qi, kp, mode: (b * nq + qi[s], 0)
    k_map = lambda b, s, qi, kp, mode: (b * (nq // 2) + kp[s], 0)
    grid_spec = pltpu.PrefetchScalarGridSpec(
        num_scalar_prefetch=3,
        grid=(bsz, len(steps)),
        in_specs=[
            pl.BlockSpec(memory_space=pltpu.SMEM),
            pl.BlockSpec((TQ, ATTN_WIDTH), q_map),
            pl.BlockSpec((2 * TQ, ATTN_WIDTH), k_map),
            pl.BlockSpec((2 * TQ, ATTN_WIDTH), k_map),
            pl.BlockSpec((1, LANES), lambda b, s, qi, kp, mode: (0, 0)),
        ],
        out_specs=pl.BlockSpec((TQ, ATTN_WIDTH), q_map),
        scratch_shapes=[
            pltpu.VMEM((N_HEADS, 2 * TQ, LANES), BF16),
            pltpu.VMEM((N_HEADS, 2 * TQ, LANES), F32),
            pltpu.VMEM((N_HEADS, 2 * TQ, LANES), F32),
            pltpu.VMEM((N_HEADS, 2 * TQ, LANES), F32),
        ],
    )
    return pl.pallas_call(
        _attn_kernel,
        grid_spec=grid_spec,
        out_shape=jax.ShapeDtypeStruct((bsz * seq, ATTN_WIDTH), BF16),
        compiler_params=pltpu.CompilerParams(
            dimension_semantics=("arbitrary", "arbitrary"),
            vmem_limit_bytes=VMEM_LIMIT),
        name="diff_attn",
    )(qi, kp, mode, lam, q, k, v, subln_g)


def _gelu_tanh(x):
    c = math.sqrt(2.0 / math.pi)
    return 0.5 * x * (1.0 + jnp.tanh(c * (x + 0.044715 * (x * x * x))))


def _s5_kernel(u_ref, p_ref, pt_ref, b_ref, are_ref, aim_ref, c_ref, d_ref, o_ref,
               st_ref, state_ref):
    bsz, chunk, width = u_ref.shape
    rows = bsz * chunk

    @pl.when(pl.program_id(0) == 0)
    def _init():
        state_ref[...] = jnp.zeros(state_ref.shape, F32)

    u_bt = u_ref[...].reshape(rows, width)
    u_tm = jnp.dot(p_ref[...], u_bt, preferred_element_type=F32)
    u_tm_b = u_tm.astype(BF16)

    slab = SCAN_LANES // SSM_STATE * SSM_GROUP
    g_parts = []
    for ch in range(N_STATE // SCAN_LANES):
        re = slice(ch * SCAN_LANES, (ch + 1) * SCAN_LANES)
        im = slice(N_STATE + ch * SCAN_LANES, N_STATE + (ch + 1) * SCAN_LANES)
        cs = slice(ch * slab, (ch + 1) * slab)
        st_ref[:, re] = jnp.dot(u_tm_b[:, cs], b_ref[cs, re], preferred_element_type=F32)
        st_ref[:, im] = jnp.dot(u_tm_b[:, cs], b_ref[cs, im], preferred_element_type=F32)
        a_re = are_ref[:, re]
        a_im = aim_ref[:, re]

        def body(t, carry, re=re, im=im, a_re=a_re, a_im=a_im):
            s_re, s_im = carry
            r0 = pl.multiple_of(t * SUBLANES, SUBLANES)
            n_re = a_re * s_re - a_im * s_im + st_ref[pl.ds(r0, SUBLANES), re]
            n_im = a_re * s_im + a_im * s_re + st_ref[pl.ds(r0, SUBLANES), im]
            st_ref[pl.ds(r0, SUBLANES), re] = n_re
            st_ref[pl.ds(r0, SUBLANES), im] = n_im
            return n_re, n_im

        s_re, s_im = lax.fori_loop(0, chunk, body, (state_ref[:, re], state_ref[:, im]),
                                   unroll=True)
        state_ref[:, re] = s_re
        state_ref[:, im] = s_im
        y = (jnp.dot(st_ref[:, re].astype(BF16), c_ref[re, cs], preferred_element_type=F32)
             + jnp.dot(st_ref[:, im].astype(BF16), c_ref[im, cs], preferred_element_type=F32))
        y = y + d_ref[:, cs] * u_tm[:, cs]
        g_parts.append(_gelu_tanh(y).astype(BF16))

    g_tm = jnp.concatenate(g_parts, axis=1)
    g_bt = jnp.dot(pt_ref[...], g_tm, preferred_element_type=F32)
    o_ref[...] = g_bt.astype(BF16).reshape(bsz, chunk, width)


def _s5_scan(u3, perm, perm_t, bmat, a_re, a_im, cmat, dvec):
    bsz, seq, width = u3.shape
    rows = bsz * SSM_CHUNK
    blk = pl.BlockSpec((bsz, SSM_CHUNK, width), lambda c: (0, c, 0))
    return pl.pallas_call(
        _s5_kernel,
        grid=(seq // SSM_CHUNK,),
        in_specs=[
            blk,
            _const_spec((rows, rows)),
            _const_spec((rows, rows)),
            _const_spec((width, 2 * N_STATE)),
            _const_spec((bsz, N_STATE)),
            _const_spec((bsz, N_STATE)),
            _const_spec((2 * N_STATE, width)),
            _const_spec((1, width)),
        ],
        out_specs=blk,
        out_shape=jax.ShapeDtypeStruct(u3.shape, BF16),
        scratch_shapes=[
            pltpu.VMEM((rows, 2 * N_STATE), F32),
            pltpu.VMEM((bsz, 2 * N_STATE), F32),
        ],
        compiler_params=pltpu.CompilerParams(
            dimension_semantics=("arbitrary",), vmem_limit_bytes=VMEM_LIMIT),
        name="s5_scan",
    )(u3, perm, perm_t, bmat, a_re, a_im, cmat, dvec)


def _merge_kernel(x_ref, oa_ref, gy_ref, g1_ref, wg_ref, woa_ref, wglu_ref, wout_ref,
                  g2_ref, wr_ref, br_ref, x2_ref, pay_ref, cnt_ref):
    n_sub = TM_MERGE // TM_DISP
    logits = [None] * n_sub

    def rows_of(k):
        return slice(k * TM_DISP, (k + 1) * TM_DISP)

    def tail(k):
        _route_rows(logits[k], pay_ref.at[rows_of(k), :], cnt_ref.at[k])

    for k in range(n_sub):
        rs = rows_of(k)
        logits[k] = _merge_rows(x_ref[rs, :], oa_ref[rs, :], gy_ref[rs, :], g1_ref, wg_ref,
                                woa_ref, wglu_ref, wout_ref, g2_ref, wr_ref, br_ref,
                                x2_ref.at[rs, :], pay_ref.at[rs, :])
        if k > 0:
            tail(k - 1)
    tail(n_sub - 1)


def _merge_rows(x, o_in, gy, g1_ref, wg_ref, woa_ref, wglu_ref, wout_ref, g2_ref, wr_ref,
                br_ref, x2_ref, pay_ref):
    ms = jnp.mean(x * x, axis=-1, keepdims=True)
    h = (x * lax.rsqrt(ms + EPS) * g1_ref[...]).astype(BF16)
    o_a = jnp.dot(o_in, woa_ref[...], preferred_element_type=F32)
    z_lin = jnp.dot(gy, wglu_ref[:, 0:D_MODEL], preferred_element_type=F32)
    z_gate = jnp.dot(gy, wglu_ref[:, D_MODEL:2 * D_MODEL], preferred_element_type=F32)
    o_s = z_lin * jax.nn.sigmoid(z_gate)
    gate_a = jax.nn.sigmoid(jnp.dot(h, wg_ref[:, 0:D_MODEL], preferred_element_type=F32))
    merged = gate_a * o_a
    gate_s = jax.nn.sigmoid(
        jnp.dot(h, wg_ref[:, D_MODEL:2 * D_MODEL], preferred_element_type=F32))
    merged = merged + gate_s * o_s
    x2 = x + jnp.dot(merged.astype(BF16), wout_ref[...], preferred_element_type=F32)
    x2_ref[...] = x2

    ms2 = jnp.mean(x2 * x2, axis=-1, keepdims=True)
    h2 = x2 * lax.rsqrt(ms2 + EPS) * g2_ref[...]
    h2_hi = h2.astype(BF16)
    h2_lo = (h2 - h2_hi.astype(F32)).astype(BF16)
    pay_ref[:, 0:D_MODEL] = h2_hi

    both = jnp.dot(h2_hi, wr_ref[...], preferred_element_type=F32)
    return (both[:, 0:LANES] + both[:, LANES:2 * LANES]
            + jnp.dot(h2_lo, wr_ref[:, 0:LANES], preferred_element_type=F32)
            + br_ref[...])


def _route_rows(logits, pay_ref, cnt_ref):
    lane = lax.broadcasted_iota(jnp.int32, logits.shape, 1).astype(F32)
    is_grp = (lane >= N_EXPERTS) & (lane < N_EXPERTS + N_EXPERT_GROUPS)
    gl = jnp.where(is_grp, logits, NEG_BIG)
    gmax = jnp.max(gl, axis=-1, keepdims=True)
    gsum = jnp.sum(jnp.where(is_grp, jnp.exp(gl - gmax), 0.0), axis=-1, keepdims=True)
    p_grp = 1.0 / gsum
    big = float(4 * LANES)
    grp = jnp.min(jnp.where(is_grp & (gl == gmax), lane, big), axis=-1,
                  keepdims=True) - N_EXPERTS
    sel = logits
    for g in range(1, N_EXPERT_GROUPS):
        rolled = pltpu.roll(logits, LANES - g * EXPERTS_PER_GROUP, axis=1)
        sel = jnp.where(grp == g, rolled, sel)
    in_grp = lane < EXPERTS_PER_GROUP
    es = jnp.where(in_grp, sel, NEG_BIG)
    top1 = jnp.max(es, axis=-1, keepdims=True)
    i1 = jnp.min(jnp.where(in_grp & (es == top1), lane, big), axis=-1, keepdims=True)
    es2 = jnp.where(lane == i1, NEG_BIG, es)
    top2 = jnp.max(es2, axis=-1, keepdims=True)
    i2 = jnp.min(jnp.where(in_grp & (lane != i1) & (es2 == top2), lane, big), axis=-1,
                 keepdims=True)
    e2 = jnp.exp(top2 - top1)
    w1 = p_grp / (1.0 + e2)
    w2 = p_grp * e2 / (1.0 + e2)
    e1 = grp * EXPERTS_PER_GROUP + i1
    e2x = grp * EXPERTS_PER_GROUP + i2

    def hi_lo(w):
        hi = w.astype(BF16).astype(F32)
        return hi, (w - hi).astype(BF16).astype(F32)

    w1h, w1l = hi_lo(w1)
    w2h, w2l = hi_lo(w2)
    pay_ref[:, D_MODEL:D_MODEL + LANES] = (
        jnp.where(lane == 0.0, e1, 0.0) + jnp.where(lane == 1.0, e2x, 0.0)
        + jnp.where(lane == 2.0, w1h, 0.0) + jnp.where(lane == 3.0, w1l, 0.0)
        + jnp.where(lane == 4.0, w2h, 0.0) + jnp.where(lane == 5.0, w2l, 0.0)).astype(BF16)

    picked = jnp.where((lane == e1) | (lane == e2x), 1.0, 0.0)
    cnt_ref[...] = jnp.sum(picked, axis=0, keepdims=True)


def _merge(x2d, o_attn, gy, g1, w_gates, w_oa, w_glu, w_out, g2, w_router, b_r):
    t = x2d.shape[0]
    tok = lambda i: (i, 0)
    sub_tiles = TM_MERGE // TM_DISP
    return pl.pallas_call(
        _merge_kernel,
        grid=(t // TM_MERGE,),
        in_specs=[
            pl.BlockSpec((TM_MERGE, D_MODEL), tok),
            pl.BlockSpec((TM_MERGE, ATTN_WIDTH), tok),
            pl.BlockSpec((TM_MERGE, SSM_WIDTH), tok),
            _const_spec((1, D_MODEL)),
            _const_spec((D_MODEL, 2 * D_MODEL)),
            _const_spec((ATTN_WIDTH, D_MODEL)),
            _const_spec((SSM_WIDTH, 2 * D_MODEL)),
            _const_spec((D_MODEL, D_MODEL)),
            _const_spec((1, D_MODEL)),
            _const_spec((D_MODEL, 2 * LANES)),
            _const_spec((1, LANES)),
        ],
        out_specs=[
            pl.BlockSpec((TM_MERGE, D_MODEL), tok),
            pl.BlockSpec((TM_MERGE, XS_COLS), tok),
            pl.BlockSpec((sub_tiles, 1, LANES), lambda i: (i, 0, 0)),
        ],
        out_shape=[
            jax.ShapeDtypeStruct((t, D_MODEL), F32),
            jax.ShapeDtypeStruct((t, XS_COLS), BF16),
            jax.ShapeDtypeStruct((t // TM_DISP, 1, LANES), F32),
        ],
        compiler_params=pltpu.CompilerParams(
            dimension_semantics=("arbitrary",), vmem_limit_bytes=VMEM_LIMIT),
        name="merge",
    )(x2d, o_attn, gy, g1, w_gates, w_oa, w_glu, w_out, g2, w_router, b_r)


def _moe_layout(t):
    ntiles = t // TM_DISP
    max_rows = 2 * t + ntiles * N_EXPERTS * (ROW_ALIGN - 1)
    nsteps = -(-max_rows // TS_MOE) + N_EXPERTS
    return ntiles, nsteps


def _lane_col(arr, lane, k):
    return jnp.sum(jnp.where(lane == k, arr, 0.0), axis=-1, keepdims=True)


def _tile_positions(aux, tri_ref, upper_ref, pos_ref):
    lane = lax.broadcasted_iota(jnp.int32, aux.shape, 1)
    lane_f = lane.astype(F32)
    is1 = lane_f == _lane_col(aux, lane, 0)
    is2 = lane_f == _lane_col(aux, lane, 1)
    picked = jnp.where(is1 | is2, 1.0, 0.0)
    rank_all = jnp.dot(tri_ref[...], picked.astype(BF16), preferred_element_type=F32)
    cnt_row = jnp.sum(picked, axis=0, keepdims=True)
    pad_row = jnp.floor((cnt_row + (ROW_ALIGN - 1)) * (1.0 / ROW_ALIGN)) * ROW_ALIGN
    off = jnp.dot(jnp.broadcast_to(pad_row, (SUBLANES, LANES)).astype(BF16), upper_ref[...],
                  preferred_element_type=F32)[0:1, :]
    posmat = rank_all + off
    pos1 = jnp.sum(jnp.where(is1, posmat, 0.0), axis=-1, keepdims=True)
    pos2 = jnp.sum(jnp.where(is2, posmat, 0.0), axis=-1, keepdims=True)
    pos_ref[...] = jnp.where(lane == 0, pos1, 0.0) + jnp.where(lane == 1, pos2, 0.0)

    def digits(p):
        hi = jnp.floor(p * (1.0 / POS_RADIX))
        return hi, p - POS_RADIX * hi

    d1h, d1l = digits(pos1)
    d2h, d2l = digits(pos2)
    dig = (jnp.where(lane == 0, d1h, 0.0) + jnp.where(lane == 1, d1l, 0.0)
           + jnp.where(lane == 2, d2h, 0.0) + jnp.where(lane == 3, d2l, 0.0)).astype(BF16)
    eye8 = (lax.broadcasted_iota(jnp.int32, (SUBLANES, LANES), 0)
            == lax.broadcasted_iota(jnp.int32, (SUBLANES, LANES), 1)).astype(BF16)
    rows = lax.dot_general(eye8, dig, (((1,), (1,)), ((), ())), preferred_element_type=F32)
    return (POS_RADIX * rows[0:1, :] + rows[1:2, :], POS_RADIX * rows[2:3, :] + rows[3:4, :])


def _compact_tile(payload, p1_row, p2_row):
    sub = lax.broadcasted_iota(jnp.int32, (SORT_ROWS, TM_DISP), 0).astype(F32)
    perm = jnp.where((sub == p1_row) | (sub == p2_row), 1.0, 0.0).astype(BF16)
    return jnp.dot(perm, payload, preferred_element_type=F32).astype(BF16)


def _dispatch_kernel(dst_ref, npiece_ref, tail_ref, ntail_ref, pay_ref, tri_ref, upper_ref,
                     xs_ref, pos_ref, buf_ref, zbuf_ref, sem_ref, tsem_ref):
    i = pl.program_id(0)
    last = pl.num_programs(0) - 1
    cur = i % 2

    def piece(tile, slot, sub, p):
        return pltpu.make_async_copy(buf_ref.at[slot, sub, p],
                                     xs_ref.at[dst_ref[tile * MAX_PIECES + p]],
                                     sem_ref.at[slot, sub])

    def start_all(tile, slot, sub):
        def body(p, c):
            piece(tile, slot, sub, p).start()
            return c
        lax.fori_loop(0, npiece_ref[tile], body, 0)

    def wait_all(tile, slot, sub):
        def body(p, c):
            piece(tile, slot, sub, p).wait()
            return c
        lax.fori_loop(0, npiece_ref[tile], body, 0)

    places = []
    for sub in range(DISP_SUB):
        rs = slice(sub * TM_DISP, (sub + 1) * TM_DISP)
        places.append(_tile_positions(pay_ref[rs, D_MODEL:XS_COLS].astype(F32), tri_ref,
                                      upper_ref, pos_ref.at[rs, :]))
    for sub in range(DISP_SUB):
        rs = slice(sub * TM_DISP, (sub + 1) * TM_DISP)
        buf_ref[cur, sub] = _compact_tile(pay_ref[rs, :], *places[sub]).reshape(
            MAX_PIECES, ROW_ALIGN, XS_COLS)
    for sub in range(DISP_SUB):
        start_all(i * DISP_SUB + sub, cur, sub)

    @pl.when(i > 0)
    def _wait_prev():
        for sub in range(DISP_SUB):
            wait_all((i - 1) * DISP_SUB + sub, 1 - cur, sub)

    @pl.when(i == last)
    def _finish():
        for sub in range(DISP_SUB):
            wait_all(i * DISP_SUB + sub, cur, sub)
        zbuf_ref[...] = jnp.zeros(zbuf_ref.shape, BF16)

        def tail(p):
            return pltpu.make_async_copy(zbuf_ref, xs_ref.at[tail_ref[p]], tsem_ref.at[0])

        def start_tail(p, c):
            tail(p).start()
            return c

        def wait_tail(p, c):
            tail(p).wait()
            return c

        lax.fori_loop(0, ntail_ref[0], start_tail, 0)
        lax.fori_loop(0, ntail_ref[0], wait_tail, 0)


def _dispatch(payload, dst, npiece, tail, ntail, rows):
    t = payload.shape[0]
    tm = TM_DISP * DISP_SUB
    tok = lambda i, *_: (i, 0)
    const = lambda i, *_: (0, 0)
    tri = (jnp.arange(TM_DISP)[:, None] > jnp.arange(TM_DISP)[None, :]).astype(BF16)
    upper = (jnp.arange(LANES)[:, None] < jnp.arange(LANES)[None, :]).astype(BF16)
    grid_spec = pltpu.PrefetchScalarGridSpec(
        num_scalar_prefetch=4,
        grid=(t // tm,),
        in_specs=[
            pl.BlockSpec((tm, XS_COLS), tok),
            pl.BlockSpec((TM_DISP, TM_DISP), const),
            pl.BlockSpec((LANES, LANES), const),
        ],
        out_specs=[
            pl.BlockSpec(memory_space=pl.ANY),
            pl.BlockSpec((tm, LANES), tok),
        ],
        scratch_shapes=[
            pltpu.VMEM((2, DISP_SUB, MAX_PIECES, ROW_ALIGN, XS_COLS), BF16),
            pltpu.VMEM((ROW_ALIGN, XS_COLS), BF16),
            pltpu.SemaphoreType.DMA((2, DISP_SUB)),
            pltpu.SemaphoreType.DMA((1,)),
        ],
    )
    xs, pos = pl.pallas_call(
        _dispatch_kernel,
        grid_spec=grid_spec,
        out_shape=[
            jax.ShapeDtypeStruct((rows // ROW_ALIGN, ROW_ALIGN, XS_COLS), BF16),
            jax.ShapeDtypeStruct((t, LANES), F32),
        ],
        compiler_params=pltpu.CompilerParams(
            dimension_semantics=("arbitrary",), vmem_limit_bytes=VMEM_LIMIT),
        name="dispatch",
    )(dst // ROW_ALIGN, npiece, tail // ROW_ALIGN, ntail, payload, tri, upper)
    return xs.reshape(rows, XS_COLS), pos


def _moe_kernel(blk_ref, exp_ref, nvalid_ref, fresh_ref, xs_hbm, wg_ref, wu_ref, wd_ref,
                ys_ref, wgb_ref, wub_ref, wdb_ref, xring_ref, xsem_ref):
    s = pl.program_id(0)
    n = pl.num_programs(0)

    def fetch(step):
        slot = step % (XS_AHEAD + 1)
        row0 = pl.multiple_of(blk_ref[step] * TS_MOE, TS_MOE)
        return pltpu.make_async_copy(xs_hbm.at[pl.ds(row0, TS_MOE)], xring_ref.at[slot],
                                     xsem_ref.at[slot])

    @pl.when(s == 0)
    def _prime():
        for k in range(XS_AHEAD):
            @pl.when(k < n)
            def _start(k=k):
                fetch(k).start()

    @pl.when(s + XS_AHEAD < n)
    def _prefetch():
        fetch(s + XS_AHEAD).start()

    fetch(s).wait()
    xs_ref = xring_ref.at[s % (XS_AHEAD + 1)]

    @pl.when(fresh_ref[s] == 1)
    def _cast_weights():
        wgb_ref[...] = wg_ref[...].astype(BF16)
        wub_ref[...] = wu_ref[...].astype(BF16)
        wdb_ref[...] = wd_ref[...].astype(BF16)

    @pl.when(s < nvalid_ref[0])
    def _compute():
        t = xs_ref[:, 0:D_MODEL]
        aux = xs_ref[:, D_MODEL:XS_COLS].astype(F32)
        lane = lax.broadcasted_iota(jnp.int32, aux.shape, 1)
        first = _lane_col(aux, lane, 0) == exp_ref[s].astype(F32)
        w = jnp.where(first, _lane_col(aux, lane, 2) + _lane_col(aux, lane, 3),
                      _lane_col(aux, lane, 4) + _lane_col(aux, lane, 5))
        hg = jnp.dot(t, wgb_ref[...], preferred_element_type=F32)
        hu = jnp.dot(t, wub_ref[...], preferred_element_type=F32)
        a = (hg * jax.nn.sigmoid(hg) * hu * w).astype(BF16)
        ys_ref[...] = jnp.dot(a, wdb_ref[...], preferred_element_type=F32).astype(BF16)

    @pl.when(s == nvalid_ref[0])
    def _spare():
        ys_ref[...] = jnp.zeros(ys_ref.shape, BF16)


def _moe(xs, blk, exp, nvalid, fresh, wg, wu, wd):
    row = lambda s, blk, exp, nv, fr: (blk[s], 0)
    wsel = lambda s, blk, exp, nv, fr: (exp[s], 0, 0)
    grid_spec = pltpu.PrefetchScalarGridSpec(
        num_scalar_prefetch=4,
        grid=(blk.shape[0],),
        in_specs=[
            pl.BlockSpec(memory_space=pl.ANY),
            pl.BlockSpec((None, D_MODEL, D_EXPERT), wsel),
            pl.BlockSpec((None, D_MODEL, D_EXPERT), wsel),
            pl.BlockSpec((None, D_EXPERT, D_MODEL), wsel),
        ],
        out_specs=pl.BlockSpec((TS_MOE, D_MODEL), row),
        scratch_shapes=[
            pltpu.VMEM((D_MODEL, D_EXPERT), BF16),
            pltpu.VMEM((D_MODEL, D_EXPERT), BF16),
            pltpu.VMEM((D_EXPERT, D_MODEL), BF16),
            pltpu.VMEM((XS_AHEAD + 1, TS_MOE, XS_COLS), BF16),
            pltpu.SemaphoreType.DMA((XS_AHEAD + 1,)),
        ],
    )
    return pl.pallas_call(
        _moe_kernel,
        grid_spec=grid_spec,
        out_shape=jax.ShapeDtypeStruct((xs.shape[0], D_MODEL), BF16),
        compiler_params=pltpu.CompilerParams(
            dimension_semantics=("arbitrary",), vmem_limit_bytes=VMEM_LIMIT),
        name="moe",
    )(blk, exp, nvalid, fresh, xs, wg, wu, wd)


def _combine_kernel(dst_ref, npiece_ref, x2_ref, pos_ref, ys_ref, o_ref, stage_ref, sem_ref):
    i = pl.program_id(0)
    n = pl.num_programs(0)
    cur = i % 2
    tm = TM_DISP

    def piece(tile, slot, sub, p):
        row = pl.multiple_of(p * ROW_ALIGN, ROW_ALIGN)
        src = pl.multiple_of(dst_ref[tile * MAX_PIECES + p], ROW_ALIGN)
        return pltpu.make_async_copy(ys_ref.at[pl.ds(src, ROW_ALIGN)],
                                     stage_ref.at[slot, sub, pl.ds(row, ROW_ALIGN)],
                                     sem_ref.at[slot, sub])

    def start_step(step, slot):
        for sub in range(DISP_SUB):
            tile = step * DISP_SUB + sub

            def body(p, c, tile=tile, sub=sub):
                piece(tile, slot, sub, p).start()
                return c
            lax.fori_loop(0, npiece_ref[tile], body, 0)

    @pl.when(i == 0)
    def _first():
        stage_ref[...] = jnp.zeros(stage_ref.shape, BF16)
        start_step(0, 0)

    @pl.when(i + 1 < n)
    def _prefetch():
        start_step(i + 1, 1 - cur)

    for sub in range(DISP_SUB):
        tile = i * DISP_SUB + sub

        def wait_body(p, c, tile=tile, sub=sub):
            piece(tile, cur, sub, p).wait()
            return c

        lax.fori_loop(0, npiece_ref[tile], wait_body, 0)

    for sub in range(DISP_SUB):
        rs = slice(sub * tm, (sub + 1) * tm)
        pos = pos_ref[rs, :]
        lane128 = lax.broadcasted_iota(jnp.int32, pos.shape, 1)
        p1 = _lane_col(pos, lane128, 0)
        p2 = _lane_col(pos, lane128, 1)
        lane = lax.broadcasted_iota(jnp.int32, (tm, SORT_ROWS), 1).astype(F32)
        pick = jnp.where((lane == p1) | (lane == p2), 1.0, 0.0).astype(BF16)
        o_ref[rs, :] = x2_ref[rs, :] + jnp.dot(pick, stage_ref[cur, sub],
                                               preferred_element_type=F32)


def _combine(x2, pos, ys, dst, npiece):
    t = x2.shape[0]
    tm = TM_DISP * DISP_SUB
    tok = lambda i, *_: (i, 0)
    grid_spec = pltpu.PrefetchScalarGridSpec(
        num_scalar_prefetch=2,
        grid=(t // tm,),
        in_specs=[
            pl.BlockSpec((tm, D_MODEL), tok),
            pl.BlockSpec((tm, LANES), tok),
            pl.BlockSpec(memory_space=pl.ANY),
        ],
        out_specs=pl.BlockSpec((tm, D_MODEL), tok),
        scratch_shapes=[
            pltpu.VMEM((2, DISP_SUB, SORT_ROWS, D_MODEL), BF16),
            pltpu.SemaphoreType.DMA((2, DISP_SUB)),
        ],
    )
    return pl.pallas_call(
        _combine_kernel,
        grid_spec=grid_spec,
        out_shape=jax.ShapeDtypeStruct((t, D_MODEL), F32),
        compiler_params=pltpu.CompilerParams(
            dimension_semantics=("arbitrary",), vmem_limit_bytes=VMEM_LIMIT),
        name="combine",
    )(dst, npiece, x2, pos, ys)


def _sparse_moe(payload, cnt, x2, wg, wu, wd):
    t = payload.shape[0]
    ntiles, nsteps = _moe_layout(t)
    rows = (nsteps + 1) * TS_MOE
    i32 = jnp.int32
    counts = cnt[:, 0, :N_EXPERTS].astype(i32)
    padded = (counts + (ROW_ALIGN - 1)) // ROW_ALIGN * ROW_ALIGN
    total = jnp.sum(padded, axis=0)
    reg_rows = (total + (TS_MOE - 1)) // TS_MOE * TS_MOE
    region = jnp.cumsum(reg_rows) - reg_rows
    base = region[None, :] + jnp.cumsum(padded, axis=0) - padded
    np_e = padded // ROW_ALIGN
    cum = jnp.cumsum(np_e, axis=1)
    npiece = cum[:, -1].astype(i32)
    p = jnp.arange(MAX_PIECES, dtype=i32)
    owns = (p[None, :, None] >= (cum - np_e)[:, None, :]) & (p[None, :, None] < cum[:, None, :])
    piece_dst = base[:, None, :] + (p[None, :, None] - (cum - np_e)[:, None, :]) * ROW_ALIGN
    dst = jnp.sum(jnp.where(owns, piece_dst, 0), axis=2).reshape(-1).astype(i32)
    nt_e = (reg_rows - total) // ROW_ALIGN
    cumt = jnp.cumsum(nt_e)
    q = jnp.arange(N_EXPERTS * (TS_MOE // ROW_ALIGN), dtype=i32)
    owns_q = (q[:, None] >= (cumt - nt_e)[None, :]) & (q[:, None] < cumt[None, :])
    tail_dst = (region + total)[None, :] + (q[:, None] - (cumt - nt_e)[None, :]) * ROW_ALIGN
    tail = jnp.sum(jnp.where(owns_q, tail_dst, 0), axis=1).astype(i32)
    ntail = cumt[-1:].astype(i32)
    ntile_e = reg_rows // TS_MOE
    first_t = jnp.cumsum(ntile_e) - ntile_e
    nvalid = jnp.sum(ntile_e).reshape(1).astype(i32)
    s = jnp.arange(nsteps, dtype=i32)
    owns_s = (s[:, None] >= first_t[None, :]) & (s[:, None] < (first_t + ntile_e)[None, :])
    spare = s >= nvalid[0]
    fresh = jnp.any(owns_s & (s[:, None] == first_t[None, :]), axis=1).astype(i32)
    exp = jnp.where(spare, N_EXPERTS - 1, jnp.sum(
        jnp.where(owns_s, jnp.arange(N_EXPERTS, dtype=i32)[None, :], 0), axis=1)).astype(i32)
    blk = jnp.where(spare, nsteps, jnp.sum(
        jnp.where(owns_s, (region // TS_MOE - first_t)[None, :] + s[:, None], 0),
        axis=1)).astype(i32)
    xs, pos = _dispatch(payload, dst, npiece, tail, ntail, rows)
    ys = _moe(xs, blk, exp, nvalid, fresh, wg, wu, wd)
    return _combine(x2, pos, ys, dst, npiece)


def _rope_tables(positions):
    inv = ROPE_THETA ** (-jnp.arange(0, ROPE_DIM, 2, dtype=F32) / ROPE_DIM)
    ang = positions.astype(F32).reshape(-1, 1) * inv
    cos, sin = jnp.cos(ang), jnp.sin(ang)
    d = jnp.arange(LANES) % HEAD_DIM
    in_rope = d < ROPE_DIM
    place = (((d % ROPE_HALF)[None, :] == jnp.arange(ROPE_HALF)[:, None])
             & in_rope[None, :]).astype(F32)
    sign = jnp.where(d < ROPE_HALF, -1.0, 1.0)
    hp = lax.Precision.HIGHEST
    cos_t = jnp.dot(cos, place, precision=hp) + (1.0 - in_rope.astype(F32))
    sin_t = jnp.dot(sin, place * sign, precision=hp)
    return cos_t, sin_t


def _s5_params(lam_re, lam_im, log_dt, b_re, b_im, c_re, c_im, bsz):
    dt = jnp.exp(log_dt)[:, None]
    mag = jnp.exp(lam_re * dt)
    lb_re = mag * jnp.cos(lam_im * dt)
    lb_im = mag * jnp.sin(lam_im * dt)
    den = lam_re * lam_re + lam_im * lam_im
    k_re = ((lb_re - 1.0) * lam_re + lb_im * lam_im) / den
    k_im = (lb_im * lam_re - (lb_re - 1.0) * lam_im) / den
    bb_re = k_re[..., None] * b_re - k_im[..., None] * b_im
    bb_im = k_re[..., None] * b_im + k_im[..., None] * b_re
    eye = jnp.eye(SSM_GROUPS, dtype=F32)
    blk_b = lambda m: jnp.einsum('gph,gk->ghkp', m, eye).reshape(SSM_WIDTH, N_STATE)
    blk_c = lambda m: jnp.einsum('ghp,gk->gpkh', m, eye).reshape(N_STATE, SSM_WIDTH)
    bmat = jnp.concatenate([blk_b(bb_re), blk_b(bb_im)], axis=1).astype(BF16)
    cmat = jnp.concatenate([blk_c(c_re), blk_c(-c_im)], axis=0).astype(BF16)
    a_re = jnp.broadcast_to(lb_re.reshape(1, N_STATE), (bsz, N_STATE))
    a_im = jnp.broadcast_to(lb_im.reshape(1, N_STATE), (bsz, N_STATE))
    return bmat, a_re, a_im, cmat


def _time_major_perm(bsz, chunk):
    r = jnp.arange(bsz * chunk)
    src = (r % bsz) * chunk + r // bsz
    perm = (src[:, None] == jnp.arange(bsz * chunk)[None, :]).astype(BF16)
    return perm, perm.T


def kernel(x, positions, norm_mix_g, w_in, q_norm_g, k_norm_g, lambda_q1, lambda_k1, lambda_q2, lambda_k2, subln_g, w_o_attn, ssm_lambda_re, ssm_lambda_im, ssm_log_dt, ssm_b_re, ssm_b_im, ssm_c_re, ssm_c_im, ssm_d, w_glu, w_out, norm_ffn_g, w_router_group, b_router_group, w_router_expert, b_router_expert, w_expert_gate, w_expert_up, w_expert_down):
    bsz, seq, _ = x.shape
    assert bsz == SUBLANES and seq % (2 * TQ) == 0 and seq % SSM_CHUNK == 0
    assert norm_mix_g.shape[0] == 1
    t = bsz * seq
    x2d = x.reshape(t, D_MODEL)
    l = 0

    cos_t, sin_t = _rope_tables(positions)
    w_qkvu = w_in[l][:, :QKVU_COLS].astype(BF16)
    w_gates = w_in[l][:, QKVU_COLS:].astype(BF16)
    qg = jnp.tile(q_norm_g[l].reshape(1, HEAD_DIM), (1, 2))
    kg = jnp.tile(k_norm_g[l].reshape(1, HEAD_DIM), (1, 2))
    q, k, v, u = _in_proj(x2d, norm_mix_g[l].reshape(1, D_MODEL), w_qkvu, cos_t, sin_t, qg, kg)

    lam = (jnp.exp(jnp.sum(lambda_q1[l] * lambda_k1[l]))
           - jnp.exp(jnp.sum(lambda_q2[l] * lambda_k2[l])) + LAM_INIT).reshape(1)
    o_attn = _diff_attn(q, k, v, lam, subln_g[l].reshape(1, V_DIM), bsz, seq)

    bmat, a_re, a_im, cmat = _s5_params(
        ssm_lambda_re[l], ssm_lambda_im[l], ssm_log_dt[l], ssm_b_re[l], ssm_b_im[l],
        ssm_c_re[l], ssm_c_im[l], bsz)
    perm, perm_t = _time_major_perm(bsz, SSM_CHUNK)
    gy = _s5_scan(u.reshape(bsz, seq, SSM_WIDTH), perm, perm_t, bmat, a_re, a_im, cmat,
                  ssm_d[l].reshape(1, SSM_WIDTH)).reshape(t, SSM_WIDTH)

    w_r = jnp.concatenate(
        [w_router_expert[l].reshape(D_MODEL, N_EXPERTS), w_router_group[l],
         jnp.zeros((D_MODEL, LANES - N_EXPERTS - N_EXPERT_GROUPS), F32)], axis=1)
    b_r = jnp.concatenate(
        [b_router_expert[l].reshape(N_EXPERTS), b_router_group[l],
         jnp.zeros((LANES - N_EXPERTS - N_EXPERT_GROUPS,), F32)]).reshape(1, LANES)
    wr_hi = w_r.astype(BF16)
    wr_lo = (w_r - wr_hi.astype(F32)).astype(BF16)
    x2, payload, cnt = _merge(
        x2d, o_attn, gy, norm_mix_g[l].reshape(1, D_MODEL), w_gates,
        w_o_attn[l].astype(BF16), w_glu[l].astype(BF16), w_out[l].astype(BF16),
        norm_ffn_g[l].reshape(1, D_MODEL), jnp.concatenate([wr_hi, wr_lo], axis=1), b_r)

    wg = w_expert_gate[l].reshape(N_EXPERTS, D_MODEL, D_EXPERT)
    wu = w_expert_up[l].reshape(N_EXPERTS, D_MODEL, D_EXPERT)
    wd = w_expert_down[l].reshape(N_EXPERTS, D_EXPERT, D_MODEL)
    out = _sparse_moe(payload, cnt, x2, wg, wu, wd)
    return out.reshape(bsz, seq, D_MODEL)
```

```python
import math

import jax
import jax.numpy as jnp
from jax import lax
from jax.experimental import pallas as pl
from jax.experimental.pallas import tpu as pltpu

F32 = jnp.float32
BF16 = jnp.bfloat16

D_MODEL = 1024
N_HEADS = 4
HEAD_DIM = 64
V_DIM = 2 * HEAD_DIM
ATTN_WIDTH = N_HEADS * V_DIM
ROPE_THETA = 500000.0
ROPE_DIM = HEAD_DIM // 4
ROPE_HALF = ROPE_DIM // 2
SSM_WIDTH = D_MODEL // 2
SSM_GROUP = 16
SSM_GROUPS = SSM_WIDTH // SSM_GROUP
SSM_STATE = 64
N_STATE = SSM_GROUPS * SSM_STATE
N_EXPERT_GROUPS = 4
EXPERTS_PER_GROUP = 8
N_EXPERTS = N_EXPERT_GROUPS * EXPERTS_PER_GROUP
D_EXPERT = D_MODEL // 4
EPS = 1e-6
LAM_INIT = 0.8 - 0.6 * math.exp(-0.3 * 0)
QKVU_COLS = 4 * ATTN_WIDTH
LANES = 128
SUBLANES = 8
NEG_BIG = -1e30
LOG2_E = math.log2(math.e)

VMEM_LIMIT = 48 * 1024 * 1024

TM_IN = 1024
TQ = 512
SSM_CHUNK = 64
S5_SUB = 4
SCAN_LANES = 512
TM_MERGE = 1024
TM_DISP = 256
DISP_SUB = 4
TS_MOE = 1024
ROW_ALIGN = 16
XS_COLS = D_MODEL + LANES
SORT_ROWS = 2 * TM_DISP + N_EXPERTS * ROW_ALIGN
MAX_PIECES = SORT_ROWS // ROW_ALIGN
POS_RADIX = 32


def _const_spec(shape):
    return pl.BlockSpec(shape, lambda *_: (0,) * len(shape))


def _in_proj_kernel(x_ref, g_ref, w_ref, cos_ref, sin_ref, qg_ref, kg_ref,
                    q_ref, k_ref, v_ref, u_ref):
    x = x_ref[...]
    ms = jnp.mean(x * x, axis=-1, keepdims=True)
    h = (x * lax.rsqrt(ms + EPS) * g_ref[...]).astype(BF16)
    cos_t = cos_ref[...]
    sin_t = sin_ref[...]
    lane = lax.broadcasted_iota(jnp.int32, (x.shape[0], LANES), 1)
    low_half = (lane % HEAD_DIM) < ROPE_HALF
    same_comp = (lax.broadcasted_iota(jnp.int32, (LANES, LANES), 0) // HEAD_DIM
                 == lax.broadcasted_iota(jnp.int32, (LANES, LANES), 1) // HEAD_DIM
                 ).astype(BF16)

    def norm_rope(blk, gain, scale):
        ssum = jnp.dot((blk * blk).astype(BF16), same_comp, preferred_element_type=F32)
        nb = blk * lax.rsqrt(ssum * (1.0 / HEAD_DIM) + EPS) * gain
        up = pltpu.roll(nb, LANES - ROPE_HALF, axis=1)
        dn = pltpu.roll(nb, ROPE_HALF, axis=1)
        partner = jnp.where(low_half, up, dn)
        return ((nb * cos_t + partner * sin_t) * scale).astype(BF16)

    for j in range(0, N_HEADS, 2):
        sl2 = slice(j * LANES, (j + 2) * LANES)
        qb = jnp.dot(h, w_ref[:, sl2], preferred_element_type=F32)
        kb = jnp.dot(h, w_ref[:, ATTN_WIDTH + j * LANES:ATTN_WIDTH + (j + 2) * LANES],
                     preferred_element_type=F32)
        for jj in range(2):
            sl = slice((j + jj) * LANES, (j + jj + 1) * LANES)
            half = slice(jj * LANES, (jj + 1) * LANES)
            q_ref[:, sl] = norm_rope(qb[:, half], qg_ref[...], LOG2_E * HEAD_DIM ** -0.5)
            k_ref[:, sl] = norm_rope(kb[:, half], kg_ref[...], 1.0)
    v_ref[...] = jnp.dot(h, w_ref[:, 2 * ATTN_WIDTH:3 * ATTN_WIDTH],
                         preferred_element_type=F32).astype(BF16)
    u_ref[...] = jnp.dot(h, w_ref[:, 3 * ATTN_WIDTH:4 * ATTN_WIDTH],
                         preferred_element_type=F32).astype(BF16)


def _in_proj(x2d, g, w, cos_t, sin_t, qg, kg):
    t = x2d.shape[0]
    tok = lambda i: (i, 0)
    out = jax.ShapeDtypeStruct((t, ATTN_WIDTH), BF16)
    return pl.pallas_call(
        _in_proj_kernel,
        grid=(t // TM_IN,),
        in_specs=[
            pl.BlockSpec((TM_IN, D_MODEL), tok),
            _const_spec((1, D_MODEL)),
            _const_spec((D_MODEL, QKVU_COLS)),
            pl.BlockSpec((TM_IN, LANES), tok),
            pl.BlockSpec((TM_IN, LANES), tok),
            _const_spec((1, LANES)),
            _const_spec((1, LANES)),
        ],
        out_specs=[pl.BlockSpec((TM_IN, ATTN_WIDTH), tok)] * 4,
        out_shape=[out] * 4,
        compiler_params=pltpu.CompilerParams(
            dimension_semantics=("arbitrary",), vmem_limit_bytes=VMEM_LIMIT),
        name="in_proj",
    )(x2d, g, w, cos_t, sin_t, qg, kg)


def _attn_kernel(qi_ref, kp_ref, mode_ref, lam_ref, q_ref, k_ref, v_ref, sg_ref, o_ref,
                 qs_ref, m_ref, l_ref, acc_ref):
    step = pl.program_id(1)
    kp = kp_ref[step]
    mode = mode_ref[step]

    hq = TQ // 2

    @pl.when(kp == 0)
    def _init():
        for h in range(N_HEADS):
            q = q_ref[:, h * LANES:(h + 1) * LANES]
            lane = lax.broadcasted_iota(jnp.int32, q.shape, 1)
            zero = jnp.zeros_like(q)
            only0 = jnp.where(lane < HEAD_DIM, q, zero)
            only1 = jnp.where(lane < HEAD_DIM, zero, q)
            for half in range(2):
                rows = slice(half * hq, (half + 1) * hq)
                qs_ref[h, 2 * half * hq:(2 * half + 1) * hq, :] = only0[rows, :]
                qs_ref[h, (2 * half + 1) * hq:(2 * half + 2) * hq, :] = only1[rows, :]
        m_ref[...] = jnp.full(m_ref.shape, NEG_BIG, F32)
        l_ref[...] = jnp.zeros(l_ref.shape, F32)
        acc_ref[...] = jnp.zeros(acc_ref.shape, F32)

    def update(h, rows, key0, nkeys, diag_offset=None):
        hs = slice(h * LANES, (h + 1) * LANES)
        ks = slice(key0, key0 + nkeys)
        s = lax.dot_general(qs_ref[h, rows, :], k_ref[ks, hs], (((1,), (1,)), ((), ())),
                            preferred_element_type=F32)
        if diag_offset is not None:
            row = lax.broadcasted_iota(jnp.int32, s.shape, 0)
            row = row % hq + row // TQ * hq + diag_offset
            col = lax.broadcasted_iota(jnp.int32, s.shape, 1)
            s = jnp.where(col <= row, s, NEG_BIG)
        m_old = m_ref[h, rows, :]
        m_new = jnp.maximum(m_old, jnp.max(s, axis=-1, keepdims=True))
        alpha = jnp.exp2(m_old - m_new)
        p = jnp.exp2(s - jnp.concatenate([m_new] * (nkeys // LANES), axis=1))
        v_ones = jnp.concatenate([v_ref[ks, hs], jnp.ones((nkeys, LANES), BF16)], axis=1)
        pv = jnp.dot(p.astype(BF16), v_ones, preferred_element_type=F32)
        l_ref[h, rows, :] = alpha * l_ref[h, rows, :] + pv[:, LANES:2 * LANES]
        acc_ref[h, rows, :] = alpha * acc_ref[h, rows, :] + pv[:, 0:LANES]
        m_ref[h, rows, :] = m_new

    all_rows = slice(0, 2 * TQ)

    def update_diag(h, key0):
        update(h, slice(0, TQ), key0, hq, diag_offset=0)
        update(h, slice(TQ, 2 * TQ), key0, TQ, diag_offset=hq)

    def finish(h):
        o = acc_ref[h] / l_ref[h]
        d = jnp.concatenate(
            [o[2 * half * hq:(2 * half + 1) * hq, :]
             - lam_ref[0] * o[(2 * half + 1) * hq:(2 * half + 2) * hq, :]
             for half in range(2)], axis=0)
        ms = jnp.mean(d * d, axis=-1, keepdims=True)
        d = d * lax.rsqrt(ms + EPS) * sg_ref[...] * (1.0 - LAM_INIT)
        o_ref[:, h * LANES:(h + 1) * LANES] = d.astype(BF16)

    @pl.when(mode == 0)
    def _below():
        for h in range(N_HEADS):
            update(h, all_rows, 0, TQ)
            update(h, all_rows, TQ, TQ)

    @pl.when(mode == 1)
    def _below_then_diag():
        for h in range(N_HEADS):
            update(h, all_rows, 0, TQ)
            update(h, all_rows, TQ, TQ, diag_offset=0)
            finish(h)

    @pl.when(mode == 2)
    def _diag_only():
        for h in range(N_HEADS):
            update_diag(h, 0)
            finish(h)


def _diff_attn(q, k, v, lam, subln_g, bsz, seq):
    nq = seq // TQ
    steps = []
    for i in range(nq):
        for p in range(i // 2 + 1):
            mode = 0 if 2 * p + 1 < i else (1 if 2 * p + 1 == i else 2)
            steps.append((i, p, mode))
    qi = jnp.asarray([s[0] for s in steps], jnp.int32)
    kp = jnp.asarray([s[1] for s in steps], jnp.int32)
    mode = jnp.asarray([s[2] for s in steps], jnp.int32)
    q_map = lambda b, s, qi, kp, mode: (b * nq + qi[s], 0)
    k_map = lambda b, s, qi, kp, mode: (b * (nq // 2) + kp[s], 0)
    grid_spec = pltpu.PrefetchScalarGridSpec(
        num_scalar_prefetch=3,
        grid=(bsz, len(steps)),
        in_specs=[
            pl.BlockSpec(memory_space=pltpu.SMEM),
            pl.BlockSpec((TQ, ATTN_WIDTH), q_map),
            pl.BlockSpec((2 * TQ, ATTN_WIDTH), k_map),
            pl.BlockSpec((2 * TQ, ATTN_WIDTH), k_map),
            pl.BlockSpec((1, LANES), lambda b, s, qi, kp, mode: (0, 0)),
        ],
        out_specs=pl.BlockSpec((TQ, ATTN_WIDTH), q_map),
        scratch_shapes=[
            pltpu.VMEM((N_HEADS, 2 * TQ, LANES), BF16),
            pltpu.VMEM((N_HEADS, 2 * TQ, LANES), F32),
            pltpu.VMEM((N_HEADS, 2 * TQ, LANES), F32),
            pltpu.VMEM((N_HEADS, 2 * TQ, LANES), F32),
        ],
    )
    return pl.pallas_call(
        _attn_kernel,
        grid_spec=grid_spec,
        out_shape=jax.ShapeDtypeStruct((bsz * seq, ATTN_WIDTH), BF16),
        compiler_params=pltpu.CompilerParams(
            dimension_semantics=("arbitrary", "arbitrary"),
            vmem_limit_bytes=VMEM_LIMIT),
        name="diff_attn",
    )(qi, kp, mode, lam, q, k, v, subln_g)


def _gelu_tanh(x):
    c = math.sqrt(2.0 / math.pi)
    return 0.5 * x * (1.0 + jnp.tanh(c * (x + 0.044715 * (x * x * x))))


def _s5_kernel(u_ref, p_ref, pt_ref, b_ref, are_ref, aim_ref, c_ref, d_ref, o_ref,
               st_ref, state_ref):
    bsz, _, width = u_ref.shape
    chunk = SSM_CHUNK
    rows = bsz * chunk
    n_ch = N_STATE // SCAN_LANES

    @pl.when(pl.program_id(0) == 0)
    def _init():
        state_ref[...] = jnp.zeros(state_ref.shape, F32)

    slab = SCAN_LANES // SSM_STATE * SSM_GROUP
    state = [(state_ref[:, ch * SCAN_LANES:(ch + 1) * SCAN_LANES],
              state_ref[:, N_STATE + ch * SCAN_LANES:N_STATE + (ch + 1) * SCAN_LANES])
             for ch in range(n_ch)]

    for sub in range(S5_SUB):
        st = st_ref.at[sub]
        u_bt = u_ref[:, sub * chunk:(sub + 1) * chunk, :].reshape(rows, width)
        u_tm = jnp.dot(p_ref[...], u_bt, preferred_element_type=F32)
        u_tm_b = u_tm.astype(BF16)
        for ch in range(n_ch):
            re = slice(ch * SCAN_LANES, (ch + 1) * SCAN_LANES)
            im = slice(N_STATE + ch * SCAN_LANES, N_STATE + (ch + 1) * SCAN_LANES)
            cs = slice(ch * slab, (ch + 1) * slab)
            st[:, re] = jnp.dot(u_tm_b[:, cs], b_ref[cs, re], preferred_element_type=F32)
            st[:, im] = jnp.dot(u_tm_b[:, cs], b_ref[cs, im], preferred_element_type=F32)
        g_parts = []
        for ch in range(n_ch):
            re = slice(ch * SCAN_LANES, (ch + 1) * SCAN_LANES)
            im = slice(N_STATE + ch * SCAN_LANES, N_STATE + (ch + 1) * SCAN_LANES)
            cs = slice(ch * slab, (ch + 1) * slab)
            a_re = are_ref[:, re]
            a_im = aim_ref[:, re]

            def body(t, carry, st=st, re=re, im=im, a_re=a_re, a_im=a_im):
                s_re, s_im = carry
                r0 = pl.multiple_of(t * SUBLANES, SUBLANES)
                n_re = a_re * s_re - a_im * s_im + st[pl.ds(r0, SUBLANES), re]
                n_im = a_re * s_im + a_im * s_re + st[pl.ds(r0, SUBLANES), im]
                st[pl.ds(r0, SUBLANES), re] = n_re
                st[pl.ds(r0, SUBLANES), im] = n_im
                return n_re, n_im

            state[ch] = lax.fori_loop(0, chunk, body, state[ch], unroll=True)
            y = (jnp.dot(st[:, re].astype(BF16), c_ref[re, cs], preferred_element_type=F32)
                 + jnp.dot(st[:, im].astype(BF16), c_ref[im, cs], preferred_element_type=F32))
            y = y + d_ref[:, cs] * u_tm[:, cs]
            g_parts.append(_gelu_tanh(y).astype(BF16))

        g_tm = jnp.concatenate(g_parts, axis=1)
        g_bt = jnp.dot(pt_ref[...], g_tm, preferred_element_type=F32)
        o_ref[:, sub * chunk:(sub + 1) * chunk, :] = g_bt.astype(BF16).reshape(
            bsz, chunk, width)

    for ch in range(n_ch):
        state_ref[:, ch * SCAN_LANES:(ch + 1) * SCAN_LANES] = state[ch][0]
        state_ref[:, N_STATE + ch * SCAN_LANES:N_STATE + (ch + 1) * SCAN_LANES] = state[ch][1]


def _s5_scan(u3, perm, perm_t, bmat, a_re, a_im, cmat, dvec):
    bsz, seq, width = u3.shape
    rows = bsz * SSM_CHUNK
    blk = pl.BlockSpec((bsz, S5_SUB * SSM_CHUNK, width), lambda c: (0, c, 0))
    return pl.pallas_call(
        _s5_kernel,
        grid=(seq // (S5_SUB * SSM_CHUNK),),
        in_specs=[
            blk,
            _const_spec((rows, rows)),
            _const_spec((rows, rows)),
            _const_spec((width, 2 * N_STATE)),
            _const_spec((bsz, N_STATE)),
            _const_spec((bsz, N_STATE)),
            _const_spec((2 * N_STATE, width)),
            _const_spec((1, width)),
        ],
        out_specs=blk,
        out_shape=jax.ShapeDtypeStruct(u3.shape, BF16),
        scratch_shapes=[
            pltpu.VMEM((S5_SUB, rows, 2 * N_STATE), F32),
            pltpu.VMEM((bsz, 2 * N_STATE), F32),
        ],
        compiler_params=pltpu.CompilerParams(
            dimension_semantics=("arbitrary",), vmem_limit_bytes=VMEM_LIMIT),
        name="s5_scan",
    )(u3, perm, perm_t, bmat, a_re, a_im, cmat, dvec)


def _merge_kernel(x_ref, oa_ref, gy_ref, g1_ref, wg_ref, woa_ref, wglu_ref, wout_ref,
                  g2_ref, wr_ref, br_ref, x2_ref, pay_ref, cnt_ref):
    n_sub = TM_MERGE // TM_DISP
    logits = [None] * n_sub

    def rows_of(k):
        return slice(k * TM_DISP, (k + 1) * TM_DISP)

    def tail(k):
        _route_rows(logits[k], pay_ref.at[rows_of(k), :], cnt_ref.at[k])

    for k in range(n_sub):
        rs = rows_of(k)
        logits[k] = _merge_rows(x_ref[rs, :], oa_ref[rs, :], gy_ref[rs, :], g1_ref, wg_ref,
                                woa_ref, wglu_ref, wout_ref, g2_ref, wr_ref, br_ref,
                                x2_ref.at[rs, :], pay_ref.at[rs, :])
        if k > 0:
            tail(k - 1)
    tail(n_sub - 1)


def _merge_rows(x, o_in, gy, g1_ref, wg_ref, woa_ref, wglu_ref, wout_ref, g2_ref, wr_ref,
                br_ref, x2_ref, pay_ref):
    ms = jnp.mean(x * x, axis=-1, keepdims=True)
    h = (x * lax.rsqrt(ms + EPS) * g1_ref[...]).astype(BF16)
    o_a = jnp.dot(o_in, woa_ref[...], preferred_element_type=F32)
    z_lin = jnp.dot(gy, wglu_ref[:, 0:D_MODEL], preferred_element_type=F32)
    z_gate = jnp.dot(gy, wglu_ref[:, D_MODEL:2 * D_MODEL], preferred_element_type=F32)
    o_s = z_lin * jax.nn.sigmoid(z_gate)
    gate_a = jax.nn.sigmoid(jnp.dot(h, wg_ref[:, 0:D_MODEL], preferred_element_type=F32))
    merged = gate_a * o_a
    gate_s = jax.nn.sigmoid(
        jnp.dot(h, wg_ref[:, D_MODEL:2 * D_MODEL], preferred_element_type=F32))
    merged = merged + gate_s * o_s
    x2 = x + jnp.dot(merged.astype(BF16), wout_ref[...], preferred_element_type=F32)
    x2_ref[...] = x2

    ms2 = jnp.mean(x2 * x2, axis=-1, keepdims=True)
    h2 = x2 * lax.rsqrt(ms2 + EPS) * g2_ref[...]
    h2_hi = h2.astype(BF16)
    h2_lo = (h2 - h2_hi.astype(F32)).astype(BF16)
    pay_ref[:, 0:D_MODEL] = h2_hi

    both = jnp.dot(h2_hi, wr_ref[...], preferred_element_type=F32)
    return (both[:, 0:LANES] + both[:, LANES:2 * LANES]
            + jnp.dot(h2_lo, wr_ref[:, 0:LANES], preferred_element_type=F32)
            + br_ref[...])


def _route_rows(logits, pay_ref, cnt_ref):
    lane = lax.broadcasted_iota(jnp.int32, logits.shape, 1).astype(F32)
    is_grp = (lane >= N_EXPERTS) & (lane < N_EXPERTS + N_EXPERT_GROUPS)
    gl = jnp.where(is_grp, logits, NEG_BIG)
    gmax = jnp.max(gl, axis=-1, keepdims=True)
    gsum = jnp.sum(jnp.where(is_grp, jnp.exp(gl - gmax), 0.0), axis=-1, keepdims=True)
    p_grp = 1.0 / gsum
    big = float(4 * LANES)
    grp = jnp.min(jnp.where(is_grp & (gl == gmax), lane, big), axis=-1,
                  keepdims=True) - N_EXPERTS
    sel = logits
    for g in range(1, N_EXPERT_GROUPS):
        rolled = pltpu.roll(logits, LANES - g * EXPERTS_PER_GROUP, axis=1)
        sel = jnp.where(grp == g, rolled, sel)
    in_grp = lane < EXPERTS_PER_GROUP
    es = jnp.where(in_grp, sel, NEG_BIG)
    top1 = jnp.max(es, axis=-1, keepdims=True)
    i1 = jnp.min(jnp.where(in_grp & (es == top1), lane, big), axis=-1, keepdims=True)
    es2 = jnp.where(lane == i1, NEG_BIG, es)
    top2 = jnp.max(es2, axis=-1, keepdims=True)
    i2 = jnp.min(jnp.where(in_grp & (lane != i1) & (es2 == top2), lane, big), axis=-1,
                 keepdims=True)
    e2 = jnp.exp(top2 - top1)
    w1 = p_grp / (1.0 + e2)
    w2 = p_grp * e2 / (1.0 + e2)
    e1 = grp * EXPERTS_PER_GROUP + i1
    e2x = grp * EXPERTS_PER_GROUP + i2

    def hi_lo(w):
        hi = w.astype(BF16).astype(F32)
        return hi, (w - hi).astype(BF16).astype(F32)

    w1h, w1l = hi_lo(w1)
    w2h, w2l = hi_lo(w2)
    pay_ref[:, D_MODEL:D_MODEL + LANES] = (
        jnp.where(lane == 0.0, e1, 0.0) + jnp.where(lane == 1.0, e2x, 0.0)
        + jnp.where(lane == 2.0, w1h, 0.0) + jnp.where(lane == 3.0, w1l, 0.0)
        + jnp.where(lane == 4.0, w2h, 0.0) + jnp.where(lane == 5.0, w2l, 0.0)).astype(BF16)

    picked = jnp.where((lane == e1) | (lane == e2x), 1.0, 0.0)
    cnt_ref[...] = jnp.sum(picked, axis=0, keepdims=True)


def _merge(x2d, o_attn, gy, g1, w_gates, w_oa, w_glu, w_out, g2, w_router, b_r):
    t = x2d.shape[0]
    tok = lambda i: (i, 0)
    sub_tiles = TM_MERGE // TM_DISP
    return pl.pallas_call(
        _merge_kernel,
        grid=(t // TM_MERGE,),
        in_specs=[
            pl.BlockSpec((TM_MERGE, D_MODEL), tok),
            pl.BlockSpec((TM_MERGE, ATTN_WIDTH), tok),
            pl.BlockSpec((TM_MERGE, SSM_WIDTH), tok),
            _const_spec((1, D_MODEL)),
            _const_spec((D_MODEL, 2 * D_MODEL)),
            _const_spec((ATTN_WIDTH, D_MODEL)),
            _const_spec((SSM_WIDTH, 2 * D_MODEL)),
            _const_spec((D_MODEL, D_MODEL)),
            _const_spec((1, D_MODEL)),
            _const_spec((D_MODEL, 2 * LANES)),
            _const_spec((1, LANES)),
        ],
        out_specs=[
            pl.BlockSpec((TM_MERGE, D_MODEL), tok),
            pl.BlockSpec((TM_MERGE, XS_COLS), tok),
            pl.BlockSpec((sub_tiles, 1, LANES), lambda i: (i, 0, 0)),
        ],
        out_shape=[
            jax.ShapeDtypeStruct((t, D_MODEL), F32),
            jax.ShapeDtypeStruct((t, XS_COLS), BF16),
            jax.ShapeDtypeStruct((t // TM_DISP, 1, LANES), F32),
        ],
        compiler_params=pltpu.CompilerParams(
            dimension_semantics=("arbitrary",), vmem_limit_bytes=VMEM_LIMIT),
        name="merge",
    )(x2d, o_attn, gy, g1, w_gates, w_oa, w_glu, w_out, g2, w_router, b_r)


def _moe_layout(t):
    ntiles = t // TM_DISP
    max_rows = 2 * t + ntiles * N_EXPERTS * (ROW_ALIGN - 1)
    nsteps = -(-max_rows // TS_MOE) + N_EXPERTS
    return ntiles, nsteps


def _lane_col(arr, lane, k):
    return jnp.sum(jnp.where(lane == k, arr, 0.0), axis=-1, keepdims=True)


def _tile_positions(aux, tri_ref, upper_ref, pos_ref):
    lane = lax.broadcasted_iota(jnp.int32, aux.shape, 1)
    lane_f = lane.astype(F32)
    is1 = lane_f == _lane_col(aux, lane, 0)
    is2 = lane_f == _lane_col(aux, lane, 1)
    picked = jnp.where(is1 | is2, 1.0, 0.0)
    rank_all = jnp.dot(tri_ref[...], picked.astype(BF16), preferred_element_type=F32)
    cnt_row = jnp.sum(picked, axis=0, keepdims=True)
    pad_row = jnp.floor((cnt_row + (ROW_ALIGN - 1)) * (1.0 / ROW_ALIGN)) * ROW_ALIGN
    off = jnp.dot(jnp.broadcast_to(pad_row, (SUBLANES, LANES)).astype(BF16), upper_ref[...],
                  preferred_element_type=F32)[0:1, :]
    posmat = rank_all + off
    pos1 = jnp.sum(jnp.where(is1, posmat, 0.0), axis=-1, keepdims=True)
    pos2 = jnp.sum(jnp.where(is2, posmat, 0.0), axis=-1, keepdims=True)
    pos_ref[...] = jnp.where(lane == 0, pos1, 0.0) + jnp.where(lane == 1, pos2, 0.0)

    def digits(p):
        hi = jnp.floor(p * (1.0 / POS_RADIX))
        return hi, p - POS_RADIX * hi

    d1h, d1l = digits(pos1)
    d2h, d2l = digits(pos2)
    dig = (jnp.where(lane == 0, d1h, 0.0) + jnp.where(lane == 1, d1l, 0.0)
           + jnp.where(lane == 2, d2h, 0.0) + jnp.where(lane == 3, d2l, 0.0)).astype(BF16)
    eye8 = (lax.broadcasted_iota(jnp.int32, (SUBLANES, LANES), 0)
            == lax.broadcasted_iota(jnp.int32, (SUBLANES, LANES), 1)).astype(BF16)
    rows = lax.dot_general(eye8, dig, (((1,), (1,)), ((), ())), preferred_element_type=F32)
    return (POS_RADIX * rows[0:1, :] + rows[1:2, :], POS_RADIX * rows[2:3, :] + rows[3:4, :])


def _compact_tile(payload, p1_row, p2_row):
    sub = lax.broadcasted_iota(jnp.int32, (SORT_ROWS, TM_DISP), 0).astype(F32)
    perm = jnp.where((sub == p1_row) | (sub == p2_row), 1.0, 0.0).astype(BF16)
    return jnp.dot(perm, payload, preferred_element_type=F32).astype(BF16)


def _dispatch_kernel(dst_ref, npiece_ref, tail_ref, ntail_ref, pay_ref, tri_ref, upper_ref,
                     xs_ref, pos_ref, buf_ref, zbuf_ref, sem_ref, tsem_ref):
    i = pl.program_id(0)
    last = pl.num_programs(0) - 1
    cur = i % 2

    def piece(tile, slot, sub, p):
        return pltpu.make_async_copy(buf_ref.at[slot, sub, p],
                                     xs_ref.at[dst_ref[tile * MAX_PIECES + p]],
                                     sem_ref.at[slot, sub])

    def start_all(tile, slot, sub):
        def body(p, c):
            piece(tile, slot, sub, p).start()
            return c
        lax.fori_loop(0, npiece_ref[tile], body, 0)

    def wait_all(tile, slot, sub):
        def body(p, c):
            piece(tile, slot, sub, p).wait()
            return c
        lax.fori_loop(0, npiece_ref[tile], body, 0)

    places = []
    for sub in range(DISP_SUB):
        rs = slice(sub * TM_DISP, (sub + 1) * TM_DISP)
        places.append(_tile_positions(pay_ref[rs, D_MODEL:XS_COLS].astype(F32), tri_ref,
                                      upper_ref, pos_ref.at[rs, :]))
    for sub in range(DISP_SUB):
        rs = slice(sub * TM_DISP, (sub + 1) * TM_DISP)
        buf_ref[cur, sub] = _compact_tile(pay_ref[rs, :], *places[sub]).reshape(
            MAX_PIECES, ROW_ALIGN, XS_COLS)
    for sub in range(DISP_SUB):
        start_all(i * DISP_SUB + sub, cur, sub)

    @pl.when(i > 0)
    def _wait_prev():
        for sub in range(DISP_SUB):
            wait_all((i - 1) * DISP_SUB + sub, 1 - cur, sub)

    @pl.when(i == last)
    def _finish():
        for sub in range(DISP_SUB):
            wait_all(i * DISP_SUB + sub, cur, sub)
        zbuf_ref[...] = jnp.zeros(zbuf_ref.shape, BF16)

        def tail(p):
            return pltpu.make_async_copy(zbuf_ref, xs_ref.at[tail_ref[p]], tsem_ref.at[0])

        def start_tail(p, c):
            tail(p).start()
            return c

        def wait_tail(p, c):
            tail(p).wait()
            return c

        lax.fori_loop(0, ntail_ref[0], start_tail, 0)
        lax.fori_loop(0, ntail_ref[0], wait_tail, 0)


def _dispatch(payload, dst, npiece, tail, ntail, rows):
    t = payload.shape[0]
    tm = TM_DISP * DISP_SUB
    tok = lambda i, *_: (i, 0)
    const = lambda i, *_: (0, 0)
    tri = (jnp.arange(TM_DISP)[:, None] > jnp.arange(TM_DISP)[None, :]).astype(BF16)
    upper = (jnp.arange(LANES)[:, None] < jnp.arange(LANES)[None, :]).astype(BF16)
    grid_spec = pltpu.PrefetchScalarGridSpec(
        num_scalar_prefetch=4,
        grid=(t // tm,),
        in_specs=[
            pl.BlockSpec((tm, XS_COLS), tok),
            pl.BlockSpec((TM_DISP, TM_DISP), const),
            pl.BlockSpec((LANES, LANES), const),
        ],
        out_specs=[
            pl.BlockSpec(memory_space=pl.ANY),
            pl.BlockSpec((tm, LANES), tok),
        ],
        scratch_shapes=[
            pltpu.VMEM((2, DISP_SUB, MAX_PIECES, ROW_ALIGN, XS_COLS), BF16),
            pltpu.VMEM((ROW_ALIGN, XS_COLS), BF16),
            pltpu.SemaphoreType.DMA((2, DISP_SUB)),
            pltpu.SemaphoreType.DMA((1,)),
        ],
    )
    xs, pos = pl.pallas_call(
        _dispatch_kernel,
        grid_spec=grid_spec,
        out_shape=[
            jax.ShapeDtypeStruct((rows // ROW_ALIGN, ROW_ALIGN, XS_COLS), BF16),
            jax.ShapeDtypeStruct((t, LANES), F32),
        ],
        compiler_params=pltpu.CompilerParams(
            dimension_semantics=("arbitrary",), vmem_limit_bytes=VMEM_LIMIT),
        name="dispatch",
    )(dst // ROW_ALIGN, npiece, tail // ROW_ALIGN, ntail, payload, tri, upper)
    return xs.reshape(rows, XS_COLS), pos


def _moe_kernel(blk_ref, exp_ref, nvalid_ref, fresh_ref, xs_ref, wg_ref, wu_ref, wd_ref,
                ys_ref, wgb_ref, wub_ref, wdb_ref):
    s = pl.program_id(0)

    @pl.when(fresh_ref[s] == 1)
    def _cast_weights():
        wgb_ref[...] = wg_ref[...].astype(BF16)
        wub_ref[...] = wu_ref[...].astype(BF16)
        wdb_ref[...] = wd_ref[...].astype(BF16)

    @pl.when(s < nvalid_ref[0])
    def _compute():
        t = xs_ref[:, 0:D_MODEL]
        aux = xs_ref[:, D_MODEL:XS_COLS].astype(F32)
        lane = lax.broadcasted_iota(jnp.int32, aux.shape, 1)
        first = _lane_col(aux, lane, 0) == exp_ref[s].astype(F32)
        w = jnp.where(first, _lane_col(aux, lane, 2) + _lane_col(aux, lane, 3),
                      _lane_col(aux, lane, 4) + _lane_col(aux, lane, 5))
        hg = jnp.dot(t, wgb_ref[...], preferred_element_type=F32)
        hu = jnp.dot(t, wub_ref[...], preferred_element_type=F32)
        a = (hg * jax.nn.sigmoid(hg) * hu * w).astype(BF16)
        ys_ref[...] = jnp.dot(a, wdb_ref[...], preferred_element_type=F32).astype(BF16)

    @pl.when(s == nvalid_ref[0])
    def _spare():
        ys_ref[...] = jnp.zeros(ys_ref.shape, BF16)


def _moe(xs, blk, exp, nvalid, fresh, wg, wu, wd):
    row = lambda s, blk, exp, nv, fr: (blk[s], 0)
    wsel = lambda s, blk, exp, nv, fr: (exp[s], 0, 0)
    grid_spec = pltpu.PrefetchScalarGridSpec(
        num_scalar_prefetch=4,
        grid=(blk.shape[0],),
        in_specs=[
            pl.BlockSpec((TS_MOE, XS_COLS), row),
            pl.BlockSpec((None, D_MODEL, D_EXPERT), wsel),
            pl.BlockSpec((None, D_MODEL, D_EXPERT), wsel),
            pl.BlockSpec((None, D_EXPERT, D_MODEL), wsel),
        ],
        out_specs=pl.BlockSpec((TS_MOE, D_MODEL), row),
        scratch_shapes=[
            pltpu.VMEM((D_MODEL, D_EXPERT), BF16),
            pltpu.VMEM((D_MODEL, D_EXPERT), BF16),
            pltpu.VMEM((D_EXPERT, D_MODEL), BF16),
        ],
    )
    return pl.pallas_call(
        _moe_kernel,
        grid_spec=grid_spec,
        out_shape=jax.ShapeDtypeStruct((xs.shape[0], D_MODEL), BF16),
        compiler_params=pltpu.CompilerParams(
            dimension_semantics=("arbitrary",), vmem_limit_bytes=VMEM_LIMIT),
        name="moe",
    )(blk, exp, nvalid, fresh, xs, wg, wu, wd)


def _combine_kernel(dst_ref, npiece_ref, x2_ref, pos_ref, ys_ref, o_ref, stage_ref, sem_ref):
    i = pl.program_id(0)
    n = pl.num_programs(0)
    cur = i % 2
    tm = TM_DISP

    def piece(tile, slot, sub, p):
        row = pl.multiple_of(p * ROW_ALIGN, ROW_ALIGN)
        src = pl.multiple_of(dst_ref[tile * MAX_PIECES + p], ROW_ALIGN)
        return pltpu.make_async_copy(ys_ref.at[pl.ds(src, ROW_ALIGN)],
                                     stage_ref.at[slot, sub, pl.ds(row, ROW_ALIGN)],
                                     sem_ref.at[slot, sub])

    def start_step(step, slot):
        for sub in range(DISP_SUB):
            tile = step * DISP_SUB + sub

            def body(p, c, tile=tile, sub=sub):
                piece(tile, slot, sub, p).start()
                return c
            lax.fori_loop(0, npiece_ref[tile], body, 0)

    @pl.when(i == 0)
    def _first():
        stage_ref[...] = jnp.zeros(stage_ref.shape, BF16)
        start_step(0, 0)

    @pl.when(i + 1 < n)
    def _prefetch():
        start_step(i + 1, 1 - cur)

    for sub in range(DISP_SUB):
        tile = i * DISP_SUB + sub

        def wait_body(p, c, tile=tile, sub=sub):
            piece(tile, cur, sub, p).wait()
            return c

        lax.fori_loop(0, npiece_ref[tile], wait_body, 0)

    for sub in range(DISP_SUB):
        rs = slice(sub * tm, (sub + 1) * tm)
        pos = pos_ref[rs, :]
        lane128 = lax.broadcasted_iota(jnp.int32, pos.shape, 1)
        p1 = _lane_col(pos, lane128, 0)
        p2 = _lane_col(pos, lane128, 1)
        lane = lax.broadcasted_iota(jnp.int32, (tm, SORT_ROWS), 1).astype(F32)
        pick = jnp.where((lane == p1) | (lane == p2), 1.0, 0.0).astype(BF16)
        o_ref[rs, :] = x2_ref[rs, :] + jnp.dot(pick, stage_ref[cur, sub],
                                               preferred_element_type=F32)


def _combine(x2, pos, ys, dst, npiece):
    t = x2.shape[0]
    tm = TM_DISP * DISP_SUB
    tok = lambda i, *_: (i, 0)
    grid_spec = pltpu.PrefetchScalarGridSpec(
        num_scalar_prefetch=2,
        grid=(t // tm,),
        in_specs=[
            pl.BlockSpec((tm, D_MODEL), tok),
            pl.BlockSpec((tm, LANES), tok),
            pl.BlockSpec(memory_space=pl.ANY),
        ],
        out_specs=pl.BlockSpec((tm, D_MODEL), tok),
        scratch_shapes=[
            pltpu.VMEM((2, DISP_SUB, SORT_ROWS, D_MODEL), BF16),
            pltpu.SemaphoreType.DMA((2, DISP_SUB)),
        ],
    )
    return pl.pallas_call(
        _combine_kernel,
        grid_spec=grid_spec,
        out_shape=jax.ShapeDtypeStruct((t, D_MODEL), F32),
        compiler_params=pltpu.CompilerParams(
            dimension_semantics=("arbitrary",), vmem_limit_bytes=VMEM_LIMIT),
        name="combine",
    )(dst, npiece, x2, pos, ys)


def _sparse_moe(payload, cnt, x2, wg, wu, wd):
    t = payload.shape[0]
    ntiles, nsteps = _moe_layout(t)
    rows = (nsteps + 1) * TS_MOE
    i32 = jnp.int32
    counts = cnt[:, 0, :N_EXPERTS].astype(i32)
    padded = (counts + (ROW_ALIGN - 1)) // ROW_ALIGN * ROW_ALIGN
    total = jnp.sum(padded, axis=0)
    reg_rows = (total + (TS_MOE - 1)) // TS_MOE * TS_MOE
    region = jnp.cumsum(reg_rows) - reg_rows
    base = region[None, :] + jnp.cumsum(padded, axis=0) - padded
    np_e = padded // ROW_ALIGN
    cum = jnp.cumsum(np_e, axis=1)
    npiece = cum[:, -1].astype(i32)
    p = jnp.arange(MAX_PIECES, dtype=i32)
    owns = (p[None, :, None] >= (cum - np_e)[:, None, :]) & (p[None, :, None] < cum[:, None, :])
    piece_dst = base[:, None, :] + (p[None, :, None] - (cum - np_e)[:, None, :]) * ROW_ALIGN
    dst = jnp.sum(jnp.where(owns, piece_dst, 0), axis=2).reshape(-1).astype(i32)
    nt_e = (reg_rows - total) // ROW_ALIGN
    cumt = jnp.cumsum(nt_e)
    q = jnp.arange(N_EXPERTS * (TS_MOE // ROW_ALIGN), dtype=i32)
    owns_q = (q[:, None] >= (cumt - nt_e)[None, :]) & (q[:, None] < cumt[None, :])
    tail_dst = (region + total)[None, :] + (q[:, None] - (cumt - nt_e)[None, :]) * ROW_ALIGN
    tail = jnp.sum(jnp.where(owns_q, tail_dst, 0), axis=1).astype(i32)
    ntail = cumt[-1:].astype(i32)
    ntile_e = reg_rows // TS_MOE
    first_t = jnp.cumsum(ntile_e) - ntile_e
    nvalid = jnp.sum(ntile_e).reshape(1).astype(i32)
    s = jnp.arange(nsteps, dtype=i32)
    owns_s = (s[:, None] >= first_t[None, :]) & (s[:, None] < (first_t + ntile_e)[None, :])
    spare = s >= nvalid[0]
    fresh = jnp.any(owns_s & (s[:, None] == first_t[None, :]), axis=1).astype(i32)
    exp = jnp.where(spare, N_EXPERTS - 1, jnp.sum(
        jnp.where(owns_s, jnp.arange(N_EXPERTS, dtype=i32)[None, :], 0), axis=1)).astype(i32)
    blk = jnp.where(spare, nsteps, jnp.sum(
        jnp.where(owns_s, (region // TS_MOE - first_t)[None, :] + s[:, None], 0),
        axis=1)).astype(i32)
    xs, pos = _dispatch(payload, dst, npiece, tail, ntail, rows)
    ys = _moe(xs, blk, exp, nvalid, fresh, wg, wu, wd)
    return _combine(x2, pos, ys, dst, npiece)


def _rope_tables(positions):
    inv = ROPE_THETA ** (-jnp.arange(0, ROPE_DIM, 2, dtype=F32) / ROPE_DIM)
    ang = positions.astype(F32).reshape(-1, 1) * inv
    cos, sin = jnp.cos(ang), jnp.sin(ang)
    d = jnp.arange(LANES) % HEAD_DIM
    in_rope = d < ROPE_DIM
    place = (((d % ROPE_HALF)[None, :] == jnp.arange(ROPE_HALF)[:, None])
             & in_rope[None, :]).astype(F32)
    sign = jnp.where(d < ROPE_HALF, -1.0, 1.0)
    hp = lax.Precision.HIGHEST
    cos_t = jnp.dot(cos, place, precision=hp) + (1.0 - in_rope.astype(F32))
    sin_t = jnp.dot(sin, place * sign, precision=hp)
    return cos_t, sin_t


def _s5_params(lam_re, lam_im, log_dt, b_re, b_im, c_re, c_im, bsz):
    dt = jnp.exp(log_dt)[:, None]
    mag = jnp.exp(lam_re * dt)
    lb_re = mag * jnp.cos(lam_im * dt)
    lb_im = mag * jnp.sin(lam_im * dt)
    den = lam_re * lam_re + lam_im * lam_im
    k_re = ((lb_re - 1.0) * lam_re + lb_im * lam_im) / den
    k_im = (lb_im * lam_re - (lb_re - 1.0) * lam_im) / den
    bb_re = k_re[..., None] * b_re - k_im[..., None] * b_im
    bb_im = k_re[..., None] * b_im + k_im[..., None] * b_re
    eye = jnp.eye(SSM_GROUPS, dtype=F32)
    blk_b = lambda m: jnp.einsum('gph,gk->ghkp', m, eye).reshape(SSM_WIDTH, N_STATE)
    blk_c = lambda m: jnp.einsum('ghp,gk->gpkh', m, eye).reshape(N_STATE, SSM_WIDTH)
    bmat = jnp.concatenate([blk_b(bb_re), blk_b(bb_im)], axis=1).astype(BF16)
    cmat = jnp.concatenate([blk_c(c_re), blk_c(-c_im)], axis=0).astype(BF16)
    a_re = jnp.broadcast_to(lb_re.reshape(1, N_STATE), (bsz, N_STATE))
    a_im = jnp.broadcast_to(lb_im.reshape(1, N_STATE), (bsz, N_STATE))
    return bmat, a_re, a_im, cmat


def _time_major_perm(bsz, chunk):
    r = jnp.arange(bsz * chunk)
    src = (r % bsz) * chunk + r // bsz
    perm = (src[:, None] == jnp.arange(bsz * chunk)[None, :]).astype(BF16)
    return perm, perm.T


def kernel(x, positions, norm_mix_g, w_in, q_norm_g, k_norm_g, lambda_q1, lambda_k1, lambda_q2, lambda_k2, subln_g, w_o_attn, ssm_lambda_re, ssm_lambda_im, ssm_log_dt, ssm_b_re, ssm_b_im, ssm_c_re, ssm_c_im, ssm_d, w_glu, w_out, norm_ffn_g, w_router_group, b_router_group, w_router_expert, b_router_expert, w_expert_gate, w_expert_up, w_expert_down):
    bsz, seq, _ = x.shape
    assert bsz == SUBLANES and seq % (2 * TQ) == 0 and seq % (S5_SUB * SSM_CHUNK) == 0
    assert norm_mix_g.shape[0] == 1
    t = bsz * seq
    x2d = x.reshape(t, D_MODEL)
    l = 0

    cos_t, sin_t = _rope_tables(positions)
    w_qkvu = w_in[l][:, :QKVU_COLS].astype(BF16)
    w_gates = w_in[l][:, QKVU_COLS:].astype(BF16)
    qg = jnp.tile(q_norm_g[l].reshape(1, HEAD_DIM), (1, 2))
    kg = jnp.tile(k_norm_g[l].reshape(1, HEAD_DIM), (1, 2))
    q, k, v, u = _in_proj(x2d, norm_mix_g[l].reshape(1, D_MODEL), w_qkvu, cos_t, sin_t, qg, kg)

    lam = (jnp.exp(jnp.sum(lambda_q1[l] * lambda_k1[l]))
           - jnp.exp(jnp.sum(lambda_q2[l] * lambda_k2[l])) + LAM_INIT).reshape(1)
    o_attn = _diff_attn(q, k, v, lam, subln_g[l].reshape(1, V_DIM), bsz, seq)

    bmat, a_re, a_im, cmat = _s5_params(
        ssm_lambda_re[l], ssm_lambda_im[l], ssm_log_dt[l], ssm_b_re[l], ssm_b_im[l],
        ssm_c_re[l], ssm_c_im[l], bsz)
    perm, perm_t = _time_major_perm(bsz, SSM_CHUNK)
    gy = _s5_scan(u.reshape(bsz, seq, SSM_WIDTH), perm, perm_t, bmat, a_re, a_im, cmat,
                  ssm_d[l].reshape(1, SSM_WIDTH)).reshape(t, SSM_WIDTH)

    w_r = jnp.concatenate(
        [w_router_expert[l].reshape(D_MODEL, N_EXPERTS), w_router_group[l],
         jnp.zeros((D_MODEL, LANES - N_EXPERTS - N_EXPERT_GROUPS), F32)], axis=1)
    b_r = jnp.concatenate(
        [b_router_expert[l].reshape(N_EXPERTS), b_router_group[l],
         jnp.zeros((LANES - N_EXPERTS - N_EXPERT_GROUPS,), F32)]).reshape(1, LANES)
    wr_hi = w_r.astype(BF16)
    wr_lo = (w_r - wr_hi.astype(F32)).astype(BF16)
    x2, payload, cnt = _merge(
        x2d, o_attn, gy, norm_mix_g[l].reshape(1, D_MODEL), w_gates,
        w_o_attn[l].astype(BF16), w_glu[l].astype(BF16), w_out[l].astype(BF16),
        norm_ffn_g[l].reshape(1, D_MODEL), jnp.concatenate([wr_hi, wr_lo], axis=1), b_r)

    wg = w_expert_gate[l].reshape(N_EXPERTS, D_MODEL, D_EXPERT)
    wu = w_expert_up[l].reshape(N_EXPERTS, D_MODEL, D_EXPERT)
    wd = w_expert_down[l].reshape(N_EXPERTS, D_EXPERT, D_MODEL)
    out = _sparse_moe(payload, cnt, x2, wg, wu, wd)
    return out.reshape(bsz, seq, D_MODEL)
```

```python
import math

import jax
import jax.numpy as jnp
from jax import lax
from jax.experimental import pallas as pl
from jax.experimental.pallas import tpu as pltpu

F32 = jnp.float32
BF16 = jnp.bfloat16

D_MODEL = 1024
N_HEADS = 4
HEAD_DIM = 64
V_DIM = 2 * HEAD_DIM
ATTN_WIDTH = N_HEADS * V_DIM
ROPE_THETA = 500000.0
ROPE_DIM = HEAD_DIM // 4
ROPE_HALF = ROPE_DIM // 2
SSM_WIDTH = D_MODEL // 2
SSM_GROUP = 16
SSM_GROUPS = SSM_WIDTH // SSM_GROUP
SSM_STATE = 64
N_STATE = SSM_GROUPS * SSM_STATE
N_EXPERT_GROUPS = 4
EXPERTS_PER_GROUP = 8
N_EXPERTS = N_EXPERT_GROUPS * EXPERTS_PER_GROUP
D_EXPERT = D_MODEL // 4
EPS = 1e-6
LAM_INIT = 0.8 - 0.6 * math.exp(-0.3 * 0)
QKVU_COLS = 4 * ATTN_WIDTH
LANES = 128
SUBLANES = 8
NEG_BIG = -1e30
LOG2_E = math.log2(math.e)

VMEM_LIMIT = 48 * 1024 * 1024

TM_IN = 1024
TQ = 512
SSM_CHUNK = 64
S5_SUB = 4
SCAN_LANES = 512
TM_MERGE = 1024
TM_DISP = 256
DISP_SUB = 4
TS_MOE = 1024
ROW_ALIGN = 16
XS_COLS = D_MODEL + LANES
SORT_ROWS = 2 * TM_DISP + N_EXPERTS * ROW_ALIGN
MAX_PIECES = SORT_ROWS // ROW_ALIGN
POS_RADIX = 32


def _const_spec(shape):
    return pl.BlockSpec(shape, lambda *_: (0,) * len(shape))


def _in_proj_kernel(x_ref, g_ref, w_ref, cs_ref, qg_ref, kg_ref,
                    q_ref, k_ref, v_ref, u_ref):
    x = x_ref[...]
    ms = jnp.mean(x * x, axis=-1, keepdims=True)
    h = (x * lax.rsqrt(ms + EPS) * g_ref[...]).astype(BF16)
    lane = lax.broadcasted_iota(jnp.int32, (x.shape[0], LANES), 1)
    low_half = (lane % HEAD_DIM) < ROPE_HALF
    in_rope = (lane % HEAD_DIM) < ROPE_DIM
    freq = lane % ROPE_HALF
    cs = jnp.concatenate(
        [cs_ref[...], jnp.zeros((x.shape[0], LANES - 2 * ROPE_HALF), F32)], axis=1)
    cos_t = jnp.where(in_rope, jnp.take_along_axis(cs, freq, axis=1), 1.0)
    sin_g = jnp.take_along_axis(cs, freq + ROPE_HALF, axis=1)
    sin_t = jnp.where(in_rope, jnp.where(low_half, -sin_g, sin_g), 0.0)
    same_comp = (lax.broadcasted_iota(jnp.int32, (LANES, LANES), 0) // HEAD_DIM
                 == lax.broadcasted_iota(jnp.int32, (LANES, LANES), 1) // HEAD_DIM
                 ).astype(BF16)

    def norm_rope(blk, gain, scale):
        ssum = jnp.dot((blk * blk).astype(BF16), same_comp, preferred_element_type=F32)
        nb = blk * lax.rsqrt(ssum * (1.0 / HEAD_DIM) + EPS) * gain
        up = pltpu.roll(nb, LANES - ROPE_HALF, axis=1)
        dn = pltpu.roll(nb, ROPE_HALF, axis=1)
        partner = jnp.where(low_half, up, dn)
        return ((nb * cos_t + partner * sin_t) * scale).astype(BF16)

    for j in range(0, N_HEADS, 2):
        sl2 = slice(j * LANES, (j + 2) * LANES)
        qb = jnp.dot(h, w_ref[:, sl2], preferred_element_type=F32)
        kb = jnp.dot(h, w_ref[:, ATTN_WIDTH + j * LANES:ATTN_WIDTH + (j + 2) * LANES],
                     preferred_element_type=F32)
        for jj in range(2):
            sl = slice((j + jj) * LANES, (j + jj + 1) * LANES)
            half = slice(jj * LANES, (jj + 1) * LANES)
            q_ref[:, sl] = norm_rope(qb[:, half], qg_ref[...], LOG2_E * HEAD_DIM ** -0.5)
            k_ref[:, sl] = norm_rope(kb[:, half], kg_ref[...], 1.0)
    v_ref[...] = jnp.dot(h, w_ref[:, 2 * ATTN_WIDTH:3 * ATTN_WIDTH],
                         preferred_element_type=F32).astype(BF16)
    u_ref[...] = jnp.dot(h, w_ref[:, 3 * ATTN_WIDTH:4 * ATTN_WIDTH],
                         preferred_element_type=F32).astype(BF16)


def _in_proj(x2d, g, w, cos_sin, qg, kg):
    t = x2d.shape[0]
    tok = lambda i: (i, 0)
    out = jax.ShapeDtypeStruct((t, ATTN_WIDTH), BF16)
    return pl.pallas_call(
        _in_proj_kernel,
        grid=(t // TM_IN,),
        in_specs=[
            pl.BlockSpec((TM_IN, D_MODEL), tok),
            _const_spec((1, D_MODEL)),
            _const_spec((D_MODEL, QKVU_COLS)),
            pl.BlockSpec((TM_IN, 2 * ROPE_HALF), tok),
            _const_spec((1, LANES)),
            _const_spec((1, LANES)),
        ],
        out_specs=[pl.BlockSpec((TM_IN, ATTN_WIDTH), tok)] * 4,
        out_shape=[out] * 4,
        compiler_params=pltpu.CompilerParams(
            dimension_semantics=("arbitrary",), vmem_limit_bytes=VMEM_LIMIT),
        name="in_proj",
    )(x2d, g, w, cos_sin, qg, kg)


def _attn_kernel(qi_ref, kp_ref, mode_ref, lam_ref, q_ref, k_ref, v_ref, sg_ref, o_ref,
                 qs_ref, m_ref, l_ref, acc_ref):
    step = pl.program_id(1)
    kp = kp_ref[step]
    mode = mode_ref[step]

    hq = TQ // 2

    @pl.when(kp == 0)
    def _init():
        for h in range(N_HEADS):
            q = q_ref[:, h * LANES:(h + 1) * LANES]
            lane = lax.broadcasted_iota(jnp.int32, q.shape, 1)
            zero = jnp.zeros_like(q)
            only0 = jnp.where(lane < HEAD_DIM, q, zero)
            only1 = jnp.where(lane < HEAD_DIM, zero, q)
            for half in range(2):
                rows = slice(half * hq, (half + 1) * hq)
                qs_ref[h, 2 * half * hq:(2 * half + 1) * hq, :] = only0[rows, :]
                qs_ref[h, (2 * half + 1) * hq:(2 * half + 2) * hq, :] = only1[rows, :]
        m_ref[...] = jnp.full(m_ref.shape, NEG_BIG, F32)
        l_ref[...] = jnp.zeros(l_ref.shape, F32)
        acc_ref[...] = jnp.zeros(acc_ref.shape, F32)

    def update(h, rows, key0, nkeys, diag_offset=None):
        hs = slice(h * LANES, (h + 1) * LANES)
        ks = slice(key0, key0 + nkeys)
        s = lax.dot_general(qs_ref[h, rows, :], k_ref[ks, hs], (((1,), (1,)), ((), ())),
                            preferred_element_type=F32)
        if diag_offset is not None:
            row = lax.broadcasted_iota(jnp.int32, s.shape, 0)
            row = row % hq + row // TQ * hq + diag_offset
            col = lax.broadcasted_iota(jnp.int32, s.shape, 1)
            s = jnp.where(col <= row, s, NEG_BIG)
        m_old = m_ref[h, rows, :]
        m_new = jnp.maximum(m_old, jnp.max(s, axis=-1, keepdims=True))
        alpha = jnp.exp2(m_old - m_new)
        p = jnp.exp2(s - jnp.concatenate([m_new] * (nkeys // LANES), axis=1))
        v_ones = jnp.concatenate([v_ref[ks, hs], jnp.ones((nkeys, LANES), BF16)], axis=1)
        pv = jnp.dot(p.astype(BF16), v_ones, preferred_element_type=F32)
        l_ref[h, rows, :] = alpha * l_ref[h, rows, :] + pv[:, LANES:2 * LANES]
        acc_ref[h, rows, :] = alpha * acc_ref[h, rows, :] + pv[:, 0:LANES]
        m_ref[h, rows, :] = m_new

    all_rows = slice(0, 2 * TQ)

    def update_diag(h, key0):
        update(h, slice(0, TQ), key0, hq, diag_offset=0)
        update(h, slice(TQ, 2 * TQ), key0, TQ, diag_offset=hq)

    def finish(h):
        o = acc_ref[h] / l_ref[h]
        d = jnp.concatenate(
            [o[2 * half * hq:(2 * half + 1) * hq, :]
             - lam_ref[0] * o[(2 * half + 1) * hq:(2 * half + 2) * hq, :]
             for half in range(2)], axis=0)
        ms = jnp.mean(d * d, axis=-1, keepdims=True)
        d = d * lax.rsqrt(ms + EPS) * sg_ref[...] * (1.0 - LAM_INIT)
        o_ref[:, h * LANES:(h + 1) * LANES] = d.astype(BF16)

    @pl.when(mode == 0)
    def _below():
        for h in range(N_HEADS):
            update(h, all_rows, 0, TQ)
            update(h, all_rows, TQ, TQ)

    @pl.when(mode == 1)
    def _below_then_diag():
        for h in range(N_HEADS):
            update(h, all_rows, 0, TQ)
            update(h, all_rows, TQ, TQ, diag_offset=0)
            finish(h)

    @pl.when(mode == 2)
    def _diag_only():
        for h in range(N_HEADS):
            update_diag(h, 0)
            finish(h)


def _diff_attn(q, k, v, lam, subln_g, bsz, seq):
    nq = seq // TQ
    steps = []
    for i in range(nq):
        for p in range(i // 2 + 1):
            mode = 0 if 2 * p + 1 < i else (1 if 2 * p + 1 == i else 2)
            steps.append((i, p, mode))
    qi = jnp.asarray([s[0] for s in steps], jnp.int32)
    kp = jnp.asarray([s[1] for s in steps], jnp.int32)
    mode = jnp.asarray([s[2] for s in steps], jnp.int32)
    q_map = lambda b, s, qi, kp, mode: (b * nq + qi[s], 0)
    k_map = lambda b, s, qi, kp, mode: (b * (nq // 2) + kp[s], 0)
    grid_spec = pltpu.PrefetchScalarGridSpec(
        num_scalar_prefetch=3,
        grid=(bsz, len(steps)),
        in_specs=[
            pl.BlockSpec(memory_space=pltpu.SMEM),
            pl.BlockSpec((TQ, ATTN_WIDTH), q_map),
            pl.BlockSpec((2 * TQ, ATTN_WIDTH), k_map),
            pl.BlockSpec((2 * TQ, ATTN_WIDTH), k_map),
            pl.BlockSpec((1, LANES), lambda b, s, qi, kp, mode: (0, 0)),
        ],
        out_specs=pl.BlockSpec((TQ, ATTN_WIDTH), q_map),
        scratch_shapes=[
            pltpu.VMEM((N_HEADS, 2 * TQ, LANES), BF16),
            pltpu.VMEM((N_HEADS, 2 * TQ, LANES), F32),
            pltpu.VMEM((N_HEADS, 2 * TQ, LANES), F32),
            pltpu.VMEM((N_HEADS, 2 * TQ, LANES), F32),
        ],
    )
    return pl.pallas_call(
        _attn_kernel,
        grid_spec=grid_spec,
        out_shape=jax.ShapeDtypeStruct((bsz * seq, ATTN_WIDTH), BF16),
        compiler_params=pltpu.CompilerParams(
            dimension_semantics=("arbitrary", "arbitrary"),
            vmem_limit_bytes=VMEM_LIMIT),
        name="diff_attn",
    )(qi, kp, mode, lam, q, k, v, subln_g)


def _gelu_tanh(x):
    c = math.sqrt(2.0 / math.pi)
    return 0.5 * x * (1.0 + jnp.tanh(c * (x + 0.044715 * (x * x * x))))


def _s5_kernel(u_ref, p_ref, pt_ref, b_ref, are_ref, aim_ref, c_ref, d_ref, o_ref,
               st_ref, state_ref):
    bsz, _, width = u_ref.shape
    chunk = SSM_CHUNK
    rows = bsz * chunk
    n_ch = N_STATE // SCAN_LANES

    @pl.when(pl.program_id(0) == 0)
    def _init():
        state_ref[...] = jnp.zeros(state_ref.shape, F32)

    slab = SCAN_LANES // SSM_STATE * SSM_GROUP
    state = [(state_ref[:, ch * SCAN_LANES:(ch + 1) * SCAN_LANES],
              state_ref[:, N_STATE + ch * SCAN_LANES:N_STATE + (ch + 1) * SCAN_LANES])
             for ch in range(n_ch)]

    for sub in range(S5_SUB):
        st = st_ref.at[sub]
        u_bt = u_ref[:, sub * chunk:(sub + 1) * chunk, :].reshape(rows, width)
        u_tm = jnp.dot(p_ref[...], u_bt, preferred_element_type=F32)
        u_tm_b = u_tm.astype(BF16)
        for ch in range(n_ch):
            re = slice(ch * SCAN_LANES, (ch + 1) * SCAN_LANES)
            im = slice(N_STATE + ch * SCAN_LANES, N_STATE + (ch + 1) * SCAN_LANES)
            cs = slice(ch * slab, (ch + 1) * slab)
            st[:, re] = jnp.dot(u_tm_b[:, cs], b_ref[cs, re], preferred_element_type=F32)
            st[:, im] = jnp.dot(u_tm_b[:, cs], b_ref[cs, im], preferred_element_type=F32)
        g_parts = []
        for ch in range(n_ch):
            re = slice(ch * SCAN_LANES, (ch + 1) * SCAN_LANES)
            im = slice(N_STATE + ch * SCAN_LANES, N_STATE + (ch + 1) * SCAN_LANES)
            cs = slice(ch * slab, (ch + 1) * slab)
            a_re = are_ref[:, re]
            a_im = aim_ref[:, re]

            def body(t, carry, st=st, re=re, im=im, a_re=a_re, a_im=a_im):
                s_re, s_im = carry
                r0 = pl.multiple_of(t * SUBLANES, SUBLANES)
                n_re = a_re * s_re - a_im * s_im + st[pl.ds(r0, SUBLANES), re]
                n_im = a_re * s_im + a_im * s_re + st[pl.ds(r0, SUBLANES), im]
                st[pl.ds(r0, SUBLANES), re] = n_re
                st[pl.ds(r0, SUBLANES), im] = n_im
                return n_re, n_im

            state[ch] = lax.fori_loop(0, chunk, body, state[ch], unroll=True)
            y = (jnp.dot(st[:, re].astype(BF16), c_ref[re, cs], preferred_element_type=F32)
                 + jnp.dot(st[:, im].astype(BF16), c_ref[im, cs], preferred_element_type=F32))
            y = y + d_ref[:, cs] * u_tm[:, cs]
            g_parts.append(_gelu_tanh(y).astype(BF16))

        g_tm = jnp.concatenate(g_parts, axis=1)
        g_bt = jnp.dot(pt_ref[...], g_tm, preferred_element_type=F32)
        o_ref[:, sub * chunk:(sub + 1) * chunk, :] = g_bt.astype(BF16).reshape(
            bsz, chunk, width)

    for ch in range(n_ch):
        state_ref[:, ch * SCAN_LANES:(ch + 1) * SCAN_LANES] = state[ch][0]
        state_ref[:, N_STATE + ch * SCAN_LANES:N_STATE + (ch + 1) * SCAN_LANES] = state[ch][1]


def _s5_scan(u3, perm, perm_t, bmat, a_re, a_im, cmat, dvec):
    bsz, seq, width = u3.shape
    rows = bsz * SSM_CHUNK
    blk = pl.BlockSpec((bsz, S5_SUB * SSM_CHUNK, width), lambda c: (0, c, 0))
    return pl.pallas_call(
        _s5_kernel,
        grid=(seq // (S5_SUB * SSM_CHUNK),),
        in_specs=[
            blk,
            _const_spec((rows, rows)),
            _const_spec((rows, rows)),
            _const_spec((width, 2 * N_STATE)),
            _const_spec((bsz, N_STATE)),
            _const_spec((bsz, N_STATE)),
            _const_spec((2 * N_STATE, width)),
            _const_spec((1, width)),
        ],
        out_specs=blk,
        out_shape=jax.ShapeDtypeStruct(u3.shape, BF16),
        scratch_shapes=[
            pltpu.VMEM((S5_SUB, rows, 2 * N_STATE), F32),
            pltpu.VMEM((bsz, 2 * N_STATE), F32),
        ],
        compiler_params=pltpu.CompilerParams(
            dimension_semantics=("arbitrary",), vmem_limit_bytes=VMEM_LIMIT),
        name="s5_scan",
    )(u3, perm, perm_t, bmat, a_re, a_im, cmat, dvec)


def _merge_kernel(x_ref, oa_ref, gy_ref, g1_ref, wg_ref, woa_ref, wglu_ref, wout_ref,
                  g2_ref, wr_ref, br_ref, x2_ref, pay_ref, cnt_ref):
    n_sub = TM_MERGE // TM_DISP
    logits = [None] * n_sub

    def rows_of(k):
        return slice(k * TM_DISP, (k + 1) * TM_DISP)

    def tail(k):
        _route_rows(logits[k], pay_ref.at[rows_of(k), :], cnt_ref.at[k])

    for k in range(n_sub):
        rs = rows_of(k)
        logits[k] = _merge_rows(x_ref[rs, :], oa_ref[rs, :], gy_ref[rs, :], g1_ref, wg_ref,
                                woa_ref, wglu_ref, wout_ref, g2_ref, wr_ref, br_ref,
                                x2_ref.at[rs, :], pay_ref.at[rs, :])
        if k > 0:
            tail(k - 1)
    tail(n_sub - 1)


def _merge_rows(x, o_in, gy, g1_ref, wg_ref, woa_ref, wglu_ref, wout_ref, g2_ref, wr_ref,
                br_ref, x2_ref, pay_ref):
    ms = jnp.mean(x * x, axis=-1, keepdims=True)
    h = (x * lax.rsqrt(ms + EPS) * g1_ref[...]).astype(BF16)
    o_a = jnp.dot(o_in, woa_ref[...], preferred_element_type=F32)
    z_lin = jnp.dot(gy, wglu_ref[:, 0:D_MODEL], preferred_element_type=F32)
    z_gate = jnp.dot(gy, wglu_ref[:, D_MODEL:2 * D_MODEL], preferred_element_type=F32)
    o_s = z_lin * jax.nn.sigmoid(z_gate)
    gate_a = jax.nn.sigmoid(jnp.dot(h, wg_ref[:, 0:D_MODEL], preferred_element_type=F32))
    merged = gate_a * o_a
    gate_s = jax.nn.sigmoid(
        jnp.dot(h, wg_ref[:, D_MODEL:2 * D_MODEL], preferred_element_type=F32))
    merged = merged + gate_s * o_s
    x2 = x + jnp.dot(merged.astype(BF16), wout_ref[...], preferred_element_type=F32)
    x2_ref[...] = x2

    ms2 = jnp.mean(x2 * x2, axis=-1, keepdims=True)
    h2 = x2 * lax.rsqrt(ms2 + EPS) * g2_ref[...]
    h2_hi = h2.astype(BF16)
    h2_lo = (h2 - h2_hi.astype(F32)).astype(BF16)
    pay_ref[:, 0:D_MODEL] = h2_hi

    both = jnp.dot(h2_hi, wr_ref[...], preferred_element_type=F32)
    return (both[:, 0:LANES] + both[:, LANES:2 * LANES]
            + jnp.dot(h2_lo, wr_ref[:, 0:LANES], preferred_element_type=F32)
            + br_ref[...])


def _route_rows(logits, pay_ref, cnt_ref):
    lane = lax.broadcasted_iota(jnp.int32, logits.shape, 1).astype(F32)
    is_grp = (lane >= N_EXPERTS) & (lane < N_EXPERTS + N_EXPERT_GROUPS)
    gl = jnp.where(is_grp, logits, NEG_BIG)
    gmax = jnp.max(gl, axis=-1, keepdims=True)
    gsum = jnp.sum(jnp.where(is_grp, jnp.exp(gl - gmax), 0.0), axis=-1, keepdims=True)
    p_grp = 1.0 / gsum
    big = float(4 * LANES)
    grp = jnp.min(jnp.where(is_grp & (gl == gmax), lane, big), axis=-1,
                  keepdims=True) - N_EXPERTS
    sel = logits
    for g in range(1, N_EXPERT_GROUPS):
        rolled = pltpu.roll(logits, LANES - g * EXPERTS_PER_GROUP, axis=1)
        sel = jnp.where(grp == g, rolled, sel)
    in_grp = lane < EXPERTS_PER_GROUP
    es = jnp.where(in_grp, sel, NEG_BIG)
    top1 = jnp.max(es, axis=-1, keepdims=True)
    i1 = jnp.min(jnp.where(in_grp & (es == top1), lane, big), axis=-1, keepdims=True)
    es2 = jnp.where(lane == i1, NEG_BIG, es)
    top2 = jnp.max(es2, axis=-1, keepdims=True)
    i2 = jnp.min(jnp.where(in_grp & (lane != i1) & (es2 == top2), lane, big), axis=-1,
                 keepdims=True)
    e2 = jnp.exp(top2 - top1)
    w1 = p_grp / (1.0 + e2)
    w2 = p_grp * e2 / (1.0 + e2)
    e1 = grp * EXPERTS_PER_GROUP + i1
    e2x = grp * EXPERTS_PER_GROUP + i2

    def hi_lo(w):
        hi = w.astype(BF16).astype(F32)
        return hi, (w - hi).astype(BF16).astype(F32)

    w1h, w1l = hi_lo(w1)
    w2h, w2l = hi_lo(w2)
    pay_ref[:, D_MODEL:D_MODEL + LANES] = (
        jnp.where(lane == 0.0, e1, 0.0) + jnp.where(lane == 1.0, e2x, 0.0)
        + jnp.where(lane == 2.0, w1h, 0.0) + jnp.where(lane == 3.0, w1l, 0.0)
        + jnp.where(lane == 4.0, w2h, 0.0) + jnp.where(lane == 5.0, w2l, 0.0)).astype(BF16)

    picked = jnp.where((lane == e1) | (lane == e2x), 1.0, 0.0)
    cnt_ref[...] = jnp.sum(picked, axis=0, keepdims=True)


def _merge(x2d, o_attn, gy, g1, w_gates, w_oa, w_glu, w_out, g2, w_router, b_r):
    t = x2d.shape[0]
    tok = lambda i: (i, 0)
    sub_tiles = TM_MERGE // TM_DISP
    return pl.pallas_call(
        _merge_kernel,
        grid=(t // TM_MERGE,),
        in_specs=[
            pl.BlockSpec((TM_MERGE, D_MODEL), tok),
            pl.BlockSpec((TM_MERGE, ATTN_WIDTH), tok),
            pl.BlockSpec((TM_MERGE, SSM_WIDTH), tok),
            _const_spec((1, D_MODEL)),
            _const_spec((D_MODEL, 2 * D_MODEL)),
            _const_spec((ATTN_WIDTH, D_MODEL)),
            _const_spec((SSM_WIDTH, 2 * D_MODEL)),
            _const_spec((D_MODEL, D_MODEL)),
            _const_spec((1, D_MODEL)),
            _const_spec((D_MODEL, 2 * LANES)),
            _const_spec((1, LANES)),
        ],
        out_specs=[
            pl.BlockSpec((TM_MERGE, D_MODEL), tok),
            pl.BlockSpec((TM_MERGE, XS_COLS), tok),
            pl.BlockSpec((sub_tiles, 1, LANES), lambda i: (i, 0, 0)),
        ],
        out_shape=[
            jax.ShapeDtypeStruct((t, D_MODEL), F32),
            jax.ShapeDtypeStruct((t, XS_COLS), BF16),
            jax.ShapeDtypeStruct((t // TM_DISP, 1, LANES), F32),
        ],
        compiler_params=pltpu.CompilerParams(
            dimension_semantics=("arbitrary",), vmem_limit_bytes=VMEM_LIMIT),
        name="merge",
    )(x2d, o_attn, gy, g1, w_gates, w_oa, w_glu, w_out, g2, w_router, b_r)


def _moe_layout(t):
    ntiles = t // TM_DISP
    max_rows = 2 * t + ntiles * N_EXPERTS * (ROW_ALIGN - 1)
    nsteps = -(-max_rows // TS_MOE) + N_EXPERTS
    return ntiles, nsteps


def _lane_col(arr, lane, k):
    return jnp.sum(jnp.where(lane == k, arr, 0.0), axis=-1, keepdims=True)


def _tile_positions(aux, tri_ref, upper_ref, pos_ref):
    lane = lax.broadcasted_iota(jnp.int32, aux.shape, 1)
    lane_f = lane.astype(F32)
    is1 = lane_f == _lane_col(aux, lane, 0)
    is2 = lane_f == _lane_col(aux, lane, 1)
    picked = jnp.where(is1 | is2, 1.0, 0.0)
    rank_all = jnp.dot(tri_ref[...], picked.astype(BF16), preferred_element_type=F32)
    cnt_row = jnp.sum(picked, axis=0, keepdims=True)
    pad_row = jnp.floor((cnt_row + (ROW_ALIGN - 1)) * (1.0 / ROW_ALIGN)) * ROW_ALIGN
    off = jnp.dot(jnp.broadcast_to(pad_row, (SUBLANES, LANES)).astype(BF16), upper_ref[...],
                  preferred_element_type=F32)[0:1, :]
    posmat = rank_all + off
    pos1 = jnp.sum(jnp.where(is1, posmat, 0.0), axis=-1, keepdims=True)
    pos2 = jnp.sum(jnp.where(is2, posmat, 0.0), axis=-1, keepdims=True)
    pos_ref[...] = jnp.where(lane == 0, pos1, 0.0) + jnp.where(lane == 1, pos2, 0.0)

    def digits(p):
        hi = jnp.floor(p * (1.0 / POS_RADIX))
        return hi, p - POS_RADIX * hi

    d1h, d1l = digits(pos1)
    d2h, d2l = digits(pos2)
    dig = (jnp.where(lane == 0, d1h, 0.0) + jnp.where(lane == 1, d1l, 0.0)
           + jnp.where(lane == 2, d2h, 0.0) + jnp.where(lane == 3, d2l, 0.0)).astype(BF16)
    eye8 = (lax.broadcasted_iota(jnp.int32, (SUBLANES, LANES), 0)
            == lax.broadcasted_iota(jnp.int32, (SUBLANES, LANES), 1)).astype(BF16)
    rows = lax.dot_general(eye8, dig, (((1,), (1,)), ((), ())), preferred_element_type=F32)
    return (POS_RADIX * rows[0:1, :] + rows[1:2, :], POS_RADIX * rows[2:3, :] + rows[3:4, :])


def _compact_tile(payload, p1_row, p2_row):
    sub = lax.broadcasted_iota(jnp.int32, (SORT_ROWS, TM_DISP), 0).astype(F32)
    perm = jnp.where((sub == p1_row) | (sub == p2_row), 1.0, 0.0).astype(BF16)
    return jnp.dot(perm, payload, preferred_element_type=F32).astype(BF16)


def _dispatch_kernel(dst_ref, npiece_ref, tail_ref, ntail_ref, pay_ref, tri_ref, upper_ref,
                     xs_ref, pos_ref, buf_ref, zbuf_ref, sem_ref, tsem_ref):
    i = pl.program_id(0)
    last = pl.num_programs(0) - 1
    cur = i % 2

    def piece(tile, slot, sub, p):
        return pltpu.make_async_copy(buf_ref.at[slot, sub, p],
                                     xs_ref.at[dst_ref[tile * MAX_PIECES + p]],
                                     sem_ref.at[slot, sub])

    def start_all(tile, slot, sub):
        def body(p, c):
            piece(tile, slot, sub, p).start()
            return c
        lax.fori_loop(0, npiece_ref[tile], body, 0)

    def wait_all(tile, slot, sub):
        def body(p, c):
            piece(tile, slot, sub, p).wait()
            return c
        lax.fori_loop(0, npiece_ref[tile], body, 0)

    places = []
    for sub in range(DISP_SUB):
        rs = slice(sub * TM_DISP, (sub + 1) * TM_DISP)
        places.append(_tile_positions(pay_ref[rs, D_MODEL:XS_COLS].astype(F32), tri_ref,
                                      upper_ref, pos_ref.at[rs, :]))
    for sub in range(DISP_SUB):
        rs = slice(sub * TM_DISP, (sub + 1) * TM_DISP)
        buf_ref[cur, sub] = _compact_tile(pay_ref[rs, :], *places[sub]).reshape(
            MAX_PIECES, ROW_ALIGN, XS_COLS)
    for sub in range(DISP_SUB):
        start_all(i * DISP_SUB + sub, cur, sub)

    @pl.when(i > 0)
    def _wait_prev():
        for sub in range(DISP_SUB):
            wait_all((i - 1) * DISP_SUB + sub, 1 - cur, sub)

    @pl.when(i == last)
    def _finish():
        for sub in range(DISP_SUB):
            wait_all(i * DISP_SUB + sub, cur, sub)
        zbuf_ref[...] = jnp.zeros(zbuf_ref.shape, BF16)

        def tail(p):
            return pltpu.make_async_copy(zbuf_ref, xs_ref.at[tail_ref[p]], tsem_ref.at[0])

        def start_tail(p, c):
            tail(p).start()
            return c

        def wait_tail(p, c):
            tail(p).wait()
            return c

        lax.fori_loop(0, ntail_ref[0], start_tail, 0)
        lax.fori_loop(0, ntail_ref[0], wait_tail, 0)


def _dispatch(payload, dst, npiece, tail, ntail, rows):
    t = payload.shape[0]
    tm = TM_DISP * DISP_SUB
    tok = lambda i, *_: (i, 0)
    const = lambda i, *_: (0, 0)
    tri = (jnp.arange(TM_DISP)[:, None] > jnp.arange(TM_DISP)[None, :]).astype(BF16)
    upper = (jnp.arange(LANES)[:, None] < jnp.arange(LANES)[None, :]).astype(BF16)
    grid_spec = pltpu.PrefetchScalarGridSpec(
        num_scalar_prefetch=4,
        grid=(t // tm,),
        in_specs=[
            pl.BlockSpec((tm, XS_COLS), tok),
            pl.BlockSpec((TM_DISP, TM_DISP), const),
            pl.BlockSpec((LANES, LANES), const),
        ],
        out_specs=[
            pl.BlockSpec(memory_space=pl.ANY),
            pl.BlockSpec((tm, LANES), tok),
        ],
        scratch_shapes=[
            pltpu.VMEM((2, DISP_SUB, MAX_PIECES, ROW_ALIGN, XS_COLS), BF16),
            pltpu.VMEM((ROW_ALIGN, XS_COLS), BF16),
            pltpu.SemaphoreType.DMA((2, DISP_SUB)),
            pltpu.SemaphoreType.DMA((1,)),
        ],
    )
    xs, pos = pl.pallas_call(
        _dispatch_kernel,
        grid_spec=grid_spec,
        out_shape=[
            jax.ShapeDtypeStruct((rows // ROW_ALIGN, ROW_ALIGN, XS_COLS), BF16),
            jax.ShapeDtypeStruct((t, LANES), F32),
        ],
        compiler_params=pltpu.CompilerParams(
            dimension_semantics=("arbitrary",), vmem_limit_bytes=VMEM_LIMIT),
        name="dispatch",
    )(dst // ROW_ALIGN, npiece, tail // ROW_ALIGN, ntail, payload, tri, upper)
    return xs.reshape(rows, XS_COLS), pos


def _moe_kernel(blk_ref, exp_ref, nvalid_ref, fresh_ref, xs_ref, wg_ref, wu_ref, wd_ref,
                ys_ref, wgb_ref, wub_ref, wdb_ref):
    s = pl.program_id(0)

    @pl.when(fresh_ref[s] == 1)
    def _cast_weights():
        wgb_ref[...] = wg_ref[...].astype(BF16)
        wub_ref[...] = wu_ref[...].astype(BF16)
        wdb_ref[...] = wd_ref[...].astype(BF16)

    @pl.when(s < nvalid_ref[0])
    def _compute():
        t = xs_ref[:, 0:D_MODEL]
        aux = xs_ref[:, D_MODEL:XS_COLS].astype(F32)
        lane = lax.broadcasted_iota(jnp.int32, aux.shape, 1)
        first = _lane_col(aux, lane, 0) == exp_ref[s].astype(F32)
        w = jnp.where(first, _lane_col(aux, lane, 2) + _lane_col(aux, lane, 3),
                      _lane_col(aux, lane, 4) + _lane_col(aux, lane, 5))
        hg = jnp.dot(t, wgb_ref[...], preferred_element_type=F32)
        hu = jnp.dot(t, wub_ref[...], preferred_element_type=F32)
        a = (hg * jax.nn.sigmoid(hg) * hu * w).astype(BF16)
        ys_ref[...] = jnp.dot(a, wdb_ref[...], preferred_element_type=F32).astype(BF16)

    @pl.when(s == nvalid_ref[0])
    def _spare():
        ys_ref[...] = jnp.zeros(ys_ref.shape, BF16)


def _moe(xs, blk, exp, nvalid, fresh, wg, wu, wd):
    row = lambda s, blk, exp, nv, fr: (blk[s], 0)
    wsel = lambda s, blk, exp, nv, fr: (exp[s], 0, 0)
    grid_spec = pltpu.PrefetchScalarGridSpec(
        num_scalar_prefetch=4,
        grid=(blk.shape[0],),
        in_specs=[
            pl.BlockSpec((TS_MOE, XS_COLS), row),
            pl.BlockSpec((None, D_MODEL, D_EXPERT), wsel),
            pl.BlockSpec((None, D_MODEL, D_EXPERT), wsel),
            pl.BlockSpec((None, D_EXPERT, D_MODEL), wsel),
        ],
        out_specs=pl.BlockSpec((TS_MOE, D_MODEL), row),
        scratch_shapes=[
            pltpu.VMEM((D_MODEL, D_EXPERT), BF16),
            pltpu.VMEM((D_MODEL, D_EXPERT), BF16),
            pltpu.VMEM((D_EXPERT, D_MODEL), BF16),
        ],
    )
    return pl.pallas_call(
        _moe_kernel,
        grid_spec=grid_spec,
        out_shape=jax.ShapeDtypeStruct((xs.shape[0], D_MODEL), BF16),
        compiler_params=pltpu.CompilerParams(
            dimension_semantics=("arbitrary",), vmem_limit_bytes=VMEM_LIMIT),
        name="moe",
    )(blk, exp, nvalid, fresh, xs, wg, wu, wd)


def _combine_kernel(dst_ref, npiece_ref, x2_ref, pos_ref, ys_ref, o_ref, stage_ref, sem_ref):
    i = pl.program_id(0)
    n = pl.num_programs(0)
    cur = i % 2
    tm = TM_DISP

    def piece(tile, slot, sub, p):
        row = pl.multiple_of(p * ROW_ALIGN, ROW_ALIGN)
        src = pl.multiple_of(dst_ref[tile * MAX_PIECES + p], ROW_ALIGN)
        return pltpu.make_async_copy(ys_ref.at[pl.ds(src, ROW_ALIGN)],
                                     stage_ref.at[slot, sub, pl.ds(row, ROW_ALIGN)],
                                     sem_ref.at[slot, sub])

    def start_step(step, slot):
        for sub in range(DISP_SUB):
            tile = step * DISP_SUB + sub

            def body(p, c, tile=tile, sub=sub):
                piece(tile, slot, sub, p).start()
                return c
            lax.fori_loop(0, npiece_ref[tile], body, 0)

    @pl.when(i == 0)
    def _first():
        stage_ref[...] = jnp.zeros(stage_ref.shape, BF16)
        start_step(0, 0)

    @pl.when(i + 1 < n)
    def _prefetch():
        start_step(i + 1, 1 - cur)

    for sub in range(DISP_SUB):
        tile = i * DISP_SUB + sub

        def wait_body(p, c, tile=tile, sub=sub):
            piece(tile, cur, sub, p).wait()
            return c

        lax.fori_loop(0, npiece_ref[tile], wait_body, 0)

    for sub in range(DISP_SUB):
        rs = slice(sub * tm, (sub + 1) * tm)
        pos = pos_ref[rs, :]
        lane128 = lax.broadcasted_iota(jnp.int32, pos.shape, 1)
        p1 = _lane_col(pos, lane128, 0)
        p2 = _lane_col(pos, lane128, 1)
        lane = lax.broadcasted_iota(jnp.int32, (tm, SORT_ROWS), 1).astype(F32)
        pick = jnp.where((lane == p1) | (lane == p2), 1.0, 0.0).astype(BF16)
        o_ref[rs, :] = x2_ref[rs, :] + jnp.dot(pick, stage_ref[cur, sub],
                                               preferred_element_type=F32)


def _combine(x2, pos, ys, dst, npiece):
    t = x2.shape[0]
    tm = TM_DISP * DISP_SUB
    tok = lambda i, *_: (i, 0)
    grid_spec = pltpu.PrefetchScalarGridSpec(
        num_scalar_prefetch=2,
        grid=(t // tm,),
        in_specs=[
            pl.BlockSpec((tm, D_MODEL), tok),
            pl.BlockSpec((tm, LANES), tok),
            pl.BlockSpec(memory_space=pl.ANY),
        ],
        out_specs=pl.BlockSpec((tm, D_MODEL), tok),
        scratch_shapes=[
            pltpu.VMEM((2, DISP_SUB, SORT_ROWS, D_MODEL), BF16),
            pltpu.SemaphoreType.DMA((2, DISP_SUB)),
        ],
    )
    return pl.pallas_call(
        _combine_kernel,
        grid_spec=grid_spec,
        out_shape=jax.ShapeDtypeStruct((t, D_MODEL), F32),
        compiler_params=pltpu.CompilerParams(
            dimension_semantics=("arbitrary",), vmem_limit_bytes=VMEM_LIMIT),
        name="combine",
    )(dst, npiece, x2, pos, ys)


def _sparse_moe(payload, cnt, x2, wg, wu, wd):
    t = payload.shape[0]
    ntiles, nsteps = _moe_layout(t)
    rows = (nsteps + 1) * TS_MOE
    i32 = jnp.int32
    counts = cnt[:, 0, :N_EXPERTS].astype(i32)
    padded = (counts + (ROW_ALIGN - 1)) // ROW_ALIGN * ROW_ALIGN
    total = jnp.sum(padded, axis=0)
    reg_rows = (total + (TS_MOE - 1)) // TS_MOE * TS_MOE
    region = jnp.cumsum(reg_rows) - reg_rows
    base = region[None, :] + jnp.cumsum(padded, axis=0) - padded
    np_e = padded // ROW_ALIGN
    cum = jnp.cumsum(np_e, axis=1)
    npiece = cum[:, -1].astype(i32)
    p = jnp.arange(MAX_PIECES, dtype=i32)
    owns = (p[None, :, None] >= (cum - np_e)[:, None, :]) & (p[None, :, None] < cum[:, None, :])
    piece_dst = base[:, None, :] + (p[None, :, None] - (cum - np_e)[:, None, :]) * ROW_ALIGN
    dst = jnp.sum(jnp.where(owns, piece_dst, 0), axis=2).reshape(-1).astype(i32)
    nt_e = (reg_rows - total) // ROW_ALIGN
    cumt = jnp.cumsum(nt_e)
    q = jnp.arange(N_EXPERTS * (TS_MOE // ROW_ALIGN), dtype=i32)
    owns_q = (q[:, None] >= (cumt - nt_e)[None, :]) & (q[:, None] < cumt[None, :])
    tail_dst = (region + total)[None, :] + (q[:, None] - (cumt - nt_e)[None, :]) * ROW_ALIGN
    tail = jnp.sum(jnp.where(owns_q, tail_dst, 0), axis=1).astype(i32)
    ntail = cumt[-1:].astype(i32)
    ntile_e = reg_rows // TS_MOE
    first_t = jnp.cumsum(ntile_e) - ntile_e
    nvalid = jnp.sum(ntile_e).reshape(1).astype(i32)
    s = jnp.arange(nsteps, dtype=i32)
    owns_s = (s[:, None] >= first_t[None, :]) & (s[:, None] < (first_t + ntile_e)[None, :])
    spare = s >= nvalid[0]
    fresh = jnp.any(owns_s & (s[:, None] == first_t[None, :]), axis=1).astype(i32)
    exp = jnp.where(spare, N_EXPERTS - 1, jnp.sum(
        jnp.where(owns_s, jnp.arange(N_EXPERTS, dtype=i32)[None, :], 0), axis=1)).astype(i32)
    blk = jnp.where(spare, nsteps, jnp.sum(
        jnp.where(owns_s, (region // TS_MOE - first_t)[None, :] + s[:, None], 0),
        axis=1)).astype(i32)
    xs, pos = _dispatch(payload, dst, npiece, tail, ntail, rows)
    ys = _moe(xs, blk, exp, nvalid, fresh, wg, wu, wd)
    return _combine(x2, pos, ys, dst, npiece)


def _rope_tables(positions):
    inv = ROPE_THETA ** (-jnp.arange(0, ROPE_DIM, 2, dtype=F32) / ROPE_DIM)
    ang = positions.astype(F32).reshape(-1, 1) * inv
    return jnp.concatenate([jnp.cos(ang), jnp.sin(ang)], axis=1)


def _s5_params(lam_re, lam_im, log_dt, b_re, b_im, c_re, c_im, bsz):
    dt = jnp.exp(log_dt)[:, None]
    mag = jnp.exp(lam_re * dt)
    lb_re = mag * jnp.cos(lam_im * dt)
    lb_im = mag * jnp.sin(lam_im * dt)
    den = lam_re * lam_re + lam_im * lam_im
    k_re = ((lb_re - 1.0) * lam_re + lb_im * lam_im) / den
    k_im = (lb_im * lam_re - (lb_re - 1.0) * lam_im) / den
    bb_re = k_re[..., None] * b_re - k_im[..., None] * b_im
    bb_im = k_re[..., None] * b_im + k_im[..., None] * b_re
    eye = jnp.eye(SSM_GROUPS, dtype=F32)
    blk_b = lambda m: jnp.einsum('gph,gk->ghkp', m, eye).reshape(SSM_WIDTH, N_STATE)
    blk_c = lambda m: jnp.einsum('ghp,gk->gpkh', m, eye).reshape(N_STATE, SSM_WIDTH)
    bmat = jnp.concatenate([blk_b(bb_re), blk_b(bb_im)], axis=1).astype(BF16)
    cmat = jnp.concatenate([blk_c(c_re), blk_c(-c_im)], axis=0).astype(BF16)
    a_re = jnp.broadcast_to(lb_re.reshape(1, N_STATE), (bsz, N_STATE))
    a_im = jnp.broadcast_to(lb_im.reshape(1, N_STATE), (bsz, N_STATE))
    return bmat, a_re, a_im, cmat


def _time_major_perm(bsz, chunk):
    r = jnp.arange(bsz * chunk)
    src = (r % bsz) * chunk + r // bsz
    perm = (src[:, None] == jnp.arange(bsz * chunk)[None, :]).astype(BF16)
    return perm, perm.T


def kernel(x, positions, norm_mix_g, w_in, q_norm_g, k_norm_g, lambda_q1, lambda_k1, lambda_q2, lambda_k2, subln_g, w_o_attn, ssm_lambda_re, ssm_lambda_im, ssm_log_dt, ssm_b_re, ssm_b_im, ssm_c_re, ssm_c_im, ssm_d, w_glu, w_out, norm_ffn_g, w_router_group, b_router_group, w_router_expert, b_router_expert, w_expert_gate, w_expert_up, w_expert_down):
    bsz, seq, _ = x.shape
    assert bsz == SUBLANES and seq % (2 * TQ) == 0 and seq % (S5_SUB * SSM_CHUNK) == 0
    assert norm_mix_g.shape[0] == 1
    t = bsz * seq
    x2d = x.reshape(t, D_MODEL)
    l = 0

    cos_sin = _rope_tables(positions)
    w_qkvu = w_in[l][:, :QKVU_COLS].astype(BF16)
    w_gates = w_in[l][:, QKVU_COLS:].astype(BF16)
    qg = jnp.tile(q_norm_g[l].reshape(1, HEAD_DIM), (1, 2))
    kg = jnp.tile(k_norm_g[l].reshape(1, HEAD_DIM), (1, 2))
    q, k, v, u = _in_proj(x2d, norm_mix_g[l].reshape(1, D_MODEL), w_qkvu, cos_sin, qg, kg)

    lam = (jnp.exp(jnp.sum(lambda_q1[l] * lambda_k1[l]))
           - jnp.exp(jnp.sum(lambda_q2[l] * lambda_k2[l])) + LAM_INIT).reshape(1)
    o_attn = _diff_attn(q, k, v, lam, subln_g[l].reshape(1, V_DIM), bsz, seq)

    bmat, a_re, a_im, cmat = _s5_params(
        ssm_lambda_re[l], ssm_lambda_im[l], ssm_log_dt[l], ssm_b_re[l], ssm_b_im[l],
        ssm_c_re[l], ssm_c_im[l], bsz)
    perm, perm_t = _time_major_perm(bsz, SSM_CHUNK)
    gy = _s5_scan(u.reshape(bsz, seq, SSM_WIDTH), perm, perm_t, bmat, a_re, a_im, cmat,
                  ssm_d[l].reshape(1, SSM_WIDTH)).reshape(t, SSM_WIDTH)

    w_r = jnp.concatenate(
        [w_router_expert[l].reshape(D_MODEL, N_EXPERTS), w_router_group[l],
         jnp.zeros((D_MODEL, LANES - N_EXPERTS - N_EXPERT_GROUPS), F32)], axis=1)
    b_r = jnp.concatenate(
        [b_router_expert[l].reshape(N_EXPERTS), b_router_group[l],
         jnp.zeros((LANES - N_EXPERTS - N_EXPERT_GROUPS,), F32)]).reshape(1, LANES)
    wr_hi = w_r.astype(BF16)
    wr_lo = (w_r - wr_hi.astype(F32)).astype(BF16)
    x2, payload, cnt = _merge(
        x2d, o_attn, gy, norm_mix_g[l].reshape(1, D_MODEL), w_gates,
        w_o_attn[l].astype(BF16), w_glu[l].astype(BF16), w_out[l].astype(BF16),
        norm_ffn_g[l].reshape(1, D_MODEL), jnp.concatenate([wr_hi, wr_lo], axis=1), b_r)

    wg = w_expert_gate[l].reshape(N_EXPERTS, D_MODEL, D_EXPERT)
    wu = w_expert_up[l].reshape(N_EXPERTS, D_MODEL, D_EXPERT)
    wd = w_expert_down[l].reshape(N_EXPERTS, D_EXPERT, D_MODEL)
    out = _sparse_moe(payload, cnt, x2, wg, wu, wd)
    return out.reshape(bsz, seq, D_MODEL)
```

```python
import math

import jax
import jax.numpy as jnp
from jax import lax
from jax.experimental import pallas as pl
from jax.experimental.pallas import tpu as pltpu

F32 = jnp.float32
BF16 = jnp.bfloat16

D_MODEL = 1024
N_HEADS = 4
HEAD_DIM = 64
V_DIM = 2 * HEAD_DIM
ATTN_WIDTH = N_HEADS * V_DIM
ROPE_THETA = 500000.0
ROPE_DIM = HEAD_DIM // 4
ROPE_HALF = ROPE_DIM // 2
SSM_WIDTH = D_MODEL // 2
SSM_GROUP = 16
SSM_GROUPS = SSM_WIDTH // SSM_GROUP
SSM_STATE = 64
N_STATE = SSM_GROUPS * SSM_STATE
N_EXPERT_GROUPS = 4
EXPERTS_PER_GROUP = 8
N_EXPERTS = N_EXPERT_GROUPS * EXPERTS_PER_GROUP
D_EXPERT = D_MODEL // 4
EPS = 1e-6
LAM_INIT = 0.8 - 0.6 * math.exp(-0.3 * 0)
QKVU_COLS = 4 * ATTN_WIDTH
LANES = 128
SUBLANES = 8
NEG_BIG = -1e30
LOG2_E = math.log2(math.e)

VMEM_LIMIT = 48 * 1024 * 1024

TM_IN = 1024
TQ = 512
SSM_CHUNK = 64
S5_SUB = 4
SCAN_LANES = 512
TM_MERGE = 1024
TM_DISP = 256
DISP_SUB = 4
TS_MOE = 1024
ROW_ALIGN = 16
XS_COLS = D_MODEL + LANES
SORT_ROWS = 2 * TM_DISP + N_EXPERTS * ROW_ALIGN
MAX_PIECES = SORT_ROWS // ROW_ALIGN
POS_RADIX = 32


def _const_spec(shape):
    return pl.BlockSpec(shape, lambda *_: (0,) * len(shape))


def _in_proj_kernel(x_ref, g_ref, w_ref, cs_ref, qg_ref, kg_ref,
                    q_ref, k_ref, v_ref, u_ref):
    x = x_ref[...]
    ms = jnp.mean(x * x, axis=-1, keepdims=True)
    h = (x * lax.rsqrt(ms + EPS) * g_ref[...]).astype(BF16)
    lane = lax.broadcasted_iota(jnp.int32, (x.shape[0], LANES), 1)
    low_half = (lane % HEAD_DIM) < ROPE_HALF
    in_rope = (lane % HEAD_DIM) < ROPE_DIM
    freq = lane % ROPE_HALF
    cs = jnp.concatenate(
        [cs_ref[...], jnp.zeros((x.shape[0], LANES - 2 * ROPE_HALF), F32)], axis=1)
    cos_t = jnp.where(in_rope, jnp.take_along_axis(cs, freq, axis=1), 1.0)
    sin_g = jnp.take_along_axis(cs, freq + ROPE_HALF, axis=1)
    sin_t = jnp.where(in_rope, jnp.where(low_half, -sin_g, sin_g), 0.0)
    same_comp = (lax.broadcasted_iota(jnp.int32, (LANES, LANES), 0) // HEAD_DIM
                 == lax.broadcasted_iota(jnp.int32, (LANES, LANES), 1) // HEAD_DIM
                 ).astype(BF16)

    def norm_rope(blk, gain, scale):
        ssum = jnp.dot((blk * blk).astype(BF16), same_comp, preferred_element_type=F32)
        nb = blk * lax.rsqrt(ssum * (1.0 / HEAD_DIM) + EPS) * gain
        up = pltpu.roll(nb, LANES - ROPE_HALF, axis=1)
        dn = pltpu.roll(nb, ROPE_HALF, axis=1)
        partner = jnp.where(low_half, up, dn)
        return ((nb * cos_t + partner * sin_t) * scale).astype(BF16)

    for j in range(0, N_HEADS, 2):
        sl2 = slice(j * LANES, (j + 2) * LANES)
        qb = jnp.dot(h, w_ref[:, sl2], preferred_element_type=F32)
        kb = jnp.dot(h, w_ref[:, ATTN_WIDTH + j * LANES:ATTN_WIDTH + (j + 2) * LANES],
                     preferred_element_type=F32)
        for jj in range(2):
            sl = slice((j + jj) * LANES, (j + jj + 1) * LANES)
            half = slice(jj * LANES, (jj + 1) * LANES)
            q_ref[:, sl] = norm_rope(qb[:, half], qg_ref[...], LOG2_E * HEAD_DIM ** -0.5)
            k_ref[:, sl] = norm_rope(kb[:, half], kg_ref[...], 1.0)
    v_ref[...] = jnp.dot(h, w_ref[:, 2 * ATTN_WIDTH:3 * ATTN_WIDTH],
                         preferred_element_type=F32).astype(BF16)
    u_ref[...] = jnp.dot(h, w_ref[:, 3 * ATTN_WIDTH:4 * ATTN_WIDTH],
                         preferred_element_type=F32).astype(BF16)


def _in_proj(x2d, g, w, cos_sin, qg, kg):
    t = x2d.shape[0]
    tok = lambda i: (i, 0)
    out = jax.ShapeDtypeStruct((t, ATTN_WIDTH), BF16)
    return pl.pallas_call(
        _in_proj_kernel,
        grid=(t // TM_IN,),
        in_specs=[
            pl.BlockSpec((TM_IN, D_MODEL), tok),
            _const_spec((1, D_MODEL)),
            _const_spec((D_MODEL, QKVU_COLS)),
            pl.BlockSpec((TM_IN, 2 * ROPE_HALF), tok),
            _const_spec((1, LANES)),
            _const_spec((1, LANES)),
        ],
        out_specs=[pl.BlockSpec((TM_IN, ATTN_WIDTH), tok)] * 4,
        out_shape=[out] * 4,
        compiler_params=pltpu.CompilerParams(
            dimension_semantics=("arbitrary",), vmem_limit_bytes=VMEM_LIMIT),
        name="in_proj",
    )(x2d, g, w, cos_sin, qg, kg)


def _attn_kernel(qi_ref, kp_ref, mode_ref, lam_ref, q_ref, k_ref, v_ref, sg_ref, o_ref,
                 qs_ref, m_ref, l_ref, acc_ref):
    step = pl.program_id(1)
    kp = kp_ref[step]
    mode = mode_ref[step]

    hq = TQ // 2

    @pl.when(kp == 0)
    def _init():
        for h in range(N_HEADS):
            q = q_ref[:, h * LANES:(h + 1) * LANES]
            lane = lax.broadcasted_iota(jnp.int32, q.shape, 1)
            zero = jnp.zeros_like(q)
            only0 = jnp.where(lane < HEAD_DIM, q, zero)
            only1 = jnp.where(lane < HEAD_DIM, zero, q)
            for half in range(2):
                rows = slice(half * hq, (half + 1) * hq)
                qs_ref[h, 2 * half * hq:(2 * half + 1) * hq, :] = only0[rows, :]
                qs_ref[h, (2 * half + 1) * hq:(2 * half + 2) * hq, :] = only1[rows, :]
        m_ref[...] = jnp.full(m_ref.shape, NEG_BIG, F32)
        l_ref[...] = jnp.zeros(l_ref.shape, F32)
        acc_ref[...] = jnp.zeros(acc_ref.shape, F32)

    def update(h, rows, key0, nkeys, diag_offset=None):
        hs = slice(h * LANES, (h + 1) * LANES)
        ks = slice(key0, key0 + nkeys)
        s = lax.dot_general(qs_ref[h, rows, :], k_ref[ks, hs], (((1,), (1,)), ((), ())),
                            preferred_element_type=F32)
        if diag_offset is not None:
            row = lax.broadcasted_iota(jnp.int32, s.shape, 0)
            row = row % hq + row // TQ * hq + diag_offset
            col = lax.broadcasted_iota(jnp.int32, s.shape, 1)
            s = jnp.where(col <= row, s, NEG_BIG)
        m_old = m_ref[h, rows, :]
        m_new = jnp.maximum(m_old, jnp.max(s, axis=-1, keepdims=True))
        alpha = jnp.exp2(m_old - m_new)
        p = jnp.exp2(s - jnp.concatenate([m_new] * (nkeys // LANES), axis=1))
        v_ones = jnp.concatenate([v_ref[ks, hs], jnp.ones((nkeys, LANES), BF16)], axis=1)
        pv = jnp.dot(p.astype(BF16), v_ones, preferred_element_type=F32)
        l_ref[h, rows, :] = alpha * l_ref[h, rows, :] + pv[:, LANES:2 * LANES]
        acc_ref[h, rows, :] = alpha * acc_ref[h, rows, :] + pv[:, 0:LANES]
        m_ref[h, rows, :] = m_new

    all_rows = slice(0, 2 * TQ)

    def update_diag(h, key0):
        update(h, slice(0, TQ), key0, hq, diag_offset=0)
        update(h, slice(TQ, 2 * TQ), key0, TQ, diag_offset=hq)

    def finish(h):
        o = acc_ref[h] / l_ref[h]
        d = jnp.concatenate(
            [o[2 * half * hq:(2 * half + 1) * hq, :]
             - lam_ref[0] * o[(2 * half + 1) * hq:(2 * half + 2) * hq, :]
             for half in range(2)], axis=0)
        ms = jnp.mean(d * d, axis=-1, keepdims=True)
        d = d * lax.rsqrt(ms + EPS) * sg_ref[...] * (1.0 - LAM_INIT)
        o_ref[:, h * LANES:(h + 1) * LANES] = d.astype(BF16)

    @pl.when(mode == 0)
    def _below():
        for h in range(N_HEADS):
            update(h, all_rows, 0, TQ)
            update(h, all_rows, TQ, TQ)

    @pl.when(mode == 1)
    def _below_then_diag():
        for h in range(N_HEADS):
            update(h, all_rows, 0, TQ)
            update(h, all_rows, TQ, TQ, diag_offset=0)
            finish(h)

    @pl.when(mode == 2)
    def _diag_only():
        for h in range(N_HEADS):
            update_diag(h, 0)
            finish(h)


def _diff_attn(q, k, v, lam, subln_g, bsz, seq):
    nq = seq // TQ
    steps = []
    for i in range(nq):
        for p in range(i // 2 + 1):
            mode = 0 if 2 * p + 1 < i else (1 if 2 * p + 1 == i else 2)
            steps.append((i, p, mode))
    qi = jnp.asarray([s[0] for s in steps], jnp.int32)
    kp = jnp.asarray([s[1] for s in steps], jnp.int32)
    mode = jnp.asarray([s[2] for s in steps], jnp.int32)
    q_map = lambda b, s, qi, kp, mode: (b * nq + qi[s], 0)
    k_map = lambda b, s, qi, kp, mode: (b * (nq // 2) + kp[s], 0)
    grid_spec = pltpu.PrefetchScalarGridSpec(
        num_scalar_prefetch=3,
        grid=(bsz, len(steps)),
        in_specs=[
            pl.BlockSpec(memory_space=pltpu.SMEM),
            pl.BlockSpec((TQ, ATTN_WIDTH), q_map),
            pl.BlockSpec((2 * TQ, ATTN_WIDTH), k_map),
            pl.BlockSpec((2 * TQ, ATTN_WIDTH), k_map),
            pl.BlockSpec((1, LANES), lambda b, s, qi, kp, mode: (0, 0)),
        ],
        out_specs=pl.BlockSpec((TQ, ATTN_WIDTH), q_map),
        scratch_shapes=[
            pltpu.VMEM((N_HEADS, 2 * TQ, LANES), BF16),
            pltpu.VMEM((N_HEADS, 2 * TQ, LANES), F32),
            pltpu.VMEM((N_HEADS, 2 * TQ, LANES), F32),
            pltpu.VMEM((N_HEADS, 2 * TQ, LANES), F32),
        ],
    )
    return pl.pallas_call(
        _attn_kernel,
        grid_spec=grid_spec,
        out_shape=jax.ShapeDtypeStruct((bsz * seq, ATTN_WIDTH), BF16),
        compiler_params=pltpu.CompilerParams(
            dimension_semantics=("arbitrary", "arbitrary"),
            vmem_limit_bytes=VMEM_LIMIT),
        name="diff_attn",
    )(qi, kp, mode, lam, q, k, v, subln_g)


def _gelu_tanh(x):
    c = math.sqrt(2.0 / math.pi)
    return 0.5 * x * (1.0 + jnp.tanh(c * (x + 0.044715 * (x * x * x))))


def _s5_kernel(u_ref, p_ref, pt_ref, b_ref, are_ref, aim_ref, c_ref, d_ref, o_ref,
               st_ref, state_ref):
    bsz, _, width = u_ref.shape
    chunk = SSM_CHUNK
    rows = bsz * chunk
    n_ch = N_STATE // SCAN_LANES

    @pl.when(pl.program_id(0) == 0)
    def _init():
        state_ref[...] = jnp.zeros(state_ref.shape, F32)

    slab = SCAN_LANES // SSM_STATE * SSM_GROUP
    state = [(state_ref[:, ch * SCAN_LANES:(ch + 1) * SCAN_LANES],
              state_ref[:, N_STATE + ch * SCAN_LANES:N_STATE + (ch + 1) * SCAN_LANES])
             for ch in range(n_ch)]

    for sub in range(S5_SUB):
        st = st_ref.at[sub]
        u_bt = u_ref[:, sub * chunk:(sub + 1) * chunk, :].reshape(rows, width)
        u_tm = jnp.dot(p_ref[...], u_bt, preferred_element_type=F32)
        u_tm_b = u_tm.astype(BF16)
        for ch in range(n_ch):
            re = slice(ch * SCAN_LANES, (ch + 1) * SCAN_LANES)
            im = slice(N_STATE + ch * SCAN_LANES, N_STATE + (ch + 1) * SCAN_LANES)
            cs = slice(ch * slab, (ch + 1) * slab)
            st[:, re] = jnp.dot(u_tm_b[:, cs], b_ref[cs, re], preferred_element_type=F32)
            st[:, im] = jnp.dot(u_tm_b[:, cs], b_ref[cs, im], preferred_element_type=F32)
        g_parts = []
        for ch in range(n_ch):
            re = slice(ch * SCAN_LANES, (ch + 1) * SCAN_LANES)
            im = slice(N_STATE + ch * SCAN_LANES, N_STATE + (ch + 1) * SCAN_LANES)
            cs = slice(ch * slab, (ch + 1) * slab)
            a_re = are_ref[:, re]
            a_im = aim_ref[:, re]

            def body(t, carry, st=st, re=re, im=im, a_re=a_re, a_im=a_im):
                s_re, s_im = carry
                r0 = pl.multiple_of(t * SUBLANES, SUBLANES)
                n_re = a_re * s_re - a_im * s_im + st[pl.ds(r0, SUBLANES), re]
                n_im = a_re * s_im + a_im * s_re + st[pl.ds(r0, SUBLANES), im]
                st[pl.ds(r0, SUBLANES), re] = n_re
                st[pl.ds(r0, SUBLANES), im] = n_im
                return n_re, n_im

            state[ch] = lax.fori_loop(0, chunk, body, state[ch], unroll=True)
            y = (jnp.dot(st[:, re].astype(BF16), c_ref[re, cs], preferred_element_type=F32)
                 + jnp.dot(st[:, im].astype(BF16), c_ref[im, cs], preferred_element_type=F32))
            y = y + d_ref[:, cs] * u_tm[:, cs]
            g_parts.append(_gelu_tanh(y).astype(BF16))

        g_tm = jnp.concatenate(g_parts, axis=1)
        g_bt = jnp.dot(pt_ref[...], g_tm, preferred_element_type=F32)
        o_ref[:, sub * chunk:(sub + 1) * chunk, :] = g_bt.astype(BF16).reshape(
            bsz, chunk, width)

    for ch in range(n_ch):
        state_ref[:, ch * SCAN_LANES:(ch + 1) * SCAN_LANES] = state[ch][0]
        state_ref[:, N_STATE + ch * SCAN_LANES:N_STATE + (ch + 1) * SCAN_LANES] = state[ch][1]


def _s5_scan(u3, perm, perm_t, bmat, a_re, a_im, cmat, dvec):
    bsz, seq, width = u3.shape
    rows = bsz * SSM_CHUNK
    blk = pl.BlockSpec((bsz, S5_SUB * SSM_CHUNK, width), lambda c: (0, c, 0))
    return pl.pallas_call(
        _s5_kernel,
        grid=(seq // (S5_SUB * SSM_CHUNK),),
        in_specs=[
            blk,
            _const_spec((rows, rows)),
            _const_spec((rows, rows)),
            _const_spec((width, 2 * N_STATE)),
            _const_spec((bsz, N_STATE)),
            _const_spec((bsz, N_STATE)),
            _const_spec((2 * N_STATE, width)),
            _const_spec((1, width)),
        ],
        out_specs=blk,
        out_shape=jax.ShapeDtypeStruct(u3.shape, BF16),
        scratch_shapes=[
            pltpu.VMEM((S5_SUB, rows, 2 * N_STATE), F32),
            pltpu.VMEM((bsz, 2 * N_STATE), F32),
        ],
        compiler_params=pltpu.CompilerParams(
            dimension_semantics=("arbitrary",), vmem_limit_bytes=VMEM_LIMIT),
        name="s5_scan",
    )(u3, perm, perm_t, bmat, a_re, a_im, cmat, dvec)


def _merge_kernel(x_ref, oa_ref, gy_ref, g1_ref, wg_ref, woa_ref, wglu_ref, wout_ref,
                  g2_ref, wr_ref, br_ref, x2_ref, pay_ref, cnt_ref):
    n_sub = TM_MERGE // TM_DISP
    logits = [None] * n_sub

    def rows_of(k):
        return slice(k * TM_DISP, (k + 1) * TM_DISP)

    def tail(k):
        _route_rows(logits[k], pay_ref.at[rows_of(k), :], cnt_ref.at[k])

    for k in range(n_sub):
        rs = rows_of(k)
        logits[k] = _merge_rows(x_ref[rs, :], oa_ref[rs, :], gy_ref[rs, :], g1_ref, wg_ref,
                                woa_ref, wglu_ref, wout_ref, g2_ref, wr_ref, br_ref,
                                x2_ref.at[rs, :], pay_ref.at[rs, :])
        if k > 0:
            tail(k - 1)
    tail(n_sub - 1)


def _merge_rows(x, o_in, gy, g1_ref, wg_ref, woa_ref, wglu_ref, wout_ref, g2_ref, wr_ref,
                br_ref, x2_ref, pay_ref):
    ms = jnp.mean(x * x, axis=-1, keepdims=True)
    h = (x * lax.rsqrt(ms + EPS) * g1_ref[...]).astype(BF16)
    o_a = jnp.dot(o_in, woa_ref[...], preferred_element_type=F32)
    z_lin = jnp.dot(gy, wglu_ref[:, 0:D_MODEL], preferred_element_type=F32)
    z_gate = jnp.dot(gy, wglu_ref[:, D_MODEL:2 * D_MODEL], preferred_element_type=F32)
    o_s = z_lin * jax.nn.sigmoid(z_gate)
    gate_a = jax.nn.sigmoid(jnp.dot(h, wg_ref[:, 0:D_MODEL], preferred_element_type=F32))
    merged = gate_a * o_a
    gate_s = jax.nn.sigmoid(
        jnp.dot(h, wg_ref[:, D_MODEL:2 * D_MODEL], preferred_element_type=F32))
    merged = merged + gate_s * o_s
    x2 = x + jnp.dot(merged.astype(BF16), wout_ref[...], preferred_element_type=F32)
    x2_ref[...] = x2

    ms2 = jnp.mean(x2 * x2, axis=-1, keepdims=True)
    h2 = x2 * lax.rsqrt(ms2 + EPS) * g2_ref[...]
    h2_hi = h2.astype(BF16)
    h2_lo = (h2 - h2_hi.astype(F32)).astype(BF16)
    pay_ref[:, 0:D_MODEL] = h2_hi

    both = jnp.dot(h2_hi, wr_ref[...], preferred_element_type=F32)
    return (both[:, 0:LANES] + both[:, LANES:2 * LANES]
            + jnp.dot(h2_lo, wr_ref[:, 0:LANES], preferred_element_type=F32)
            + br_ref[...])


def _route_rows(logits, pay_ref, cnt_ref):
    lane = lax.broadcasted_iota(jnp.int32, logits.shape, 1).astype(F32)
    is_grp = (lane >= N_EXPERTS) & (lane < N_EXPERTS + N_EXPERT_GROUPS)
    gl = jnp.where(is_grp, logits, NEG_BIG)
    gmax = jnp.max(gl, axis=-1, keepdims=True)
    gsum = jnp.sum(jnp.where(is_grp, jnp.exp(gl - gmax), 0.0), axis=-1, keepdims=True)
    p_grp = 1.0 / gsum
    big = float(4 * LANES)
    grp = jnp.min(jnp.where(is_grp & (gl == gmax), lane, big), axis=-1,
                  keepdims=True) - N_EXPERTS
    sel = logits
    for g in range(1, N_EXPERT_GROUPS):
        rolled = pltpu.roll(logits, LANES - g * EXPERTS_PER_GROUP, axis=1)
        sel = jnp.where(grp == g, rolled, sel)
    in_grp = lane < EXPERTS_PER_GROUP
    es = jnp.where(in_grp, sel, NEG_BIG)
    top1 = jnp.max(es, axis=-1, keepdims=True)
    i1 = jnp.min(jnp.where(in_grp & (es == top1), lane, big), axis=-1, keepdims=True)
    es2 = jnp.where(lane == i1, NEG_BIG, es)
    top2 = jnp.max(es2, axis=-1, keepdims=True)
    i2 = jnp.min(jnp.where(in_grp & (lane != i1) & (es2 == top2), lane, big), axis=-1,
                 keepdims=True)
    e2 = jnp.exp(top2 - top1)
    w1 = p_grp / (1.0 + e2)
    w2 = p_grp * e2 / (1.0 + e2)
    e1 = grp * EXPERTS_PER_GROUP + i1
    e2x = grp * EXPERTS_PER_GROUP + i2

    def hi_lo(w):
        hi = w.astype(BF16).astype(F32)
        return hi, (w - hi).astype(BF16).astype(F32)

    w1h, w1l = hi_lo(w1)
    w2h, w2l = hi_lo(w2)
    pay_ref[:, D_MODEL:D_MODEL + LANES] = (
        jnp.where(lane == 0.0, e1, 0.0) + jnp.where(lane == 1.0, e2x, 0.0)
        + jnp.where(lane == 2.0, w1h, 0.0) + jnp.where(lane == 3.0, w1l, 0.0)
        + jnp.where(lane == 4.0, w2h, 0.0) + jnp.where(lane == 5.0, w2l, 0.0)).astype(BF16)

    picked = jnp.where((lane == e1) | (lane == e2x), 1.0, 0.0)
    cnt_ref[...] = jnp.sum(picked, axis=0, keepdims=True)


def _merge(x2d, o_attn, gy, g1, w_gates, w_oa, w_glu, w_out, g2, w_router, b_r):
    t = x2d.shape[0]
    tok = lambda i: (i, 0)
    sub_tiles = TM_MERGE // TM_DISP
    return pl.pallas_call(
        _merge_kernel,
        grid=(t // TM_MERGE,),
        in_specs=[
            pl.BlockSpec((TM_MERGE, D_MODEL), tok),
            pl.BlockSpec((TM_MERGE, ATTN_WIDTH), tok),
            pl.BlockSpec((TM_MERGE, SSM_WIDTH), tok),
            _const_spec((1, D_MODEL)),
            _const_spec((D_MODEL, 2 * D_MODEL)),
            _const_spec((ATTN_WIDTH, D_MODEL)),
            _const_spec((SSM_WIDTH, 2 * D_MODEL)),
            _const_spec((D_MODEL, D_MODEL)),
            _const_spec((1, D_MODEL)),
            _const_spec((D_MODEL, 2 * LANES)),
            _const_spec((1, LANES)),
        ],
        out_specs=[
            pl.BlockSpec((TM_MERGE, D_MODEL), tok),
            pl.BlockSpec((TM_MERGE, XS_COLS), tok),
            pl.BlockSpec((sub_tiles, 1, LANES), lambda i: (i, 0, 0)),
        ],
        out_shape=[
            jax.ShapeDtypeStruct((t, D_MODEL), F32),
            jax.ShapeDtypeStruct((t, XS_COLS), BF16),
            jax.ShapeDtypeStruct((t // TM_DISP, 1, LANES), F32),
        ],
        compiler_params=pltpu.CompilerParams(
            dimension_semantics=("arbitrary",), vmem_limit_bytes=VMEM_LIMIT),
        name="merge",
    )(x2d, o_attn, gy, g1, w_gates, w_oa, w_glu, w_out, g2, w_router, b_r)


def _moe_layout(t):
    ntiles = t // TM_DISP
    max_rows = 2 * t + ntiles * N_EXPERTS * (ROW_ALIGN - 1)
    nsteps = -(-max_rows // TS_MOE) + N_EXPERTS
    return ntiles, nsteps


def _lane_col(arr, lane, k):
    return jnp.sum(jnp.where(lane == k, arr, 0.0), axis=-1, keepdims=True)


def _tile_positions(aux, tri_ref, upper_ref, pos_ref):
    lane = lax.broadcasted_iota(jnp.int32, aux.shape, 1)
    lane_f = lane.astype(F32)
    is1 = lane_f == _lane_col(aux, lane, 0)
    is2 = lane_f == _lane_col(aux, lane, 1)
    picked = jnp.where(is1 | is2, 1.0, 0.0)
    rank_all = jnp.dot(tri_ref[...], picked.astype(BF16), preferred_element_type=F32)
    cnt_row = jnp.sum(picked, axis=0, keepdims=True)
    pad_row = jnp.floor((cnt_row + (ROW_ALIGN - 1)) * (1.0 / ROW_ALIGN)) * ROW_ALIGN
    off = jnp.dot(jnp.broadcast_to(pad_row, (SUBLANES, LANES)).astype(BF16), upper_ref[...],
                  preferred_element_type=F32)[0:1, :]
    posmat = rank_all + off
    pos1 = jnp.sum(jnp.where(is1, posmat, 0.0), axis=-1, keepdims=True)
    pos2 = jnp.sum(jnp.where(is2, posmat, 0.0), axis=-1, keepdims=True)
    pos_ref[...] = jnp.where(lane == 0, pos1, 0.0) + jnp.where(lane == 1, pos2, 0.0)

    def digits(p):
        hi = jnp.floor(p * (1.0 / POS_RADIX))
        return hi, p - POS_RADIX * hi

    d1h, d1l = digits(pos1)
    d2h, d2l = digits(pos2)
    dig = (jnp.where(lane == 0, d1h, 0.0) + jnp.where(lane == 1, d1l, 0.0)
           + jnp.where(lane == 2, d2h, 0.0) + jnp.where(lane == 3, d2l, 0.0)).astype(BF16)
    eye8 = (lax.broadcasted_iota(jnp.int32, (SUBLANES, LANES), 0)
            == lax.broadcasted_iota(jnp.int32, (SUBLANES, LANES), 1)).astype(BF16)
    rows = lax.dot_general(eye8, dig, (((1,), (1,)), ((), ())), preferred_element_type=F32)
    return (POS_RADIX * rows[0:1, :] + rows[1:2, :], POS_RADIX * rows[2:3, :] + rows[3:4, :])


def _compact_tile(payload, p1_row, p2_row):
    sub = lax.broadcasted_iota(jnp.int32, (SORT_ROWS, TM_DISP), 0).astype(F32)
    perm = jnp.where((sub == p1_row) | (sub == p2_row), 1.0, 0.0).astype(BF16)
    return jnp.dot(perm, payload, preferred_element_type=F32).astype(BF16)


def _dispatch_kernel(dst_ref, npiece_ref, tail_ref, ntail_ref, pay_ref, tri_ref, upper_ref,
                     xs_ref, pos_ref, buf_ref, zbuf_ref, sem_ref, tsem_ref):
    i = pl.program_id(0)
    last = pl.num_programs(0) - 1
    cur = i % 2

    def piece(tile, slot, sub, p):
        return pltpu.make_async_copy(buf_ref.at[slot, sub, p],
                                     xs_ref.at[dst_ref[tile * MAX_PIECES + p]],
                                     sem_ref.at[slot, sub])

    def start_all(tile, slot, sub):
        def body(p, c):
            piece(tile, slot, sub, p).start()
            return c
        lax.fori_loop(0, npiece_ref[tile], body, 0)

    def wait_all(tile, slot, sub):
        def body(p, c):
            piece(tile, slot, sub, p).wait()
            return c
        lax.fori_loop(0, npiece_ref[tile], body, 0)

    places = []
    for sub in range(DISP_SUB):
        rs = slice(sub * TM_DISP, (sub + 1) * TM_DISP)
        places.append(_tile_positions(pay_ref[rs, D_MODEL:XS_COLS].astype(F32), tri_ref,
                                      upper_ref, pos_ref.at[rs, :]))
    for sub in range(DISP_SUB):
        rs = slice(sub * TM_DISP, (sub + 1) * TM_DISP)
        buf_ref[cur, sub] = _compact_tile(pay_ref[rs, :], *places[sub]).reshape(
            MAX_PIECES, ROW_ALIGN, XS_COLS)
    for sub in range(DISP_SUB):
        start_all(i * DISP_SUB + sub, cur, sub)

    @pl.when(i > 0)
    def _wait_prev():
        for sub in range(DISP_SUB):
            wait_all((i - 1) * DISP_SUB + sub, 1 - cur, sub)

    @pl.when(i == last)
    def _finish():
        for sub in range(DISP_SUB):
            wait_all(i * DISP_SUB + sub, cur, sub)
        zbuf_ref[...] = jnp.zeros(zbuf_ref.shape, BF16)

        def tail(p):
            return pltpu.make_async_copy(zbuf_ref, xs_ref.at[tail_ref[p]], tsem_ref.at[0])

        def start_tail(p, c):
            tail(p).start()
            return c

        def wait_tail(p, c):
            tail(p).wait()
            return c

        lax.fori_loop(0, ntail_ref[0], start_tail, 0)
        lax.fori_loop(0, ntail_ref[0], wait_tail, 0)


def _dispatch(payload, dst, npiece, tail, ntail, rows):
    t = payload.shape[0]
    tm = TM_DISP * DISP_SUB
    tok = lambda i, *_: (i, 0)
    const = lambda i, *_: (0, 0)
    tri = (jnp.arange(TM_DISP)[:, None] > jnp.arange(TM_DISP)[None, :]).astype(BF16)
    upper = (jnp.arange(LANES)[:, None] < jnp.arange(LANES)[None, :]).astype(BF16)
    grid_spec = pltpu.PrefetchScalarGridSpec(
        num_scalar_prefetch=4,
        grid=(t // tm,),
        in_specs=[
            pl.BlockSpec((tm, XS_COLS), tok),
            pl.BlockSpec((TM_DISP, TM_DISP), const),
            pl.BlockSpec((LANES, LANES), const),
        ],
        out_specs=[
            pl.BlockSpec(memory_space=pl.ANY),
            pl.BlockSpec((tm, LANES), tok),
        ],
        scratch_shapes=[
            pltpu.VMEM((2, DISP_SUB, MAX_PIECES, ROW_ALIGN, XS_COLS), BF16),
            pltpu.VMEM((ROW_ALIGN, XS_COLS), BF16),
            pltpu.SemaphoreType.DMA((2, DISP_SUB)),
            pltpu.SemaphoreType.DMA((1,)),
        ],
    )
    xs, pos = pl.pallas_call(
        _dispatch_kernel,
        grid_spec=grid_spec,
        out_shape=[
            jax.ShapeDtypeStruct((rows // ROW_ALIGN, ROW_ALIGN, XS_COLS), BF16),
            jax.ShapeDtypeStruct((t, LANES), F32),
        ],
        compiler_params=pltpu.CompilerParams(
            dimension_semantics=("arbitrary",), vmem_limit_bytes=VMEM_LIMIT),
        name="dispatch",
    )(dst // ROW_ALIGN, npiece, tail // ROW_ALIGN, ntail, payload, tri, upper)
    return xs.reshape(rows, XS_COLS), pos


def _moe_kernel(blk_ref, exp_ref, nvalid_ref, fresh_ref, xs_ref, wg_ref, wu_ref, wd_ref,
                ys_ref, wgb_ref, wub_ref, wdb_ref):
    s = pl.program_id(0)

    @pl.when(fresh_ref[s] == 1)
    def _cast_weights():
        wgb_ref[...] = wg_ref[...].astype(BF16)
        wub_ref[...] = wu_ref[...].astype(BF16)
        wdb_ref[...] = wd_ref[...].astype(BF16)

    @pl.when(s < nvalid_ref[0])
    def _compute():
        t = xs_ref[:, 0:D_MODEL]
        aux = xs_ref[:, D_MODEL:XS_COLS].astype(F32)
        lane = lax.broadcasted_iota(jnp.int32, aux.shape, 1)
        first = _lane_col(aux, lane, 0) == exp_ref[s].astype(F32)
        w = jnp.where(first, _lane_col(aux, lane, 2) + _lane_col(aux, lane, 3),
                      _lane_col(aux, lane, 4) + _lane_col(aux, lane, 5))
        hg = jnp.dot(t, wgb_ref[...], preferred_element_type=F32)
        hu = jnp.dot(t, wub_ref[...], preferred_element_type=F32)
        a = (hg * jax.nn.sigmoid(hg) * hu * w).astype(BF16)
        ys_ref[...] = jnp.dot(a, wdb_ref[...], preferred_element_type=F32).astype(BF16)

    @pl.when(s == nvalid_ref[0])
    def _spare():
        ys_ref[...] = jnp.zeros(ys_ref.shape, BF16)


def _moe(xs, blk, exp, nvalid, fresh, wg, wu, wd):
    row = lambda s, blk, exp, nv, fr: (blk[s], 0)
    wsel = lambda s, blk, exp, nv, fr: (exp[s], 0, 0)
    grid_spec = pltpu.PrefetchScalarGridSpec(
        num_scalar_prefetch=4,
        grid=(blk.shape[0],),
        in_specs=[
            pl.BlockSpec((TS_MOE, XS_COLS), row),
            pl.BlockSpec((None, D_MODEL, D_EXPERT), wsel),
            pl.BlockSpec((None, D_MODEL, D_EXPERT), wsel),
            pl.BlockSpec((None, D_EXPERT, D_MODEL), wsel),
        ],
        out_specs=pl.BlockSpec((TS_MOE, D_MODEL), row),
        scratch_shapes=[
            pltpu.VMEM((D_MODEL, D_EXPERT), BF16),
            pltpu.VMEM((D_MODEL, D_EXPERT), BF16),
            pltpu.VMEM((D_EXPERT, D_MODEL), BF16),
        ],
    )
    return pl.pallas_call(
        _moe_kernel,
        grid_spec=grid_spec,
        out_shape=jax.ShapeDtypeStruct((xs.shape[0], D_MODEL), BF16),
        compiler_params=pltpu.CompilerParams(
            dimension_semantics=("arbitrary",), vmem_limit_bytes=VMEM_LIMIT),
        name="moe",
    )(blk, exp, nvalid, fresh, xs, wg, wu, wd)


def _combine_kernel(dst_ref, npiece_ref, x2_ref, pos_ref, ys_ref, o_ref, stage_ref, sem_ref):
    i = pl.program_id(0)
    n = pl.num_programs(0)
    cur = i % 2
    tm = TM_DISP

    def piece(tile, slot, sub, p):
        row = pl.multiple_of(p * ROW_ALIGN, ROW_ALIGN)
        src = pl.multiple_of(dst_ref[tile * MAX_PIECES + p], ROW_ALIGN)
        return pltpu.make_async_copy(ys_ref.at[pl.ds(src, ROW_ALIGN)],
                                     stage_ref.at[slot, sub, pl.ds(row, ROW_ALIGN)],
                                     sem_ref.at[slot, sub])

    def start_step(step, slot):
        for sub in range(DISP_SUB):
            tile = step * DISP_SUB + sub

            def body(p, c, tile=tile, sub=sub):
                piece(tile, slot, sub, p).start()
                return c
            lax.fori_loop(0, npiece_ref[tile], body, 0)

    @pl.when(i == 0)
    def _first():
        stage_ref[...] = jnp.zeros(stage_ref.shape, BF16)
        start_step(0, 0)

    @pl.when(i + 1 < n)
    def _prefetch():
        start_step(i + 1, 1 - cur)

    for sub in range(DISP_SUB):
        tile = i * DISP_SUB + sub

        def wait_body(p, c, tile=tile, sub=sub):
            piece(tile, cur, sub, p).wait()
            return c

        lax.fori_loop(0, npiece_ref[tile], wait_body, 0)

    for sub in range(DISP_SUB):
        rs = slice(sub * tm, (sub + 1) * tm)
        pos = pos_ref[rs, :]
        lane128 = lax.broadcasted_iota(jnp.int32, pos.shape, 1)
        p1 = _lane_col(pos, lane128, 0)
        p2 = _lane_col(pos, lane128, 1)
        lane = lax.broadcasted_iota(jnp.int32, (tm, SORT_ROWS), 1).astype(F32)
        pick = jnp.where((lane == p1) | (lane == p2), 1.0, 0.0).astype(BF16)
        o_ref[rs, :] = x2_ref[rs, :] + jnp.dot(pick, stage_ref[cur, sub],
                                               preferred_element_type=F32)


def _combine(x2, pos, ys, dst, npiece):
    t = x2.shape[0]
    tm = TM_DISP * DISP_SUB
    tok = lambda i, *_: (i, 0)
    grid_spec = pltpu.PrefetchScalarGridSpec(
        num_scalar_prefetch=2,
        grid=(t // tm,),
        in_specs=[
            pl.BlockSpec((tm, D_MODEL), tok),
            pl.BlockSpec((tm, LANES), tok),
            pl.BlockSpec(memory_space=pl.ANY),
        ],
        out_specs=pl.BlockSpec((tm, D_MODEL), tok),
        scratch_shapes=[
            pltpu.VMEM((2, DISP_SUB, SORT_ROWS, D_MODEL), BF16),
            pltpu.SemaphoreType.DMA((2, DISP_SUB)),
        ],
    )
    return pl.pallas_call(
        _combine_kernel,
        grid_spec=grid_spec,
        out_shape=jax.ShapeDtypeStruct((t, D_MODEL), F32),
        compiler_params=pltpu.CompilerParams(
            dimension_semantics=("arbitrary",), vmem_limit_bytes=VMEM_LIMIT),
        name="combine",
    )(dst, npiece, x2, pos, ys)


def _sparse_moe(payload, cnt, x2, wg, wu, wd):
    t = payload.shape[0]
    ntiles, nsteps = _moe_layout(t)
    rows = (nsteps + 1) * TS_MOE
    i32 = jnp.int32
    counts = cnt[:, 0, :N_EXPERTS].astype(i32)
    padded = (counts + (ROW_ALIGN - 1)) // ROW_ALIGN * ROW_ALIGN
    total = jnp.sum(padded, axis=0)
    reg_rows = (total + (TS_MOE - 1)) // TS_MOE * TS_MOE
    region = jnp.cumsum(reg_rows) - reg_rows
    base = region[None, :] + jnp.cumsum(padded, axis=0) - padded
    np_e = padded // ROW_ALIGN
    cum = jnp.cumsum(np_e, axis=1)
    npiece = cum[:, -1].astype(i32)
    p = jnp.arange(MAX_PIECES, dtype=i32)
    owns = (p[None, :, None] >= (cum - np_e)[:, None, :]) & (p[None, :, None] < cum[:, None, :])
    piece_dst = base[:, None, :] + (p[None, :, None] - (cum - np_e)[:, None, :]) * ROW_ALIGN
    dst = jnp.sum(jnp.where(owns, piece_dst, 0), axis=2).reshape(-1).astype(i32)
    nt_e = (reg_rows - total) // ROW_ALIGN
    cumt = jnp.cumsum(nt_e)
    q = jnp.arange(N_EXPERTS * (TS_MOE // ROW_ALIGN), dtype=i32)
    owns_q = (q[:, None] >= (cumt - nt_e)[None, :]) & (q[:, None] < cumt[None, :])
    tail_dst = (region + total)[None, :] + (q[:, None] - (cumt - nt_e)[None, :]) * ROW_ALIGN
    tail = jnp.sum(jnp.where(owns_q, tail_dst, 0), axis=1).astype(i32)
    ntail = cumt[-1:].astype(i32)
    ntile_e = reg_rows // TS_MOE
    first_t = jnp.cumsum(ntile_e) - ntile_e
    nvalid = jnp.sum(ntile_e).reshape(1).astype(i32)
    s = jnp.arange(nsteps, dtype=i32)
    owns_s = (s[:, None] >= first_t[None, :]) & (s[:, None] < (first_t + ntile_e)[None, :])
    spare = s >= nvalid[0]
    fresh = jnp.any(owns_s & (s[:, None] == first_t[None, :]), axis=1).astype(i32)
    exp = jnp.where(spare, N_EXPERTS - 1, jnp.sum(
        jnp.where(owns_s, jnp.arange(N_EXPERTS, dtype=i32)[None, :], 0), axis=1)).astype(i32)
    blk = jnp.where(spare, nsteps, jnp.sum(
        jnp.where(owns_s, (region // TS_MOE - first_t)[None, :] + s[:, None], 0),
        axis=1)).astype(i32)
    xs, pos = _dispatch(payload, dst, npiece, tail, ntail, rows)
    ys = _moe(xs, blk, exp, nvalid, fresh, wg, wu, wd)
    return _combine(x2, pos, ys, dst, npiece)


def _rope_tables(positions):
    inv = ROPE_THETA ** (-jnp.arange(0, ROPE_DIM, 2, dtype=F32) / ROPE_DIM)
    ang = positions.astype(F32).reshape(-1, 1) * inv
    return jnp.concatenate([jnp.cos(ang), jnp.sin(ang)], axis=1)


def _s5_params(lam_re, lam_im, log_dt, b_re, b_im, c_re, c_im, bsz):
    dt = jnp.exp(log_dt)[:, None]
    mag = jnp.exp(lam_re * dt)
    lb_re = mag * jnp.cos(lam_im * dt)
    lb_im = mag * jnp.sin(lam_im * dt)
    den = lam_re * lam_re + lam_im * lam_im
    k_re = ((lb_re - 1.0) * lam_re + lb_im * lam_im) / den
    k_im = (lb_im * lam_re - (lb_re - 1.0) * lam_im) / den
    bb_re = k_re[..., None] * b_re - k_im[..., None] * b_im
    bb_im = k_re[..., None] * b_im + k_im[..., None] * b_re
    def blk_b(m):
        rows = m.transpose(0, 2, 1).reshape(SSM_WIDTH, SSM_STATE)
        own = (jnp.arange(SSM_WIDTH)[:, None] // SSM_GROUP
               == jnp.arange(N_STATE)[None, :] // SSM_STATE)
        return jnp.where(own, jnp.tile(rows, (1, SSM_GROUPS)), 0.0)

    def blk_c(m):
        rows = m.transpose(0, 2, 1).reshape(N_STATE, SSM_GROUP)
        own = (jnp.arange(N_STATE)[:, None] // SSM_STATE
               == jnp.arange(SSM_WIDTH)[None, :] // SSM_GROUP)
        return jnp.where(own, jnp.tile(rows, (1, SSM_GROUPS)), 0.0)
    bmat = jnp.concatenate([blk_b(bb_re), blk_b(bb_im)], axis=1).astype(BF16)
    cmat = jnp.concatenate([blk_c(c_re), blk_c(-c_im)], axis=0).astype(BF16)
    a_re = jnp.broadcast_to(lb_re.reshape(1, N_STATE), (bsz, N_STATE))
    a_im = jnp.broadcast_to(lb_im.reshape(1, N_STATE), (bsz, N_STATE))
    return bmat, a_re, a_im, cmat


def _time_major_perm(bsz, chunk):
    r = jnp.arange(bsz * chunk)
    src = (r % bsz) * chunk + r // bsz
    perm = (src[:, None] == jnp.arange(bsz * chunk)[None, :]).astype(BF16)
    return perm, perm.T


def kernel(x, positions, norm_mix_g, w_in, q_norm_g, k_norm_g, lambda_q1, lambda_k1, lambda_q2, lambda_k2, subln_g, w_o_attn, ssm_lambda_re, ssm_lambda_im, ssm_log_dt, ssm_b_re, ssm_b_im, ssm_c_re, ssm_c_im, ssm_d, w_glu, w_out, norm_ffn_g, w_router_group, b_router_group, w_router_expert, b_router_expert, w_expert_gate, w_expert_up, w_expert_down):
    bsz, seq, _ = x.shape
    assert bsz == SUBLANES and seq % (2 * TQ) == 0 and seq % (S5_SUB * SSM_CHUNK) == 0
    assert norm_mix_g.shape[0] == 1
    t = bsz * seq
    x2d = x.reshape(t, D_MODEL)
    l = 0

    cos_sin = _rope_tables(positions)
    w_qkvu = w_in[l][:, :QKVU_COLS].astype(BF16)
    w_gates = w_in[l][:, QKVU_COLS:].astype(BF16)
    qg = jnp.tile(q_norm_g[l].reshape(1, HEAD_DIM), (1, 2))
    kg = jnp.tile(k_norm_g[l].reshape(1, HEAD_DIM), (1, 2))
    q, k, v, u = _in_proj(x2d, norm_mix_g[l].reshape(1, D_MODEL), w_qkvu, cos_sin, qg, kg)

    lam = (jnp.exp(jnp.sum(lambda_q1[l] * lambda_k1[l]))
           - jnp.exp(jnp.sum(lambda_q2[l] * lambda_k2[l])) + LAM_INIT).reshape(1)
    o_attn = _diff_attn(q, k, v, lam, subln_g[l].reshape(1, V_DIM), bsz, seq)

    bmat, a_re, a_im, cmat = _s5_params(
        ssm_lambda_re[l], ssm_lambda_im[l], ssm_log_dt[l], ssm_b_re[l], ssm_b_im[l],
        ssm_c_re[l], ssm_c_im[l], bsz)
    perm, perm_t = _time_major_perm(bsz, SSM_CHUNK)
    gy = _s5_scan(u.reshape(bsz, seq, SSM_WIDTH), perm, perm_t, bmat, a_re, a_im, cmat,
                  ssm_d[l].reshape(1, SSM_WIDTH)).reshape(t, SSM_WIDTH)

    w_r = jnp.concatenate(
        [w_router_expert[l].reshape(D_MODEL, N_EXPERTS), w_router_group[l],
         jnp.zeros((D_MODEL, LANES - N_EXPERTS - N_EXPERT_GROUPS), F32)], axis=1)
    b_r = jnp.concatenate(
        [b_router_expert[l].reshape(N_EXPERTS), b_router_group[l],
         jnp.zeros((LANES - N_EXPERTS - N_EXPERT_GROUPS,), F32)]).reshape(1, LANES)
    wr_hi = w_r.astype(BF16)
    wr_lo = (w_r - wr_hi.astype(F32)).astype(BF16)
    x2, payload, cnt = _merge(
        x2d, o_attn, gy, norm_mix_g[l].reshape(1, D_MODEL), w_gates,
        w_o_attn[l].astype(BF16), w_glu[l].astype(BF16), w_out[l].astype(BF16),
        norm_ffn_g[l].reshape(1, D_MODEL), jnp.concatenate([wr_hi, wr_lo], axis=1), b_r)

    wg = w_expert_gate[l].reshape(N_EXPERTS, D_MODEL, D_EXPERT)
    wu = w_expert_up[l].reshape(N_EXPERTS, D_MODEL, D_EXPERT)
    wd = w_expert_down[l].reshape(N_EXPERTS, D_EXPERT, D_MODEL)
    out = _sparse_moe(payload, cnt, x2, wg, wu, wd)
    return out.reshape(bsz, seq, D_MODEL)
```
